```python
import math
import jax, jax.numpy as jnp
from jax import lax
import numpy as np

D_MODEL = 1024
BATCH = 4
SEQ = 4096
DEPTH = 2
DEC_BATCH = 128
DEC_SEQ = 4
PAST_LEN = 2048
PAGE_SIZE = 128

HEAD_DIM = 64
ATT_GROUPS = ((128, 1), (512, 4), (2048, 16))
HEADS_PER_GROUP = 4
N_ATT_HEADS = HEADS_PER_GROUP * len(ATT_GROUPS)
ATT_WIDTH = N_ATT_HEADS * HEAD_DIM
SSM_HEAD_DIM = 64
SSM_WIDTH = D_MODEL
SSM_HEADS = SSM_WIDTH // SSM_HEAD_DIM
SSM_STATE = 128
SSM_GROUPS = 2
SSM_HPG = SSM_HEADS // SSM_GROUPS
SSM_CONV = 4
SSM_CHUNK = 128
XBC_WIDTH = SSM_WIDTH + 2 * SSM_GROUPS * SSM_STATE
MIX_WIDTH = ATT_WIDTH + SSM_WIDTH
IN_SPLITS = (ATT_WIDTH, 2 * ATT_WIDTH, 3 * ATT_WIDTH, 3 * ATT_WIDTH + SSM_WIDTH, 3 * ATT_WIDTH + SSM_WIDTH + XBC_WIDTH)
IN_WIDTH = IN_SPLITS[-1] + SSM_HEADS
D_FF = 128 * int(math.ceil(8 * D_MODEL / 3 / 128))
FFN_CONV = 3
EPS = 1e-6

kernel_name = 'hymba_ssd_dilated_convffn_step'


def rmsnorm(x, g):
    xf = x.astype(jnp.float32)
    y = xf * lax.rsqrt(jnp.mean(xf * xf, axis=-1, keepdims=True) + EPS)
    return (y * g.astype(jnp.float32)).astype(x.dtype)


def alibi_slopes():
    h = jnp.arange(1, N_ATT_HEADS + 1, dtype=jnp.float32)
    return jnp.exp2(-8.0 * h / N_ATT_HEADS)


def softmax_lse(s):
    mx = jnp.max(s, axis=-1, keepdims=True)
    e = jnp.exp(s - mx)
    den = jnp.sum(e, axis=-1, keepdims=True)
    return e / den, (mx + jnp.log(den))[..., 0]


def causal_dwconv(x, prev, w, b):
    width = w.shape[0]
    L = x.shape[1]
    xp = jnp.concatenate([prev.astype(x.dtype), x], axis=1)
    y = xp[:, 0:L] * w[0]
    for i in range(1, width):
        y = y + xp[:, i:i + L] * w[i]
    return y + b, xp[:, L:]


def dilated_attn_prompt(q, k, v, slopes, window, dil):
    bsz, S, H, hd = q.shape
    band = window // dil
    span = band * dil
    lp = -(-S // span) * span
    m = lp // dil
    nb = m // band

    def to_blocks(t):
        t = jnp.pad(t, ((0, 0), (0, lp - S), (0, 0), (0, 0)))
        t = t.reshape(bsz, m, dil, H, hd).transpose(0, 2, 1, 3, 4)
        return t.reshape(bsz, dil, nb, band, H, hd)

    def with_prev(t):
        prev = jnp.pad(t, ((0, 0), (0, 0), (1, 0), (0, 0), (0, 0), (0, 0)))[:, :, :-1]
        return jnp.concatenate([prev, t], axis=3)

    qb = to_blocks(q)
    kb = with_prev(to_blocks(k))
    vb = with_prev(to_blocks(v))
    s = jnp.einsum('brnihd,brnjhd->brnhij', qb, kb).astype(jnp.float32) * (hd ** -0.5)
    step = (jnp.arange(band)[:, None] + band) - jnp.arange(2 * band)[None, :]
    valid = (step >= 0) & (step <= band)
    valid = valid[None] & ((jnp.arange(nb)[:, None, None] > 0) | (jnp.arange(2 * band)[None, None, :] >= band))
    bias = -slopes.astype(jnp.float32)[:, None, None] * (step * dil).astype(jnp.float32)
    s = jnp.where(valid[:, None], s + bias, -jnp.inf)
    p, lse = softmax_lse(s)
    o = jnp.einsum('brnhij,brnjhd->brnihd', p.astype(v.dtype), vb)
    o = o.reshape(bsz, dil, m, H, hd).transpose(0, 2, 1, 3, 4).reshape(bsz, lp, H, hd)[:, :S]
    lse = lse.transpose(0, 1, 2, 4, 3).reshape(bsz, dil, m, H).transpose(0, 2, 1, 3).reshape(bsz, lp, H)[:, :S]
    return o, lse


def dilated_attn_sample(q, k, v, kv_cache, slopes, window, dil):
    T, hd = q.shape[1], q.shape[-1]
    lc = kv_cache.shape[1]
    band = window // dil
    keys = jnp.concatenate([kv_cache[:, :, 0].astype(k.dtype), k], axis=1)
    vals = jnp.concatenate([kv_cache[:, :, 1].astype(v.dtype), v], axis=1)
    steps = jnp.arange(band + 1)
    idx = lc + jnp.arange(T)[:, None] - steps[None, :] * dil
    valid = idx >= 0
    idx = jnp.maximum(idx, 0)
    kg = keys[:, idx]
    vg = vals[:, idx]
    s = jnp.einsum('bthd,btjhd->bthj', q, kg).astype(jnp.float32) * (hd ** -0.5)
    bias = -slopes.astype(jnp.float32)[:, None] * (steps * dil).astype(jnp.float32)[None, :]
    s = jnp.where(valid[:, None, :], s + bias, -jnp.inf)
    p, lse = softmax_lse(s)
    o = jnp.einsum('bthj,btjhd->bthd', p.astype(v.dtype), vg)
    return o, lse


def ssd_scan(x, dt, a, bm, cm, h0):
    bsz, L, H, P = x.shape
    G, N = bm.shape[2], bm.shape[3]
    hpg = H // G
    f32 = jnp.float32
    q = min(SSM_CHUNK, L)
    lp = -(-L // q) * q
    nc = lp // q

    def padl(t):
        return jnp.pad(t, [(0, 0), (0, lp - L)] + [(0, 0)] * (t.ndim - 2))

    xc = padl(x.astype(f32)).reshape(bsz, nc, q, G, hpg, P)
    dtc = padl(dt).reshape(bsz, nc, q, G, hpg)
    bc = padl(bm.astype(f32)).reshape(bsz, nc, q, G, N)
    cc = padl(cm.astype(f32)).reshape(bsz, nc, q, G, N)
    acum = jnp.cumsum(dtc * a.reshape(G, hpg), axis=2)
    causal = jnp.tril(jnp.ones((q, q), bool))[:, :, None, None]
    seg = acum[:, :, :, None] - acum[:, :, None, :]
    decay = jnp.exp(jnp.where(causal, seg, -jnp.inf))
    xdt = xc * dtc[..., None]
    cb = jnp.einsum('bcign,bcjgn->bcijg', cc, bc)
    y_intra = jnp.einsum('bcijg,bcijgh,bcjghp->bcighp', cb, decay, xdt)
    decay_end = jnp.exp(acum[:, :, -1:] - acum)
    states = jnp.einsum('bcjgn,bcjgh,bcjghp->bcghpn', bc, decay_end, xdt)
    chunk_decay = jnp.exp(acum[:, :, -1])

    def step(h, inp):
        s_c, d_c = inp
        return h * d_c[..., None, None] + s_c, h

    h_last, h_prev = lax.scan(step, h0.reshape(bsz, G, hpg, P, N),
                              (jnp.moveaxis(states, 1, 0), jnp.moveaxis(chunk_decay, 1, 0)))
    h_prev = jnp.moveaxis(h_prev, 0, 1)
    y_inter = jnp.einsum('bcign,bcigh,bcghpn->bcighp', cc, jnp.exp(acum), h_prev)
    y = (y_intra + y_inter).reshape(bsz, lp, H, P)[:, :L]
    return y, h_last.reshape(bsz, H, P, N)


def hybrid_layer(x, lw, kv_caches, h0, conv_prev, ffn_prev):
    (norm_mix, w_in, q_norm, k_norm, conv_w, conv_b, dt_bias, a_log, d_skip, ssm_norm, w_out,
     norm_ffn, w_up, ffn_conv_w, ffn_conv_b, w_down) = lw
    bsz, L, _ = x.shape
    h = rmsnorm(x, norm_mix)
    proj = h @ w_in
    q, k, v, z, xbc, dt_raw = jnp.split(proj, IN_SPLITS, axis=-1)

    q = rmsnorm(q.reshape(bsz, L, N_ATT_HEADS, HEAD_DIM), q_norm)
    k = rmsnorm(k.reshape(bsz, L, N_ATT_HEADS, HEAD_DIM), k_norm)
    v = v.reshape(bsz, L, N_ATT_HEADS, HEAD_DIM)
    slopes = alibi_slopes().reshape(len(ATT_GROUPS), HEADS_PER_GROUP)
    outs, lses, new_kv = [], [], []
    for g, (win, dil) in enumerate(ATT_GROUPS):
        hs = slice(g * HEADS_PER_GROUP, (g + 1) * HEADS_PER_GROUP)
        qg, kg, vg = q[:, :, hs], k[:, :, hs], v[:, :, hs]
        if kv_caches is None:
            o, lse = dilated_attn_prompt(qg, kg, vg, slopes[g], win, dil)
            keep = min(win, L)
            new_kv.append(jnp.stack([kg[:, L - keep:], vg[:, L - keep:]], axis=2))
        else:
            o, lse = dilated_attn_sample(qg, kg, vg, kv_caches[g], slopes[g], win, dil)
            new_kv.append(jnp.stack([kg, vg], axis=2))
        outs.append(o)
        lses.append(lse)
    alpha = jax.nn.softmax(jnp.stack(lses, axis=0), axis=0)
    att = jnp.concatenate([o * alpha[g][..., None].astype(o.dtype) for g, o in enumerate(outs)], axis=2)
    att = att.reshape(bsz, L, ATT_WIDTH)

    xbc_c, conv_new = causal_dwconv(xbc, conv_prev, conv_w, conv_b)
    xbc_c = jax.nn.silu(xbc_c)
    xs, bm, cm = jnp.split(xbc_c, (SSM_WIDTH, SSM_WIDTH + SSM_GROUPS * SSM_STATE), axis=-1)
    dt = jax.nn.softplus(dt_raw.astype(jnp.float32) + dt_bias.astype(jnp.float32))
    a = -jnp.exp(a_log.astype(jnp.float32))
    xs = xs.reshape(bsz, L, SSM_HEADS, SSM_HEAD_DIM)
    y, h_last = ssd_scan(xs, dt, a, bm.reshape(bsz, L, SSM_GROUPS, SSM_STATE),
                         cm.reshape(bsz, L, SSM_GROUPS, SSM_STATE), h0)
    y = y + d_skip.astype(jnp.float32)[:, None] * xs.astype(jnp.float32)
    y = y.reshape(bsz, L, SSM_WIDTH) * jax.nn.silu(z.astype(jnp.float32))
    y = rmsnorm(y.astype(x.dtype), ssm_norm)

    x = x + jnp.concatenate([att, y], axis=-1) @ w_out

    u = rmsnorm(x, norm_ffn) @ w_up
    u, ffn_new = causal_dwconv(u, ffn_prev, ffn_conv_w, ffn_conv_b)
    gate, up = jnp.split(u, 2, axis=-1)
    x = x + (jax.nn.silu(gate) * up) @ w_down
    return x, new_kv, h_last, conv_new, ffn_new


def setup_inputs(seed: int = 0) -> dict:
    key = jax.random.key(seed)
    ks = jax.random.split(key, 24)
    f32 = jnp.float32

    def nrm(k, shape, scale):
        return jax.random.normal(k, shape, f32) * scale

    kv_shape = lambda win: (DEPTH, DEC_BATCH, min(win, PAST_LEN), 2, HEADS_PER_GROUP, HEAD_DIM)
    dt0 = jnp.exp(jax.random.uniform(ks[14], (DEPTH, SSM_HEADS), f32, math.log(1e-3), math.log(1e-1)))
    return {
        'x_prompt': nrm(ks[0], (BATCH, SEQ, D_MODEL), 1.0),
        'x_sample': nrm(ks[1], (DEC_BATCH, DEC_SEQ, D_MODEL), 1.0),
        'cache_kv0': nrm(ks[2], kv_shape(ATT_GROUPS[0][0]), 1.0),
        'cache_kv1': nrm(ks[3], kv_shape(ATT_GROUPS[1][0]), 1.0),
        'cache_kv2': nrm(ks[4], kv_shape(ATT_GROUPS[2][0]), 1.0),
        'state_ssm': nrm(ks[5], (DEPTH, DEC_BATCH, SSM_HEADS, SSM_HEAD_DIM, SSM_STATE), 0.1),
        'state_conv': nrm(ks[6], (DEPTH, DEC_BATCH, SSM_CONV - 1, XBC_WIDTH), 1.0),
        'state_ffn_conv': nrm(ks[7], (DEPTH, DEC_BATCH, FFN_CONV - 1, 2 * D_FF), 1.0),
        'norm_mix': 1.0 + nrm(ks[8], (DEPTH, D_MODEL), 0.02),
        'w_in': nrm(ks[9], (DEPTH, D_MODEL, IN_WIDTH), D_MODEL ** -0.5),
        'q_norm': 1.0 + nrm(ks[10], (DEPTH, HEAD_DIM), 0.02),
        'k_norm': 1.0 + nrm(ks[11], (DEPTH, HEAD_DIM), 0.02),
        'conv_w': nrm(ks[12], (DEPTH, SSM_CONV, XBC_WIDTH), SSM_CONV ** -0.5),
        'conv_b': nrm(ks[13], (DEPTH, XBC_WIDTH), 0.02),
        'dt_bias': dt0 + jnp.log(-jnp.expm1(-dt0)),
        'a_log': jnp.log(jax.random.uniform(ks[15], (DEPTH, SSM_HEADS), f32, 1.0, 16.0)),
        'd_skip': 1.0 + nrm(ks[16], (DEPTH, SSM_HEADS), 0.1),
        'ssm_norm': 1.0 + nrm(ks[17], (DEPTH, SSM_WIDTH), 0.02),
        'w_out': nrm(ks[18], (DEPTH, MIX_WIDTH, D_MODEL), MIX_WIDTH ** -0.5),
        'norm_ffn': 1.0 + nrm(ks[19], (DEPTH, D_MODEL), 0.02),
        'w_up': nrm(ks[20], (DEPTH, D_MODEL, 2 * D_FF), D_MODEL ** -0.5),
        'ffn_conv_w': nrm(ks[21], (DEPTH, FFN_CONV, 2 * D_FF), FFN_CONV ** -0.5),
        'ffn_conv_b': nrm(ks[22], (DEPTH, 2 * D_FF), 0.02),
        'w_down': nrm(ks[23], (DEPTH, D_FF, D_MODEL), D_FF ** -0.5),
    }


def reference(x_prompt, x_sample, cache_kv0, cache_kv1, cache_kv2, state_ssm, state_conv, state_ffn_conv,
              norm_mix, w_in, q_norm, k_norm, conv_w, conv_b, dt_bias, a_log, d_skip, ssm_norm, w_out,
              norm_ffn, w_up, ffn_conv_w, ffn_conv_b, w_down):
    stacked = (norm_mix, w_in, q_norm, k_norm, conv_w, conv_b, dt_bias, a_log, d_skip, ssm_norm, w_out,
               norm_ffn, w_up, ffn_conv_w, ffn_conv_b, w_down)
    y_prompt, y_sample = x_prompt, x_sample
    nbp = x_prompt.shape[0]
    kv_p, kv_s = ([], [], []), ([], [], [])
    ssm_p, conv_p, ffn_p, ssm_s, conv_s, ffn_s = [], [], [], [], [], []
    for layer in range(DEPTH):
        lw = tuple(w[layer] for w in stacked)
        h0 = jnp.zeros((nbp, SSM_HEADS, SSM_HEAD_DIM, SSM_STATE), jnp.float32)
        conv0 = jnp.zeros((nbp, SSM_CONV - 1, XBC_WIDTH), x_prompt.dtype)
        ffn0 = jnp.zeros((nbp, FFN_CONV - 1, 2 * D_FF), x_prompt.dtype)
        y_prompt, kv, h, c, f = hybrid_layer(y_prompt, lw, None, h0, conv0, ffn0)
        for g in range(len(ATT_GROUPS)):
            kv_p[g].append(kv[g])
        ssm_p.append(h)
        conv_p.append(c)
        ffn_p.append(f)
        y_sample, kv, h, c, f = hybrid_layer(
            y_sample, lw, (cache_kv0[layer], cache_kv1[layer], cache_kv2[layer]),
            state_ssm[layer].astype(jnp.float32), state_conv[layer], state_ffn_conv[layer])
        for g in range(len(ATT_GROUPS)):
            kv_s[g].append(kv[g])
        ssm_s.append(h)
        conv_s.append(c)
        ffn_s.append(f)
    new_kv0_prompt = jnp.stack(kv_p[0])
    new_kv1_prompt = jnp.stack(kv_p[1])
    new_kv2_prompt = jnp.stack(kv_p[2])
    new_ssm_prompt = jnp.stack(ssm_p)
    new_conv_prompt = jnp.stack(conv_p)
    new_ffn_conv_prompt = jnp.stack(ffn_p)
    new_kv0_sample = jnp.stack(kv_s[0])
    new_kv1_sample = jnp.stack(kv_s[1])
    new_kv2_sample = jnp.stack(kv_s[2])
    new_ssm_sample = jnp.stack(ssm_s)
    new_conv_sample = jnp.stack(conv_s)
    new_ffn_conv_sample = jnp.stack(ffn_s)
    return (y_prompt, y_sample,
            new_kv0_prompt, new_kv1_prompt, new_kv2_prompt, new_ssm_prompt, new_conv_prompt, new_ffn_conv_prompt,
            new_kv0_sample, new_kv1_sample, new_kv2_sample, new_ssm_sample, new_conv_sample, new_ffn_conv_sample)
```

```python
import functools
import math

import jax
import jax.numpy as jnp
from jax import lax
from jax.experimental import pallas as pl
from jax.experimental.pallas import tpu as pltpu

F32 = jnp.float32
BF16 = jnp.bfloat16

D_MODEL = 1024
HEAD_DIM = 64
ATT_GROUPS = ((128, 1), (512, 4), (2048, 16))
BAND = 128
HEADS_PER_GROUP = 4
GROUP_W = HEADS_PER_GROUP * HEAD_DIM
N_ATT_HEADS = HEADS_PER_GROUP * len(ATT_GROUPS)
ATT_WIDTH = N_ATT_HEADS * HEAD_DIM
QKV_W = 3 * ATT_WIDTH
SSM_HEAD_DIM = 64
SSM_WIDTH = 1024
SSM_HEADS = SSM_WIDTH // SSM_HEAD_DIM
SSM_STATE = 128
SSM_GROUPS = 2
SSM_HPG = SSM_HEADS // SSM_GROUPS
SSM_GROUP_W = SSM_HPG * SSM_HEAD_DIM
SSM_CONV = 4
SSM_CHUNK = 128
BC_W = SSM_GROUPS * SSM_STATE
XBC_WIDTH = SSM_WIDTH + 2 * BC_W
D_FF = 2816
FFN_CONV = 3
FF_CHUNK = 256
EPS = 1e-6
DT_PAD = 128
SUBLANES = 8
SAMPLE_ROWS = 128
VMEM_LIMIT = 56 * 1024 * 1024
NEG_INF = float("-inf")


def _dot(a, b):
    return jnp.dot(a, b, preferred_element_type=F32)


def _dot_nt(a, b):
    return lax.dot_general(a, b, (((1,), (1,)), ((), ())), preferred_element_type=F32)


def _dot_tn(a, b):
    return lax.dot_general(a, b, (((0,), (0,)), ((), ())), preferred_element_type=F32)


def _split(x, pieces):
    out = []
    r = x
    for _ in range(pieces):
        p = r.astype(BF16)
        out.append(p)
        r = r - p.astype(F32)
    return out


def _dot_sel(x, sel, pieces=3):
    acc = None
    for p in _split(x, pieces):
        t = _dot(p, sel)
        acc = t if acc is None else acc + t
    return acc


def _sel_dot(sel, x, pieces=3):
    acc = None
    for p in _split(x, pieces):
        t = _dot(sel, p)
        acc = t if acc is None else acc + t
    return acc


def _sel_dot_nt(x, sel, pieces=3):
    acc = None
    for p in _split(x, pieces):
        t = _dot_nt(p, sel)
        acc = t if acc is None else acc + t
    return acc


def _silu(x):
    return x * jax.nn.sigmoid(x)


def _softplus(x):
    return jnp.maximum(x, 0.0) + jnp.log1p(jnp.exp(-jnp.abs(x)))


def _rms(x, g):
    ms = jnp.mean(x * x, axis=-1, keepdims=True)
    return x * lax.rsqrt(ms + EPS) * g


def _const_spec(shape):
    nd = len(shape)
    return pl.BlockSpec(shape, lambda *_: (0,) * nd, pipeline_mode=pl.Buffered(1))


def _params(sem):
    return pltpu.CompilerParams(dimension_semantics=sem, vmem_limit_bytes=VMEM_LIMIT)


def _head_block_diag(width):
    h = jnp.arange(width) // HEAD_DIM
    return jnp.where(h[:, None] == h[None, :], 1.0 / HEAD_DIM, 0.0).astype(BF16)


def _head_rms(t, g, bd):
    ms = _dot_sel(t * t, bd, pieces=2)
    return t * lax.rsqrt(ms + EPS) * g


def _shift_rows(u, d, head):
    rolled = pltpu.roll(u, d, axis=0)
    hrolled = pltpu.roll(head, d, axis=0)
    rows = lax.broadcasted_iota(jnp.int32, head.shape, 0)
    first = jnp.where(rows < d, hrolled, rolled[0:SUBLANES])
    return jnp.concatenate([first, rolled[SUBLANES:]], axis=0)


def _inproj_body(x_ref, g_ref, wq_ref, wz_ref, wx_ref, wd_ref, dtb_ref,
                 qkv_ref, z_ref, xbc_ref, dt_ref):
    h = _rms(x_ref[...], g_ref[...]).astype(BF16)
    qkv_ref[...] = _dot(h, wq_ref[...])
    z_ref[...] = _dot(h, wz_ref[...])
    xbc_ref[...] = _dot(h, wx_ref[...])
    dt_ref[...] = _softplus(_dot(h, wd_ref[...]) + dtb_ref[...])


def _inproj(x, g, wq, wz, wx, wd, dtb, tm):
    m = x.shape[0]
    row = lambda w: pl.BlockSpec((tm, w), lambda i: (i, 0))
    widths = (QKV_W, SSM_WIDTH, XBC_WIDTH, DT_PAD)
    return pl.pallas_call(
        _inproj_body,
        grid=(m // tm,),
        in_specs=[row(D_MODEL), _const_spec(g.shape), _const_spec(wq.shape), _const_spec(wz.shape),
                  _const_spec(wx.shape), _const_spec(wd.shape), _const_spec(dtb.shape)],
        out_specs=[row(w) for w in widths],
        out_shape=[jax.ShapeDtypeStruct((m, w), F32) for w in widths],
        compiler_params=_params(("parallel",)),
        name="inproj",
    )(x, g, wq, wz, wx, wd, dtb)


def _attn_prompt_body(q_ref, k_ref, v_ref, bias_ref, gq_ref, gk_ref, bd_ref,
                      o_ref, lse_ref, kn_ref, kv_scr):
    n = pl.program_id(2)

    @pl.when(n == 0)
    def _():
        kv_scr[:, 0:BAND, :] = jnp.zeros((2, BAND, GROUP_W), BF16)

    bd = bd_ref[...]
    qn = _head_rms(q_ref[0], gq_ref[...], bd)
    kn = _head_rms(k_ref[0], gk_ref[...], bd)
    kn_ref[0] = kn
    kv_scr[0, BAND:2 * BAND, :] = kn.astype(BF16)
    kv_scr[1, BAND:2 * BAND, :] = v_ref[0].astype(BF16)
    qb = (qn * (HEAD_DIM ** -0.5)).astype(BF16)
    for h in range(HEADS_PER_GROUP):
        sl = slice(h * HEAD_DIM, (h + 1) * HEAD_DIM)
        s = _dot_nt(qb[:, sl], kv_scr[0, :, sl]) + bias_ref[0, h]
        mx = jnp.max(s, axis=-1, keepdims=True)
        e = jnp.exp(s - mx)
        den = jnp.sum(e, axis=-1, keepdims=True)
        o_ref[0, :, sl] = _dot(e.astype(BF16), kv_scr[1, :, sl]) / den
        lse_ref[0, :, sl] = jnp.broadcast_to(mx + jnp.log(den), (BAND, HEAD_DIM))
    kv_scr[:, 0:BAND, :] = kv_scr[:, BAND:2 * BAND, :]


def _prompt_bias(slopes, dil):
    i = jnp.arange(BAND)[:, None]
    j = jnp.arange(2 * BAND)[None, :]
    step = i + BAND - j
    valid = (step >= 0) & (step <= BAND)
    bias = -slopes.astype(F32)[:, None, None] * (step * dil).astype(F32)
    full = jnp.where(valid[None], bias, NEG_INF)
    first = jnp.where((valid & (j >= BAND))[None], bias, NEG_INF)
    return jnp.stack([first, full])


def _attn_prompt(qkv, g, dil, bias, gq, gk, bd):
    bsz, seq, _ = qkv.shape
    assert seq % (BAND * dil) == 0
    m = seq // dil
    nb = m // BAND
    qkv_r = qkv.reshape(bsz, m, dil * QKV_W)
    per_res = QKV_W // GROUP_W
    qspec = lambda off: pl.BlockSpec((1, BAND, GROUP_W), lambda b, r, n: (b, n, r * per_res + off))
    ospec = pl.BlockSpec((1, BAND, GROUP_W), lambda b, r, n: (b, n, r))
    ngrp = len(ATT_GROUPS)
    outs = pl.pallas_call(
        _attn_prompt_body,
        grid=(bsz, dil, nb),
        in_specs=[qspec(g), qspec(ngrp + g), qspec(2 * ngrp + g),
                  pl.BlockSpec((1, HEADS_PER_GROUP, BAND, 2 * BAND), lambda b, r, n: (jnp.minimum(n, 1), 0, 0, 0)),
                  _const_spec(gq.shape), _const_spec(gk.shape), _const_spec(bd.shape)],
        out_specs=[ospec, ospec, ospec],
        out_shape=[jax.ShapeDtypeStruct((bsz, m, dil * GROUP_W), F32)] * 3,
        scratch_shapes=[pltpu.VMEM((2, 2 * BAND, GROUP_W), BF16)],
        compiler_params=_params(("parallel", "parallel", "arbitrary")),
        name=f"attn_prompt_g{g}",
    )(qkv_r, qkv_r, qkv_r, bias, gq, gk, bd)
    return [t.reshape(bsz, seq, GROUP_W) for t in outs]


def _ssd_prompt_body(xbc_ref, z_ref, dt_ref, dtt_ref, cw_ref, cb_ref, alr_ref, alc_ref, dsk_ref, gn_ref,
                     tri_ref, exp_ref, y_ref, hl_ref, tail_scr, state_scr):
    c = pl.program_id(1)
    q = SSM_CHUNK

    @pl.when(c == 0)
    def _():
        tail_scr[...] = jnp.zeros_like(tail_scr)
        state_scr[...] = jnp.zeros_like(state_scr)

    raw = xbc_ref[0]
    tail = tail_scr[...]
    cw = cw_ref[...]
    acc = _shift_rows(raw, 3, tail) * cw[0:1]
    acc = acc + _shift_rows(raw, 2, tail) * cw[1:2]
    acc = acc + _shift_rows(raw, 1, tail) * cw[2:3]
    acc = acc + raw * cw[3:4]
    tail_scr[...] = raw[q - SUBLANES:q]
    xc = _silu(acc + cb_ref[...])
    xs = xc[:, :SSM_WIDTH]
    bm = xc[:, SSM_WIDTH:SSM_WIDTH + BC_W]
    cm = xc[:, SSM_WIDTH + BC_W:]

    tri = tri_ref[...]
    dt = dt_ref[0][:, :SSM_HEADS]
    acum = _sel_dot(tri, dt * (-jnp.exp(alr_ref[...])))
    acum_t = _sel_dot_nt(dtt_ref[0] * (-jnp.exp(alc_ref[...])), tri)
    last = acum[q - 1:q, :]
    expand = exp_ref[...]
    dt_x = _dot_sel(dt, expand)
    ea_x = _dot_sel(jnp.exp(acum), expand)
    de_x = _dot_sel(jnp.exp(last - acum), expand)
    xdt = xs * dt_x
    xde = (xdt * de_x).astype(BF16)
    xdt_b = xdt.astype(BF16)
    causal = lax.broadcasted_iota(jnp.int32, (q, q), 0) >= lax.broadcasted_iota(jnp.int32, (q, q), 1)
    low = lax.broadcasted_iota(jnp.int32, (q, 2 * SSM_HEAD_DIM), 1) < SSM_HEAD_DIM

    y_parts = []
    for g in range(SSM_GROUPS):
        bm_g = bm[:, g * SSM_STATE:(g + 1) * SSM_STATE].astype(BF16)
        cm_g = cm[:, g * SSM_STATE:(g + 1) * SSM_STATE].astype(BF16)
        cb = _dot_nt(cm_g, bm_g)
        rows = slice(g * SSM_GROUP_W, (g + 1) * SSM_GROUP_W)
        st_g = state_scr[rows, :]
        inter = _dot_nt(cm_g, st_g.astype(BF16))
        for hp in range(SSM_HPG // 2):
            pair = None
            for k in range(2):
                h = g * SSM_HPG + 2 * hp + k
                seg = acum[:, h:h + 1] - acum_t[h:h + 1, :]
                decay = jnp.exp(jnp.where(causal, seg, NEG_INF))
                gmat = (cb * decay).astype(BF16)
                lanes = slice((2 * hp) * SSM_HEAD_DIM + g * SSM_GROUP_W,
                              (2 * hp + 2) * SSM_HEAD_DIM + g * SSM_GROUP_W)
                x_pair = xdt_b[:, lanes]
                keep = low if k == 0 else jnp.logical_not(low)
                t = _dot(gmat, jnp.where(keep, x_pair, jnp.zeros_like(x_pair)))
                pair = t if pair is None else pair + t
            y_parts.append(pair + ea_x[:, lanes] * inter[:, lanes.start - g * SSM_GROUP_W:lanes.stop - g * SSM_GROUP_W])
        new = _dot_tn(xde[:, rows], bm_g)
        for hh in range(SSM_HPG):
            h = g * SSM_HPG + hh
            cd = jnp.exp(acum_t[h:h + 1, q - 1:q])
            r = slice(h * SSM_HEAD_DIM, (h + 1) * SSM_HEAD_DIM)
            rl = slice(hh * SSM_HEAD_DIM, (hh + 1) * SSM_HEAD_DIM)
            state_scr[r, :] = st_g[rl, :] * cd + new[rl, :]

    y = jnp.concatenate(y_parts, axis=1) + dsk_ref[...] * xs
    y = y * _silu(z_ref[0])
    y_ref[0] = _rms(y, gn_ref[...])

    @pl.when(c == pl.num_programs(1) - 1)
    def _():
        hl_ref[0] = state_scr[...]


def _ssd_prompt(xbc, z, dt, dtt, cw, cb, alr, alc, dsk, gn, tri, expand):
    bsz, seq, _ = xbc.shape
    assert seq % SSM_CHUNK == 0
    nc = seq // SSM_CHUNK
    blk = lambda w: pl.BlockSpec((1, SSM_CHUNK, w), lambda b, c: (b, c, 0))
    consts = (cw, cb, alr, alc, dsk, gn, tri, expand)
    return pl.pallas_call(
        _ssd_prompt_body,
        grid=(bsz, nc),
        in_specs=[blk(XBC_WIDTH), blk(SSM_WIDTH), blk(DT_PAD),
                  pl.BlockSpec((1, SSM_HEADS, SSM_CHUNK), lambda b, c: (b, 0, c))]
                 + [_const_spec(a.shape) for a in consts],
        out_specs=[blk(SSM_WIDTH), pl.BlockSpec((1, SSM_WIDTH, SSM_STATE), lambda b, c: (b, 0, 0))],
        out_shape=[jax.ShapeDtypeStruct((bsz, seq, SSM_WIDTH), F32),
                   jax.ShapeDtypeStruct((bsz, SSM_WIDTH, SSM_STATE), F32)],
        scratch_shapes=[pltpu.VMEM((SUBLANES, XBC_WIDTH), F32), pltpu.VMEM((SSM_WIDTH, SSM_STATE), F32)],
        compiler_params=_params(("parallel", "arbitrary")),
        name="ssd_prompt",
    )(xbc, z, dt, dtt, *consts)


def _outproj_body(o0_ref, o1_ref, o2_ref, l0_ref, l1_ref, l2_ref, y_ref, x_ref, wa_ref, wy_ref, out_ref):
    ls = (l0_ref[...], l1_ref[...], l2_ref[...])
    mx = jnp.maximum(jnp.maximum(ls[0], ls[1]), ls[2])
    es = [jnp.exp(l - mx) for l in ls]
    inv = 1.0 / (es[0] + es[1] + es[2])
    acc = x_ref[...] + _dot(y_ref[...].astype(BF16), wy_ref[...])
    for g, o_ref in enumerate((o0_ref, o1_ref, o2_ref)):
        att = (o_ref[...] * (es[g] * inv)).astype(BF16)
        acc = acc + _dot(att, wa_ref[g * GROUP_W:(g + 1) * GROUP_W, :])
    out_ref[...] = acc


def _outproj(os_, ls_, y, x, wa, wy, tm):
    m = x.shape[0]
    row = lambda w: pl.BlockSpec((tm, w), lambda i: (i, 0))
    return pl.pallas_call(
        _outproj_body,
        grid=(m // tm,),
        in_specs=[row(GROUP_W)] * 6 + [row(SSM_WIDTH), row(D_MODEL), _const_spec(wa.shape), _const_spec(wy.shape)],
        out_specs=row(D_MODEL),
        out_shape=jax.ShapeDtypeStruct((m, D_MODEL), F32),
        compiler_params=_params(("parallel",)),
        name="outproj",
    )(*os_, *ls_, y, x, wa, wy)


def _ffn_chunk(h, c, wu_ref, wd_ref, fw_ref, fb_ref, shifted):
    halves = []
    for base in (0, D_FF):
        cols = slice(base + c * FF_CHUNK, base + (c + 1) * FF_CHUNK)
        u = _dot(h, wu_ref[:, cols])
        s2, s1 = shifted(u, cols)
        w = fw_ref[:, cols]
        halves.append(s2 * w[0:1] + s1 * w[1:2] + u * w[2:3] + fb_ref[:, cols])
    act = (_silu(halves[0]) * halves[1]).astype(BF16)
    return _dot(act, wd_ref[c * FF_CHUNK:(c + 1) * FF_CHUNK, :])


def _ffn_prompt_body(x_ref, g_ref, wu_ref, fw_ref, fb_ref, wd_ref, out_ref, tail_ref, carry_scr):
    i = pl.program_id(1)

    @pl.when(i == 0)
    def _():
        carry_scr[...] = jnp.zeros_like(carry_scr)

    x = x_ref[0]
    tm = x.shape[0]
    h = _rms(x, g_ref[...]).astype(BF16)

    def shifted(u, cols):
        head = carry_scr[:, cols]
        carry_scr[:, cols] = u[tm - SUBLANES:tm]
        return _shift_rows(u, 2, head), _shift_rows(u, 1, head)

    acc = x
    for c in range(D_FF // FF_CHUNK):
        acc = acc + _ffn_chunk(h, c, wu_ref, wd_ref, fw_ref, fb_ref, shifted)
    out_ref[0] = acc

    @pl.when(i == pl.num_programs(1) - 1)
    def _():
        tail_ref[0] = carry_scr[...]


def _ffn_prompt(x, g, wu, fw, fb, wd, tm):
    bsz, seq, _ = x.shape
    return pl.pallas_call(
        _ffn_prompt_body,
        grid=(bsz, seq // tm),
        in_specs=[pl.BlockSpec((1, tm, D_MODEL), lambda b, i: (b, i, 0))]
                 + [_const_spec(a.shape) for a in (g, wu, fw, fb, wd)],
        out_specs=[pl.BlockSpec((1, tm, D_MODEL), lambda b, i: (b, i, 0)),
                   pl.BlockSpec((1, SUBLANES, 2 * D_FF), lambda b, i: (b, 0, 0))],
        out_shape=[jax.ShapeDtypeStruct((bsz, seq, D_MODEL), F32),
                   jax.ShapeDtypeStruct((bsz, SUBLANES, 2 * D_FF), F32)],
        scratch_shapes=[pltpu.VMEM((SUBLANES, 2 * D_FF), F32)],
        compiler_params=_params(("parallel", "arbitrary")),
        name="ffn_prompt",
    )(x, g, wu, fw, fb, wd)


def _ffn_sample_body(steps, x_ref, g_ref, wu_ref, fw_ref, fb_ref, wd_ref, p1_ref, p2_ref, out_ref, u_ref):
    x = x_ref[...]
    rows = x.shape[0]
    h = _rms(x, g_ref[...]).astype(BF16)
    t = lax.broadcasted_iota(jnp.int32, (rows, FF_CHUNK), 0) % steps

    def shifted(u, cols):
        u_ref[:, cols] = u
        s1 = jnp.where(t >= 1, pltpu.roll(u, 1, axis=0), p1_ref[:, cols])
        s2 = jnp.where(t >= 2, pltpu.roll(u, 2, axis=0), p2_ref[:, cols])
        return s2, s1

    acc = x
    for c in range(D_FF // FF_CHUNK):
        acc = acc + _ffn_chunk(h, c, wu_ref, wd_ref, fw_ref, fb_ref, shifted)
    out_ref[...] = acc


def _ffn_sample(x, steps, g, wu, fw, fb, wd, p1, p2):
    m = x.shape[0]
    tr = min(SAMPLE_ROWS, m)
    assert m % tr == 0 and tr % steps == 0
    row = lambda w: pl.BlockSpec((tr, w), lambda i: (i, 0))
    return pl.pallas_call(
        functools.partial(_ffn_sample_body, steps),
        grid=(m // tr,),
        in_specs=[row(D_MODEL)] + [_const_spec(a.shape) for a in (g, wu, fw, fb, wd)] + [row(2 * D_FF)] * 2,
        out_specs=[row(D_MODEL), row(2 * D_FF)],
        out_shape=[jax.ShapeDtypeStruct((m, D_MODEL), F32), jax.ShapeDtypeStruct((m, 2 * D_FF), F32)],
        compiler_params=_params(("parallel",)),
        name="ffn_sample",
    )(x, g, wu, fw, fb, wd, p1, p2)


def _attn_sample_body(steps, dils, qkv_ref, kv0_ref, kv1_ref, kv2_ref, cb0_ref, cb1_ref, cb2_ref, nb_ref,
                      gq_ref, gk_ref, bd_ref, hm_ref, o_ref, lse_ref, kn_ref):
    bb = qkv_ref.shape[0]
    kv_refs = (kv0_ref, kv1_ref, kv2_ref)
    cb_refs = (cb0_ref, cb1_ref, cb2_ref)
    bd = bd_ref[...]
    hm = hm_ref[...]
    kv_w = 2 * GROUP_W

    def one(b, carry):
        qkv = qkv_ref[b]
        qn = _head_rms(qkv[:, :ATT_WIDTH], gq_ref[...], bd) * (HEAD_DIM ** -0.5)
        kn = _head_rms(qkv[:, ATT_WIDTH:2 * ATT_WIDTH], gk_ref[...], bd)
        kn_ref[b] = kn
        vn = qkv[:, 2 * ATT_WIDTH:]
        o_rows, l_rows = [], []
        for t in range(steps):
            o_g, l_g = [], []
            for g, dil in enumerate(dils):
                lanes = slice(g * GROUP_W, (g + 1) * GROUP_W)
                res = t % dil
                qbd = (qn[t:t + 1, lanes] * hm)
                kc = kv_refs[g][b, :, res * kv_w:res * kv_w + GROUP_W].astype(BF16)
                vc = kv_refs[g][b, :, res * kv_w + GROUP_W:(res + 1) * kv_w].astype(BF16)
                s_c = _dot_nt(qbd.astype(BF16), kc) + cb_refs[g][t]
                mx = jnp.max(s_c, axis=-1, keepdims=True)
                s_n = []
                for t2 in range(t + 1):
                    if (t - t2) % dil == 0 and (t - t2) // dil <= BAND:
                        sn = jnp.sum(qbd * kn[t2:t2 + 1, lanes], axis=-1, keepdims=True) + nb_ref[g, t - t2]
                        s_n.append((t2, sn))
                        mx = jnp.maximum(mx, sn)
                e_c = jnp.exp(s_c - mx)
                den = jnp.sum(e_c, axis=-1, keepdims=True)
                acc = _dot(e_c.astype(BF16), vc)
                for t2, sn in s_n:
                    e_n = jnp.exp(sn - mx)
                    den = den + e_n
                    acc = acc + e_n * vn[t2:t2 + 1, lanes]
                o_g.append(jnp.sum(acc / den * hm, axis=0, keepdims=True))
                l_g.append(jnp.sum((mx + jnp.log(den)) * hm, axis=0, keepdims=True))
            o_rows.append(jnp.concatenate(o_g, axis=1))
            l_rows.append(jnp.concatenate(l_g, axis=1))
        o_ref[b] = jnp.concatenate(o_rows, axis=0)
        lse_ref[b] = jnp.concatenate(l_rows, axis=0)
        return carry

    lax.fori_loop(0, bb, one, 0)


def _sample_cache_bias(slopes, dil, steps):
    t = jnp.arange(steps)[:, None, None]
    m = jnp.arange(BAND)[None, None, :]
    tdiv = t // dil
    j = BAND - m + tdiv
    bias = -slopes.astype(F32)[None, :, None] * (j * dil).astype(F32)
    return jnp.where((m >= tdiv) & (j >= 1), bias, NEG_INF)


def _attn_sample(qkv3, caches, cbias, nbias, gq, gk, bd, hm, bb):
    bsz, steps, _ = qkv3.shape
    dils = tuple(d for _, d in ATT_GROUPS)
    kv_w = 2 * GROUP_W
    kv_specs = []
    for cache, dil in zip(caches, dils):
        nres = min(dil, steps)
        assert dil == 1 or steps <= dil
        kv_specs.append(pl.BlockSpec((bb, BAND, nres * kv_w), lambda i: (i, 0, 0)))
    blk = lambda w: pl.BlockSpec((bb, steps, w), lambda i: (i, 0, 0))
    consts = (*cbias, nbias, gq, gk, bd, hm)
    return pl.pallas_call(
        functools.partial(_attn_sample_body, steps, dils),
        grid=(bsz // bb,),
        in_specs=[blk(QKV_W)] + kv_specs + [_const_spec(a.shape) for a in consts],
        out_specs=[blk(ATT_WIDTH)] * 3,
        out_shape=[jax.ShapeDtypeStruct((bsz, steps, ATT_WIDTH), F32)] * 3,
        compiler_params=_params(("parallel",)),
        name="attn_sample",
    )(qkv3, *caches, *consts)


def _ssd_sample_pre_body(steps, xbc_ref, z_ref, dt_ref, p1_ref, p2_ref, p3_ref, cw_ref, cb_ref, al_ref, dsk_ref,
                         exp_ref, gexp_ref, bm_ref, cm_ref, xde_ref, pre_ref, ea_ref, zg_ref, cd_ref):
    raw = xbc_ref[...]
    rows = raw.shape[0]
    tw = lax.broadcasted_iota(jnp.int32, (rows, XBC_WIDTH), 0) % steps
    cw = cw_ref[...]
    acc = jnp.where(tw >= 3, pltpu.roll(raw, 3, axis=0), p3_ref[...]) * cw[0:1]
    acc = acc + jnp.where(tw >= 2, pltpu.roll(raw, 2, axis=0), p2_ref[...]) * cw[1:2]
    acc = acc + jnp.where(tw >= 1, pltpu.roll(raw, 1, axis=0), p1_ref[...]) * cw[2:3]
    acc = acc + raw * cw[3:4]
    xc = _silu(acc + cb_ref[...])
    xs = xc[:, :SSM_WIDTH]
    bm = xc[:, SSM_WIDTH:SSM_WIDTH + BC_W]
    cm = xc[:, SSM_WIDTH + BC_W:]
    bm_ref[...] = bm
    cm_ref[...] = cm

    dt = dt_ref[...][:, :SSM_HEADS]
    da = dt * (-jnp.exp(al_ref[...]))
    th = lax.broadcasted_iota(jnp.int32, (rows, SSM_HEADS), 0) % steps
    acum = da
    for d in range(1, steps):
        acum = acum + jnp.where(th >= d, pltpu.roll(da, d, axis=0), 0.0)
    tail = jnp.zeros_like(da)
    for d in range(1, steps):
        tail = tail + jnp.where(th + d < steps, pltpu.roll(da, rows - d, axis=0), 0.0)
    expand = exp_ref[...]
    dt_x = _dot_sel(dt, expand)
    acum_x = _dot_sel(acum, expand)
    ea_ref[...] = jnp.exp(acum_x)
    xdt = xs * dt_x
    xde_ref[...] = xdt * _dot_sel(jnp.exp(tail), expand)
    cd_ref[...] = jnp.exp(acum)

    ts = lax.broadcasted_iota(jnp.int32, (rows, SSM_WIDTH), 0) % steps
    gexp = gexp_ref[...]
    y = dsk_ref[...] * xs
    for d in range(steps):
        if d == 0:
            bm_d, xdt_d, ac_d = bm, xdt, acum_x
        else:
            bm_d = pltpu.roll(bm, d, axis=0)
            xdt_d = pltpu.roll(xdt, d, axis=0)
            ac_d = pltpu.roll(acum_x, d, axis=0)
        cb_x = _dot_sel(cm * bm_d, gexp, pieces=2)
        term = cb_x * jnp.exp(acum_x - ac_d) * xdt_d
        y = y + jnp.where(ts >= d, term, 0.0)
    pre_ref[...] = y
    zg_ref[...] = _silu(z_ref[...])


def _ssd_sample_pre(xbc, z, dt, p1, p2, p3, cw, cb, al, dsk, expand, gexp, steps):
    m = xbc.shape[0]
    tr = min(SAMPLE_ROWS, m)
    assert m % tr == 0 and tr % steps == 0
    row = lambda w: pl.BlockSpec((tr, w), lambda i: (i, 0))
    consts = (cw, cb, al, dsk, expand, gexp)
    widths = (BC_W, BC_W, SSM_WIDTH, SSM_WIDTH, SSM_WIDTH, SSM_WIDTH, SSM_HEADS)
    return pl.pallas_call(
        functools.partial(_ssd_sample_pre_body, steps),
        grid=(m // tr,),
        in_specs=[row(XBC_WIDTH), row(SSM_WIDTH), row(DT_PAD)] + [row(XBC_WIDTH)] * 3
                 + [_const_spec(a.shape) for a in consts],
        out_specs=[row(w) for w in widths],
        out_shape=[jax.ShapeDtypeStruct((m, w), F32) for w in widths],
        compiler_params=_params(("parallel",)),
        name="ssd_sample_pre",
    )(xbc, z, dt, p1, p2, p3, *consts)


def _ssd_sample_state_body(steps, bm_ref, cm_ref, xde_ref, pre_ref, ea_ref, zg_ref, cd_ref, gn_ref, st_ref,
                           y_ref, ns_ref):
    bb = st_ref.shape[0]
    blk = pl.program_id(0)

    def one(b, carry):
        bm = bm_ref[b].astype(BF16)
        cm = cm_ref[b].astype(BF16)
        xde = xde_ref[b].astype(BF16)
        inter = []
        for g in range(SSM_GROUPS):
            rows = slice(g * SSM_GROUP_W, (g + 1) * SSM_GROUP_W)
            lanes = slice(g * SSM_STATE, (g + 1) * SSM_STATE)
            st_g = st_ref[b, rows, :]
            inter.append(_dot_nt(cm[:, lanes], st_g.astype(BF16)))
            new = _dot_tn(xde[:, rows], bm[:, lanes])
            for hh in range(SSM_HPG):
                h = g * SSM_HPG + hh
                cd = cd_ref[blk * bb + b, h]
                r = slice(h * SSM_HEAD_DIM, (h + 1) * SSM_HEAD_DIM)
                rl = slice(hh * SSM_HEAD_DIM, (hh + 1) * SSM_HEAD_DIM)
                ns_ref[b, r, :] = st_g[rl, :] * cd + new[rl, :]
        y = pre_ref[b] + ea_ref[b] * jnp.concatenate(inter, axis=1)
        y_ref[b] = _rms(y * zg_ref[b], gn_ref[...])
        return carry

    lax.fori_loop(0, bb, one, 0)


def _ssd_sample_state(bm3, cm3, xde3, pre3, ea3, zg3, cd, gn, state, bb):
    bsz, steps, _ = bm3.shape
    blk = lambda w: pl.BlockSpec((bb, steps, w), lambda i: (i, 0, 0))
    st_spec = pl.BlockSpec((bb, SSM_WIDTH, SSM_STATE), lambda i: (i, 0, 0))
    return pl.pallas_call(
        functools.partial(_ssd_sample_state_body, steps),
        grid=(bsz // bb,),
        in_specs=[blk(BC_W), blk(BC_W), blk(SSM_WIDTH), blk(SSM_WIDTH), blk(SSM_WIDTH), blk(SSM_WIDTH),
                  pl.BlockSpec(memory_space=pltpu.SMEM), _const_spec(gn.shape), st_spec],
        out_specs=[blk(SSM_WIDTH), st_spec],
        out_shape=[jax.ShapeDtypeStruct((bsz, steps, SSM_WIDTH), F32),
                   jax.ShapeDtypeStruct((bsz, SSM_WIDTH, SSM_STATE), F32)],
        compiler_params=_params(("parallel",)),
        name="ssd_sample_state",
    )(bm3, cm3, xde3, pre3, ea3, zg3, cd, gn, state)


def _alibi_slopes():
    h = jnp.arange(1, N_ATT_HEADS + 1, dtype=F32)
    return jnp.exp2(-8.0 * h / N_ATT_HEADS).reshape(len(ATT_GROUPS), HEADS_PER_GROUP)


def _prev_rows(prev, d, steps):
    bsz, km1, c = prev.shape
    assert d <= km1 and d <= steps
    p = jnp.concatenate([prev[:, km1 - d:], jnp.zeros((bsz, steps - d, c), prev.dtype)], axis=1)
    return p.reshape(bsz * steps, c)


def _layer_weights(lw):
    (norm_mix, w_in, q_norm, k_norm, conv_w, conv_b, dt_bias, a_log, d_skip, ssm_norm, w_out,
     norm_ffn, w_up, ffn_conv_w, ffn_conv_b, w_down) = lw
    o_z = QKV_W
    o_x = o_z + SSM_WIDTH
    o_d = o_x + XBC_WIDTH
    w = {}
    w["norm_mix"] = norm_mix.reshape(1, D_MODEL)
    w["wq"] = w_in[:, :o_z].astype(BF16)
    w["wz"] = w_in[:, o_z:o_x].astype(BF16)
    w["wx"] = w_in[:, o_x:o_d].astype(BF16)
    w["wd"] = jnp.pad(w_in[:, o_d:], ((0, 0), (0, DT_PAD - SSM_HEADS))).astype(BF16)
    w["dtb"] = jnp.pad(dt_bias, (0, DT_PAD - SSM_HEADS)).reshape(1, DT_PAD)
    w["gq_g"] = jnp.tile(q_norm, HEADS_PER_GROUP).reshape(1, GROUP_W)
    w["gk_g"] = jnp.tile(k_norm, HEADS_PER_GROUP).reshape(1, GROUP_W)
    w["gq_a"] = jnp.tile(q_norm, N_ATT_HEADS).reshape(1, ATT_WIDTH)
    w["gk_a"] = jnp.tile(k_norm, N_ATT_HEADS).reshape(1, ATT_WIDTH)
    w["cw"] = conv_w
    w["cb"] = conv_b.reshape(1, XBC_WIDTH)
    w["alr"] = a_log.reshape(1, SSM_HEADS)
    w["alc"] = a_log.reshape(SSM_HEADS, 1)
    w["dsk"] = jnp.repeat(d_skip, SSM_HEAD_DIM).reshape(1, SSM_WIDTH)
    w["gn"] = ssm_norm.reshape(1, SSM_WIDTH)
    w["wa"] = w_out[:ATT_WIDTH].astype(BF16)
    w["wy"] = w_out[ATT_WIDTH:].astype(BF16)
    w["norm_ffn"] = norm_ffn.reshape(1, D_MODEL)
    w["wu"] = w_up.astype(BF16)
    w["fw"] = ffn_conv_w
    w["fb"] = ffn_conv_b.reshape(1, 2 * D_FF)
    w["wdn"] = w_down.astype(BF16)
    return w


def _constants(steps):
    c = {}
    slopes = _alibi_slopes()
    c["slopes"] = slopes
    c["bd_g"] = _head_block_diag(GROUP_W)
    c["bd_a"] = _head_block_diag(ATT_WIDTH)
    i = jnp.arange(SSM_CHUNK)
    c["tri"] = (i[None, :] <= i[:, None]).astype(BF16)
    lane_head = jnp.arange(SSM_WIDTH) // SSM_HEAD_DIM
    c["expand"] = (jnp.arange(SSM_HEADS)[:, None] == lane_head[None, :]).astype(BF16)
    bc_group = jnp.arange(BC_W) // SSM_STATE
    c["gexp"] = (bc_group[:, None] == (lane_head // SSM_HPG)[None, :]).astype(BF16)
    c["hm"] = (jnp.arange(HEADS_PER_GROUP)[:, None] == (jnp.arange(GROUP_W) // HEAD_DIM)[None, :]).astype(F32)
    c["pbias"] = [_prompt_bias(slopes[g], dil) for g, (_, dil) in enumerate(ATT_GROUPS)]
    c["cbias"] = [_sample_cache_bias(slopes[g], dil, steps) for g, (_, dil) in enumerate(ATT_GROUPS)]
    dist = jnp.arange(steps, dtype=F32)
    c["nbias"] = (-slopes[:, None, :] * dist[None, :, None])[..., None]
    return c


def _prompt_layer(x, w, c, tm):
    bsz, seq, _ = x.shape
    m = bsz * seq
    qkv, z, xbc, dt = _inproj(x.reshape(m, D_MODEL), w["norm_mix"], w["wq"], w["wz"], w["wx"], w["wd"], w["dtb"], tm)
    qkv3 = qkv.reshape(bsz, seq, QKV_W)
    os_, ls_, new_kv = [], [], []
    for g, (win, dil) in enumerate(ATT_GROUPS):
        o, lse, kn = _attn_prompt(qkv3, g, dil, c["pbias"][g], w["gq_g"], w["gk_g"], c["bd_g"])
        os_.append(o.reshape(m, GROUP_W))
        ls_.append(lse.reshape(m, GROUP_W))
        keep = min(win, seq)
        v_g = qkv3[:, seq - keep:, 2 * ATT_WIDTH + g * GROUP_W:2 * ATT_WIDTH + (g + 1) * GROUP_W]
        kv = jnp.stack([kn[:, seq - keep:], v_g], axis=2)
        new_kv.append(kv.reshape(bsz, keep, 2, HEADS_PER_GROUP, HEAD_DIM))
    xbc3 = xbc.reshape(bsz, seq, XBC_WIDTH)
    dt3 = dt.reshape(bsz, seq, DT_PAD)
    dtt = jnp.swapaxes(dt3[:, :, :SSM_HEADS], 1, 2)
    y, h_last = _ssd_prompt(xbc3, z.reshape(bsz, seq, SSM_WIDTH), dt3, dtt, w["cw"], w["cb"], w["alr"], w["alc"],
                            w["dsk"], w["gn"], c["tri"], c["expand"])
    x1 = _outproj(os_, ls_, y.reshape(m, SSM_WIDTH), x.reshape(m, D_MODEL), w["wa"], w["wy"], tm)
    x2, tail = _ffn_prompt(x1.reshape(bsz, seq, D_MODEL), w["norm_ffn"], w["wu"], w["fw"], w["fb"], w["wdn"], tm)
    h_last = h_last.reshape(bsz, SSM_HEADS, SSM_HEAD_DIM, SSM_STATE)
    conv_new = xbc3[:, seq - (SSM_CONV - 1):]
    ffn_new = tail[:, SUBLANES - (FFN_CONV - 1):]
    return x2, new_kv, h_last, conv_new, ffn_new


def _sample_layer(x, w, c, caches, state, conv_prev, ffn_prev):
    bsz, steps, _ = x.shape
    m = bsz * steps
    assert steps >= SSM_CONV - 1
    xf = x.reshape(m, D_MODEL)
    qkv, z, xbc, dt = _inproj(xf, w["norm_mix"], w["wq"], w["wz"], w["wx"], w["wd"], w["dtb"], m)
    qkv3 = qkv.reshape(bsz, steps, QKV_W)
    views = []
    for cache, (win, dil) in zip(caches, ATT_GROUPS):
        assert cache.shape[1] == BAND * dil
        views.append(cache.reshape(bsz, BAND, dil * 2 * GROUP_W))
    o, lse, kn = _attn_sample(qkv3, views, c["cbias"], c["nbias"], w["gq_a"], w["gk_a"], c["bd_a"], c["hm"], 8)
    new_kv = []
    for g in range(len(ATT_GROUPS)):
        lanes = slice(g * GROUP_W, (g + 1) * GROUP_W)
        v_g = qkv3[:, :, 2 * ATT_WIDTH + g * GROUP_W:2 * ATT_WIDTH + (g + 1) * GROUP_W]
        kv = jnp.stack([kn[:, :, lanes], v_g], axis=2)
        new_kv.append(kv.reshape(bsz, steps, 2, HEADS_PER_GROUP, HEAD_DIM))
    os_ = [o.reshape(m, ATT_WIDTH)[:, g * GROUP_W:(g + 1) * GROUP_W] for g in range(len(ATT_GROUPS))]
    ls_ = [lse.reshape(m, ATT_WIDTH)[:, g * GROUP_W:(g + 1) * GROUP_W] for g in range(len(ATT_GROUPS))]

    ps = [_prev_rows(conv_prev, d, steps) for d in (1, 2, 3)]
    bm, cm, xde, pre, ea, zg, cd = _ssd_sample_pre(xbc, z, dt, *ps, w["cw"], w["cb"], w["alr"], w["dsk"],
                                                   c["expand"], c["gexp"], steps)
    r3 = lambda a: a.reshape(bsz, steps, a.shape[-1])
    cd_last = r3(cd)[:, steps - 1]
    y, new_state = _ssd_sample_state(r3(bm), r3(cm), r3(xde), r3(pre), r3(ea), r3(zg), cd_last, w["gn"],
                                     state.reshape(bsz, SSM_WIDTH, SSM_STATE), 8)
    x1 = _outproj(os_, ls_, y.reshape(m, SSM_WIDTH), xf, w["wa"], w["wy"], m)
    fp = [_prev_rows(ffn_prev, d, steps) for d in (1, 2)]
    x2, u_raw = _ffn_sample(x1, steps, w["norm_ffn"], w["wu"], w["fw"], w["fb"], w["wdn"], *fp)
    xbc3 = xbc.reshape(bsz, steps, XBC_WIDTH)
    conv_new = jnp.concatenate([conv_prev, xbc3], axis=1)[:, steps:]
    ffn_new = jnp.concatenate([ffn_prev, u_raw.reshape(bsz, steps, 2 * D_FF)], axis=1)[:, steps:]
    new_state = new_state.reshape(bsz, SSM_HEADS, SSM_HEAD_DIM, SSM_STATE)
    return x2.reshape(bsz, steps, D_MODEL), new_kv, new_state, conv_new, ffn_new


def kernel(x_prompt, x_sample, cache_kv0, cache_kv1, cache_kv2, state_ssm, state_conv, state_ffn_conv, norm_mix, w_in, q_norm, k_norm, conv_w, conv_b, dt_bias, a_log, d_skip, ssm_norm, w_out, norm_ffn, w_up, ffn_conv_w, ffn_conv_b, w_down):
    stacked = (norm_mix, w_in, q_norm, k_norm, conv_w, conv_b, dt_bias, a_log, d_skip, ssm_norm, w_out,
               norm_ffn, w_up, ffn_conv_w, ffn_conv_b, w_down)
    depth = w_in.shape[0]
    steps = x_sample.shape[1]
    c = _constants(steps)
    tm = 512
    assert x_prompt.shape[1] % tm == 0
    y_prompt, y_sample = x_prompt, x_sample
    outs_p = [[] for _ in range(6)]
    outs_s = [[] for _ in range(6)]
    for layer in range(depth):
        w = _layer_weights(tuple(a[layer] for a in stacked))
        y_prompt, kv, h, cv, f = _prompt_layer(y_prompt, w, c, tm)
        for lst, val in zip(outs_p, (*kv, h, cv, f)):
            lst.append(val)
        y_sample, kv, h, cv, f = _sample_layer(
            y_sample, w, c, (cache_kv0[layer], cache_kv1[layer], cache_kv2[layer]),
            state_ssm[layer], state_conv[layer], state_ffn_conv[layer])
        for lst, val in zip(outs_s, (*kv, h, cv, f)):
            lst.append(val)
    return (y_prompt, y_sample, *[jnp.stack(l) for l in outs_p], *[jnp.stack(l) for l in outs_s])
```

```python
import functools
import math

import jax
import jax.numpy as jnp
from jax import lax
from jax.experimental import pallas as pl
from jax.experimental.pallas import tpu as pltpu

F32 = jnp.float32
BF16 = jnp.bfloat16

D_MODEL = 1024
HEAD_DIM = 64
ATT_GROUPS = ((128, 1), (512, 4), (2048, 16))
BAND = 128
HEADS_PER_GROUP = 4
GROUP_W = HEADS_PER_GROUP * HEAD_DIM
N_ATT_HEADS = HEADS_PER_GROUP * len(ATT_GROUPS)
ATT_WIDTH = N_ATT_HEADS * HEAD_DIM
QKV_W = 3 * ATT_WIDTH
PAIR_W = 2 * HEAD_DIM
PAIRS = ATT_WIDTH // PAIR_W
QKV_SLABS = QKV_W // PAIR_W
SSM_HEAD_DIM = 64
SSM_WIDTH = 1024
SSM_HEADS = SSM_WIDTH // SSM_HEAD_DIM
SSM_STATE = 128
SSM_GROUPS = 2
SSM_HPG = SSM_HEADS // SSM_GROUPS
SSM_GROUP_W = SSM_HPG * SSM_HEAD_DIM
SSM_CONV = 4
SSM_CHUNK = 128
BC_W = SSM_GROUPS * SSM_STATE
XBC_WIDTH = SSM_WIDTH + 2 * BC_W
D_FF = 2816
FFN_CONV = 3
FF_CHUNK = 256
EPS = 1e-6
DT_PAD = 128
SUBLANES = 8
SAMPLE_ROWS = 128
PROMPT_ATT_ROWS = 512
VMEM_LIMIT = 56 * 1024 * 1024
NEG_INF = float("-inf")


def _dot(a, b):
    return jnp.dot(a, b, preferred_element_type=F32)


def _dot_nt(a, b):
    return lax.dot_general(a, b, (((1,), (1,)), ((), ())), preferred_element_type=F32)


def _dot_tn(a, b):
    return lax.dot_general(a, b, (((0,), (0,)), ((), ())), preferred_element_type=F32)


def _split(x, pieces):
    out = []
    r = x
    for _ in range(pieces):
        p = r.astype(BF16)
        out.append(p)
        r = r - p.astype(F32)
    return out


def _dot_sel(x, sel, pieces=3):
    acc = None
    for p in _split(x, pieces):
        t = _dot(p, sel)
        acc = t if acc is None else acc + t
    return acc


def _sel_dot(sel, x, pieces=3):
    acc = None
    for p in _split(x, pieces):
        t = _dot(sel, p)
        acc = t if acc is None else acc + t
    return acc


def _sel_dot_nt(x, sel, pieces=3):
    acc = None
    for p in _split(x, pieces):
        t = _dot_nt(p, sel)
        acc = t if acc is None else acc + t
    return acc


def _silu(x):
    return x * jax.nn.sigmoid(x)


def _softplus(x):
    return jnp.maximum(x, 0.0) + jnp.log1p(jnp.exp(-jnp.abs(x)))


def _rms(x, g):
    ms = jnp.mean(x * x, axis=-1, keepdims=True)
    return x * lax.rsqrt(ms + EPS) * g


def _const_spec(shape):
    nd = len(shape)
    return pl.BlockSpec(shape, lambda *_: (0,) * nd, pipeline_mode=pl.Buffered(1))


def _params(sem):
    return pltpu.CompilerParams(dimension_semantics=sem, vmem_limit_bytes=VMEM_LIMIT)


def _head_block_diag(width):
    h = jnp.arange(width) // HEAD_DIM
    return jnp.where(h[:, None] == h[None, :], 1.0 / HEAD_DIM, 0.0).astype(BF16)


def _head_rms(t, g, bd):
    ms = _dot_sel(t * t, bd, pieces=2)
    return t * lax.rsqrt(ms + EPS) * g


def _shift_rows(u, d, head):
    rolled = pltpu.roll(u, d, axis=0)
    hrolled = pltpu.roll(head, d, axis=0)
    rows = lax.broadcasted_iota(jnp.int32, head.shape, 0)
    first = jnp.where(rows < d, hrolled, rolled[0:SUBLANES])
    return jnp.concatenate([first, rolled[SUBLANES:]], axis=0)


def _inproj_body(x_ref, g_ref, wq_ref, wz_ref, wx_ref, wd_ref, dtb_ref,
                 qkv_ref, z_ref, xbc_ref, dt_ref):
    h = _rms(x_ref[...], g_ref[...]).astype(BF16)
    qkv = _dot(h, wq_ref[...])
    for s in range(QKV_SLABS):
        qkv_ref[s] = qkv[:, s * PAIR_W:(s + 1) * PAIR_W]
    z_ref[...] = _dot(h, wz_ref[...])
    xbc_ref[...] = _dot(h, wx_ref[...])
    dt_ref[...] = _softplus(_dot(h, wd_ref[...]) + dtb_ref[...])


def _inproj(x, g, wq, wz, wx, wd, dtb, tm):
    m = x.shape[0]
    row = lambda w: pl.BlockSpec((tm, w), lambda i: (i, 0))
    widths = (SSM_WIDTH, XBC_WIDTH, DT_PAD)
    return pl.pallas_call(
        _inproj_body,
        grid=(m // tm,),
        in_specs=[row(D_MODEL), _const_spec(g.shape), _const_spec(wq.shape), _const_spec(wz.shape),
                  _const_spec(wx.shape), _const_spec(wd.shape), _const_spec(dtb.shape)],
        out_specs=[pl.BlockSpec((QKV_SLABS, tm, PAIR_W), lambda i: (0, i, 0))] + [row(w) for w in widths],
        out_shape=[jax.ShapeDtypeStruct((QKV_SLABS, m, PAIR_W), F32)]
                  + [jax.ShapeDtypeStruct((m, w), F32) for w in widths],
        compiler_params=_params(("parallel",)),
        name="inproj",
    )(x, g, wq, wz, wx, wd, dtb)


def _attn_prompt_body(dil, nbk, q0_ref, q1_ref, k0_ref, k1_ref, v0_ref, v1_ref, bias_ref, gq_ref, gk_ref, bd_ref,
                      o_ref, lse_ref, kn_ref, kv_scr):
    n = pl.program_id(1)
    rows_per = nbk * BAND

    @pl.when(n == 0)
    def _():
        kv_scr[:, :, :, 0:BAND, :] = jnp.zeros((2, 2, dil, BAND, PAIR_W), BF16)

    variant = jnp.minimum(n, 1)
    bd = bd_ref[...]
    gq = gq_ref[...]
    gk = gk_ref[...]
    low = lax.broadcasted_iota(jnp.int32, (BAND, PAIR_W), 1) < HEAD_DIM

    def rows_of(start, count):
        return pl.ds(start, count, stride=dil) if dil > 1 else pl.ds(start, count)

    def residue(r):
        for p, (q_ref, k_ref, v_ref) in enumerate(((q0_ref, k0_ref, v0_ref), (q1_ref, k1_ref, v1_ref))):
            rows = rows_of(r, rows_per)
            qn = _head_rms(q_ref[0, 0, rows, :], gq, bd) * (HEAD_DIM ** -0.5)
            kn = _head_rms(k_ref[0, 0, rows, :], gk, bd)
            kn_ref[p, 0, rows, :] = kn
            kv_scr[0, p, r, BAND:, :] = kn.astype(BF16)
            kv_scr[1, p, r, BAND:, :] = v_ref[0, 0, rows, :].astype(BF16)
            for j in range(nbk):
                qj = qn[j * BAND:(j + 1) * BAND]
                kwin = kv_scr[0, p, r, j * BAND:(j + 2) * BAND, :]
                vwin = kv_scr[1, p, r, j * BAND:(j + 2) * BAND, :]
                outs = []
                for hh in range(2):
                    keep = low if hh == 0 else jnp.logical_not(low)
                    qm = jnp.where(keep, qj, 0.0).astype(BF16)
                    s = _dot_nt(qm, kwin) + bias_ref[variant if j == 0 else 1, 2 * p + hh]
                    mx = jnp.max(s, axis=-1, keepdims=True)
                    e = jnp.exp(s - mx)
                    den = jnp.sum(e, axis=-1, keepdims=True)
                    outs.append((_dot(e.astype(BF16), vwin) / den, mx + jnp.log(den)))
                out_rows = rows_of(r + j * BAND * dil, BAND)
                o_ref[p, 0, out_rows, :] = jnp.where(low, outs[0][0], outs[1][0])
                lse_ref[p, 0, out_rows, :] = jnp.where(low, outs[0][1], outs[1][1])
            for kv in range(2):
                kv_scr[kv, p, r, 0:BAND, :] = kv_scr[kv, p, r, rows_per:rows_per + BAND, :]

    if dil <= 4:
        for r in range(dil):
            residue(r)
    else:
        def step(r, carry):
            residue(r)
            return carry
        lax.fori_loop(0, dil, step, 0)


def _prompt_bias(slopes, dil):
    i = jnp.arange(BAND)[:, None]
    j = jnp.arange(2 * BAND)[None, :]
    step = i + BAND - j
    valid = (step >= 0) & (step <= BAND)
    bias = -slopes.astype(F32)[:, None, None] * (step * dil).astype(F32)
    full = jnp.where(valid[None], bias, NEG_INF)
    first = jnp.where((valid & (j >= BAND))[None], bias, NEG_INF)
    return jnp.stack([first, full])


def _attn_prompt(qkv_t, g, dil, nbk, bias, gq, gk, bd):
    _, bsz, seq, _ = qkv_t.shape
    blk = nbk * BAND * dil
    assert seq % blk == 0
    slab = lambda s: pl.BlockSpec((1, 1, blk, PAIR_W), lambda b, n: (s, b, n, 0))
    ospec = pl.BlockSpec((2, 1, blk, PAIR_W), lambda b, n: (0, b, n, 0))
    first = [kind * PAIRS + 2 * g for kind in range(3)]
    return pl.pallas_call(
        functools.partial(_attn_prompt_body, dil, nbk),
        grid=(bsz, seq // blk),
        in_specs=[slab(first[0]), slab(first[0] + 1), slab(first[1]), slab(first[1] + 1),
                  slab(first[2]), slab(first[2] + 1),
                  _const_spec(bias.shape), _const_spec(gq.shape), _const_spec(gk.shape), _const_spec(bd.shape)],
        out_specs=[ospec, ospec, ospec],
        out_shape=[jax.ShapeDtypeStruct((2, bsz, seq, PAIR_W), F32)] * 3,
        scratch_shapes=[pltpu.VMEM((2, 2, dil, (nbk + 1) * BAND, PAIR_W), BF16)],
        compiler_params=_params(("parallel", "arbitrary")),
        name=f"attn_prompt_g{g}",
    )(*([qkv_t] * 6), bias, gq, gk, bd)


def _ssd_prompt_body(xbc_ref, z_ref, dt_ref, dtt_ref, cw_ref, cb_ref, alr_ref, alc_ref, dsk_ref, gn_ref,
                     tri_ref, exp_ref, y_ref, hl_ref, tail_scr, state_scr):
    c = pl.program_id(1)
    q = SSM_CHUNK

    @pl.when(c == 0)
    def _():
        tail_scr[...] = jnp.zeros_like(tail_scr)
        state_scr[...] = jnp.zeros_like(state_scr)

    raw = xbc_ref[0]
    tail = tail_scr[...]
    cw = cw_ref[...]
    acc = _shift_rows(raw, 3, tail) * cw[0:1]
    acc = acc + _shift_rows(raw, 2, tail) * cw[1:2]
    acc = acc + _shift_rows(raw, 1, tail) * cw[2:3]
    acc = acc + raw * cw[3:4]
    tail_scr[...] = raw[q - SUBLANES:q]
    xc = _silu(acc + cb_ref[...])
    xs = xc[:, :SSM_WIDTH]
    bm = xc[:, SSM_WIDTH:SSM_WIDTH + BC_W]
    cm = xc[:, SSM_WIDTH + BC_W:]

    tri = tri_ref[...]
    dt = dt_ref[0][:, :SSM_HEADS]
    acum = _sel_dot(tri, dt * (-jnp.exp(alr_ref[...])))
    acum_t = _sel_dot_nt(dtt_ref[0] * (-jnp.exp(alc_ref[...])), tri)
    last = acum[q - 1:q, :]
    expand = exp_ref[...]
    dt_x = _dot_sel(dt, expand)
    ea_x = _dot_sel(jnp.exp(acum), expand)
    de_x = _dot_sel(jnp.exp(last - acum), expand)
    xdt = xs * dt_x
    xde = (xdt * de_x).astype(BF16)
    xdt_b = xdt.astype(BF16)
    causal = lax.broadcasted_iota(jnp.int32, (q, q), 0) >= lax.broadcasted_iota(jnp.int32, (q, q), 1)
    low = lax.broadcasted_iota(jnp.int32, (q, 2 * SSM_HEAD_DIM), 1) < SSM_HEAD_DIM

    y_parts = []
    for g in range(SSM_GROUPS):
        bm_g = bm[:, g * SSM_STATE:(g + 1) * SSM_STATE].astype(BF16)
        cm_g = cm[:, g * SSM_STATE:(g + 1) * SSM_STATE].astype(BF16)
        cb = _dot_nt(cm_g, bm_g)
        rows = slice(g * SSM_GROUP_W, (g + 1) * SSM_GROUP_W)
        st_g = state_scr[rows, :]
        inter = _dot_nt(cm_g, st_g.astype(BF16))
        for hp in range(SSM_HPG // 2):
            pair = None
            for k in range(2):
                h = g * SSM_HPG + 2 * hp + k
                seg = acum[:, h:h + 1] - acum_t[h:h + 1, :]
                decay = jnp.exp(jnp.where(causal, seg, NEG_INF))
                gmat = (cb * decay).astype(BF16)
                lanes = slice((2 * hp) * SSM_HEAD_DIM + g * SSM_GROUP_W,
                              (2 * hp + 2) * SSM_HEAD_DIM + g * SSM_GROUP_W)
                x_pair = xdt_b[:, lanes]
                keep = low if k == 0 else jnp.logical_not(low)
                t = _dot(gmat, jnp.where(keep, x_pair, jnp.zeros_like(x_pair)))
                pair = t if pair is None else pair + t
            y_parts.append(pair + ea_x[:, lanes] * inter[:, lanes.start - g * SSM_GROUP_W:lanes.stop - g * SSM_GROUP_W])
        new = _dot_tn(xde[:, rows], bm_g)
        for hh in range(SSM_HPG):
            h = g * SSM_HPG + hh
            cd = jnp.exp(acum_t[h:h + 1, q - 1:q])
            r = slice(h * SSM_HEAD_DIM, (h + 1) * SSM_HEAD_DIM)
            rl = slice(hh * SSM_HEAD_DIM, (hh + 1) * SSM_HEAD_DIM)
            state_scr[r, :] = st_g[rl, :] * cd + new[rl, :]

    y = jnp.concatenate(y_parts, axis=1) + dsk_ref[...] * xs
    y = y * _silu(z_ref[0])
    y_ref[0] = _rms(y, gn_ref[...])

    @pl.when(c == pl.num_programs(1) - 1)
    def _():
        hl_ref[0] = state_scr[...]


def _ssd_prompt(xbc, z, dt, dtt, cw, cb, alr, alc, dsk, gn, tri, expand):
    bsz, seq, _ = xbc.shape
    assert seq % SSM_CHUNK == 0
    nc = seq // SSM_CHUNK
    blk = lambda w: pl.BlockSpec((1, SSM_CHUNK, w), lambda b, c: (b, c, 0))
    consts = (cw, cb, alr, alc, dsk, gn, tri, expand)
    return pl.pallas_call(
        _ssd_prompt_body,
        grid=(bsz, nc),
        in_specs=[blk(XBC_WIDTH), blk(SSM_WIDTH), blk(DT_PAD),
                  pl.BlockSpec((1, SSM_HEADS, SSM_CHUNK), lambda b, c: (b, 0, c))]
                 + [_const_spec(a.shape) for a in consts],
        out_specs=[blk(SSM_WIDTH), pl.BlockSpec((1, SSM_WIDTH, SSM_STATE), lambda b, c: (b, 0, 0))],
        out_shape=[jax.ShapeDtypeStruct((bsz, seq, SSM_WIDTH), F32),
                   jax.ShapeDtypeStruct((bsz, SSM_WIDTH, SSM_STATE), F32)],
        scratch_shapes=[pltpu.VMEM((SUBLANES, XBC_WIDTH), F32), pltpu.VMEM((SSM_WIDTH, SSM_STATE), F32)],
        compiler_params=_params(("parallel", "arbitrary")),
        name="ssd_prompt",
    )(xbc, z, dt, dtt, *consts)


def _outproj_body(slabs, *refs):
    n_arr = len(slabs)
    o_refs, l_refs = refs[:n_arr], refs[n_arr:2 * n_arr]
    y_ref, x_ref, wa_ref, wy_ref, out_ref = refs[2 * n_arr:]
    o_sl = [r[s] for r, cnt in zip(o_refs, slabs) for s in range(cnt)]
    l_sl = [r[s] for r, cnt in zip(l_refs, slabs) for s in range(cnt)]
    ngrp = len(ATT_GROUPS)
    att = [None] * PAIRS
    for p in range(PAIRS // ngrp):
        ls = [l_sl[2 * g + p] for g in range(ngrp)]
        mx = functools.reduce(jnp.maximum, ls)
        es = [jnp.exp(l - mx) for l in ls]
        inv = 1.0 / functools.reduce(jnp.add, es)
        for g in range(ngrp):
            att[2 * g + p] = (o_sl[2 * g + p] * (es[g] * inv)).astype(BF16)
    acc = x_ref[...] + _dot(y_ref[...].astype(BF16), wy_ref[...])
    out_ref[...] = acc + _dot(jnp.concatenate(att, axis=1), wa_ref[...])


def _outproj(os_, ls_, y, x, wa, wy, tm):
    m = x.shape[0]
    slabs = tuple(a.shape[0] for a in os_)
    assert sum(slabs) == PAIRS
    row = lambda w: pl.BlockSpec((tm, w), lambda i: (i, 0))
    slab_specs = [pl.BlockSpec((n, tm, PAIR_W), lambda i: (0, i, 0)) for n in slabs]
    return pl.pallas_call(
        functools.partial(_outproj_body, slabs),
        grid=(m // tm,),
        in_specs=slab_specs * 2 + [row(SSM_WIDTH), row(D_MODEL), _const_spec(wa.shape), _const_spec(wy.shape)],
        out_specs=row(D_MODEL),
        out_shape=jax.ShapeDtypeStruct((m, D_MODEL), F32),
        compiler_params=_params(("parallel",)),
        name="outproj",
    )(*os_, *ls_, y, x, wa, wy)


def _ffn_chunk(h, c, wu_ref, wd_ref, fw_ref, fb_ref, shifted):
    halves = []
    for base in (0, D_FF):
        cols = slice(base + c * FF_CHUNK, base + (c + 1) * FF_CHUNK)
        u = _dot(h, wu_ref[:, cols])
        s2, s1 = shifted(u, cols)
        w = fw_ref[:, cols]
        halves.append(s2 * w[0:1] + s1 * w[1:2] + u * w[2:3] + fb_ref[:, cols])
    act = (_silu(halves[0]) * halves[1]).astype(BF16)
    return _dot(act, wd_ref[c * FF_CHUNK:(c + 1) * FF_CHUNK, :])


def _ffn_prompt_body(x_ref, g_ref, wu_ref, fw_ref, fb_ref, wd_ref, out_ref, tail_ref, carry_scr):
    i = pl.program_id(1)

    @pl.when(i == 0)
    def _():
        carry_scr[...] = jnp.zeros_like(carry_scr)

    x = x_ref[0]
    tm = x.shape[0]
    h = _rms(x, g_ref[...]).astype(BF16)

    def shifted(u, cols):
        head = carry_scr[:, cols]
        carry_scr[:, cols] = u[tm - SUBLANES:tm]
        return _shift_rows(u, 2, head), _shift_rows(u, 1, head)

    acc = x
    for c in range(D_FF // FF_CHUNK):
        acc = acc + _ffn_chunk(h, c, wu_ref, wd_ref, fw_ref, fb_ref, shifted)
    out_ref[0] = acc

    @pl.when(i == pl.num_programs(1) - 1)
    def _():
        tail_ref[0] = carry_scr[...]


def _ffn_prompt(x, g, wu, fw, fb, wd, tm):
    bsz, seq, _ = x.shape
    return pl.pallas_call(
        _ffn_prompt_body,
        grid=(bsz, seq // tm),
        in_specs=[pl.BlockSpec((1, tm, D_MODEL), lambda b, i: (b, i, 0))]
                 + [_const_spec(a.shape) for a in (g, wu, fw, fb, wd)],
        out_specs=[pl.BlockSpec((1, tm, D_MODEL), lambda b, i: (b, i, 0)),
                   pl.BlockSpec((1, SUBLANES, 2 * D_FF), lambda b, i: (b, 0, 0))],
        out_shape=[jax.ShapeDtypeStruct((bsz, seq, D_MODEL), F32),
                   jax.ShapeDtypeStruct((bsz, SUBLANES, 2 * D_FF), F32)],
        scratch_shapes=[pltpu.VMEM((SUBLANES, 2 * D_FF), F32)],
        compiler_params=_params(("parallel", "arbitrary")),
        name="ffn_prompt",
    )(x, g, wu, fw, fb, wd)


def _ffn_sample_body(steps, x_ref, g_ref, wu_ref, fw_ref, fb_ref, wd_ref, p1_ref, p2_ref, out_ref, u_ref):
    x = x_ref[...]
    rows = x.shape[0]
    h = _rms(x, g_ref[...]).astype(BF16)
    t = lax.broadcasted_iota(jnp.int32, (rows, FF_CHUNK), 0) % steps

    def shifted(u, cols):
        u_ref[:, cols] = u
        s1 = jnp.where(t >= 1, pltpu.roll(u, 1, axis=0), p1_ref[:, cols])
        s2 = jnp.where(t >= 2, pltpu.roll(u, 2, axis=0), p2_ref[:, cols])
        return s2, s1

    acc = x
    for c in range(D_FF // FF_CHUNK):
        acc = acc + _ffn_chunk(h, c, wu_ref, wd_ref, fw_ref, fb_ref, shifted)
    out_ref[...] = acc


def _ffn_sample(x, steps, g, wu, fw, fb, wd, p1, p2):
    m = x.shape[0]
    tr = min(SAMPLE_ROWS, m)
    assert m % tr == 0 and tr % steps == 0
    row = lambda w: pl.BlockSpec((tr, w), lambda i: (i, 0))
    return pl.pallas_call(
        functools.partial(_ffn_sample_body, steps),
        grid=(m // tr,),
        in_specs=[row(D_MODEL)] + [_const_spec(a.shape) for a in (g, wu, fw, fb, wd)] + [row(2 * D_FF)] * 2,
        out_specs=[row(D_MODEL), row(2 * D_FF)],
        out_shape=[jax.ShapeDtypeStruct((m, D_MODEL), F32), jax.ShapeDtypeStruct((m, 2 * D_FF), F32)],
        compiler_params=_params(("parallel",)),
        name="ffn_sample",
    )(x, g, wu, fw, fb, wd, p1, p2)


def _attn_sample_body(steps, dils, qkv_ref, kv0_ref, kv1_ref, kv2_ref, cb0_ref, cb1_ref, cb2_ref, nb_ref,
                      gq_ref, gk_ref, bd_ref, hm_ref, o_ref, lse_ref, kn_ref):
    rows = qkv_ref.shape[1]
    n_b = rows // steps
    kv_refs = (kv0_ref, kv1_ref, kv2_ref)
    cb_refs = (cb0_ref, cb1_ref, cb2_ref)
    bd = bd_ref[...]
    hm = hm_ref[...]
    row_batch = lax.broadcasted_iota(jnp.int32, (HEADS_PER_GROUP * rows, 1), 0) % rows // steps
    for g in range(len(dils)):
        slab = lambda kind: jnp.concatenate([qkv_ref[kind * PAIRS + 2 * g], qkv_ref[kind * PAIRS + 2 * g + 1]], axis=1)
        qn = _head_rms(slab(0), gq_ref[...], bd) * (HEAD_DIM ** -0.5)
        kn = _head_rms(slab(1), gk_ref[...], bd)
        vn = slab(2)
        kn_ref[2 * g] = kn[:, :PAIR_W]
        kn_ref[2 * g + 1] = kn[:, PAIR_W:]
        qbd = jnp.concatenate([qn * hm[h:h + 1] for h in range(HEADS_PER_GROUP)], axis=0)
        qbd_b = qbd.astype(BF16)
        o_sel = None
        l_sel = None
        for b in range(n_b):
            kc = kv_refs[g][0, b, 0].astype(BF16)
            vc = kv_refs[g][0, b, 1].astype(BF16)
            s_c = _dot(qbd_b, kc) + cb_refs[g][...]
            mx = jnp.max(s_c, axis=-1, keepdims=True)
            s_n = []
            for t2 in range(steps):
                row = b * steps + t2
                sn = jnp.sum(qbd * kn[row:row + 1], axis=-1, keepdims=True) + nb_ref[g, t2]
                s_n.append(sn)
                mx = jnp.maximum(mx, sn)
            e_c = jnp.exp(s_c - mx)
            den = jnp.sum(e_c, axis=-1, keepdims=True)
            acc = _dot_nt(e_c.astype(BF16), vc)
            for t2, sn in enumerate(s_n):
                row = b * steps + t2
                e_n = jnp.exp(sn - mx)
                den = den + e_n
                acc = acc + e_n * vn[row:row + 1]
            o_b = acc / den
            l_b = mx + jnp.log(den)
            o_sel = o_b if o_sel is None else jnp.where(row_batch == b, o_b, o_sel)
            l_sel = l_b if l_sel is None else jnp.where(row_batch == b, l_b, l_sel)
        o_g = None
        l_g = None
        for h in range(HEADS_PER_GROUP):
            o_h = o_sel[h * rows:(h + 1) * rows] * hm[h:h + 1]
            l_h = l_sel[h * rows:(h + 1) * rows] * hm[h:h + 1]
            o_g = o_h if o_g is None else o_g + o_h
            l_g = l_h if l_g is None else l_g + l_h
        for p in range(2):
            o_ref[2 * g + p] = o_g[:, p * PAIR_W:(p + 1) * PAIR_W]
            lse_ref[2 * g + p] = l_g[:, p * PAIR_W:(p + 1) * PAIR_W]


def _sample_cache_bias(slopes, dil, steps, cache_len, rows):
    t = (jnp.arange(rows) % steps)[None, :, None]
    c = jnp.arange(cache_len)[None, None, :]
    dist = cache_len + t - c
    j = dist // dil
    valid = (dist % dil == 0) & (j >= 1) & (j <= BAND)
    bias = -slopes.astype(F32)[:, None, None] * dist.astype(F32)
    return jnp.where(valid, bias, NEG_INF).reshape(slopes.shape[0] * rows, cache_len)


def _sample_new_bias(slopes, dil, steps, rows):
    t2 = jnp.arange(steps)[:, None, None]
    t = (jnp.arange(rows) % steps)[None, None, :]
    dist = t - t2
    valid = (dist >= 0) & (dist % dil == 0) & (dist // dil <= BAND)
    bias = -slopes.astype(F32)[None, :, None] * dist.astype(F32)
    return jnp.where(valid, bias, NEG_INF).reshape(steps, slopes.shape[0] * rows, 1)


def _attn_sample(qkv_t, caches, layer, steps, cbias, nbias, gq, gk, bd, hm):
    m = qkv_t.shape[1]
    rows = SUBLANES
    assert rows % steps == 0 and m % rows == 0
    n_b = rows // steps
    dils = tuple(d for _, d in ATT_GROUPS)
    kv_specs = [pl.BlockSpec((1, n_b) + c.shape[2:], lambda i: (layer, i, 0, 0, 0)) for c in caches]
    consts = (*cbias, nbias, gq, gk, bd, hm)
    ospec = pl.BlockSpec((PAIRS, rows, PAIR_W), lambda i: (0, i, 0))
    return pl.pallas_call(
        functools.partial(_attn_sample_body, steps, dils),
        grid=(m // rows,),
        in_specs=[pl.BlockSpec((QKV_SLABS, rows, PAIR_W), lambda i: (0, i, 0))] + kv_specs
                 + [_const_spec(a.shape) for a in consts],
        out_specs=[ospec] * 3,
        out_shape=[jax.ShapeDtypeStruct((PAIRS, m, PAIR_W), F32)] * 3,
        compiler_params=_params(("parallel",)),
        name="attn_sample",
    )(qkv_t, *caches, *consts)


def _ssd_sample_pre_body(steps, xbc_ref, z_ref, dt_ref, p1_ref, p2_ref, p3_ref, cw_ref, cb_ref, al_ref, dsk_ref,
                         exp_ref, gexp_ref, bm_ref, cm_ref, xde_ref, pre_ref, ea_ref, zg_ref, cd_ref):
    raw = xbc_ref[...]
    rows = raw.shape[0]
    tw = lax.broadcasted_iota(jnp.int32, (rows, XBC_WIDTH), 0) % steps
    cw = cw_ref[...]
    acc = jnp.where(tw >= 3, pltpu.roll(raw, 3, axis=0), p3_ref[...]) * cw[0:1]
    acc = acc + jnp.where(tw >= 2, pltpu.roll(raw, 2, axis=0), p2_ref[...]) * cw[1:2]
    acc = acc + jnp.where(tw >= 1, pltpu.roll(raw, 1, axis=0), p1_ref[...]) * cw[2:3]
    acc = acc + raw * cw[3:4]
    xc = _silu(acc + cb_ref[...])
    xs = xc[:, :SSM_WIDTH]
    bm = xc[:, SSM_WIDTH:SSM_WIDTH + BC_W]
    cm = xc[:, SSM_WIDTH + BC_W:]
    bm_ref[...] = bm
    cm_ref[...] = cm

    dt = dt_ref[...][:, :SSM_HEADS]
    da = dt * (-jnp.exp(al_ref[...]))
    th = lax.broadcasted_iota(jnp.int32, (rows, SSM_HEADS), 0) % steps
    acum = da
    for d in range(1, steps):
        acum = acum + jnp.where(th >= d, pltpu.roll(da, d, axis=0), 0.0)
    tail = jnp.zeros_like(da)
    for d in range(1, steps):
        tail = tail + jnp.where(th + d < steps, pltpu.roll(da, rows - d, axis=0), 0.0)
    expand = exp_ref[...]
    dt_x = _dot_sel(dt, expand)
    acum_x = _dot_sel(acum, expand)
    ea_ref[...] = jnp.exp(acum_x)
    xdt = xs * dt_x
    xde_ref[...] = xdt * _dot_sel(jnp.exp(tail), expand)
    cd_ref[...] = jnp.exp(acum)

    ts = lax.broadcasted_iota(jnp.int32, (rows, SSM_WIDTH), 0) % steps
    gexp = gexp_ref[...]
    y = dsk_ref[...] * xs
    for d in range(steps):
        if d == 0:
            bm_d, xdt_d, ac_d = bm, xdt, acum_x
        else:
            bm_d = pltpu.roll(bm, d, axis=0)
            xdt_d = pltpu.roll(xdt, d, axis=0)
            ac_d = pltpu.roll(acum_x, d, axis=0)
        cb_x = _dot_sel(cm * bm_d, gexp, pieces=2)
        term = cb_x * jnp.exp(acum_x - ac_d) * xdt_d
        y = y + jnp.where(ts >= d, term, 0.0)
    pre_ref[...] = y
    zg_ref[...] = _silu(z_ref[...])


def _ssd_sample_pre(xbc, z, dt, p1, p2, p3, cw, cb, al, dsk, expand, gexp, steps):
    m = xbc.shape[0]
    tr = min(SAMPLE_ROWS, m)
    assert m % tr == 0 and tr % steps == 0
    row = lambda w: pl.BlockSpec((tr, w), lambda i: (i, 0))
    consts = (cw, cb, al, dsk, expand, gexp)
    widths = (BC_W, BC_W, SSM_WIDTH, SSM_WIDTH, SSM_WIDTH, SSM_WIDTH, SSM_HEADS)
    return pl.pallas_call(
        functools.partial(_ssd_sample_pre_body, steps),
        grid=(m // tr,),
        in_specs=[row(XBC_WIDTH), row(SSM_WIDTH), row(DT_PAD)] + [row(XBC_WIDTH)] * 3
                 + [_const_spec(a.shape) for a in consts],
        out_specs=[row(w) for w in widths],
        out_shape=[jax.ShapeDtypeStruct((m, w), F32) for w in widths],
        compiler_params=_params(("parallel",)),
        name="ssd_sample_pre",
    )(xbc, z, dt, p1, p2, p3, *consts)


def _ssd_sample_state_body(steps, bm_ref, cm_ref, xde_ref, pre_ref, ea_ref, zg_ref, cd_ref, gn_ref, st_ref,
                           y_ref, ns_ref):
    bb = st_ref.shape[1]
    blk = pl.program_id(0)

    def one(b, carry):
        bm = bm_ref[b].astype(BF16)
        cm = cm_ref[b].astype(BF16)
        xde = xde_ref[b].astype(BF16)
        inter = []
        for g in range(SSM_GROUPS):
            rows = slice(g * SSM_GROUP_W, (g + 1) * SSM_GROUP_W)
            lanes = slice(g * SSM_STATE, (g + 1) * SSM_STATE)
            st_g = st_ref[0, b, rows, :]
            inter.append(_dot_nt(cm[:, lanes], st_g.astype(BF16)))
            new = _dot_tn(xde[:, rows], bm[:, lanes])
            for hh in range(SSM_HPG):
                h = g * SSM_HPG + hh
                cd = cd_ref[blk * bb + b, h]
                r = slice(h * SSM_HEAD_DIM, (h + 1) * SSM_HEAD_DIM)
                rl = slice(hh * SSM_HEAD_DIM, (hh + 1) * SSM_HEAD_DIM)
                ns_ref[b, r, :] = st_g[rl, :] * cd + new[rl, :]
        y = pre_ref[b] + ea_ref[b] * jnp.concatenate(inter, axis=1)
        y_ref[b] = _rms(y * zg_ref[b], gn_ref[...])
        return carry

    lax.fori_loop(0, bb, one, 0)


def _ssd_sample_state(bm3, cm3, xde3, pre3, ea3, zg3, cd, gn, state, layer, bb):
    bsz, steps, _ = bm3.shape
    blk = lambda w: pl.BlockSpec((bb, steps, w), lambda i: (i, 0, 0))
    st_in = pl.BlockSpec((1, bb, SSM_WIDTH, SSM_STATE), lambda i: (layer, i, 0, 0))
    st_spec = pl.BlockSpec((bb, SSM_WIDTH, SSM_STATE), lambda i: (i, 0, 0))
    return pl.pallas_call(
        functools.partial(_ssd_sample_state_body, steps),
        grid=(bsz // bb,),
        in_specs=[blk(BC_W), blk(BC_W), blk(SSM_WIDTH), blk(SSM_WIDTH), blk(SSM_WIDTH), blk(SSM_WIDTH),
                  pl.BlockSpec(memory_space=pltpu.SMEM), _const_spec(gn.shape), st_in],
        out_specs=[blk(SSM_WIDTH), st_spec],
        out_shape=[jax.ShapeDtypeStruct((bsz, steps, SSM_WIDTH), F32),
                   jax.ShapeDtypeStruct((bsz, SSM_WIDTH, SSM_STATE), F32)],
        compiler_params=_params(("parallel",)),
        name="ssd_sample_state",
    )(bm3, cm3, xde3, pre3, ea3, zg3, cd, gn, state)


def _alibi_slopes():
    h = jnp.arange(1, N_ATT_HEADS + 1, dtype=F32)
    return jnp.exp2(-8.0 * h / N_ATT_HEADS).reshape(len(ATT_GROUPS), HEADS_PER_GROUP)


def _cache_view(cache):
    depth, bsz, cache_len = cache.shape[:3]
    return jnp.transpose(cache, (0, 1, 3, 4, 5, 2)).reshape(depth, bsz, 2, GROUP_W, cache_len)


def _prev_rows(prev, d, steps):
    bsz, km1, c = prev.shape
    assert d <= km1 and d <= steps
    p = jnp.concatenate([prev[:, km1 - d:], jnp.zeros((bsz, steps - d, c), prev.dtype)], axis=1)
    return p.reshape(bsz * steps, c)


def _layer_weights(lw):
    (norm_mix, w_in, q_norm, k_norm, conv_w, conv_b, dt_bias, a_log, d_skip, ssm_norm, w_out,
     norm_ffn, w_up, ffn_conv_w, ffn_conv_b, w_down) = lw
    o_z = QKV_W
    o_x = o_z + SSM_WIDTH
    o_d = o_x + XBC_WIDTH
    w = {}
    w["norm_mix"] = norm_mix.reshape(1, D_MODEL)
    w["wq"] = w_in[:, :o_z].astype(BF16)
    w["wz"] = w_in[:, o_z:o_x].astype(BF16)
    w["wx"] = w_in[:, o_x:o_d].astype(BF16)
    w["wd"] = jnp.pad(w_in[:, o_d:], ((0, 0), (0, DT_PAD - SSM_HEADS))).astype(BF16)
    w["dtb"] = jnp.pad(dt_bias, (0, DT_PAD - SSM_HEADS)).reshape(1, DT_PAD)
    w["gq_g"] = jnp.tile(q_norm, HEADS_PER_GROUP).reshape(1, GROUP_W)
    w["gk_g"] = jnp.tile(k_norm, HEADS_PER_GROUP).reshape(1, GROUP_W)
    w["gq_p"] = jnp.tile(q_norm, 2).reshape(1, PAIR_W)
    w["gk_p"] = jnp.tile(k_norm, 2).reshape(1, PAIR_W)
    w["cw"] = conv_w
    w["cb"] = conv_b.reshape(1, XBC_WIDTH)
    w["alr"] = a_log.reshape(1, SSM_HEADS)
    w["alc"] = a_log.reshape(SSM_HEADS, 1)
    w["dsk"] = jnp.repeat(d_skip, SSM_HEAD_DIM).reshape(1, SSM_WIDTH)
    w["gn"] = ssm_norm.reshape(1, SSM_WIDTH)
    w["wa"] = w_out[:ATT_WIDTH].astype(BF16)
    w["wy"] = w_out[ATT_WIDTH:].astype(BF16)
    w["norm_ffn"] = norm_ffn.reshape(1, D_MODEL)
    w["wu"] = w_up.astype(BF16)
    w["fw"] = ffn_conv_w
    w["fb"] = ffn_conv_b.reshape(1, 2 * D_FF)
    w["wdn"] = w_down.astype(BF16)
    return w


def _constants(steps, cache_lens):
    c = {}
    slopes = _alibi_slopes()
    c["slopes"] = slopes
    c["bd_g"] = _head_block_diag(GROUP_W)
    c["bd_p"] = _head_block_diag(PAIR_W)
    i = jnp.arange(SSM_CHUNK)
    c["tri"] = (i[None, :] <= i[:, None]).astype(BF16)
    lane_head = jnp.arange(SSM_WIDTH) // SSM_HEAD_DIM
    c["expand"] = (jnp.arange(SSM_HEADS)[:, None] == lane_head[None, :]).astype(BF16)
    bc_group = jnp.arange(BC_W) // SSM_STATE
    c["gexp"] = (bc_group[:, None] == (lane_head // SSM_HPG)[None, :]).astype(BF16)
    c["hm"] = (jnp.arange(HEADS_PER_GROUP)[:, None] == (jnp.arange(GROUP_W) // HEAD_DIM)[None, :]).astype(F32)
    c["pbias"] = [_prompt_bias(slopes[g], dil) for g, (_, dil) in enumerate(ATT_GROUPS)]
    c["cbias"] = [_sample_cache_bias(slopes[g], dil, steps, cache_lens[g], SUBLANES)
                  for g, (_, dil) in enumerate(ATT_GROUPS)]
    c["nbias"] = jnp.stack([_sample_new_bias(slopes[g], dil, steps, SUBLANES)
                            for g, (_, dil) in enumerate(ATT_GROUPS)])
    return c


def _prompt_layer(x, w, c, tm):
    bsz, seq, _ = x.shape
    m = bsz * seq
    qkv, z, xbc, dt = _inproj(x.reshape(m, D_MODEL), w["norm_mix"], w["wq"], w["wz"], w["wx"], w["wd"], w["dtb"], tm)
    qkv_t = qkv.reshape(QKV_SLABS, bsz, seq, PAIR_W)
    os_, ls_, new_kv = [], [], []
    for g, (win, dil) in enumerate(ATT_GROUPS):
        nbk = max(1, PROMPT_ATT_ROWS // (BAND * dil))
        o, lse, kn = _attn_prompt(qkv_t, g, dil, nbk, c["pbias"][g], w["gq_p"], w["gk_p"], c["bd_p"])
        os_.append(o.reshape(2, m, PAIR_W))
        ls_.append(lse.reshape(2, m, PAIR_W))
        keep = min(win, seq)
        v_g = qkv_t[2 * PAIRS + 2 * g:2 * PAIRS + 2 * g + 2, :, seq - keep:]
        kv = jnp.stack([kn[:, :, seq - keep:], v_g])
        kv = jnp.transpose(kv, (2, 3, 0, 1, 4))
        new_kv.append(kv.reshape(bsz, keep, 2, HEADS_PER_GROUP, HEAD_DIM))
    xbc3 = xbc.reshape(bsz, seq, XBC_WIDTH)
    dt3 = dt.reshape(bsz, seq, DT_PAD)
    dtt = jnp.swapaxes(dt3[:, :, :SSM_HEADS], 1, 2)
    y, h_last = _ssd_prompt(xbc3, z.reshape(bsz, seq, SSM_WIDTH), dt3, dtt, w["cw"], w["cb"], w["alr"], w["alc"],
                            w["dsk"], w["gn"], c["tri"], c["expand"])
    x1 = _outproj(os_, ls_, y.reshape(m, SSM_WIDTH), x.reshape(m, D_MODEL), w["wa"], w["wy"], tm)
    x2, tail = _ffn_prompt(x1.reshape(bsz, seq, D_MODEL), w["norm_ffn"], w["wu"], w["fw"], w["fb"], w["wdn"], tm)
    h_last = h_last.reshape(bsz, SSM_HEADS, SSM_HEAD_DIM, SSM_STATE)
    conv_new = xbc3[:, seq - (SSM_CONV - 1):]
    ffn_new = tail[:, SUBLANES - (FFN_CONV - 1):]
    return x2, new_kv, h_last, conv_new, ffn_new


def _sample_layer(x, w, c, layer, caches, state, conv_prev, ffn_prev):
    bsz, steps, _ = x.shape
    m = bsz * steps
    assert steps >= SSM_CONV - 1
    xf = x.reshape(m, D_MODEL)
    qkv_t, z, xbc, dt = _inproj(xf, w["norm_mix"], w["wq"], w["wz"], w["wx"], w["wd"], w["dtb"], m)
    o, lse, kn = _attn_sample(qkv_t, caches, layer, steps, c["cbias"], c["nbias"], w["gq_g"], w["gk_g"],
                              c["bd_g"], c["hm"])
    new_kv = []
    for g in range(len(ATT_GROUPS)):
        v_g = qkv_t[2 * PAIRS + 2 * g:2 * PAIRS + 2 * g + 2]
        kv = jnp.stack([kn[2 * g:2 * g + 2], v_g])
        kv = jnp.transpose(kv, (2, 0, 1, 3))
        new_kv.append(kv.reshape(bsz, steps, 2, HEADS_PER_GROUP, HEAD_DIM))
    os_, ls_ = [o], [lse]

    ps = [_prev_rows(conv_prev, d, steps) for d in (1, 2, 3)]
    bm, cm, xde, pre, ea, zg, cd = _ssd_sample_pre(xbc, z, dt, *ps, w["cw"], w["cb"], w["alr"], w["dsk"],
                                                   c["expand"], c["gexp"], steps)
    r3 = lambda a: a.reshape(bsz, steps, a.shape[-1])
    cd_last = r3(cd)[:, steps - 1]
    y, new_state = _ssd_sample_state(r3(bm), r3(cm), r3(xde), r3(pre), r3(ea), r3(zg), cd_last, w["gn"],
                                     state, layer, 8)
    x1 = _outproj(os_, ls_, y.reshape(m, SSM_WIDTH), xf, w["wa"], w["wy"], m)
    fp = [_prev_rows(ffn_prev, d, steps) for d in (1, 2)]
    x2, u_raw = _ffn_sample(x1, steps, w["norm_ffn"], w["wu"], w["fw"], w["fb"], w["wdn"], *fp)
    xbc3 = xbc.reshape(bsz, steps, XBC_WIDTH)
    conv_new = jnp.concatenate([conv_prev, xbc3], axis=1)[:, steps:]
    ffn_new = jnp.concatenate([ffn_prev, u_raw.reshape(bsz, steps, 2 * D_FF)], axis=1)[:, steps:]
    new_state = new_state.reshape(bsz, SSM_HEADS, SSM_HEAD_DIM, SSM_STATE)
    return x2.reshape(bsz, steps, D_MODEL), new_kv, new_state, conv_new, ffn_new


def kernel(x_prompt, x_sample, cache_kv0, cache_kv1, cache_kv2, state_ssm, state_conv, state_ffn_conv, norm_mix, w_in, q_norm, k_norm, conv_w, conv_b, dt_bias, a_log, d_skip, ssm_norm, w_out, norm_ffn, w_up, ffn_conv_w, ffn_conv_b, w_down):
    stacked = (norm_mix, w_in, q_norm, k_norm, conv_w, conv_b, dt_bias, a_log, d_skip, ssm_norm, w_out,
               norm_ffn, w_up, ffn_conv_w, ffn_conv_b, w_down)
    depth = w_in.shape[0]
    dec_batch, steps = x_sample.shape[:2]
    caches = tuple(_cache_view(cache) for cache in (cache_kv0, cache_kv1, cache_kv2))
    for cache, (_, dil) in zip(caches, ATT_GROUPS):
        assert cache.shape[-1] == BAND * dil
    state = state_ssm.reshape(depth, dec_batch, SSM_WIDTH, SSM_STATE)
    c = _constants(steps, tuple(cache.shape[-1] for cache in caches))
    tm = 512
    assert x_prompt.shape[1] % tm == 0
    y_prompt, y_sample = x_prompt, x_sample
    outs_p = [[] for _ in range(6)]
    outs_s = [[] for _ in range(6)]
    for layer in range(depth):
        w = _layer_weights(tuple(a[layer] for a in stacked))
        y_prompt, kv, h, cv, f = _prompt_layer(y_prompt, w, c, tm)
        for lst, val in zip(outs_p, (*kv, h, cv, f)):
            lst.append(val)
        y_sample, kv, h, cv, f = _sample_layer(
            y_sample, w, c, layer, caches, state, state_conv[layer], state_ffn_conv[layer])
        for lst, val in zip(outs_s, (*kv, h, cv, f)):
            lst.append(val)
    return (y_prompt, y_sample, *[jnp.stack(l) for l in outs_p], *[jnp.stack(l) for l in outs_s])
```

```python
import functools
import math

import jax
import jax.numpy as jnp
from jax import lax
from jax.experimental import pallas as pl
from jax.experimental.pallas import tpu as pltpu

F32 = jnp.float32
BF16 = jnp.bfloat16

D_MODEL = 1024
HEAD_DIM = 64
ATT_GROUPS = ((128, 1), (512, 4), (2048, 16))
BAND = 128
HEADS_PER_GROUP = 4
GROUP_W = HEADS_PER_GROUP * HEAD_DIM
N_ATT_HEADS = HEADS_PER_GROUP * len(ATT_GROUPS)
ATT_WIDTH = N_ATT_HEADS * HEAD_DIM
QKV_W = 3 * ATT_WIDTH
PAIR_W = 2 * HEAD_DIM
PAIRS = ATT_WIDTH // PAIR_W
QKV_SLABS = QKV_W // PAIR_W
SSM_HEAD_DIM = 64
SSM_WIDTH = 1024
SSM_HEADS = SSM_WIDTH // SSM_HEAD_DIM
SSM_STATE = 128
SSM_GROUPS = 2
SSM_HPG = SSM_HEADS // SSM_GROUPS
SSM_GROUP_W = SSM_HPG * SSM_HEAD_DIM
SSM_CONV = 4
SSM_CHUNK = 128
BC_W = SSM_GROUPS * SSM_STATE
XBC_WIDTH = SSM_WIDTH + 2 * BC_W
D_FF = 2816
FFN_CONV = 3
FF_CHUNK = 256
EPS = 1e-6
DT_PAD = 128
SUBLANES = 8
LANES = 128
SAMPLE_ROWS = 128
PROMPT_ATT_ROWS = 512
RESIDUE_UNROLL = 4
VMEM_LIMIT = 56 * 1024 * 1024
NEG_INF = float("-inf")


def _dot(a, b):
    return jnp.dot(a, b, preferred_element_type=F32)


def _dot_nt(a, b):
    return lax.dot_general(a, b, (((1,), (1,)), ((), ())), preferred_element_type=F32)


def _dot_tn(a, b):
    return lax.dot_general(a, b, (((0,), (0,)), ((), ())), preferred_element_type=F32)


def _split(x, pieces):
    out = []
    r = x
    for _ in range(pieces):
        p = r.astype(BF16)
        out.append(p)
        r = r - p.astype(F32)
    return out


def _dot_sel(x, sel, pieces=3):
    acc = None
    for p in _split(x, pieces):
        t = _dot(p, sel)
        acc = t if acc is None else acc + t
    return acc


def _sel_dot(sel, x, pieces=3):
    acc = None
    for p in _split(x, pieces):
        t = _dot(sel, p)
        acc = t if acc is None else acc + t
    return acc


def _sel_dot_nt(x, sel, pieces=3):
    acc = None
    for p in _split(x, pieces):
        t = _dot_nt(p, sel)
        acc = t if acc is None else acc + t
    return acc


def _silu(x):
    return x * jax.nn.sigmoid(x)


def _softplus(x):
    return jnp.maximum(x, 0.0) + jnp.log1p(jnp.exp(-jnp.abs(x)))


def _rms(x, g):
    ms = jnp.mean(x * x, axis=-1, keepdims=True)
    return x * lax.rsqrt(ms + EPS) * g


def _const_spec(shape):
    nd = len(shape)
    return pl.BlockSpec(shape, lambda *_: (0,) * nd, pipeline_mode=pl.Buffered(1))


def _params(sem):
    return pltpu.CompilerParams(dimension_semantics=sem, vmem_limit_bytes=VMEM_LIMIT)


def _head_block_diag(width):
    h = jnp.arange(width) // HEAD_DIM
    return jnp.where(h[:, None] == h[None, :], 1.0 / HEAD_DIM, 0.0).astype(BF16)


def _head_rms(t, g, bd):
    ms = _dot_sel(t * t, bd, pieces=2)
    return t * lax.rsqrt(ms + EPS) * g


def _rows_back(slab_ref, s, d, n):
    return slab_ref[s, pl.ds(SUBLANES - d, n, stride=1), :]


def _inproj_body(x_ref, g_ref, wq_ref, wz_ref, wx_ref, wd_ref, dtb_ref,
                 qkv_ref, z_ref, xbc_ref, dt_ref):
    h = _rms(x_ref[...], g_ref[...]).astype(BF16)
    qkv = _dot(h, wq_ref[...])
    for s in range(QKV_SLABS):
        qkv_ref[s] = qkv[:, s * PAIR_W:(s + 1) * PAIR_W]
    z_ref[...] = _dot(h, wz_ref[...])
    xbc_ref[...] = _dot(h, wx_ref[...])
    dt_ref[...] = _softplus(_dot(h, wd_ref[...]) + dtb_ref[...])


def _inproj(x, g, wq, wz, wx, wd, dtb, tm):
    m = x.shape[0]
    row = lambda w: pl.BlockSpec((tm, w), lambda i: (i, 0))
    widths = (SSM_WIDTH, XBC_WIDTH, DT_PAD)
    return pl.pallas_call(
        _inproj_body,
        grid=(m // tm,),
        in_specs=[row(D_MODEL), _const_spec(g.shape), _const_spec(wq.shape), _const_spec(wz.shape),
                  _const_spec(wx.shape), _const_spec(wd.shape), _const_spec(dtb.shape)],
        out_specs=[pl.BlockSpec((QKV_SLABS, tm, PAIR_W), lambda i: (0, i, 0))] + [row(w) for w in widths],
        out_shape=[jax.ShapeDtypeStruct((QKV_SLABS, m, PAIR_W), F32)]
                  + [jax.ShapeDtypeStruct((m, w), F32) for w in widths],
        compiler_params=_params(("parallel",)),
        name="inproj",
    )(x, g, wq, wz, wx, wd, dtb)


def _attn_prompt_body(dil, nbk, q0_ref, q1_ref, k0_ref, k1_ref, v0_ref, v1_ref, bias_ref, gq_ref, gk_ref, bd_ref,
                      o_ref, lse_ref, kn_ref, kv_scr):
    n = pl.program_id(1)
    rows_per = nbk * BAND
    in_refs = ((q0_ref, q1_ref), (k0_ref, k1_ref), (v0_ref, v1_ref))

    @pl.when(n == 0)
    def _():
        kv_scr[...] = jnp.zeros(kv_scr.shape, BF16)

    variant = jnp.minimum(n, 1)
    bd = bd_ref[...]
    gq = gq_ref[...]
    gk = gk_ref[...]
    low = lax.broadcasted_iota(jnp.int32, (BAND, PAIR_W), 1) < HEAD_DIM

    def rows_of(r, start, count):
        return pl.ds(r + start * dil, count, stride=dil) if dil > 1 else pl.ds(start, count)

    def group(residues):
        items = []
        for r in residues:
            for p in range(2):
                q, k, v = (in_refs[kind][p][0, 0, rows_of(r, 0, rows_per), :] for kind in range(3))
                qn = _head_rms(q, gq, bd) * (HEAD_DIM ** -0.5)
                kn = _head_rms(k, gk, bd)
                kall = jnp.concatenate([kv_scr[0, p, r], kn.astype(BF16)], axis=0)
                vall = jnp.concatenate([kv_scr[1, p, r], v.astype(BF16)], axis=0)
                items.append((r, p, qn, kn, kall, vall))
        scores = []
        for r, p, qn, kn, kall, vall in items:
            for j in range(nbk):
                qj = qn[j * BAND:(j + 1) * BAND]
                for hh in range(2):
                    keep = low if hh == 0 else jnp.logical_not(low)
                    qm = jnp.where(keep, qj, 0.0).astype(BF16)
                    bias = bias_ref[variant if j == 0 else 1, 2 * p + hh]
                    scores.append(_dot_nt(qm, kall[j * BAND:(j + 2) * BAND]) + bias)
        probs = []
        for s in scores:
            mx = jnp.max(s, axis=-1, keepdims=True)
            e = jnp.exp(s - mx)
            den = jnp.sum(e, axis=-1, keepdims=True)
            probs.append((e.astype(BF16), den, mx + jnp.log(den)))
        results = []
        it = iter(probs)
        for r, p, qn, kn, kall, vall in items:
            for j in range(nbk):
                (e0, d0, l0), (e1, d1, l1) = next(it), next(it)
                vwin = vall[j * BAND:(j + 2) * BAND]
                o = jnp.where(low, _dot(e0, vwin) / d0, _dot(e1, vwin) / d1)
                results.append((r, p, j, o, jnp.where(low, l0, l1)))
        for r, p, j, o, lse in results:
            o_ref[p, 0, rows_of(r, j * BAND, BAND), :] = o
            lse_ref[p, 0, rows_of(r, j * BAND, BAND), :] = lse
        for r, p, qn, kn, kall, vall in items:
            kn_ref[p, 0, rows_of(r, 0, rows_per), :] = kn
            kv_scr[0, p, r] = kall[rows_per:]
            kv_scr[1, p, r] = vall[rows_per:]

    if dil <= RESIDUE_UNROLL:
        group(range(dil))
    else:
        def step(i, carry):
            group([i * RESIDUE_UNROLL + k for k in range(RESIDUE_UNROLL)])
            return carry
        lax.fori_loop(0, dil // RESIDUE_UNROLL, step, 0)


def _prompt_bias(slopes, dil):
    i = jnp.arange(BAND)[:, None]
    j = jnp.arange(2 * BAND)[None, :]
    step = i + BAND - j
    valid = (step >= 0) & (step <= BAND)
    bias = -slopes.astype(F32)[:, None, None] * (step * dil).astype(F32)
    full = jnp.where(valid[None], bias, NEG_INF)
    first = jnp.where((valid & (j >= BAND))[None], bias, NEG_INF)
    return jnp.stack([first, full])


def _attn_prompt(qkv_t, g, dil, nbk, bias, gq, gk, bd):
    _, bsz, seq, _ = qkv_t.shape
    blk = nbk * BAND * dil
    assert seq % blk == 0
    slab = lambda s: pl.BlockSpec((1, 1, blk, PAIR_W), lambda b, n: (s, b, n, 0))
    ospec = pl.BlockSpec((2, 1, blk, PAIR_W), lambda b, n: (0, b, n, 0))
    first = [kind * PAIRS + 2 * g for kind in range(3)]
    return pl.pallas_call(
        functools.partial(_attn_prompt_body, dil, nbk),
        grid=(bsz, seq // blk),
        in_specs=[slab(first[0]), slab(first[0] + 1), slab(first[1]), slab(first[1] + 1),
                  slab(first[2]), slab(first[2] + 1),
                  _const_spec(bias.shape), _const_spec(gq.shape), _const_spec(gk.shape), _const_spec(bd.shape)],
        out_specs=[ospec, ospec, ospec],
        out_shape=[jax.ShapeDtypeStruct((2, bsz, seq, PAIR_W), F32)] * 3,
        scratch_shapes=[pltpu.VMEM((2, 2, dil, BAND, PAIR_W), BF16)],
        compiler_params=_params(("parallel", "arbitrary")),
        name=f"attn_prompt_g{g}",
    )(*([qkv_t] * 6), bias, gq, gk, bd)


def _ssd_prompt_body(xbc_ref, z_ref, dt_ref, dtt_ref, cw_ref, cb_ref, alr_ref, alc_ref, dsk_ref, gn_ref,
                     tri_ref, exp_ref, y_ref, hl_ref, conv_scr, state_scr):
    c = pl.program_id(1)
    q = SSM_CHUNK

    @pl.when(c == 0)
    def _():
        conv_scr[:, 0:SUBLANES, :] = jnp.zeros((conv_scr.shape[0], SUBLANES, LANES), F32)
        state_scr[...] = jnp.zeros_like(state_scr)

    raw = xbc_ref[0]
    cw = cw_ref[...]
    z = z_ref[0]
    state = state_scr[...]
    n_slabs = XBC_WIDTH // LANES
    for s in range(n_slabs):
        conv_scr[s, SUBLANES:, :] = raw[:, s * LANES:(s + 1) * LANES]
    parts = []
    for s in range(n_slabs):
        lanes = slice(s * LANES, (s + 1) * LANES)
        acc = _rows_back(conv_scr, s, 3, q) * cw[0:1, lanes]
        acc = acc + _rows_back(conv_scr, s, 2, q) * cw[1:2, lanes]
        acc = acc + _rows_back(conv_scr, s, 1, q) * cw[2:3, lanes]
        parts.append(acc + raw[:, lanes] * cw[3:4, lanes])
    xc = _silu(jnp.concatenate(parts, axis=1) + cb_ref[...])
    xs = xc[:, :SSM_WIDTH]
    bm = xc[:, SSM_WIDTH:SSM_WIDTH + BC_W]
    cm = xc[:, SSM_WIDTH + BC_W:]

    tri = tri_ref[...]
    dt = dt_ref[0][:, :SSM_HEADS]
    acum = _sel_dot(tri, dt * (-jnp.exp(alr_ref[...])))
    acum_t = _sel_dot_nt(dtt_ref[0] * (-jnp.exp(alc_ref[...])), tri)
    last = acum[q - 1:q, :]
    expand = exp_ref[...]
    dt_x = _dot_sel(dt, expand)
    ea_x = _dot_sel(jnp.exp(acum), expand)
    de_x = _dot_sel(jnp.exp(last - acum), expand)
    xdt = xs * dt_x
    xde = (xdt * de_x).astype(BF16)
    xdt_b = xdt.astype(BF16)
    causal = lax.broadcasted_iota(jnp.int32, (q, q), 0) >= lax.broadcasted_iota(jnp.int32, (q, q), 1)
    low = lax.broadcasted_iota(jnp.int32, (q, 2 * SSM_HEAD_DIM), 1) < SSM_HEAD_DIM

    y_parts = []
    new_state = []
    for g in range(SSM_GROUPS):
        bm_g = bm[:, g * SSM_STATE:(g + 1) * SSM_STATE].astype(BF16)
        cm_g = cm[:, g * SSM_STATE:(g + 1) * SSM_STATE].astype(BF16)
        cb = _dot_nt(cm_g, bm_g)
        rows = slice(g * SSM_GROUP_W, (g + 1) * SSM_GROUP_W)
        st_g = state[rows, :]
        inter = _dot_nt(cm_g, st_g.astype(BF16))
        for hp in range(SSM_HPG // 2):
            pair = None
            for k in range(2):
                h = g * SSM_HPG + 2 * hp + k
                seg = acum[:, h:h + 1] - acum_t[h:h + 1, :]
                decay = jnp.exp(jnp.where(causal, seg, NEG_INF))
                gmat = (cb * decay).astype(BF16)
                lanes = slice((2 * hp) * SSM_HEAD_DIM + g * SSM_GROUP_W,
                              (2 * hp + 2) * SSM_HEAD_DIM + g * SSM_GROUP_W)
                x_pair = xdt_b[:, lanes]
                keep = low if k == 0 else jnp.logical_not(low)
                t = _dot(gmat, jnp.where(keep, x_pair, jnp.zeros_like(x_pair)))
                pair = t if pair is None else pair + t
            y_parts.append(pair + ea_x[:, lanes] * inter[:, lanes.start - g * SSM_GROUP_W:lanes.stop - g * SSM_GROUP_W])
        new = _dot_tn(xde[:, rows], bm_g)
        for hh in range(SSM_HPG):
            h = g * SSM_HPG + hh
            cd = jnp.exp(acum_t[h:h + 1, q - 1:q])
            r = slice(h * SSM_HEAD_DIM, (h + 1) * SSM_HEAD_DIM)
            rl = slice(hh * SSM_HEAD_DIM, (hh + 1) * SSM_HEAD_DIM)
            new_state.append((r, st_g[rl, :] * cd + new[rl, :]))

    y = jnp.concatenate(y_parts, axis=1) + dsk_ref[...] * xs
    y = y * _silu(z)
    y_ref[0] = _rms(y, gn_ref[...])
    for r, val in new_state:
        state_scr[r, :] = val
    for s in range(n_slabs):
        conv_scr[s, 0:SUBLANES, :] = raw[q - SUBLANES:q, s * LANES:(s + 1) * LANES]

    @pl.when(c == pl.num_programs(1) - 1)
    def _():
        hl_ref[0] = state_scr[...]


def _ssd_prompt(xbc, z, dt, dtt, cw, cb, alr, alc, dsk, gn, tri, expand):
    bsz, seq, _ = xbc.shape
    assert seq % SSM_CHUNK == 0
    nc = seq // SSM_CHUNK
    blk = lambda w: pl.BlockSpec((1, SSM_CHUNK, w), lambda b, c: (b, c, 0))
    consts = (cw, cb, alr, alc, dsk, gn, tri, expand)
    return pl.pallas_call(
        _ssd_prompt_body,
        grid=(bsz, nc),
        in_specs=[blk(XBC_WIDTH), blk(SSM_WIDTH), blk(DT_PAD),
                  pl.BlockSpec((1, SSM_HEADS, SSM_CHUNK), lambda b, c: (b, 0, c))]
                 + [_const_spec(a.shape) for a in consts],
        out_specs=[blk(SSM_WIDTH), pl.BlockSpec((1, SSM_WIDTH, SSM_STATE), lambda b, c: (b, 0, 0))],
        out_shape=[jax.ShapeDtypeStruct((bsz, seq, SSM_WIDTH), F32),
                   jax.ShapeDtypeStruct((bsz, SSM_WIDTH, SSM_STATE), F32)],
        scratch_shapes=[pltpu.VMEM((XBC_WIDTH // LANES, SUBLANES + SSM_CHUNK, LANES), F32),
                        pltpu.VMEM((SSM_WIDTH, SSM_STATE), F32)],
        compiler_params=_params(("parallel", "arbitrary")),
        name="ssd_prompt",
    )(xbc, z, dt, dtt, *consts)


def _outproj_body(slabs, *refs):
    n_arr = len(slabs)
    o_refs, l_refs = refs[:n_arr], refs[n_arr:2 * n_arr]
    y_ref, x_ref, wa_ref, wy_ref, out_ref = refs[2 * n_arr:]
    o_sl = [r[s] for r, cnt in zip(o_refs, slabs) for s in range(cnt)]
    l_sl = [r[s] for r, cnt in zip(l_refs, slabs) for s in range(cnt)]
    ngrp = len(ATT_GROUPS)
    att = [None] * PAIRS
    for p in range(PAIRS // ngrp):
        ls = [l_sl[2 * g + p] for g in range(ngrp)]
        mx = functools.reduce(jnp.maximum, ls)
        es = [jnp.exp(l - mx) for l in ls]
        inv = 1.0 / functools.reduce(jnp.add, es)
        for g in range(ngrp):
            att[2 * g + p] = (o_sl[2 * g + p] * (es[g] * inv)).astype(BF16)
    acc = x_ref[...] + _dot(y_ref[...].astype(BF16), wy_ref[...])
    out_ref[...] = acc + _dot(jnp.concatenate(att, axis=1), wa_ref[...])


def _outproj(os_, ls_, y, x, wa, wy, tm):
    m = x.shape[0]
    slabs = tuple(a.shape[0] for a in os_)
    assert sum(slabs) == PAIRS
    row = lambda w: pl.BlockSpec((tm, w), lambda i: (i, 0))
    slab_specs = [pl.BlockSpec((n, tm, PAIR_W), lambda i: (0, i, 0)) for n in slabs]
    return pl.pallas_call(
        functools.partial(_outproj_body, slabs),
        grid=(m // tm,),
        in_specs=slab_specs * 2 + [row(SSM_WIDTH), row(D_MODEL), _const_spec(wa.shape), _const_spec(wy.shape)],
        out_specs=row(D_MODEL),
        out_shape=jax.ShapeDtypeStruct((m, D_MODEL), F32),
        compiler_params=_params(("parallel",)),
        name="outproj",
    )(*os_, *ls_, y, x, wa, wy)


def _ffn_chunk(h, c, wu_ref, fw_ref, fb_ref, shifted):
    halves = []
    for base in (0, D_FF):
        cols = slice(base + c * FF_CHUNK, base + (c + 1) * FF_CHUNK)
        u = _dot(h, wu_ref[:, cols])
        s2, s1 = shifted(u, cols)
        w = fw_ref[:, cols]
        halves.append(s2 * w[0:1] + s1 * w[1:2] + u * w[2:3] + fb_ref[:, cols])
    return (_silu(halves[0]) * halves[1]).astype(BF16)


def _ffn_prompt_body(x_ref, g_ref, wu_ref, fw_ref, fb_ref, wd_ref, out_ref, tail_ref, u_scr, act_scr):
    i = pl.program_id(1)
    n_slabs = u_scr.shape[0]

    @pl.when(i == 0)
    def _():
        u_scr[:, 0:SUBLANES, :] = jnp.zeros((n_slabs, SUBLANES, LANES), F32)

    x = x_ref[0]
    tm = x.shape[0]
    h = _rms(x, g_ref[...]).astype(BF16)

    def shifted(u, cols):
        s2, s1 = [], []
        for k in range(FF_CHUNK // LANES):
            s = cols.start // LANES + k
            u_scr[s, SUBLANES:, :] = u[:, k * LANES:(k + 1) * LANES]
            s2.append(_rows_back(u_scr, s, 2, tm))
            s1.append(_rows_back(u_scr, s, 1, tm))
        return jnp.concatenate(s2, axis=1), jnp.concatenate(s1, axis=1)

    for c in range(D_FF // FF_CHUNK):
        act_scr[:, c * FF_CHUNK:(c + 1) * FF_CHUNK] = _ffn_chunk(h, c, wu_ref, fw_ref, fb_ref, shifted)
    out_ref[0] = x + _dot(act_scr[...], wd_ref[...])
    u_scr[:, 0:SUBLANES, :] = u_scr[:, tm:tm + SUBLANES, :]

    @pl.when(i == pl.num_programs(1) - 1)
    def _():
        for s in range(n_slabs):
            tail_ref[0, :, s * LANES:(s + 1) * LANES] = u_scr[s, 0:SUBLANES, :]


def _ffn_prompt(x, g, wu, fw, fb, wd, tm):
    bsz, seq, _ = x.shape
    return pl.pallas_call(
        _ffn_prompt_body,
        grid=(bsz, seq // tm),
        in_specs=[pl.BlockSpec((1, tm, D_MODEL), lambda b, i: (b, i, 0))]
                 + [_const_spec(a.shape) for a in (g, wu, fw, fb, wd)],
        out_specs=[pl.BlockSpec((1, tm, D_MODEL), lambda b, i: (b, i, 0)),
                   pl.BlockSpec((1, SUBLANES, 2 * D_FF), lambda b, i: (b, 0, 0))],
        out_shape=[jax.ShapeDtypeStruct((bsz, seq, D_MODEL), F32),
                   jax.ShapeDtypeStruct((bsz, SUBLANES, 2 * D_FF), F32)],
        scratch_shapes=[pltpu.VMEM((2 * D_FF // LANES, SUBLANES + tm, LANES), F32), pltpu.VMEM((tm, D_FF), BF16)],
        compiler_params=_params(("parallel", "arbitrary")),
        name="ffn_prompt",
    )(x, g, wu, fw, fb, wd)


def _ffn_sample_body(steps, x_ref, g_ref, wu_ref, fw_ref, fb_ref, wd_ref, p1_ref, p2_ref, out_ref, u_ref):
    x = x_ref[...]
    rows = x.shape[0]
    h = _rms(x, g_ref[...]).astype(BF16)
    t = lax.broadcasted_iota(jnp.int32, (rows, FF_CHUNK), 0) % steps

    def shifted(u, cols):
        u_ref[:, cols] = u
        s1 = jnp.where(t >= 1, pltpu.roll(u, 1, axis=0), p1_ref[:, cols])
        s2 = jnp.where(t >= 2, pltpu.roll(u, 2, axis=0), p2_ref[:, cols])
        return s2, s1

    act = [_ffn_chunk(h, c, wu_ref, fw_ref, fb_ref, shifted) for c in range(D_FF // FF_CHUNK)]
    out_ref[...] = x + _dot(jnp.concatenate(act, axis=1), wd_ref[...])


def _ffn_sample(x, steps, g, wu, fw, fb, wd, p1, p2):
    m = x.shape[0]
    tr = min(SAMPLE_ROWS, m)
    assert m % tr == 0 and tr % steps == 0
    row = lambda w: pl.BlockSpec((tr, w), lambda i: (i, 0))
    return pl.pallas_call(
        functools.partial(_ffn_sample_body, steps),
        grid=(m // tr,),
        in_specs=[row(D_MODEL)] + [_const_spec(a.shape) for a in (g, wu, fw, fb, wd)] + [row(2 * D_FF)] * 2,
        out_specs=[row(D_MODEL), row(2 * D_FF)],
        out_shape=[jax.ShapeDtypeStruct((m, D_MODEL), F32), jax.ShapeDtypeStruct((m, 2 * D_FF), F32)],
        compiler_params=_params(("parallel",)),
        name="ffn_sample",
    )(x, g, wu, fw, fb, wd, p1, p2)


def _attn_sample_body(steps, dils, qkv_ref, kv0_ref, kv1_ref, kv2_ref, cb0_ref, cb1_ref, cb2_ref, nb_ref,
                      gq_ref, gk_ref, bd_ref, hm_ref, o_ref, lse_ref, kn_ref):
    rows = qkv_ref.shape[1]
    n_b = rows // steps
    kv_refs = (kv0_ref, kv1_ref, kv2_ref)
    cb_refs = (cb0_ref, cb1_ref, cb2_ref)
    bd = bd_ref[...]
    hm = hm_ref[...]
    row_batch = lax.broadcasted_iota(jnp.int32, (HEADS_PER_GROUP * rows, 1), 0) % rows // steps
    results = []
    for g in range(len(dils)):
        slab = lambda kind: jnp.concatenate([qkv_ref[kind * PAIRS + 2 * g], qkv_ref[kind * PAIRS + 2 * g + 1]], axis=1)
        qn = _head_rms(slab(0), gq_ref[...], bd) * (HEAD_DIM ** -0.5)
        kn = _head_rms(slab(1), gk_ref[...], bd)
        vn = slab(2)
        results.append((g, None, None, kn))
        qbd = jnp.concatenate([qn * hm[h:h + 1] for h in range(HEADS_PER_GROUP)], axis=0)
        qbd_b = qbd.astype(BF16)
        o_sel = None
        l_sel = None
        for b in range(n_b):
            kc = kv_refs[g][0, b, 0].astype(BF16)
            vc = kv_refs[g][0, b, 1].astype(BF16)
            s_c = _dot(qbd_b, kc) + cb_refs[g][...]
            mx = jnp.max(s_c, axis=-1, keepdims=True)
            s_n = []
            for t2 in range(steps):
                row = b * steps + t2
                sn = jnp.sum(qbd * kn[row:row + 1], axis=-1, keepdims=True) + nb_ref[g, t2]
                s_n.append(sn)
                mx = jnp.maximum(mx, sn)
            e_c = jnp.exp(s_c - mx)
            den = jnp.sum(e_c, axis=-1, keepdims=True)
            acc = _dot_nt(e_c.astype(BF16), vc)
            for t2, sn in enumerate(s_n):
                row = b * steps + t2
                e_n = jnp.exp(sn - mx)
                den = den + e_n
                acc = acc + e_n * vn[row:row + 1]
            o_b = acc / den
            l_b = mx + jnp.log(den)
            o_sel = o_b if o_sel is None else jnp.where(row_batch == b, o_b, o_sel)
            l_sel = l_b if l_sel is None else jnp.where(row_batch == b, l_b, l_sel)
        o_g = None
        l_g = None
        for h in range(HEADS_PER_GROUP):
            o_h = o_sel[h * rows:(h + 1) * rows] * hm[h:h + 1]
            l_h = l_sel[h * rows:(h + 1) * rows] * hm[h:h + 1]
            o_g = o_h if o_g is None else o_g + o_h
            l_g = l_h if l_g is None else l_g + l_h
        results.append((g, o_g, l_g, None))
    for g, o_g, l_g, kn in results:
        for p in range(2):
            lanes = slice(p * PAIR_W, (p + 1) * PAIR_W)
            if kn is None:
                o_ref[2 * g + p] = o_g[:, lanes]
                lse_ref[2 * g + p] = l_g[:, lanes]
            else:
                kn_ref[2 * g + p] = kn[:, lanes]


def _sample_cache_bias(slopes, dil, steps, cache_len, rows):
    t = (jnp.arange(rows) % steps)[None, :, None]
    c = jnp.arange(cache_len)[None, None, :]
    dist = cache_len + t - c
    j = dist // dil
    valid = (dist % dil == 0) & (j >= 1) & (j <= BAND)
    bias = -slopes.astype(F32)[:, None, None] * dist.astype(F32)
    return jnp.where(valid, bias, NEG_INF).reshape(slopes.shape[0] * rows, cache_len)


def _sample_new_bias(slopes, dil, steps, rows):
    t2 = jnp.arange(steps)[:, None, None]
    t = (jnp.arange(rows) % steps)[None, None, :]
    dist = t - t2
    valid = (dist >= 0) & (dist % dil == 0) & (dist // dil <= BAND)
    bias = -slopes.astype(F32)[None, :, None] * dist.astype(F32)
    return jnp.where(valid, bias, NEG_INF).reshape(steps, slopes.shape[0] * rows, 1)


def _attn_sample(qkv_t, caches, layer, steps, cbias, nbias, gq, gk, bd, hm):
    m = qkv_t.shape[1]
    rows = SUBLANES
    assert rows % steps == 0 and m % rows == 0
    n_b = rows // steps
    dils = tuple(d for _, d in ATT_GROUPS)
    kv_specs = [pl.BlockSpec((1, n_b) + c.shape[2:], lambda i: (layer, i, 0, 0, 0)) for c in caches]
    consts = (*cbias, nbias, gq, gk, bd, hm)
    ospec = pl.BlockSpec((PAIRS, rows, PAIR_W), lambda i: (0, i, 0))
    return pl.pallas_call(
        functools.partial(_attn_sample_body, steps, dils),
        grid=(m // rows,),
        in_specs=[pl.BlockSpec((QKV_SLABS, rows, PAIR_W), lambda i: (0, i, 0))] + kv_specs
                 + [_const_spec(a.shape) for a in consts],
        out_specs=[ospec] * 3,
        out_shape=[jax.ShapeDtypeStruct((PAIRS, m, PAIR_W), F32)] * 3,
        compiler_params=_params(("parallel",)),
        name="attn_sample",
    )(qkv_t, *caches, *consts)


def _ssd_sample_pre_body(steps, xbc_ref, z_ref, dt_ref, p1_ref, p2_ref, p3_ref, cw_ref, cb_ref, al_ref, dsk_ref,
                         exp_ref, gexp_ref, bm_ref, cm_ref, xde_ref, pre_ref, ea_ref, zg_ref, cd_ref):
    raw = xbc_ref[...]
    rows = raw.shape[0]
    tw = lax.broadcasted_iota(jnp.int32, (rows, XBC_WIDTH), 0) % steps
    cw = cw_ref[...]
    acc = jnp.where(tw >= 3, pltpu.roll(raw, 3, axis=0), p3_ref[...]) * cw[0:1]
    acc = acc + jnp.where(tw >= 2, pltpu.roll(raw, 2, axis=0), p2_ref[...]) * cw[1:2]
    acc = acc + jnp.where(tw >= 1, pltpu.roll(raw, 1, axis=0), p1_ref[...]) * cw[2:3]
    acc = acc + raw * cw[3:4]
    xc = _silu(acc + cb_ref[...])
    xs = xc[:, :SSM_WIDTH]
    bm = xc[:, SSM_WIDTH:SSM_WIDTH + BC_W]
    cm = xc[:, SSM_WIDTH + BC_W:]
    bm_ref[...] = bm
    cm_ref[...] = cm

    dt = dt_ref[...][:, :SSM_HEADS]
    da = dt * (-jnp.exp(al_ref[...]))
    th = lax.broadcasted_iota(jnp.int32, (rows, SSM_HEADS), 0) % steps
    acum = da
    for d in range(1, steps):
        acum = acum + jnp.where(th >= d, pltpu.roll(da, d, axis=0), 0.0)
    tail = jnp.zeros_like(da)
    for d in range(1, steps):
        tail = tail + jnp.where(th + d < steps, pltpu.roll(da, rows - d, axis=0), 0.0)
    expand = exp_ref[...]
    dt_x = _dot_sel(dt, expand)
    acum_x = _dot_sel(acum, expand)
    ea_ref[...] = jnp.exp(acum_x)
    xdt = xs * dt_x
    xde_ref[...] = xdt * _dot_sel(jnp.exp(tail), expand)
    cd_ref[...] = jnp.exp(acum)

    ts = lax.broadcasted_iota(jnp.int32, (rows, SSM_WIDTH), 0) % steps
    gexp = gexp_ref[...]
    y = dsk_ref[...] * xs
    for d in range(steps):
        if d == 0:
            bm_d, xdt_d, ac_d = bm, xdt, acum_x
        else:
            bm_d = pltpu.roll(bm, d, axis=0)
            xdt_d = pltpu.roll(xdt, d, axis=0)
            ac_d = pltpu.roll(acum_x, d, axis=0)
        cb_x = _dot_sel(cm * bm_d, gexp, pieces=2)
        term = cb_x * jnp.exp(acum_x - ac_d) * xdt_d
        y = y + jnp.where(ts >= d, term, 0.0)
    pre_ref[...] = y
    zg_ref[...] = _silu(z_ref[...])


def _ssd_sample_pre(xbc, z, dt, p1, p2, p3, cw, cb, al, dsk, expand, gexp, steps):
    m = xbc.shape[0]
    tr = min(SAMPLE_ROWS, m)
    assert m % tr == 0 and tr % steps == 0
    row = lambda w: pl.BlockSpec((tr, w), lambda i: (i, 0))
    consts = (cw, cb, al, dsk, expand, gexp)
    widths = (BC_W, BC_W, SSM_WIDTH, SSM_WIDTH, SSM_WIDTH, SSM_WIDTH, SSM_HEADS)
    return pl.pallas_call(
        functools.partial(_ssd_sample_pre_body, steps),
        grid=(m // tr,),
        in_specs=[row(XBC_WIDTH), row(SSM_WIDTH), row(DT_PAD)] + [row(XBC_WIDTH)] * 3
                 + [_const_spec(a.shape) for a in consts],
        out_specs=[row(w) for w in widths],
        out_shape=[jax.ShapeDtypeStruct((m, w), F32) for w in widths],
        compiler_params=_params(("parallel",)),
        name="ssd_sample_pre",
    )(xbc, z, dt, p1, p2, p3, *consts)


def _ssd_sample_state_body(steps, bm_ref, cm_ref, xde_ref, pre_ref, ea_ref, zg_ref, cd_ref, gn_ref, st_ref,
                           y_ref, ns_ref):
    bb = st_ref.shape[1]
    blk = pl.program_id(0)

    def one(b, carry):
        bm = bm_ref[b].astype(BF16)
        cm = cm_ref[b].astype(BF16)
        xde = xde_ref[b].astype(BF16)
        inter = []
        for g in range(SSM_GROUPS):
            rows = slice(g * SSM_GROUP_W, (g + 1) * SSM_GROUP_W)
            lanes = slice(g * SSM_STATE, (g + 1) * SSM_STATE)
            st_g = st_ref[0, b, rows, :]
            inter.append(_dot_nt(cm[:, lanes], st_g.astype(BF16)))
            new = _dot_tn(xde[:, rows], bm[:, lanes])
            for hh in range(SSM_HPG):
                h = g * SSM_HPG + hh
                cd = cd_ref[blk * bb + b, h]
                r = slice(h * SSM_HEAD_DIM, (h + 1) * SSM_HEAD_DIM)
                rl = slice(hh * SSM_HEAD_DIM, (hh + 1) * SSM_HEAD_DIM)
                ns_ref[b, r, :] = st_g[rl, :] * cd + new[rl, :]
        y = pre_ref[b] + ea_ref[b] * jnp.concatenate(inter, axis=1)
        y_ref[b] = _rms(y * zg_ref[b], gn_ref[...])
        return carry

    lax.fori_loop(0, bb, one, 0)


def _ssd_sample_state(bm3, cm3, xde3, pre3, ea3, zg3, cd, gn, state, layer, bb):
    bsz, steps, _ = bm3.shape
    blk = lambda w: pl.BlockSpec((bb, steps, w), lambda i: (i, 0, 0))
    st_in = pl.BlockSpec((1, bb, SSM_WIDTH, SSM_STATE), lambda i: (layer, i, 0, 0))
    st_spec = pl.BlockSpec((bb, SSM_WIDTH, SSM_STATE), lambda i: (i, 0, 0))
    return pl.pallas_call(
        functools.partial(_ssd_sample_state_body, steps),
        grid=(bsz // bb,),
        in_specs=[blk(BC_W), blk(BC_W), blk(SSM_WIDTH), blk(SSM_WIDTH), blk(SSM_WIDTH), blk(SSM_WIDTH),
                  pl.BlockSpec(memory_space=pltpu.SMEM), _const_spec(gn.shape), st_in],
        out_specs=[blk(SSM_WIDTH), st_spec],
        out_shape=[jax.ShapeDtypeStruct((bsz, steps, SSM_WIDTH), F32),
                   jax.ShapeDtypeStruct((bsz, SSM_WIDTH, SSM_STATE), F32)],
        compiler_params=_params(("parallel",)),
        name="ssd_sample_state",
    )(bm3, cm3, xde3, pre3, ea3, zg3, cd, gn, state)


def _alibi_slopes():
    h = jnp.arange(1, N_ATT_HEADS + 1, dtype=F32)
    return jnp.exp2(-8.0 * h / N_ATT_HEADS).reshape(len(ATT_GROUPS), HEADS_PER_GROUP)


def _cache_view(cache):
    depth, bsz, cache_len = cache.shape[:3]
    return jnp.transpose(cache, (0, 1, 3, 4, 5, 2)).reshape(depth, bsz, 2, GROUP_W, cache_len)


def _prev_rows(prev, d, steps):
    bsz, km1, c = prev.shape
    assert d <= km1 and d <= steps
    p = jnp.concatenate([prev[:, km1 - d:], jnp.zeros((bsz, steps - d, c), prev.dtype)], axis=1)
    return p.reshape(bsz * steps, c)


def _layer_weights(lw):
    (norm_mix, w_in, q_norm, k_norm, conv_w, conv_b, dt_bias, a_log, d_skip, ssm_norm, w_out,
     norm_ffn, w_up, ffn_conv_w, ffn_conv_b, w_down) = lw
    o_z = QKV_W
    o_x = o_z + SSM_WIDTH
    o_d = o_x + XBC_WIDTH
    w = {}
    w["norm_mix"] = norm_mix.reshape(1, D_MODEL)
    w["wq"] = w_in[:, :o_z].astype(BF16)
    w["wz"] = w_in[:, o_z:o_x].astype(BF16)
    w["wx"] = w_in[:, o_x:o_d].astype(BF16)
    w["wd"] = jnp.pad(w_in[:, o_d:], ((0, 0), (0, DT_PAD - SSM_HEADS))).astype(BF16)
    w["dtb"] = jnp.pad(dt_bias, (0, DT_PAD - SSM_HEADS)).reshape(1, DT_PAD)
    w["gq_g"] = jnp.tile(q_norm, HEADS_PER_GROUP).reshape(1, GROUP_W)
    w["gk_g"] = jnp.tile(k_norm, HEADS_PER_GROUP).reshape(1, GROUP_W)
    w["gq_p"] = jnp.tile(q_norm, 2).reshape(1, PAIR_W)
    w["gk_p"] = jnp.tile(k_norm, 2).reshape(1, PAIR_W)
    w["cw"] = conv_w
    w["cb"] = conv_b.reshape(1, XBC_WIDTH)
    w["alr"] = a_log.reshape(1, SSM_HEADS)
    w["alc"] = a_log.reshape(SSM_HEADS, 1)
    w["dsk"] = jnp.repeat(d_skip, SSM_HEAD_DIM).reshape(1, SSM_WIDTH)
    w["gn"] = ssm_norm.reshape(1, SSM_WIDTH)
    w["wa"] = w_out[:ATT_WIDTH].astype(BF16)
    w["wy"] = w_out[ATT_WIDTH:].astype(BF16)
    w["norm_ffn"] = norm_ffn.reshape(1, D_MODEL)
    w["wu"] = w_up.astype(BF16)
    w["fw"] = ffn_conv_w
    w["fb"] = ffn_conv_b.reshape(1, 2 * D_FF)
    w["wdn"] = w_down.astype(BF16)
    return w


def _constants(steps, cache_lens):
    c = {}
    slopes = _alibi_slopes()
    c["slopes"] = slopes
    c["bd_g"] = _head_block_diag(GROUP_W)
    c["bd_p"] = _head_block_diag(PAIR_W)
    i = jnp.arange(SSM_CHUNK)
    c["tri"] = (i[None, :] <= i[:, None]).astype(BF16)
    lane_head = jnp.arange(SSM_WIDTH) // SSM_HEAD_DIM
    c["expand"] = (jnp.arange(SSM_HEADS)[:, None] == lane_head[None, :]).astype(BF16)
    bc_group = jnp.arange(BC_W) // SSM_STATE
    c["gexp"] = (bc_group[:, None] == (lane_head // SSM_HPG)[None, :]).astype(BF16)
    c["hm"] = (jnp.arange(HEADS_PER_GROUP)[:, None] == (jnp.arange(GROUP_W) // HEAD_DIM)[None, :]).astype(F32)
    c["pbias"] = [_prompt_bias(slopes[g], dil) for g, (_, dil) in enumerate(ATT_GROUPS)]
    c["cbias"] = [_sample_cache_bias(slopes[g], dil, steps, cache_lens[g], SUBLANES)
                  for g, (_, dil) in enumerate(ATT_GROUPS)]
    c["nbias"] = jnp.stack([_sample_new_bias(slopes[g], dil, steps, SUBLANES)
                            for g, (_, dil) in enumerate(ATT_GROUPS)])
    return c


def _prompt_layer(x, w, c, tm):
    bsz, seq, _ = x.shape
    m = bsz * seq
    qkv, z, xbc, dt = _inproj(x.reshape(m, D_MODEL), w["norm_mix"], w["wq"], w["wz"], w["wx"], w["wd"], w["dtb"], tm)
    qkv_t = qkv.reshape(QKV_SLABS, bsz, seq, PAIR_W)
    os_, ls_, new_kv = [], [], []
    for g, (win, dil) in enumerate(ATT_GROUPS):
        nbk = max(1, PROMPT_ATT_ROWS // (BAND * dil))
        o, lse, kn = _attn_prompt(qkv_t, g, dil, nbk, c["pbias"][g], w["gq_p"], w["gk_p"], c["bd_p"])
        os_.append(o.reshape(2, m, PAIR_W))
        ls_.append(lse.reshape(2, m, PAIR_W))
        keep = min(win, seq)
        v_g = qkv_t[2 * PAIRS + 2 * g:2 * PAIRS + 2 * g + 2, :, seq - keep:]
        kv = jnp.stack([kn[:, :, seq - keep:], v_g])
        kv = jnp.transpose(kv, (2, 3, 0, 1, 4))
        new_kv.append(kv.reshape(bsz, keep, 2, HEADS_PER_GROUP, HEAD_DIM))
    xbc3 = xbc.reshape(bsz, seq, XBC_WIDTH)
    dt3 = dt.reshape(bsz, seq, DT_PAD)
    dtt = jnp.swapaxes(dt3[:, :, :SSM_HEADS], 1, 2)
    y, h_last = _ssd_prompt(xbc3, z.reshape(bsz, seq, SSM_WIDTH), dt3, dtt, w["cw"], w["cb"], w["alr"], w["alc"],
                            w["dsk"], w["gn"], c["tri"], c["expand"])
    x1 = _outproj(os_, ls_, y.reshape(m, SSM_WIDTH), x.reshape(m, D_MODEL), w["wa"], w["wy"], tm)
    x2, tail = _ffn_prompt(x1.reshape(bsz, seq, D_MODEL), w["norm_ffn"], w["wu"], w["fw"], w["fb"], w["wdn"], tm)
    h_last = h_last.reshape(bsz, SSM_HEADS, SSM_HEAD_DIM, SSM_STATE)
    conv_new = xbc3[:, seq - (SSM_CONV - 1):]
    ffn_new = tail[:, SUBLANES - (FFN_CONV - 1):]
    return x2, new_kv, h_last, conv_new, ffn_new


def _sample_layer(x, w, c, layer, caches, state, conv_prev, ffn_prev):
    bsz, steps, _ = x.shape
    m = bsz * steps
    assert steps >= SSM_CONV - 1
    xf = x.reshape(m, D_MODEL)
    qkv_t, z, xbc, dt = _inproj(xf, w["norm_mix"], w["wq"], w["wz"], w["wx"], w["wd"], w["dtb"], m)
    o, lse, kn = _attn_sample(qkv_t, caches, layer, steps, c["cbias"], c["nbias"], w["gq_g"], w["gk_g"],
                              c["bd_g"], c["hm"])
    new_kv = []
    for g in range(len(ATT_GROUPS)):
        v_g = qkv_t[2 * PAIRS + 2 * g:2 * PAIRS + 2 * g + 2]
        kv = jnp.stack([kn[2 * g:2 * g + 2], v_g])
        kv = jnp.transpose(kv, (2, 0, 1, 3))
        new_kv.append(kv.reshape(bsz, steps, 2, HEADS_PER_GROUP, HEAD_DIM))
    os_, ls_ = [o], [lse]

    ps = [_prev_rows(conv_prev, d, steps) for d in (1, 2, 3)]
    bm, cm, xde, pre, ea, zg, cd = _ssd_sample_pre(xbc, z, dt, *ps, w["cw"], w["cb"], w["alr"], w["dsk"],
                                                   c["expand"], c["gexp"], steps)
    r3 = lambda a: a.reshape(bsz, steps, a.shape[-1])
    cd_last = r3(cd)[:, steps - 1]
    y, new_state = _ssd_sample_state(r3(bm), r3(cm), r3(xde), r3(pre), r3(ea), r3(zg), cd_last, w["gn"],
                                     state, layer, 8)
    x1 = _outproj(os_, ls_, y.reshape(m, SSM_WIDTH), xf, w["wa"], w["wy"], m)
    fp = [_prev_rows(ffn_prev, d, steps) for d in (1, 2)]
    x2, u_raw = _ffn_sample(x1, steps, w["norm_ffn"], w["wu"], w["fw"], w["fb"], w["wdn"], *fp)
    xbc3 = xbc.reshape(bsz, steps, XBC_WIDTH)
    conv_new = jnp.concatenate([conv_prev, xbc3], axis=1)[:, steps:]
    ffn_new = jnp.concatenate([ffn_prev, u_raw.reshape(bsz, steps, 2 * D_FF)], axis=1)[:, steps:]
    new_state = new_state.reshape(bsz, SSM_HEADS, SSM_HEAD_DIM, SSM_STATE)
    return x2.reshape(bsz, steps, D_MODEL), new_kv, new_state, conv_new, ffn_new


def kernel(x_prompt, x_sample, cache_kv0, cache_kv1, cache_kv2, state_ssm, state_conv, state_ffn_conv, norm_mix, w_in, q_norm, k_norm, conv_w, conv_b, dt_bias, a_log, d_skip, ssm_norm, w_out, norm_ffn, w_up, ffn_conv_w, ffn_conv_b, w_down):
    stacked = (norm_mix, w_in, q_norm, k_norm, conv_w, conv_b, dt_bias, a_log, d_skip, ssm_norm, w_out,
               norm_ffn, w_up, ffn_conv_w, ffn_conv_b, w_down)
    depth = w_in.shape[0]
    dec_batch, steps = x_sample.shape[:2]
    caches = tuple(_cache_view(cache) for cache in (cache_kv0, cache_kv1, cache_kv2))
    for cache, (_, dil) in zip(caches, ATT_GROUPS):
        assert cache.shape[-1] == BAND * dil
    state = state_ssm.reshape(depth, dec_batch, SSM_WIDTH, SSM_STATE)
    c = _constants(steps, tuple(cache.shape[-1] for cache in caches))
    tm = 512
    assert x_prompt.shape[1] % tm == 0
    y_prompt, y_sample = x_prompt, x_sample
    outs_p = [[] for _ in range(6)]
    outs_s = [[] for _ in range(6)]
    for layer in range(depth):
        w = _layer_weights(tuple(a[layer] for a in stacked))
        y_prompt, kv, h, cv, f = _prompt_layer(y_prompt, w, c, tm)
        for lst, val in zip(outs_p, (*kv, h, cv, f)):
            lst.append(val)
        y_sample, kv, h, cv, f = _sample_layer(
            y_sample, w, c, layer, caches, state, state_conv[layer], state_ffn_conv[layer])
        for lst, val in zip(outs_s, (*kv, h, cv, f)):
            lst.append(val)
    return (y_prompt, y_sample, *[jnp.stack(l) for l in outs_p], *[jnp.stack(l) for l in outs_s])
```

```python
import functools
import math

import jax
import jax.numpy as jnp
import numpy as np
from jax import lax
from jax.experimental import pallas as pl
from jax.experimental.pallas import tpu as pltpu

F32 = jnp.float32
BF16 = jnp.bfloat16

D_MODEL = 1024
HEAD_DIM = 64
ATT_GROUPS = ((128, 1), (512, 4), (2048, 16))
BAND = 128
HEADS_PER_GROUP = 4
GROUP_W = HEADS_PER_GROUP * HEAD_DIM
N_ATT_HEADS = HEADS_PER_GROUP * len(ATT_GROUPS)
ATT_WIDTH = N_ATT_HEADS * HEAD_DIM
QKV_W = 3 * ATT_WIDTH
PAIR_W = 2 * HEAD_DIM
PAIRS = ATT_WIDTH // PAIR_W
QKV_SLABS = QKV_W // PAIR_W
SSM_HEAD_DIM = 64
SSM_WIDTH = 1024
SSM_HEADS = SSM_WIDTH // SSM_HEAD_DIM
SSM_STATE = 128
SSM_GROUPS = 2
SSM_HPG = SSM_HEADS // SSM_GROUPS
SSM_GROUP_W = SSM_HPG * SSM_HEAD_DIM
SSM_CONV = 4
SSM_CHUNK = 128
BC_W = SSM_GROUPS * SSM_STATE
XBC_WIDTH = SSM_WIDTH + 2 * BC_W
D_FF = 2816
FFN_CONV = 3
FF_CHUNK = 256
EPS = 1e-6
IN_WIDTH = QKV_W + SSM_WIDTH + XBC_WIDTH + SSM_HEADS
SUBLANES = 8
LANES = 128
SAMPLE_ROWS = 128
PROMPT_ATT_ROWS = 512
RESIDUE_UNROLL = 4
VMEM_LIMIT = 56 * 1024 * 1024
NEG_INF = float("-inf")


def _dot(a, b):
    return jnp.dot(a, b, preferred_element_type=F32)


def _dot_nt(a, b):
    return lax.dot_general(a, b, (((1,), (1,)), ((), ())), preferred_element_type=F32)


def _dot_tn(a, b):
    return lax.dot_general(a, b, (((0,), (0,)), ((), ())), preferred_element_type=F32)


def _split(x, pieces):
    out = []
    r = x
    for _ in range(pieces):
        p = r.astype(BF16)
        out.append(p)
        r = r - p.astype(F32)
    return out


def _dot_sel(x, sel, pieces=3):
    acc = None
    for p in _split(x, pieces):
        t = _dot(p, sel)
        acc = t if acc is None else acc + t
    return acc


def _sel_dot(sel, x, pieces=3):
    acc = None
    for p in _split(x, pieces):
        t = _dot(sel, p)
        acc = t if acc is None else acc + t
    return acc


def _sel_dot_nt(x, sel, pieces=3):
    acc = None
    for p in _split(x, pieces):
        t = _dot_nt(p, sel)
        acc = t if acc is None else acc + t
    return acc


def _silu(x):
    return x * jax.nn.sigmoid(x)


def _softplus(x):
    return jnp.maximum(x, 0.0) + jnp.log1p(jnp.exp(-jnp.abs(x)))


def _rms(x, g):
    ms = jnp.mean(x * x, axis=-1, keepdims=True)
    return x * lax.rsqrt(ms + EPS) * g


def _layer_spec(arr, layer):
    nd = arr.ndim
    return pl.BlockSpec((1,) + arr.shape[1:], lambda *_: (layer,) + (0,) * (nd - 1), pipeline_mode=pl.Buffered(1))


def _const_spec(shape):
    nd = len(shape)
    return pl.BlockSpec(shape, lambda *_: (0,) * nd, pipeline_mode=pl.Buffered(1))


def _params(sem):
    return pltpu.CompilerParams(dimension_semantics=sem, vmem_limit_bytes=VMEM_LIMIT)


def _head_block_diag(width):
    h = np.arange(width) // HEAD_DIM
    return jnp.asarray(np.where(h[:, None] == h[None, :], 1.0 / HEAD_DIM, 0.0), dtype=BF16)


def _head_rms(t, g, bd):
    ms = _dot_sel(t * t, bd, pieces=2)
    return t * lax.rsqrt(ms + EPS) * g


def _rows_back(slab_ref, s, d, n):
    return slab_ref[s, pl.ds(SUBLANES - d, n, stride=1), :]


def _inproj_body(x_ref, g_ref, w_ref, dtb_ref, qkv_ref, z_ref, xbc_ref, dt_ref):
    h = _rms(x_ref[...], g_ref[...]).astype(BF16)
    qkv = _dot(h, w_ref[0, :, 0:QKV_W])
    for s in range(QKV_SLABS):
        qkv_ref[s] = qkv[:, s * PAIR_W:(s + 1) * PAIR_W]
    z_ref[...] = _dot(h, w_ref[0, :, QKV_W:QKV_W + SSM_WIDTH])
    xd = _dot(h, w_ref[0, :, QKV_W + SSM_WIDTH:IN_WIDTH])
    xbc_ref[...] = xd[:, :XBC_WIDTH]
    dt_ref[...] = _softplus(xd[:, XBC_WIDTH:] + dtb_ref[...])


def _inproj(x, g, w_in, dtb, layer, tm):
    m = x.shape[0]
    row = lambda w: pl.BlockSpec((tm, w), lambda i: (i, 0))
    widths = (SSM_WIDTH, XBC_WIDTH, SSM_HEADS)
    return pl.pallas_call(
        _inproj_body,
        grid=(m // tm,),
        in_specs=[row(D_MODEL), _const_spec(g.shape), _layer_spec(w_in, layer), _const_spec(dtb.shape)],
        out_specs=[pl.BlockSpec((QKV_SLABS, tm, PAIR_W), lambda i: (0, i, 0))] + [row(w) for w in widths],
        out_shape=[jax.ShapeDtypeStruct((QKV_SLABS, m, PAIR_W), F32)]
                  + [jax.ShapeDtypeStruct((m, w), F32) for w in widths],
        compiler_params=_params(("parallel",)),
        name="inproj",
    )(x, g, w_in, dtb)


def _attn_prompt_body(dil, nbk, q0_ref, q1_ref, k0_ref, k1_ref, v0_ref, v1_ref, bias_ref, gq_ref, gk_ref, bd_ref,
                      o_ref, lse_ref, kn_ref, kv_scr):
    n = pl.program_id(1)
    rows_per = nbk * BAND
    in_refs = ((q0_ref, q1_ref), (k0_ref, k1_ref), (v0_ref, v1_ref))

    @pl.when(n == 0)
    def _():
        kv_scr[...] = jnp.zeros(kv_scr.shape, BF16)

    variant = jnp.minimum(n, 1)
    bd = bd_ref[...]
    gq = gq_ref[...]
    gk = gk_ref[...]
    low = lax.broadcasted_iota(jnp.int32, (BAND, PAIR_W), 1) < HEAD_DIM

    def rows_of(r, start, count):
        return pl.ds(r + start * dil, count, stride=dil) if dil > 1 else pl.ds(start, count)

    def group(residues):
        items = []
        for r in residues:
            for p in range(2):
                q, k, v = (in_refs[kind][p][0, 0, rows_of(r, 0, rows_per), :] for kind in range(3))
                qn = _head_rms(q, gq, bd) * (HEAD_DIM ** -0.5)
                kn = _head_rms(k, gk, bd)
                kall = jnp.concatenate([kv_scr[0, p, r], kn.astype(BF16)], axis=0)
                vall = jnp.concatenate([kv_scr[1, p, r], v.astype(BF16)], axis=0)
                items.append((r, p, qn, kn, kall, vall))
        scores = []
        for r, p, qn, kn, kall, vall in items:
            for j in range(nbk):
                qj = qn[j * BAND:(j + 1) * BAND]
                for hh in range(2):
                    keep = low if hh == 0 else jnp.logical_not(low)
                    qm = jnp.where(keep, qj, 0.0).astype(BF16)
                    bias = bias_ref[variant if j == 0 else 1, 2 * p + hh]
                    scores.append(_dot_nt(qm, kall[j * BAND:(j + 2) * BAND]) + bias)
        probs = []
        for s in scores:
            mx = jnp.max(s, axis=-1, keepdims=True)
            e = jnp.exp(s - mx)
            den = jnp.sum(e, axis=-1, keepdims=True)
            probs.append((e.astype(BF16), den, mx + jnp.log(den)))
        results = []
        it = iter(probs)
        for r, p, qn, kn, kall, vall in items:
            for j in range(nbk):
                (e0, d0, l0), (e1, d1, l1) = next(it), next(it)
                vwin = vall[j * BAND:(j + 2) * BAND]
                o = jnp.where(low, _dot(e0, vwin) / d0, _dot(e1, vwin) / d1)
                results.append((r, p, j, o, jnp.where(low, l0, l1)))
        for r, p, j, o, lse in results:
            o_ref[p, 0, rows_of(r, j * BAND, BAND), :] = o
            lse_ref[p, 0, rows_of(r, j * BAND, BAND), :] = lse
        for r, p, qn, kn, kall, vall in items:
            kn_ref[p, 0, rows_of(r, 0, rows_per), :] = kn
            kv_scr[0, p, r] = kall[rows_per:]
            kv_scr[1, p, r] = vall[rows_per:]

    if dil <= RESIDUE_UNROLL:
        group(range(dil))
    else:
        def step(i, carry):
            group([i * RESIDUE_UNROLL + k for k in range(RESIDUE_UNROLL)])
            return carry
        lax.fori_loop(0, dil // RESIDUE_UNROLL, step, 0)


def _prompt_bias(slopes, dil):
    i = np.arange(BAND)[:, None]
    j = np.arange(2 * BAND)[None, :]
    step = i + BAND - j
    valid = (step >= 0) & (step <= BAND)
    bias = -slopes[:, None, None] * (step * dil).astype(np.float32)
    full = np.where(valid[None], bias, NEG_INF)
    first = np.where((valid & (j >= BAND))[None], bias, NEG_INF)
    return jnp.asarray(np.stack([first, full]), dtype=F32)


def _attn_prompt(qkv_t, g, dil, nbk, bias, gq, gk, bd):
    _, bsz, seq, _ = qkv_t.shape
    blk = nbk * BAND * dil
    assert seq % blk == 0
    slab = lambda s: pl.BlockSpec((1, 1, blk, PAIR_W), lambda b, n: (s, b, n, 0))
    ospec = pl.BlockSpec((2, 1, blk, PAIR_W), lambda b, n: (0, b, n, 0))
    first = [kind * PAIRS + 2 * g for kind in range(3)]
    return pl.pallas_call(
        functools.partial(_attn_prompt_body, dil, nbk),
        grid=(bsz, seq // blk),
        in_specs=[slab(first[0]), slab(first[0] + 1), slab(first[1]), slab(first[1] + 1),
                  slab(first[2]), slab(first[2] + 1),
                  _const_spec(bias.shape), _const_spec(gq.shape), _const_spec(gk.shape), _const_spec(bd.shape)],
        out_specs=[ospec, ospec, ospec],
        out_shape=[jax.ShapeDtypeStruct((2, bsz, seq, PAIR_W), F32)] * 3,
        scratch_shapes=[pltpu.VMEM((2, 2, dil, BAND, PAIR_W), BF16)],
        compiler_params=_params(("parallel", "arbitrary")),
        name=f"attn_prompt_g{g}",
    )(*([qkv_t] * 6), bias, gq, gk, bd)


def _ssd_prompt_body(xbc_ref, z_ref, dt_ref, dtt_ref, cw_ref, cb_ref, alr_ref, alc_ref, dsk_ref, gn_ref,
                     tri_ref, exp_ref, y_ref, hl_ref, conv_scr, state_scr):
    c = pl.program_id(1)
    q = SSM_CHUNK

    @pl.when(c == 0)
    def _():
        conv_scr[:, 0:SUBLANES, :] = jnp.zeros((conv_scr.shape[0], SUBLANES, LANES), F32)
        state_scr[...] = jnp.zeros_like(state_scr)

    raw = xbc_ref[0]
    cw = cw_ref[...]
    z = z_ref[0]
    state = state_scr[...]
    n_slabs = XBC_WIDTH // LANES
    for s in range(n_slabs):
        conv_scr[s, SUBLANES:, :] = raw[:, s * LANES:(s + 1) * LANES]
    parts = []
    for s in range(n_slabs):
        lanes = slice(s * LANES, (s + 1) * LANES)
        acc = _rows_back(conv_scr, s, 3, q) * cw[0:1, lanes]
        acc = acc + _rows_back(conv_scr, s, 2, q) * cw[1:2, lanes]
        acc = acc + _rows_back(conv_scr, s, 1, q) * cw[2:3, lanes]
        parts.append(acc + raw[:, lanes] * cw[3:4, lanes])
    xc = _silu(jnp.concatenate(parts, axis=1) + cb_ref[...])
    xs = xc[:, :SSM_WIDTH]
    bm = xc[:, SSM_WIDTH:SSM_WIDTH + BC_W]
    cm = xc[:, SSM_WIDTH + BC_W:]

    tri = tri_ref[...]
    dt = dt_ref[0]
    acum = _sel_dot(tri, dt * (-jnp.exp(alr_ref[...])))
    acum_t = _sel_dot_nt(dtt_ref[0] * (-jnp.exp(alc_ref[...])), tri)
    last = acum[q - 1:q, :]
    expand = exp_ref[...]
    dt_x = _dot_sel(dt, expand)
    ea_x = _dot_sel(jnp.exp(acum), expand)
    de_x = _dot_sel(jnp.exp(last - acum), expand)
    xdt = xs * dt_x
    xde = (xdt * de_x).astype(BF16)
    xdt_b = xdt.astype(BF16)
    causal = lax.broadcasted_iota(jnp.int32, (q, q), 0) >= lax.broadcasted_iota(jnp.int32, (q, q), 1)
    low = lax.broadcasted_iota(jnp.int32, (q, 2 * SSM_HEAD_DIM), 1) < SSM_HEAD_DIM

    y_parts = []
    new_state = []
    for g in range(SSM_GROUPS):
        bm_g = bm[:, g * SSM_STATE:(g + 1) * SSM_STATE].astype(BF16)
        cm_g = cm[:, g * SSM_STATE:(g + 1) * SSM_STATE].astype(BF16)
        cb = _dot_nt(cm_g, bm_g)
        rows = slice(g * SSM_GROUP_W, (g + 1) * SSM_GROUP_W)
        st_g = state[rows, :]
        inter = _dot_nt(cm_g, st_g.astype(BF16))
        for hp in range(SSM_HPG // 2):
            pair = None
            for k in range(2):
                h = g * SSM_HPG + 2 * hp + k
                seg = acum[:, h:h + 1] - acum_t[h:h + 1, :]
                decay = jnp.exp(jnp.where(causal, seg, NEG_INF))
                gmat = (cb * decay).astype(BF16)
                lanes = slice((2 * hp) * SSM_HEAD_DIM + g * SSM_GROUP_W,
                              (2 * hp + 2) * SSM_HEAD_DIM + g * SSM_GROUP_W)
                x_pair = xdt_b[:, lanes]
                keep = low if k == 0 else jnp.logical_not(low)
                t = _dot(gmat, jnp.where(keep, x_pair, jnp.zeros_like(x_pair)))
                pair = t if pair is None else pair + t
            y_parts.append(pair + ea_x[:, lanes] * inter[:, lanes.start - g * SSM_GROUP_W:lanes.stop - g * SSM_GROUP_W])
        new = _dot_tn(xde[:, rows], bm_g)
        for hh in range(SSM_HPG):
            h = g * SSM_HPG + hh
            cd = jnp.exp(acum_t[h:h + 1, q - 1:q])
            r = slice(h * SSM_HEAD_DIM, (h + 1) * SSM_HEAD_DIM)
            rl = slice(hh * SSM_HEAD_DIM, (hh + 1) * SSM_HEAD_DIM)
            new_state.append((r, st_g[rl, :] * cd + new[rl, :]))

    y = jnp.concatenate(y_parts, axis=1) + dsk_ref[...] * xs
    y = y * _silu(z)
    y_ref[0] = _rms(y, gn_ref[...])
    for r, val in new_state:
        state_scr[r, :] = val
    for s in range(n_slabs):
        conv_scr[s, 0:SUBLANES, :] = raw[q - SUBLANES:q, s * LANES:(s + 1) * LANES]

    @pl.when(c == pl.num_programs(1) - 1)
    def _():
        hl_ref[0] = state_scr[...]


def _ssd_prompt(xbc, z, dt, dtt, cw, cb, alr, alc, dsk, gn, tri, expand):
    bsz, seq, _ = xbc.shape
    assert seq % SSM_CHUNK == 0
    nc = seq // SSM_CHUNK
    blk = lambda w: pl.BlockSpec((1, SSM_CHUNK, w), lambda b, c: (b, c, 0))
    consts = (cw, cb, alr, alc, dsk, gn, tri, expand)
    return pl.pallas_call(
        _ssd_prompt_body,
        grid=(bsz, nc),
        in_specs=[blk(XBC_WIDTH), blk(SSM_WIDTH), blk(SSM_HEADS),
                  pl.BlockSpec((1, SSM_HEADS, SSM_CHUNK), lambda b, c: (b, 0, c))]
                 + [_const_spec(a.shape) for a in consts],
        out_specs=[blk(SSM_WIDTH), pl.BlockSpec((1, SSM_WIDTH, SSM_STATE), lambda b, c: (b, 0, 0))],
        out_shape=[jax.ShapeDtypeStruct((bsz, seq, SSM_WIDTH), F32),
                   jax.ShapeDtypeStruct((bsz, SSM_WIDTH, SSM_STATE), F32)],
        scratch_shapes=[pltpu.VMEM((XBC_WIDTH // LANES, SUBLANES + SSM_CHUNK, LANES), F32),
                        pltpu.VMEM((SSM_WIDTH, SSM_STATE), F32)],
        compiler_params=_params(("parallel", "arbitrary")),
        name="ssd_prompt",
    )(xbc, z, dt, dtt, *consts)


def _outproj_body(slabs, *refs):
    n_arr = len(slabs)
    o_refs, l_refs = refs[:n_arr], refs[n_arr:2 * n_arr]
    y_ref, x_ref, w_ref, out_ref = refs[2 * n_arr:]
    o_sl = [r[s] for r, cnt in zip(o_refs, slabs) for s in range(cnt)]
    l_sl = [r[s] for r, cnt in zip(l_refs, slabs) for s in range(cnt)]
    ngrp = len(ATT_GROUPS)
    att = [None] * PAIRS
    for p in range(PAIRS // ngrp):
        ls = [l_sl[2 * g + p] for g in range(ngrp)]
        mx = functools.reduce(jnp.maximum, ls)
        es = [jnp.exp(l - mx) for l in ls]
        inv = 1.0 / functools.reduce(jnp.add, es)
        for g in range(ngrp):
            att[2 * g + p] = (o_sl[2 * g + p] * (es[g] * inv)).astype(BF16)
    acc = x_ref[...] + _dot(y_ref[...].astype(BF16), w_ref[0, ATT_WIDTH:, :])
    out_ref[...] = acc + _dot(jnp.concatenate(att, axis=1), w_ref[0, 0:ATT_WIDTH, :])


def _outproj(os_, ls_, y, x, w_out, layer, tm):
    m = x.shape[0]
    slabs = tuple(a.shape[0] for a in os_)
    assert sum(slabs) == PAIRS
    row = lambda w: pl.BlockSpec((tm, w), lambda i: (i, 0))
    slab_specs = [pl.BlockSpec((n, tm, PAIR_W), lambda i: (0, i, 0)) for n in slabs]
    return pl.pallas_call(
        functools.partial(_outproj_body, slabs),
        grid=(m // tm,),
        in_specs=slab_specs * 2 + [row(SSM_WIDTH), row(D_MODEL), _layer_spec(w_out, layer)],
        out_specs=row(D_MODEL),
        out_shape=jax.ShapeDtypeStruct((m, D_MODEL), F32),
        compiler_params=_params(("parallel",)),
        name="outproj",
    )(*os_, *ls_, y, x, w_out)


def _ffn_chunk(h, c, wu_ref, fw_ref, fb_ref, shifted):
    halves = []
    for base in (0, D_FF):
        cols = slice(base + c * FF_CHUNK, base + (c + 1) * FF_CHUNK)
        u = _dot(h, wu_ref[0, :, cols])
        s2, s1 = shifted(u, cols)
        w = fw_ref[:, cols]
        halves.append(s2 * w[0:1] + s1 * w[1:2] + u * w[2:3] + fb_ref[:, cols])
    return (_silu(halves[0]) * halves[1]).astype(BF16)


def _ffn_prompt_body(x_ref, g_ref, wu_ref, fw_ref, fb_ref, wd_ref, out_ref, tail_ref, u_scr, act_scr):
    i = pl.program_id(1)
    n_slabs = u_scr.shape[0]

    @pl.when(i == 0)
    def _():
        u_scr[:, 0:SUBLANES, :] = jnp.zeros((n_slabs, SUBLANES, LANES), F32)

    x = x_ref[0]
    tm = x.shape[0]
    h = _rms(x, g_ref[...]).astype(BF16)

    def shifted(u, cols):
        s2, s1 = [], []
        for k in range(FF_CHUNK // LANES):
            s = cols.start // LANES + k
            u_scr[s, SUBLANES:, :] = u[:, k * LANES:(k + 1) * LANES]
            s2.append(_rows_back(u_scr, s, 2, tm))
            s1.append(_rows_back(u_scr, s, 1, tm))
        return jnp.concatenate(s2, axis=1), jnp.concatenate(s1, axis=1)

    for c in range(D_FF // FF_CHUNK):
        act_scr[:, c * FF_CHUNK:(c + 1) * FF_CHUNK] = _ffn_chunk(h, c, wu_ref, fw_ref, fb_ref, shifted)
    out_ref[0] = x + _dot(act_scr[...], wd_ref[0])
    u_scr[:, 0:SUBLANES, :] = u_scr[:, tm:tm + SUBLANES, :]

    @pl.when(i == pl.num_programs(1) - 1)
    def _():
        for s in range(n_slabs):
            tail_ref[0, :, s * LANES:(s + 1) * LANES] = u_scr[s, 0:SUBLANES, :]


def _ffn_prompt(x, g, wu, fw, fb, wd, layer, tm):
    bsz, seq, _ = x.shape
    return pl.pallas_call(
        _ffn_prompt_body,
        grid=(bsz, seq // tm),
        in_specs=[pl.BlockSpec((1, tm, D_MODEL), lambda b, i: (b, i, 0)), _const_spec(g.shape),
                  _layer_spec(wu, layer), _const_spec(fw.shape), _const_spec(fb.shape), _layer_spec(wd, layer)],
        out_specs=[pl.BlockSpec((1, tm, D_MODEL), lambda b, i: (b, i, 0)),
                   pl.BlockSpec((1, SUBLANES, 2 * D_FF), lambda b, i: (b, 0, 0))],
        out_shape=[jax.ShapeDtypeStruct((bsz, seq, D_MODEL), F32),
                   jax.ShapeDtypeStruct((bsz, SUBLANES, 2 * D_FF), F32)],
        scratch_shapes=[pltpu.VMEM((2 * D_FF // LANES, SUBLANES + tm, LANES), F32), pltpu.VMEM((tm, D_FF), BF16)],
        compiler_params=_params(("parallel", "arbitrary")),
        name="ffn_prompt",
    )(x, g, wu, fw, fb, wd)


def _ffn_sample_body(steps, x_ref, g_ref, wu_ref, fw_ref, fb_ref, wd_ref, prev_ref, out_ref, u_ref):
    x = x_ref[...]
    rows = x.shape[0]
    h = _rms(x, g_ref[...]).astype(BF16)
    t = lax.broadcasted_iota(jnp.int32, (rows, FF_CHUNK), 0) % steps

    def shifted(u, cols):
        u_ref[:, cols] = u
        prev = prev_ref[:, cols]
        s1 = jnp.where(t >= 1, pltpu.roll(u, 1, axis=0), pltpu.roll(prev, rows - 1, axis=0))
        s2 = jnp.where(t >= 2, pltpu.roll(u, 2, axis=0), prev)
        return s2, s1

    act = [_ffn_chunk(h, c, wu_ref, fw_ref, fb_ref, shifted) for c in range(D_FF // FF_CHUNK)]
    out_ref[...] = x + _dot(jnp.concatenate(act, axis=1), wd_ref[0])


def _ffn_sample(x, steps, g, wu, fw, fb, wd, layer, prev):
    m = x.shape[0]
    tr = min(SAMPLE_ROWS, m)
    assert m % tr == 0 and tr % steps == 0
    row = lambda w: pl.BlockSpec((tr, w), lambda i: (i, 0))
    return pl.pallas_call(
        functools.partial(_ffn_sample_body, steps),
        grid=(m // tr,),
        in_specs=[row(D_MODEL), _const_spec(g.shape), _layer_spec(wu, layer), _const_spec(fw.shape),
                  _const_spec(fb.shape), _layer_spec(wd, layer), row(2 * D_FF)],
        out_specs=[row(D_MODEL), row(2 * D_FF)],
        out_shape=[jax.ShapeDtypeStruct((m, D_MODEL), F32), jax.ShapeDtypeStruct((m, 2 * D_FF), F32)],
        compiler_params=_params(("parallel",)),
        name="ffn_sample",
    )(x, g, wu, fw, fb, wd, prev)


def _attn_sample_body(steps, dils, qkv_ref, kv0_ref, kv1_ref, kv2_ref, cb0_ref, cb1_ref, cb2_ref, nb_ref,
                      gq_ref, gk_ref, bd_ref, hm_ref, o_ref, lse_ref, kn_ref):
    rows = qkv_ref.shape[1]
    n_b = rows // steps
    kv_refs = (kv0_ref, kv1_ref, kv2_ref)
    cb_refs = (cb0_ref, cb1_ref, cb2_ref)
    bd = bd_ref[...]
    hm = hm_ref[...]
    row_batch = lax.broadcasted_iota(jnp.int32, (HEADS_PER_GROUP * rows, 1), 0) % rows // steps
    results = []
    for g in range(len(dils)):
        slab = lambda kind: jnp.concatenate([qkv_ref[kind * PAIRS + 2 * g], qkv_ref[kind * PAIRS + 2 * g + 1]], axis=1)
        qn = _head_rms(slab(0), gq_ref[...], bd) * (HEAD_DIM ** -0.5)
        kn = _head_rms(slab(1), gk_ref[...], bd)
        vn = slab(2)
        results.append((g, None, None, kn))
        qbd = jnp.concatenate([qn * hm[h:h + 1] for h in range(HEADS_PER_GROUP)], axis=0)
        qbd_b = qbd.astype(BF16)
        o_sel = None
        l_sel = None
        for b in range(n_b):
            kc = kv_refs[g][0, b, 0].astype(BF16)
            vc = kv_refs[g][0, b, 1].astype(BF16)
            s_c = _dot(qbd_b, kc) + cb_refs[g][...]
            mx = jnp.max(s_c, axis=-1, keepdims=True)
            s_n = []
            for t2 in range(steps):
                row = b * steps + t2
                sn = jnp.sum(qbd * kn[row:row + 1], axis=-1, keepdims=True) + nb_ref[g, t2]
                s_n.append(sn)
                mx = jnp.maximum(mx, sn)
            e_c = jnp.exp(s_c - mx)
            den = jnp.sum(e_c, axis=-1, keepdims=True)
            acc = _dot_nt(e_c.astype(BF16), vc)
            for t2, sn in enumerate(s_n):
                row = b * steps + t2
                e_n = jnp.exp(sn - mx)
                den = den + e_n
                acc = acc + e_n * vn[row:row + 1]
            o_b = acc / den
            l_b = mx + jnp.log(den)
            o_sel = o_b if o_sel is None else jnp.where(row_batch == b, o_b, o_sel)
            l_sel = l_b if l_sel is None else jnp.where(row_batch == b, l_b, l_sel)
        o_g = None
        l_g = None
        for h in range(HEADS_PER_GROUP):
            o_h = o_sel[h * rows:(h + 1) * rows] * hm[h:h + 1]
            l_h = l_sel[h * rows:(h + 1) * rows] * hm[h:h + 1]
            o_g = o_h if o_g is None else o_g + o_h
            l_g = l_h if l_g is None else l_g + l_h
        results.append((g, o_g, l_g, None))
    for g, o_g, l_g, kn in results:
        for p in range(2):
            lanes = slice(p * PAIR_W, (p + 1) * PAIR_W)
            if kn is None:
                o_ref[2 * g + p] = o_g[:, lanes]
                lse_ref[2 * g + p] = l_g[:, lanes]
            else:
                kn_ref[2 * g + p] = kn[:, lanes]


def _sample_cache_bias(slopes, dil, steps, cache_len, rows):
    t = (np.arange(rows) % steps)[None, :, None]
    c = np.arange(cache_len)[None, None, :]
    dist = cache_len + t - c
    j = dist // dil
    valid = (dist % dil == 0) & (j >= 1) & (j <= BAND)
    bias = -slopes[:, None, None] * dist.astype(np.float32)
    return jnp.asarray(np.where(valid, bias, NEG_INF).reshape(slopes.shape[0] * rows, cache_len), dtype=F32)


def _sample_new_bias(slopes, dil, steps, rows):
    t2 = np.arange(steps)[:, None, None]
    t = (np.arange(rows) % steps)[None, None, :]
    dist = t - t2
    valid = (dist >= 0) & (dist % dil == 0) & (dist // dil <= BAND)
    bias = -slopes[None, :, None] * dist.astype(np.float32)
    return np.where(valid, bias, NEG_INF).reshape(steps, slopes.shape[0] * rows, 1)


def _attn_sample(qkv_t, caches, layer, steps, cbias, nbias, gq, gk, bd, hm):
    m = qkv_t.shape[1]
    rows = SUBLANES
    assert rows % steps == 0 and m % rows == 0
    n_b = rows // steps
    dils = tuple(d for _, d in ATT_GROUPS)
    kv_specs = [pl.BlockSpec((1, n_b) + c.shape[2:], lambda i: (layer, i, 0, 0, 0)) for c in caches]
    consts = (*cbias, nbias, gq, gk, bd, hm)
    ospec = pl.BlockSpec((PAIRS, rows, PAIR_W), lambda i: (0, i, 0))
    return pl.pallas_call(
        functools.partial(_attn_sample_body, steps, dils),
        grid=(m // rows,),
        in_specs=[pl.BlockSpec((QKV_SLABS, rows, PAIR_W), lambda i: (0, i, 0))] + kv_specs
                 + [_const_spec(a.shape) for a in consts],
        out_specs=[ospec] * 3,
        out_shape=[jax.ShapeDtypeStruct((PAIRS, m, PAIR_W), F32)] * 3,
        compiler_params=_params(("parallel",)),
        name="attn_sample",
    )(qkv_t, *caches, *consts)


def _ssd_sample_pre_body(steps, xbc_ref, z_ref, dt_ref, prev_ref, cw_ref, cb_ref, al_ref, dsk_ref,
                         exp_ref, gexp_ref, bm_ref, cm_ref, xde_ref, pre_ref, ea_ref, zg_ref, cd_ref):
    raw = xbc_ref[...]
    rows = raw.shape[0]
    tw = lax.broadcasted_iota(jnp.int32, (rows, XBC_WIDTH), 0) % steps
    cw = cw_ref[...]
    prev = prev_ref[...]
    acc = jnp.where(tw >= 3, pltpu.roll(raw, 3, axis=0), prev) * cw[0:1]
    acc = acc + jnp.where(tw >= 2, pltpu.roll(raw, 2, axis=0), pltpu.roll(prev, rows - 1, axis=0)) * cw[1:2]
    acc = acc + jnp.where(tw >= 1, pltpu.roll(raw, 1, axis=0), pltpu.roll(prev, rows - 2, axis=0)) * cw[2:3]
    acc = acc + raw * cw[3:4]
    xc = _silu(acc + cb_ref[...])
    xs = xc[:, :SSM_WIDTH]
    bm = xc[:, SSM_WIDTH:SSM_WIDTH + BC_W]
    cm = xc[:, SSM_WIDTH + BC_W:]
    bm_ref[...] = bm
    cm_ref[...] = cm

    dt = dt_ref[...]
    da = dt * (-jnp.exp(al_ref[...]))
    th = lax.broadcasted_iota(jnp.int32, (rows, SSM_HEADS), 0) % steps
    acum = da
    for d in range(1, steps):
        acum = acum + jnp.where(th >= d, pltpu.roll(da, d, axis=0), 0.0)
    tail = jnp.zeros_like(da)
    for d in range(1, steps):
        tail = tail + jnp.where(th + d < steps, pltpu.roll(da, rows - d, axis=0), 0.0)
    expand = exp_ref[...]
    dt_x = _dot_sel(dt, expand)
    acum_x = _dot_sel(acum, expand)
    ea_ref[...] = jnp.exp(acum_x)
    xdt = xs * dt_x
    xde_ref[...] = xdt * _dot_sel(jnp.exp(tail), expand)
    cd_ref[...] = jnp.exp(acum)

    ts = lax.broadcasted_iota(jnp.int32, (rows, SSM_WIDTH), 0) % steps
    gexp = gexp_ref[...]
    y = dsk_ref[...] * xs
    for d in range(steps):
        if d == 0:
            bm_d, xdt_d, ac_d = bm, xdt, acum_x
        else:
            bm_d = pltpu.roll(bm, d, axis=0)
            xdt_d = pltpu.roll(xdt, d, axis=0)
            ac_d = pltpu.roll(acum_x, d, axis=0)
        cb_x = _dot_sel(cm * bm_d, gexp, pieces=2)
        term = cb_x * jnp.exp(acum_x - ac_d) * xdt_d
        y = y + jnp.where(ts >= d, term, 0.0)
    pre_ref[...] = y
    zg_ref[...] = _silu(z_ref[...])


def _ssd_sample_pre(xbc, z, dt, prev, cw, cb, al, dsk, expand, gexp, steps):
    m = xbc.shape[0]
    tr = min(SAMPLE_ROWS, m)
    assert m % tr == 0 and tr % steps == 0
    row = lambda w: pl.BlockSpec((tr, w), lambda i: (i, 0))
    consts = (cw, cb, al, dsk, expand, gexp)
    widths = (BC_W, BC_W, SSM_WIDTH, SSM_WIDTH, SSM_WIDTH, SSM_WIDTH, SSM_HEADS)
    return pl.pallas_call(
        functools.partial(_ssd_sample_pre_body, steps),
        grid=(m // tr,),
        in_specs=[row(XBC_WIDTH), row(SSM_WIDTH), row(SSM_HEADS), row(XBC_WIDTH)]
                 + [_const_spec(a.shape) for a in consts],
        out_specs=[row(w) for w in widths],
        out_shape=[jax.ShapeDtypeStruct((m, w), F32) for w in widths],
        compiler_params=_params(("parallel",)),
        name="ssd_sample_pre",
    )(xbc, z, dt, prev, *consts)


def _ssd_sample_state_body(steps, bm_ref, cm_ref, xde_ref, pre_ref, ea_ref, zg_ref, cd_ref, gn_ref, st_ref,
                           y_ref, ns_ref):
    bb = st_ref.shape[1]
    blk = pl.program_id(0)

    def dots(b):
        rows = slice(b * steps, (b + 1) * steps)
        bm = bm_ref[rows, :].astype(BF16)
        cm = cm_ref[rows, :].astype(BF16)
        xde = xde_ref[rows, :].astype(BF16)
        inter, new = [], []
        for g in range(SSM_GROUPS):
            srows = slice(g * SSM_GROUP_W, (g + 1) * SSM_GROUP_W)
            lanes = slice(g * SSM_STATE, (g + 1) * SSM_STATE)
            inter.append(_dot_nt(cm[:, lanes], st_ref[0, b, srows, :].astype(BF16)))
            new.append(_dot_tn(xde[:, srows], bm[:, lanes]))
        return inter, new

    def finish(b, inter, new):
        rows = slice(b * steps, (b + 1) * steps)
        y = pre_ref[rows, :] + ea_ref[rows, :] * jnp.concatenate(inter, axis=1)
        y_ref[rows, :] = _rms(y * zg_ref[rows, :], gn_ref[...])
        for h in range(SSM_HEADS):
            g, hh = divmod(h, SSM_HPG)
            cd = cd_ref[blk * bb + b, h]
            r = slice(h * SSM_HEAD_DIM, (h + 1) * SSM_HEAD_DIM)
            ns_ref[b, r, :] = st_ref[0, b, r, :] * cd + new[g][hh * SSM_HEAD_DIM:(hh + 1) * SSM_HEAD_DIM, :]

    pending = dots(0)
    for b in range(bb):
        nxt = dots(b + 1) if b + 1 < bb else None
        finish(b, *pending)
        pending = nxt


def _ssd_sample_state(bm, cm, xde, pre, ea, zg, cd, gn, state, layer, steps, bb):
    bsz = bm.shape[0] // steps
    assert bsz % bb == 0 and (bb * steps) % SUBLANES == 0
    blk = lambda w: pl.BlockSpec((bb * steps, w), lambda i: (i, 0))
    st_in = pl.BlockSpec((1, bb, SSM_WIDTH, SSM_STATE), lambda i: (layer, i, 0, 0))
    st_spec = pl.BlockSpec((bb, SSM_WIDTH, SSM_STATE), lambda i: (i, 0, 0))
    return pl.pallas_call(
        functools.partial(_ssd_sample_state_body, steps),
        grid=(bsz // bb,),
        in_specs=[blk(BC_W), blk(BC_W), blk(SSM_WIDTH), blk(SSM_WIDTH), blk(SSM_WIDTH), blk(SSM_WIDTH),
                  pl.BlockSpec(memory_space=pltpu.SMEM), _const_spec(gn.shape), st_in],
        out_specs=[blk(SSM_WIDTH), st_spec],
        out_shape=[jax.ShapeDtypeStruct((bsz * steps, SSM_WIDTH), F32),
                   jax.ShapeDtypeStruct((bsz, SSM_WIDTH, SSM_STATE), F32)],
        compiler_params=_params(("parallel",)),
        name="ssd_sample_state",
    )(bm, cm, xde, pre, ea, zg, cd, gn, state)


def _alibi_slopes():
    h = np.arange(1, N_ATT_HEADS + 1, dtype=np.float64)
    return np.exp2(-8.0 * h / N_ATT_HEADS).astype(np.float32).reshape(len(ATT_GROUPS), HEADS_PER_GROUP)


def _cache_view(cache):
    depth, bsz, cache_len = cache.shape[:3]
    return jnp.transpose(cache, (0, 1, 3, 4, 5, 2)).reshape(depth, bsz, 2, GROUP_W, cache_len)


def _prev_rows(prev, steps):
    bsz, km1, c = prev.shape
    assert km1 <= steps
    return jnp.pad(prev, ((0, 0), (0, steps - km1), (0, 0))).reshape(bsz * steps, c)


def _layer_weights(lw):
    (norm_mix, q_norm, k_norm, conv_w, conv_b, dt_bias, a_log, d_skip, ssm_norm,
     norm_ffn, ffn_conv_w, ffn_conv_b) = lw
    w = {}
    w["norm_mix"] = norm_mix.reshape(1, D_MODEL)
    w["dtb"] = dt_bias.reshape(1, SSM_HEADS)
    w["gq_g"] = jnp.tile(q_norm, HEADS_PER_GROUP).reshape(1, GROUP_W)
    w["gk_g"] = jnp.tile(k_norm, HEADS_PER_GROUP).reshape(1, GROUP_W)
    w["gq_p"] = jnp.tile(q_norm, 2).reshape(1, PAIR_W)
    w["gk_p"] = jnp.tile(k_norm, 2).reshape(1, PAIR_W)
    w["cw"] = conv_w
    w["cb"] = conv_b.reshape(1, XBC_WIDTH)
    w["alr"] = a_log.reshape(1, SSM_HEADS)
    w["alc"] = a_log.reshape(SSM_HEADS, 1)
    w["dsk"] = jnp.repeat(d_skip, SSM_HEAD_DIM).reshape(1, SSM_WIDTH)
    w["gn"] = ssm_norm.reshape(1, SSM_WIDTH)
    w["norm_ffn"] = norm_ffn.reshape(1, D_MODEL)
    w["fw"] = ffn_conv_w
    w["fb"] = ffn_conv_b.reshape(1, 2 * D_FF)
    return w


def _constants(steps, cache_lens):
    c = {}
    slopes = _alibi_slopes()
    c["bd_g"] = _head_block_diag(GROUP_W)
    c["bd_p"] = _head_block_diag(PAIR_W)
    i = np.arange(SSM_CHUNK)
    c["tri"] = jnp.asarray(i[None, :] <= i[:, None], dtype=BF16)
    lane_head = np.arange(SSM_WIDTH) // SSM_HEAD_DIM
    c["expand"] = jnp.asarray(np.arange(SSM_HEADS)[:, None] == lane_head[None, :], dtype=BF16)
    bc_group = np.arange(BC_W) // SSM_STATE
    c["gexp"] = jnp.asarray(bc_group[:, None] == (lane_head // SSM_HPG)[None, :], dtype=BF16)
    c["hm"] = jnp.asarray(np.arange(HEADS_PER_GROUP)[:, None] == (np.arange(GROUP_W) // HEAD_DIM)[None, :], dtype=F32)
    c["pbias"] = [_prompt_bias(slopes[g], dil) for g, (_, dil) in enumerate(ATT_GROUPS)]
    c["cbias"] = [_sample_cache_bias(slopes[g], dil, steps, cache_lens[g], SUBLANES)
                  for g, (_, dil) in enumerate(ATT_GROUPS)]
    c["nbias"] = jnp.asarray(np.stack([_sample_new_bias(slopes[g], dil, steps, SUBLANES)
                                       for g, (_, dil) in enumerate(ATT_GROUPS)]), dtype=F32)
    return c


def _prompt_layer(x, w, big, layer, c, tm):
    bsz, seq, _ = x.shape
    m = bsz * seq
    qkv, z, xbc, dt = _inproj(x.reshape(m, D_MODEL), w["norm_mix"], big["w_in"], w["dtb"], layer, tm)
    qkv_t = qkv.reshape(QKV_SLABS, bsz, seq, PAIR_W)
    os_, ls_, new_kv = [], [], []
    for g, (win, dil) in enumerate(ATT_GROUPS):
        nbk = max(1, PROMPT_ATT_ROWS // (BAND * dil))
        o, lse, kn = _attn_prompt(qkv_t, g, dil, nbk, c["pbias"][g], w["gq_p"], w["gk_p"], c["bd_p"])
        os_.append(o.reshape(2, m, PAIR_W))
        ls_.append(lse.reshape(2, m, PAIR_W))
        keep = min(win, seq)
        v_g = qkv_t[2 * PAIRS + 2 * g:2 * PAIRS + 2 * g + 2, :, seq - keep:]
        kv = jnp.stack([kn[:, :, seq - keep:], v_g])
        kv = jnp.transpose(kv, (2, 3, 0, 1, 4))
        new_kv.append(kv.reshape(bsz, keep, 2, HEADS_PER_GROUP, HEAD_DIM))
    xbc3 = xbc.reshape(bsz, seq, XBC_WIDTH)
    dt3 = dt.reshape(bsz, seq, SSM_HEADS)
    dtt = jnp.swapaxes(dt3, 1, 2)
    y, h_last = _ssd_prompt(xbc3, z.reshape(bsz, seq, SSM_WIDTH), dt3, dtt, w["cw"], w["cb"], w["alr"], w["alc"],
                            w["dsk"], w["gn"], c["tri"], c["expand"])
    x1 = _outproj(os_, ls_, y.reshape(m, SSM_WIDTH), x.reshape(m, D_MODEL), big["w_out"], layer, tm)
    x2, tail = _ffn_prompt(x1.reshape(bsz, seq, D_MODEL), w["norm_ffn"], big["w_up"], w["fw"], w["fb"],
                           big["w_down"], layer, tm)
    h_last = h_last.reshape(bsz, SSM_HEADS, SSM_HEAD_DIM, SSM_STATE)
    conv_new = xbc3[:, seq - (SSM_CONV - 1):]
    ffn_new = tail[:, SUBLANES - (FFN_CONV - 1):]
    return x2, new_kv, h_last, conv_new, ffn_new


def _sample_layer(x, w, big, c, layer, caches, state, conv_prev, ffn_prev):
    bsz, steps, _ = x.shape
    m = bsz * steps
    assert steps >= SSM_CONV - 1
    xf = x.reshape(m, D_MODEL)
    qkv_t, z, xbc, dt = _inproj(xf, w["norm_mix"], big["w_in"], w["dtb"], layer, m)
    o, lse, kn = _attn_sample(qkv_t, caches, layer, steps, c["cbias"], c["nbias"], w["gq_g"], w["gk_g"],
                              c["bd_g"], c["hm"])
    new_kv = []
    for g in range(len(ATT_GROUPS)):
        v_g = qkv_t[2 * PAIRS + 2 * g:2 * PAIRS + 2 * g + 2]
        kv = jnp.stack([kn[2 * g:2 * g + 2], v_g])
        kv = jnp.transpose(kv, (2, 0, 1, 3))
        new_kv.append(kv.reshape(bsz, steps, 2, HEADS_PER_GROUP, HEAD_DIM))
    os_, ls_ = [o], [lse]

    bm, cm, xde, pre, ea, zg, cd = _ssd_sample_pre(xbc, z, dt, _prev_rows(conv_prev, steps), w["cw"], w["cb"],
                                                   w["alr"], w["dsk"], c["expand"], c["gexp"], steps)
    cd_last = cd.reshape(bsz, steps, SSM_HEADS)[:, steps - 1]
    y, new_state = _ssd_sample_state(bm, cm, xde, pre, ea, zg, cd_last, w["gn"], state, layer, steps, 8)
    x1 = _outproj(os_, ls_, y, xf, big["w_out"], layer, m)
    x2, u_raw = _ffn_sample(x1, steps, w["norm_ffn"], big["w_up"], w["fw"], w["fb"], big["w_down"], layer,
                            _prev_rows(ffn_prev, steps))
    xbc3 = xbc.reshape(bsz, steps, XBC_WIDTH)
    conv_new = jnp.concatenate([conv_prev, xbc3], axis=1)[:, steps:]
    ffn_new = jnp.concatenate([ffn_prev, u_raw.reshape(bsz, steps, 2 * D_FF)], axis=1)[:, steps:]
    new_state = new_state.reshape(bsz, SSM_HEADS, SSM_HEAD_DIM, SSM_STATE)
    return x2.reshape(bsz, steps, D_MODEL), new_kv, new_state, conv_new, ffn_new


def kernel(x_prompt, x_sample, cache_kv0, cache_kv1, cache_kv2, state_ssm, state_conv, state_ffn_conv, norm_mix, w_in, q_norm, k_norm, conv_w, conv_b, dt_bias, a_log, d_skip, ssm_norm, w_out, norm_ffn, w_up, ffn_conv_w, ffn_conv_b, w_down):
    stacked = (norm_mix, q_norm, k_norm, conv_w, conv_b, dt_bias, a_log, d_skip, ssm_norm,
               norm_ffn, ffn_conv_w, ffn_conv_b)
    big = {"w_in": w_in.astype(BF16), "w_out": w_out.astype(BF16), "w_up": w_up.astype(BF16),
           "w_down": w_down.astype(BF16)}
    depth = w_in.shape[0]
    dec_batch, steps = x_sample.shape[:2]
    caches = tuple(_cache_view(cache) for cache in (cache_kv0, cache_kv1, cache_kv2))
    for cache, (_, dil) in zip(caches, ATT_GROUPS):
        assert cache.shape[-1] == BAND * dil
    state = state_ssm.reshape(depth, dec_batch, SSM_WIDTH, SSM_STATE)
    c = _constants(steps, tuple(cache.shape[-1] for cache in caches))
    tm = 512
    assert x_prompt.shape[1] % tm == 0
    y_prompt, y_sample = x_prompt, x_sample
    outs_p = [[] for _ in range(6)]
    outs_s = [[] for _ in range(6)]
    for layer in range(depth):
        w = _layer_weights(tuple(a[layer] for a in stacked))
        y_prompt, kv, h, cv, f = _prompt_layer(y_prompt, w, big, layer, c, tm)
        for lst, val in zip(outs_p, (*kv, h, cv, f)):
            lst.append(val)
        y_sample, kv, h, cv, f = _sample_layer(
            y_sample, w, big, c, layer, caches, state, state_conv[layer], state_ffn_conv[layer])
        for lst, val in zip(outs_s, (*kv, h, cv, f)):
            lst.append(val)
    return (y_prompt, y_sample, *[jnp.stack(l) for l in outs_p], *[jnp.stack(l) for l in outs_s])
```

```python
import functools
import math

import jax
import jax.numpy as jnp
import numpy as np
from jax import lax
from jax.experimental import pallas as pl
from jax.experimental.pallas import tpu as pltpu

F32 = jnp.float32
BF16 = jnp.bfloat16

D_MODEL = 1024
HEAD_DIM = 64
ATT_GROUPS = ((128, 1), (512, 4), (2048, 16))
BAND = 128
HEADS_PER_GROUP = 4
GROUP_W = HEADS_PER_GROUP * HEAD_DIM
N_ATT_HEADS = HEADS_PER_GROUP * len(ATT_GROUPS)
ATT_WIDTH = N_ATT_HEADS * HEAD_DIM
QKV_W = 3 * ATT_WIDTH
PAIR_W = 2 * HEAD_DIM
PAIRS = ATT_WIDTH // PAIR_W
QKV_SLABS = QKV_W // PAIR_W
SSM_HEAD_DIM = 64
SSM_WIDTH = 1024
SSM_HEADS = SSM_WIDTH // SSM_HEAD_DIM
SSM_STATE = 128
SSM_GROUPS = 2
SSM_HPG = SSM_HEADS // SSM_GROUPS
SSM_GROUP_W = SSM_HPG * SSM_HEAD_DIM
SSM_CONV = 4
SSM_CHUNK = 128
BC_W = SSM_GROUPS * SSM_STATE
XBC_WIDTH = SSM_WIDTH + 2 * BC_W
D_FF = 2816
FFN_CONV = 3
FF_CHUNK = 256
EPS = 1e-6
IN_WIDTH = QKV_W + SSM_WIDTH + XBC_WIDTH + SSM_HEADS
SUBLANES = 8
LANES = 128
SAMPLE_ROWS = 128
PROMPT_ATT_ROWS = 512
RESIDUE_UNROLL = 4
VMEM_LIMIT = 56 * 1024 * 1024
NEG_INF = float("-inf")


def _dot(a, b):
    return jnp.dot(a, b, preferred_element_type=F32)


def _dot_nt(a, b):
    return lax.dot_general(a, b, (((1,), (1,)), ((), ())), preferred_element_type=F32)


def _dot_tn(a, b):
    return lax.dot_general(a, b, (((0,), (0,)), ((), ())), preferred_element_type=F32)


def _split(x, pieces):
    out = []
    r = x
    for _ in range(pieces):
        p = r.astype(BF16)
        out.append(p)
        r = r - p.astype(F32)
    return out


def _dot_sel(x, sel, pieces=3):
    acc = None
    for p in _split(x, pieces):
        t = _dot(p, sel)
        acc = t if acc is None else acc + t
    return acc


def _sel_dot(sel, x, pieces=3):
    acc = None
    for p in _split(x, pieces):
        t = _dot(sel, p)
        acc = t if acc is None else acc + t
    return acc


def _sel_dot_nt(x, sel, pieces=3):
    acc = None
    for p in _split(x, pieces):
        t = _dot_nt(p, sel)
        acc = t if acc is None else acc + t
    return acc


def _silu(x):
    return x * jax.nn.sigmoid(x)


def _softplus(x):
    return jnp.maximum(x, 0.0) + jnp.log1p(jnp.exp(-jnp.abs(x)))


def _rms(x, g):
    ms = jnp.mean(x * x, axis=-1, keepdims=True)
    return x * lax.rsqrt(ms + EPS) * g


def _layer_spec(arr, layer):
    nd = arr.ndim
    return pl.BlockSpec((1,) + arr.shape[1:], lambda *_: (layer,) + (0,) * (nd - 1), pipeline_mode=pl.Buffered(1))


def _const_spec(shape):
    nd = len(shape)
    return pl.BlockSpec(shape, lambda *_: (0,) * nd, pipeline_mode=pl.Buffered(1))


def _params(sem):
    return pltpu.CompilerParams(dimension_semantics=sem, vmem_limit_bytes=VMEM_LIMIT)


def _head_block_diag(width):
    h = np.arange(width) // HEAD_DIM
    return jnp.asarray(np.where(h[:, None] == h[None, :], 1.0 / HEAD_DIM, 0.0), dtype=BF16)


def _head_rms(t, g, bd):
    ms = _dot_sel(t * t, bd, pieces=2)
    return t * lax.rsqrt(ms + EPS) * g


def _rows_back(slab_ref, s, d, n):
    return slab_ref[s, pl.ds(SUBLANES - d, n, stride=1), :]


def _inproj_body(x_ref, g_ref, w_ref, dtb_ref, qkv_ref, z_ref, xbc_ref, dt_ref):
    h = _rms(x_ref[...], g_ref[...]).astype(BF16)
    qkv = _dot(h, w_ref[0, :, 0:QKV_W])
    for s in range(QKV_SLABS):
        qkv_ref[s] = qkv[:, s * PAIR_W:(s + 1) * PAIR_W]
    z_ref[...] = _dot(h, w_ref[0, :, QKV_W:QKV_W + SSM_WIDTH])
    xd = _dot(h, w_ref[0, :, QKV_W + SSM_WIDTH:IN_WIDTH])
    xbc_ref[...] = xd[:, :XBC_WIDTH]
    dt_ref[...] = _softplus(xd[:, XBC_WIDTH:] + dtb_ref[...])


def _inproj(x, g, w_in, dtb, layer, tm):
    m = x.shape[0]
    row = lambda w: pl.BlockSpec((tm, w), lambda i: (i, 0))
    widths = (SSM_WIDTH, XBC_WIDTH, SSM_HEADS)
    return pl.pallas_call(
        _inproj_body,
        grid=(m // tm,),
        in_specs=[row(D_MODEL), _const_spec(g.shape), _layer_spec(w_in, layer), _const_spec(dtb.shape)],
        out_specs=[pl.BlockSpec((QKV_SLABS, tm, PAIR_W), lambda i: (0, i, 0))] + [row(w) for w in widths],
        out_shape=[jax.ShapeDtypeStruct((QKV_SLABS, m, PAIR_W), F32)]
                  + [jax.ShapeDtypeStruct((m, w), F32) for w in widths],
        compiler_params=_params(("parallel",)),
        name="inproj",
    )(x, g, w_in, dtb)


def _attn_prompt_body(dil, nbk, q0_ref, q1_ref, k0_ref, k1_ref, v0_ref, v1_ref, bias_ref, gq_ref, gk_ref, bd_ref,
                      o_ref, lse_ref, kn_ref, kv_scr):
    n = pl.program_id(1)
    rows_per = nbk * BAND
    in_refs = ((q0_ref, q1_ref), (k0_ref, k1_ref), (v0_ref, v1_ref))

    @pl.when(n == 0)
    def _():
        kv_scr[...] = jnp.zeros(kv_scr.shape, BF16)

    variant = jnp.minimum(n, 1)
    bd = bd_ref[...]
    gq = gq_ref[...]
    gk = gk_ref[...]
    low = lax.broadcasted_iota(jnp.int32, (BAND, PAIR_W), 1) < HEAD_DIM

    def rows_of(r, start, count):
        return pl.ds(r + start * dil, count, stride=dil) if dil > 1 else pl.ds(start, count)

    def group(residues):
        items = []
        for r in residues:
            for p in range(2):
                q, k, v = (in_refs[kind][p][0, 0, rows_of(r, 0, rows_per), :] for kind in range(3))
                qn = _head_rms(q, gq, bd) * (HEAD_DIM ** -0.5)
                kn = _head_rms(k, gk, bd)
                kall = jnp.concatenate([kv_scr[0, p, r], kn.astype(BF16)], axis=0)
                vall = jnp.concatenate([kv_scr[1, p, r], v.astype(BF16)], axis=0)
                items.append((r, p, qn, kn, kall, vall))
        scores = []
        for r, p, qn, kn, kall, vall in items:
            for j in range(nbk):
                qj = qn[j * BAND:(j + 1) * BAND]
                for hh in range(2):
                    keep = low if hh == 0 else jnp.logical_not(low)
                    qm = jnp.where(keep, qj, 0.0).astype(BF16)
                    bias = bias_ref[variant if j == 0 else 1, 2 * p + hh]
                    scores.append(_dot_nt(qm, kall[j * BAND:(j + 2) * BAND]) + bias)
        probs = []
        for s in scores:
            mx = jnp.max(s, axis=-1, keepdims=True)
            e = jnp.exp(s - mx)
            den = jnp.sum(e, axis=-1, keepdims=True)
            probs.append((e.astype(BF16), den, mx + jnp.log(den)))
        results = []
        it = iter(probs)
        for r, p, qn, kn, kall, vall in items:
            for j in range(nbk):
                (e0, d0, l0), (e1, d1, l1) = next(it), next(it)
                vwin = vall[j * BAND:(j + 2) * BAND]
                o = jnp.where(low, _dot(e0, vwin) / d0, _dot(e1, vwin) / d1)
                results.append((r, p, j, o, jnp.where(low, l0, l1)))
        for r, p, j, o, lse in results:
            o_ref[p, 0, rows_of(r, j * BAND, BAND), :] = o
            lse_ref[p, 0, rows_of(r, j * BAND, BAND), :] = lse
        for r, p, qn, kn, kall, vall in items:
            kn_ref[p, 0, rows_of(r, 0, rows_per), :] = kn
            kv_scr[0, p, r] = kall[rows_per:]
            kv_scr[1, p, r] = vall[rows_per:]

    if dil <= RESIDUE_UNROLL:
        group(range(dil))
    else:
        def step(i, carry):
            group([i * RESIDUE_UNROLL + k for k in range(RESIDUE_UNROLL)])
            return carry
        lax.fori_loop(0, dil // RESIDUE_UNROLL, step, 0)


def _prompt_bias(slopes, dil):
    i = np.arange(BAND)[:, None]
    j = np.arange(2 * BAND)[None, :]
    step = i + BAND - j
    valid = (step >= 0) & (step <= BAND)
    bias = -slopes[:, None, None] * (step * dil).astype(np.float32)
    full = np.where(valid[None], bias, NEG_INF)
    first = np.where((valid & (j >= BAND))[None], bias, NEG_INF)
    return jnp.asarray(np.stack([first, full]), dtype=F32)


def _attn_prompt(qkv_t, g, dil, nbk, bias, gq, gk, bd):
    _, bsz, seq, _ = qkv_t.shape
    blk = nbk * BAND * dil
    assert seq % blk == 0
    slab = lambda s: pl.BlockSpec((1, 1, blk, PAIR_W), lambda b, n: (s, b, n, 0))
    ospec = pl.BlockSpec((2, 1, blk, PAIR_W), lambda b, n: (0, b, n, 0))
    first = [kind * PAIRS + 2 * g for kind in range(3)]
    return pl.pallas_call(
        functools.partial(_attn_prompt_body, dil, nbk),
        grid=(bsz, seq // blk),
        in_specs=[slab(first[0]), slab(first[0] + 1), slab(first[1]), slab(first[1] + 1),
                  slab(first[2]), slab(first[2] + 1),
                  _const_spec(bias.shape), _const_spec(gq.shape), _const_spec(gk.shape), _const_spec(bd.shape)],
        out_specs=[ospec, ospec, ospec],
        out_shape=[jax.ShapeDtypeStruct((2, bsz, seq, PAIR_W), F32)] * 3,
        scratch_shapes=[pltpu.VMEM((2, 2, dil, BAND, PAIR_W), BF16)],
        compiler_params=_params(("parallel", "arbitrary")),
        name=f"attn_prompt_g{g}",
    )(*([qkv_t] * 6), bias, gq, gk, bd)


def _ssd_stage_conv(xbc_ref, conv_scr):
    raw = xbc_ref[0]
    for s in range(XBC_WIDTH // LANES):
        conv_scr[s, SUBLANES:, :] = raw[:, s * LANES:(s + 1) * LANES]
    return raw


def _ssd_chunk(raw, z_ref, dt_ref, dtt_ref, cw_ref, cb_ref, alr_ref, alc_ref, dsk_ref, gn_ref,
               tri_ref, exp_ref, conv_scr, state_scr):
    q = SSM_CHUNK

    cw = cw_ref[...]
    z = z_ref[0]
    state = state_scr[...]
    n_slabs = XBC_WIDTH // LANES
    parts = []
    for s in range(n_slabs):
        lanes = slice(s * LANES, (s + 1) * LANES)
        acc = _rows_back(conv_scr, s, 3, q) * cw[0:1, lanes]
        acc = acc + _rows_back(conv_scr, s, 2, q) * cw[1:2, lanes]
        acc = acc + _rows_back(conv_scr, s, 1, q) * cw[2:3, lanes]
        parts.append(_silu(acc + raw[:, lanes] * cw[3:4, lanes] + cb_ref[:, lanes]))
        if s % 3 == 2:
            yield
    xc = jnp.concatenate(parts, axis=1)
    xs = xc[:, :SSM_WIDTH]
    bm = xc[:, SSM_WIDTH:SSM_WIDTH + BC_W]
    cm = xc[:, SSM_WIDTH + BC_W:]

    tri = tri_ref[...]
    dt = dt_ref[0]
    acum = _sel_dot(tri, dt * (-jnp.exp(alr_ref[...])))
    acum_t = _sel_dot_nt(dtt_ref[0] * (-jnp.exp(alc_ref[...])), tri)
    yield
    last = acum[q - 1:q, :]
    expand = exp_ref[...]
    dt_x = _dot_sel(dt, expand)
    ea_x = _dot_sel(jnp.exp(acum), expand)
    de_x = _dot_sel(jnp.exp(last - acum), expand)
    xdt = xs * dt_x
    xde = (xdt * de_x).astype(BF16)
    xdt_b = xdt.astype(BF16)
    causal = lax.broadcasted_iota(jnp.int32, (q, q), 0) >= lax.broadcasted_iota(jnp.int32, (q, q), 1)
    low = lax.broadcasted_iota(jnp.int32, (q, 2 * SSM_HEAD_DIM), 1) < SSM_HEAD_DIM

    y_parts = []
    new_state = []
    for g in range(SSM_GROUPS):
        bm_g = bm[:, g * SSM_STATE:(g + 1) * SSM_STATE].astype(BF16)
        cm_g = cm[:, g * SSM_STATE:(g + 1) * SSM_STATE].astype(BF16)
        cb = _dot_nt(cm_g, bm_g)
        rows = slice(g * SSM_GROUP_W, (g + 1) * SSM_GROUP_W)
        st_g = state[rows, :]
        inter = _dot_nt(cm_g, st_g.astype(BF16))
        for hp in range(SSM_HPG // 2):
            pair = None
            for k in range(2):
                h = g * SSM_HPG + 2 * hp + k
                seg = acum[:, h:h + 1] - acum_t[h:h + 1, :]
                decay = jnp.exp(jnp.where(causal, seg, NEG_INF))
                gmat = (cb * decay).astype(BF16)
                lanes = slice((2 * hp) * SSM_HEAD_DIM + g * SSM_GROUP_W,
                              (2 * hp + 2) * SSM_HEAD_DIM + g * SSM_GROUP_W)
                x_pair = xdt_b[:, lanes]
                keep = low if k == 0 else jnp.logical_not(low)
                t = _dot(gmat, jnp.where(keep, x_pair, jnp.zeros_like(x_pair)))
                pair = t if pair is None else pair + t
            y_parts.append(pair + ea_x[:, lanes] * inter[:, lanes.start - g * SSM_GROUP_W:lanes.stop - g * SSM_GROUP_W])
            yield
        new = _dot_tn(xde[:, rows], bm_g)
        for hh in range(SSM_HPG):
            h = g * SSM_HPG + hh
            cd = jnp.exp(acum_t[h:h + 1, q - 1:q])
            r = slice(h * SSM_HEAD_DIM, (h + 1) * SSM_HEAD_DIM)
            rl = slice(hh * SSM_HEAD_DIM, (hh + 1) * SSM_HEAD_DIM)
            new_state.append((r, st_g[rl, :] * cd + new[rl, :]))

    yield
    y = jnp.concatenate(y_parts, axis=1) + dsk_ref[...] * xs
    y = y * _silu(z)
    return _rms(y, gn_ref[...]), new_state


def _ssd_store(raw, y, new_state, y_ref, conv_scr, state_scr):
    y_ref[0] = y
    for r, val in new_state:
        state_scr[r, :] = val
    for s in range(XBC_WIDTH // LANES):
        conv_scr[s, 0:SUBLANES, :] = raw[SSM_CHUNK - SUBLANES:, s * LANES:(s + 1) * LANES]


def _interleave(*gens):
    results = [None] * len(gens)
    live = list(range(len(gens)))
    while live:
        for i in list(live):
            try:
                next(gens[i])
            except StopIteration as done:
                results[i] = done.value
                live.remove(i)
    return results


def _ssd_attn_body(steps, dils, n_ssd, *refs):
    ssd_in, rest = refs[:n_ssd], refs[n_ssd:]
    (qkv_ref, kv0_ref, kv1_ref, kv2_ref, cb0_ref, cb1_ref, cb2_ref, nb_ref, gq_ref, gk_ref, bd_ref, hm_ref,
     y_ref, hl_ref, o_ref, lse_ref, kn_ref, conv_scr, state_scr) = rest
    c = pl.program_id(1)
    n_b = SUBLANES // steps
    b = (pl.program_id(0) * pl.num_programs(1) + c) % n_b

    @pl.when(c == 0)
    def _():
        conv_scr[:, 0:SUBLANES, :] = jnp.zeros((conv_scr.shape[0], SUBLANES, LANES), F32)
        state_scr[...] = jnp.zeros_like(state_scr)

    raw = _ssd_stage_conv(ssd_in[0], conv_scr)
    sample, (y, new_state) = _interleave(
        _sample_attn_batch(steps, dils, b, qkv_ref, (kv0_ref, kv1_ref, kv2_ref), (cb0_ref, cb1_ref, cb2_ref),
                           nb_ref, gq_ref, gk_ref, bd_ref, hm_ref),
        _ssd_chunk(raw, *ssd_in[1:], conv_scr, state_scr))
    _ssd_store(raw, y, new_state, y_ref, conv_scr, state_scr)

    mine = lax.broadcasted_iota(jnp.int32, (SUBLANES, PAIR_W), 0) // steps == b

    @pl.when(b == 0)
    def _():
        for g, (o_g, l_g, kn) in enumerate(sample):
            for p in range(2):
                lanes = slice(p * PAIR_W, (p + 1) * PAIR_W)
                o_ref[2 * g + p] = jnp.where(mine, o_g[:, lanes], 0.0)
                lse_ref[2 * g + p] = jnp.where(mine, l_g[:, lanes], 0.0)
                kn_ref[2 * g + p] = kn[:, lanes]

    @pl.when(b > 0)
    def _():
        for g, (o_g, l_g, kn) in enumerate(sample):
            for p in range(2):
                lanes = slice(p * PAIR_W, (p + 1) * PAIR_W)
                o_ref[2 * g + p] = jnp.where(mine, o_g[:, lanes], o_ref[2 * g + p])
                lse_ref[2 * g + p] = jnp.where(mine, l_g[:, lanes], lse_ref[2 * g + p])

    @pl.when(c == pl.num_programs(1) - 1)
    def _():
        hl_ref[0] = state_scr[...]


def _ssd_prompt_attn_sample(xbc, z, dt, dtt, ssd_consts, qkv_t, caches, layer, steps, att_consts):
    bsz, seq, _ = xbc.shape
    assert seq % SSM_CHUNK == 0
    nc = seq // SSM_CHUNK
    m = qkv_t.shape[1]
    n_b = SUBLANES // steps
    assert SUBLANES % steps == 0 and m == bsz * nc * steps
    dils = tuple(d for _, d in ATT_GROUPS)
    blk = lambda w: pl.BlockSpec((1, SSM_CHUNK, w), lambda b, c: (b, c, 0))
    tile = lambda b, c: (b * nc + c) // n_b
    kv_specs = [pl.BlockSpec((1, n_b) + cache.shape[2:], lambda b, c: (layer, tile(b, c), 0, 0, 0))
                for cache in caches]
    ospec = pl.BlockSpec((PAIRS, SUBLANES, PAIR_W), lambda b, c: (0, tile(b, c), 0))
    ssd_in = [xbc, z, dt, dtt, *ssd_consts]
    return pl.pallas_call(
        functools.partial(_ssd_attn_body, steps, dils, len(ssd_in)),
        grid=(bsz, nc),
        in_specs=[blk(XBC_WIDTH), blk(SSM_WIDTH), blk(SSM_HEADS),
                  pl.BlockSpec((1, SSM_HEADS, SSM_CHUNK), lambda b, c: (b, 0, c))]
                 + [_const_spec(a.shape) for a in ssd_consts]
                 + [pl.BlockSpec((QKV_SLABS, SUBLANES, PAIR_W), lambda b, c: (0, tile(b, c), 0))] + kv_specs
                 + [_const_spec(a.shape) for a in att_consts],
        out_specs=[blk(SSM_WIDTH), pl.BlockSpec((1, SSM_WIDTH, SSM_STATE), lambda b, c: (b, 0, 0)),
                   ospec, ospec, ospec],
        out_shape=[jax.ShapeDtypeStruct((bsz, seq, SSM_WIDTH), F32),
                   jax.ShapeDtypeStruct((bsz, SSM_WIDTH, SSM_STATE), F32)]
                  + [jax.ShapeDtypeStruct((PAIRS, m, PAIR_W), F32)] * 3,
        scratch_shapes=[pltpu.VMEM((XBC_WIDTH // LANES, SUBLANES + SSM_CHUNK, LANES), F32),
                        pltpu.VMEM((SSM_WIDTH, SSM_STATE), F32)],
        compiler_params=_params(("arbitrary", "arbitrary")),
        name="ssd_prompt_attn_sample",
    )(*ssd_in, qkv_t, *caches, *att_consts)


def _outproj_body(slabs, *refs):
    n_arr = len(slabs)
    o_refs, l_refs = refs[:n_arr], refs[n_arr:2 * n_arr]
    y_ref, x_ref, w_ref, out_ref = refs[2 * n_arr:]
    o_sl = [r[s] for r, cnt in zip(o_refs, slabs) for s in range(cnt)]
    l_sl = [r[s] for r, cnt in zip(l_refs, slabs) for s in range(cnt)]
    ngrp = len(ATT_GROUPS)
    att = [None] * PAIRS
    for p in range(PAIRS // ngrp):
        ls = [l_sl[2 * g + p] for g in range(ngrp)]
        mx = functools.reduce(jnp.maximum, ls)
        es = [jnp.exp(l - mx) for l in ls]
        inv = 1.0 / functools.reduce(jnp.add, es)
        for g in range(ngrp):
            att[2 * g + p] = (o_sl[2 * g + p] * (es[g] * inv)).astype(BF16)
    acc = x_ref[...] + _dot(y_ref[...].astype(BF16), w_ref[0, ATT_WIDTH:, :])
    out_ref[...] = acc + _dot(jnp.concatenate(att, axis=1), w_ref[0, 0:ATT_WIDTH, :])


def _outproj(os_, ls_, y, x, w_out, layer, tm):
    m = x.shape[0]
    slabs = tuple(a.shape[0] for a in os_)
    assert sum(slabs) == PAIRS
    row = lambda w: pl.BlockSpec((tm, w), lambda i: (i, 0))
    slab_specs = [pl.BlockSpec((n, tm, PAIR_W), lambda i: (0, i, 0)) for n in slabs]
    return pl.pallas_call(
        functools.partial(_outproj_body, slabs),
        grid=(m // tm,),
        in_specs=slab_specs * 2 + [row(SSM_WIDTH), row(D_MODEL), _layer_spec(w_out, layer)],
        out_specs=row(D_MODEL),
        out_shape=jax.ShapeDtypeStruct((m, D_MODEL), F32),
        compiler_params=_params(("parallel",)),
        name="outproj",
    )(*os_, *ls_, y, x, w_out)


def _ffn_chunk(h, c, wu_ref, fw_ref, fb_ref, shifted):
    halves = []
    for base in (0, D_FF):
        cols = slice(base + c * FF_CHUNK, base + (c + 1) * FF_CHUNK)
        u = _dot(h, wu_ref[0, :, cols])
        s2, s1 = shifted(u, cols)
        w = fw_ref[:, cols]
        halves.append(s2 * w[0:1] + s1 * w[1:2] + u * w[2:3] + fb_ref[:, cols])
    return (_silu(halves[0]) * halves[1]).astype(BF16)


def _ffn_prompt_body(x_ref, g_ref, wu_ref, fw_ref, fb_ref, wd_ref, out_ref, tail_ref, u_scr, act_scr):
    i = pl.program_id(1)
    n_slabs = u_scr.shape[0]

    @pl.when(i == 0)
    def _():
        u_scr[:, 0:SUBLANES, :] = jnp.zeros((n_slabs, SUBLANES, LANES), F32)

    x = x_ref[0]
    tm = x.shape[0]
    h = _rms(x, g_ref[...]).astype(BF16)

    def shifted(u, cols):
        s2, s1 = [], []
        for k in range(FF_CHUNK // LANES):
            s = cols.start // LANES + k
            u_scr[s, SUBLANES:, :] = u[:, k * LANES:(k + 1) * LANES]
            s2.append(_rows_back(u_scr, s, 2, tm))
            s1.append(_rows_back(u_scr, s, 1, tm))
        return jnp.concatenate(s2, axis=1), jnp.concatenate(s1, axis=1)

    for c in range(D_FF // FF_CHUNK):
        act_scr[:, c * FF_CHUNK:(c + 1) * FF_CHUNK] = _ffn_chunk(h, c, wu_ref, fw_ref, fb_ref, shifted)
    out_ref[0] = x + _dot(act_scr[...], wd_ref[0])
    u_scr[:, 0:SUBLANES, :] = u_scr[:, tm:tm + SUBLANES, :]

    @pl.when(i == pl.num_programs(1) - 1)
    def _():
        for s in range(n_slabs):
            tail_ref[0, :, s * LANES:(s + 1) * LANES] = u_scr[s, 0:SUBLANES, :]


def _ffn_prompt(x, g, wu, fw, fb, wd, layer, tm):
    bsz, seq, _ = x.shape
    return pl.pallas_call(
        _ffn_prompt_body,
        grid=(bsz, seq // tm),
        in_specs=[pl.BlockSpec((1, tm, D_MODEL), lambda b, i: (b, i, 0)), _const_spec(g.shape),
                  _layer_spec(wu, layer), _const_spec(fw.shape), _const_spec(fb.shape), _layer_spec(wd, layer)],
        out_specs=[pl.BlockSpec((1, tm, D_MODEL), lambda b, i: (b, i, 0)),
                   pl.BlockSpec((1, SUBLANES, 2 * D_FF), lambda b, i: (b, 0, 0))],
        out_shape=[jax.ShapeDtypeStruct((bsz, seq, D_MODEL), F32),
                   jax.ShapeDtypeStruct((bsz, SUBLANES, 2 * D_FF), F32)],
        scratch_shapes=[pltpu.VMEM((2 * D_FF // LANES, SUBLANES + tm, LANES), F32), pltpu.VMEM((tm, D_FF), BF16)],
        compiler_params=_params(("parallel", "arbitrary")),
        name="ffn_prompt",
    )(x, g, wu, fw, fb, wd)


def _ffn_sample_body(steps, x_ref, g_ref, wu_ref, fw_ref, fb_ref, wd_ref, prev_ref, out_ref, u_ref):
    x = x_ref[...]
    rows = x.shape[0]
    h = _rms(x, g_ref[...]).astype(BF16)
    t = lax.broadcasted_iota(jnp.int32, (rows, FF_CHUNK), 0) % steps

    def shifted(u, cols):
        u_ref[:, cols] = u
        prev = prev_ref[:, cols]
        s1 = jnp.where(t >= 1, pltpu.roll(u, 1, axis=0), pltpu.roll(prev, rows - 1, axis=0))
        s2 = jnp.where(t >= 2, pltpu.roll(u, 2, axis=0), prev)
        return s2, s1

    act = [_ffn_chunk(h, c, wu_ref, fw_ref, fb_ref, shifted) for c in range(D_FF // FF_CHUNK)]
    out_ref[...] = x + _dot(jnp.concatenate(act, axis=1), wd_ref[0])


def _ffn_sample(x, steps, g, wu, fw, fb, wd, layer, prev):
    m = x.shape[0]
    tr = min(SAMPLE_ROWS, m)
    assert m % tr == 0 and tr % steps == 0
    row = lambda w: pl.BlockSpec((tr, w), lambda i: (i, 0))
    return pl.pallas_call(
        functools.partial(_ffn_sample_body, steps),
        grid=(m // tr,),
        in_specs=[row(D_MODEL), _const_spec(g.shape), _layer_spec(wu, layer), _const_spec(fw.shape),
                  _const_spec(fb.shape), _layer_spec(wd, layer), row(2 * D_FF)],
        out_specs=[row(D_MODEL), row(2 * D_FF)],
        out_shape=[jax.ShapeDtypeStruct((m, D_MODEL), F32), jax.ShapeDtypeStruct((m, 2 * D_FF), F32)],
        compiler_params=_params(("parallel",)),
        name="ffn_sample",
    )(x, g, wu, fw, fb, wd, prev)


def _sample_attn_batch(steps, dils, b, qkv_ref, kv_refs, cb_refs, nb_ref, gq_ref, gk_ref, bd_ref, hm_ref):
    rows = qkv_ref.shape[1]
    assert rows == 2 * steps
    bd = bd_ref[...]
    hm = hm_ref[...]
    results = []
    for g in range(len(dils)):
        slab = lambda kind: jnp.concatenate([qkv_ref[kind * PAIRS + 2 * g], qkv_ref[kind * PAIRS + 2 * g + 1]], axis=1)
        qn = _head_rms(slab(0), gq_ref[...], bd) * (HEAD_DIM ** -0.5)
        kn = _head_rms(slab(1), gk_ref[...], bd)
        vn = slab(2)
        kn_b = jnp.where(b == 0, kn[0:steps], kn[steps:])
        vn_b = jnp.where(b == 0, vn[0:steps], vn[steps:])
        qbd = jnp.concatenate([qn * hm[h:h + 1] for h in range(HEADS_PER_GROUP)], axis=0)
        yield
        kc = kv_refs[g][0, b, 0].astype(BF16)
        s_c = _dot(qbd.astype(BF16), kc) + cb_refs[g][...]
        mx = jnp.max(s_c, axis=-1, keepdims=True)
        yield
        s_n = []
        for t2 in range(steps):
            sn = jnp.sum(qbd * kn_b[t2:t2 + 1], axis=-1, keepdims=True) + nb_ref[g, t2]
            s_n.append(sn)
            mx = jnp.maximum(mx, sn)
        e_c = jnp.exp(s_c - mx)
        den = jnp.sum(e_c, axis=-1, keepdims=True)
        yield
        vc = kv_refs[g][0, b, 1].astype(BF16)
        acc = _dot_nt(e_c.astype(BF16), vc)
        yield
        for t2, sn in enumerate(s_n):
            e_n = jnp.exp(sn - mx)
            den = den + e_n
            acc = acc + e_n * vn_b[t2:t2 + 1]
        o_all = acc / den
        l_all = mx + jnp.log(den)
        o_g = None
        l_g = None
        for h in range(HEADS_PER_GROUP):
            o_h = o_all[h * rows:(h + 1) * rows] * hm[h:h + 1]
            l_h = l_all[h * rows:(h + 1) * rows] * hm[h:h + 1]
            o_g = o_h if o_g is None else o_g + o_h
            l_g = l_h if l_g is None else l_g + l_h
        results.append((o_g, l_g, kn))
        yield
    return results


def _sample_cache_bias(slopes, dil, steps, cache_len, rows):
    t = (np.arange(rows) % steps)[None, :, None]
    c = np.arange(cache_len)[None, None, :]
    dist = cache_len + t - c
    j = dist // dil
    valid = (dist % dil == 0) & (j >= 1) & (j <= BAND)
    bias = -slopes[:, None, None] * dist.astype(np.float32)
    return jnp.asarray(np.where(valid, bias, NEG_INF).reshape(slopes.shape[0] * rows, cache_len), dtype=F32)


def _sample_new_bias(slopes, dil, steps, rows):
    t2 = np.arange(steps)[:, None, None]
    t = (np.arange(rows) % steps)[None, None, :]
    dist = t - t2
    valid = (dist >= 0) & (dist % dil == 0) & (dist // dil <= BAND)
    bias = -slopes[None, :, None] * dist.astype(np.float32)
    return np.where(valid, bias, NEG_INF).reshape(steps, slopes.shape[0] * rows, 1)


def _ssd_sample_pre_body(steps, xbc_ref, z_ref, dt_ref, prev_ref, cw_ref, cb_ref, al_ref, dsk_ref,
                         exp_ref, gexp_ref, bm_ref, cm_ref, xde_ref, pre_ref, ea_ref, zg_ref, cd_ref):
    raw = xbc_ref[...]
    rows = raw.shape[0]
    tw = lax.broadcasted_iota(jnp.int32, (rows, XBC_WIDTH), 0) % steps
    cw = cw_ref[...]
    prev = prev_ref[...]
    acc = jnp.where(tw >= 3, pltpu.roll(raw, 3, axis=0), prev) * cw[0:1]
    acc = acc + jnp.where(tw >= 2, pltpu.roll(raw, 2, axis=0), pltpu.roll(prev, rows - 1, axis=0)) * cw[1:2]
    acc = acc + jnp.where(tw >= 1, pltpu.roll(raw, 1, axis=0), pltpu.roll(prev, rows - 2, axis=0)) * cw[2:3]
    acc = acc + raw * cw[3:4]
    xc = _silu(acc + cb_ref[...])
    xs = xc[:, :SSM_WIDTH]
    bm = xc[:, SSM_WIDTH:SSM_WIDTH + BC_W]
    cm = xc[:, SSM_WIDTH + BC_W:]
    bm_ref[...] = bm
    cm_ref[...] = cm

    dt = dt_ref[...]
    da = dt * (-jnp.exp(al_ref[...]))
    th = lax.broadcasted_iota(jnp.int32, (rows, SSM_HEADS), 0) % steps
    acum = da
    for d in range(1, steps):
        acum = acum + jnp.where(th >= d, pltpu.roll(da, d, axis=0), 0.0)
    tail = jnp.zeros_like(da)
    for d in range(1, steps):
        tail = tail + jnp.where(th + d < steps, pltpu.roll(da, rows - d, axis=0), 0.0)
    expand = exp_ref[...]
    dt_x = _dot_sel(dt, expand)
    acum_x = _dot_sel(acum, expand)
    ea_ref[...] = jnp.exp(acum_x)
    xdt = xs * dt_x
    xde_ref[...] = xdt * _dot_sel(jnp.exp(tail), expand)
    cd_ref[...] = jnp.exp(acum)

    ts = lax.broadcasted_iota(jnp.int32, (rows, SSM_WIDTH), 0) % steps
    gexp = gexp_ref[...]
    y = dsk_ref[...] * xs
    for d in range(steps):
        if d == 0:
            bm_d, xdt_d, ac_d = bm, xdt, acum_x
        else:
            bm_d = pltpu.roll(bm, d, axis=0)
            xdt_d = pltpu.roll(xdt, d, axis=0)
            ac_d = pltpu.roll(acum_x, d, axis=0)
        cb_x = _dot_sel(cm * bm_d, gexp, pieces=2)
        term = cb_x * jnp.exp(acum_x - ac_d) * xdt_d
        y = y + jnp.where(ts >= d, term, 0.0)
    pre_ref[...] = y
    zg_ref[...] = _silu(z_ref[...])


def _ssd_sample_pre(xbc, z, dt, prev, cw, cb, al, dsk, expand, gexp, steps):
    m = xbc.shape[0]
    tr = min(SAMPLE_ROWS, m)
    assert m % tr == 0 and tr % steps == 0
    row = lambda w: pl.BlockSpec((tr, w), lambda i: (i, 0))
    consts = (cw, cb, al, dsk, expand, gexp)
    widths = (BC_W, BC_W, SSM_WIDTH, SSM_WIDTH, SSM_WIDTH, SSM_WIDTH, SSM_HEADS)
    return pl.pallas_call(
        functools.partial(_ssd_sample_pre_body, steps),
        grid=(m // tr,),
        in_specs=[row(XBC_WIDTH), row(SSM_WIDTH), row(SSM_HEADS), row(XBC_WIDTH)]
                 + [_const_spec(a.shape) for a in consts],
        out_specs=[row(w) for w in widths],
        out_shape=[jax.ShapeDtypeStruct((m, w), F32) for w in widths],
        compiler_params=_params(("parallel",)),
        name="ssd_sample_pre",
    )(xbc, z, dt, prev, *consts)


def _ssd_sample_state_body(steps, bm_ref, cm_ref, xde_ref, pre_ref, ea_ref, zg_ref, cd_ref, gn_ref, st_ref,
                           *rest):
    y_ref, ns_ref = rest[-2:]
    bb = st_ref.shape[1]
    blk = pl.program_id(1)
    last_pass = pl.num_programs(0) - 1

    @pl.when(pl.program_id(0) < last_pass)
    def _():
        ns_ref[...] = jnp.zeros(ns_ref.shape, F32)

    @pl.when(pl.program_id(0) == last_pass)
    def _():
        _ssd_sample_state_step(steps, bb, blk, bm_ref, cm_ref, xde_ref, pre_ref, ea_ref, zg_ref, cd_ref, gn_ref,
                               st_ref, y_ref, ns_ref)


def _ssd_sample_state_step(steps, bb, blk, bm_ref, cm_ref, xde_ref, pre_ref, ea_ref, zg_ref, cd_ref, gn_ref,
                           st_ref, y_ref, ns_ref):

    def dots(b):
        rows = slice(b * steps, (b + 1) * steps)
        bm = bm_ref[rows, :].astype(BF16)
        cm = cm_ref[rows, :].astype(BF16)
        xde = xde_ref[rows, :].astype(BF16)
        inter, new = [], []
        for g in range(SSM_GROUPS):
            srows = slice(g * SSM_GROUP_W, (g + 1) * SSM_GROUP_W)
            lanes = slice(g * SSM_STATE, (g + 1) * SSM_STATE)
            inter.append(_dot_nt(cm[:, lanes], st_ref[0, b, srows, :].astype(BF16)))
            new.append(_dot_tn(xde[:, srows], bm[:, lanes]))
        return inter, new

    def finish(b, inter, new):
        rows = slice(b * steps, (b + 1) * steps)
        y = pre_ref[rows, :] + ea_ref[rows, :] * jnp.concatenate(inter, axis=1)
        y_ref[rows, :] = _rms(y * zg_ref[rows, :], gn_ref[...])
        for h in range(SSM_HEADS):
            g, hh = divmod(h, SSM_HPG)
            cd = cd_ref[blk * bb + b, h]
            r = slice(h * SSM_HEAD_DIM, (h + 1) * SSM_HEAD_DIM)
            ns_ref[0, b, r, :] = st_ref[0, b, r, :] * cd + new[g][hh * SSM_HEAD_DIM:(hh + 1) * SSM_HEAD_DIM, :]

    pending = dots(0)
    for b in range(bb):
        nxt = dots(b + 1) if b + 1 < bb else None
        finish(b, *pending)
        pending = nxt


def _ssd_sample_state(bm, cm, xde, pre, ea, zg, cd, gn, state, layer, steps, bb, new_states):
    bsz = bm.shape[0] // steps
    depth = state.shape[0]
    assert bsz % bb == 0 and (bb * steps) % SUBLANES == 0
    passes = depth if new_states is None else 1
    row_blk = lambda p, i: jnp.where(p == passes - 1, i, 0)
    blk = lambda w: pl.BlockSpec((bb * steps, w), lambda p, i: (row_blk(p, i), 0))
    st_in = pl.BlockSpec((1, bb, SSM_WIDTH, SSM_STATE), lambda p, i: (layer, row_blk(p, i), 0, 0))
    st_out = pl.BlockSpec((1, bb, SSM_WIDTH, SSM_STATE), lambda p, i: ((layer + 1 + p) % passes if passes > 1
                                                                          else layer, i, 0, 0))
    ins = [bm, cm, xde, pre, ea, zg, cd, gn, state]
    in_specs = [blk(BC_W), blk(BC_W), blk(SSM_WIDTH), blk(SSM_WIDTH), blk(SSM_WIDTH), blk(SSM_WIDTH),
                pl.BlockSpec(memory_space=pltpu.SMEM), _const_spec(gn.shape), st_in]
    aliases = {}
    if new_states is not None:
        aliases = {len(ins): 1}
        ins.append(new_states)
        in_specs.append(pl.BlockSpec(memory_space=pl.ANY))
    return pl.pallas_call(
        functools.partial(_ssd_sample_state_body, steps),
        grid=(passes, bsz // bb),
        in_specs=in_specs,
        out_specs=[blk(SSM_WIDTH), st_out],
        out_shape=[jax.ShapeDtypeStruct((bsz * steps, SSM_WIDTH), F32),
                   jax.ShapeDtypeStruct(state.shape, F32)],
        input_output_aliases=aliases,
        compiler_params=_params(("arbitrary", "arbitrary")),
        name="ssd_sample_state",
    )(*ins)


def _alibi_slopes():
    h = np.arange(1, N_ATT_HEADS + 1, dtype=np.float64)
    return np.exp2(-8.0 * h / N_ATT_HEADS).astype(np.float32).reshape(len(ATT_GROUPS), HEADS_PER_GROUP)


def _cache_view(cache):
    depth, bsz, cache_len = cache.shape[:3]
    return jnp.transpose(cache, (0, 1, 3, 4, 5, 2)).reshape(depth, bsz, 2, GROUP_W, cache_len)


def _prev_rows(prev, steps):
    bsz, km1, c = prev.shape
    assert km1 <= steps
    return jnp.pad(prev, ((0, 0), (0, steps - km1), (0, 0))).reshape(bsz * steps, c)


def _layer_weights(lw):
    (norm_mix, q_norm, k_norm, conv_w, conv_b, dt_bias, a_log, d_skip, ssm_norm,
     norm_ffn, ffn_conv_w, ffn_conv_b) = lw
    w = {}
    w["norm_mix"] = norm_mix.reshape(1, D_MODEL)
    w["dtb"] = dt_bias.reshape(1, SSM_HEADS)
    w["gq_g"] = jnp.tile(q_norm, HEADS_PER_GROUP).reshape(1, GROUP_W)
    w["gk_g"] = jnp.tile(k_norm, HEADS_PER_GROUP).reshape(1, GROUP_W)
    w["gq_p"] = jnp.tile(q_norm, 2).reshape(1, PAIR_W)
    w["gk_p"] = jnp.tile(k_norm, 2).reshape(1, PAIR_W)
    w["cw"] = conv_w
    w["cb"] = conv_b.reshape(1, XBC_WIDTH)
    w["alr"] = a_log.reshape(1, SSM_HEADS)
    w["alc"] = a_log.reshape(SSM_HEADS, 1)
    w["dsk"] = jnp.repeat(d_skip, SSM_HEAD_DIM).reshape(1, SSM_WIDTH)
    w["gn"] = ssm_norm.reshape(1, SSM_WIDTH)
    w["norm_ffn"] = norm_ffn.reshape(1, D_MODEL)
    w["fw"] = ffn_conv_w
    w["fb"] = ffn_conv_b.reshape(1, 2 * D_FF)
    return w


def _constants(steps, cache_lens):
    c = {}
    slopes = _alibi_slopes()
    c["bd_g"] = _head_block_diag(GROUP_W)
    c["bd_p"] = _head_block_diag(PAIR_W)
    i = np.arange(SSM_CHUNK)
    c["tri"] = jnp.asarray(i[None, :] <= i[:, None], dtype=BF16)
    lane_head = np.arange(SSM_WIDTH) // SSM_HEAD_DIM
    c["expand"] = jnp.asarray(np.arange(SSM_HEADS)[:, None] == lane_head[None, :], dtype=BF16)
    bc_group = np.arange(BC_W) // SSM_STATE
    c["gexp"] = jnp.asarray(bc_group[:, None] == (lane_head // SSM_HPG)[None, :], dtype=BF16)
    c["hm"] = jnp.asarray(np.arange(HEADS_PER_GROUP)[:, None] == (np.arange(GROUP_W) // HEAD_DIM)[None, :], dtype=F32)
    c["pbias"] = [_prompt_bias(slopes[g], dil) for g, (_, dil) in enumerate(ATT_GROUPS)]
    c["cbias"] = [_sample_cache_bias(slopes[g], dil, steps, cache_lens[g], SUBLANES)
                  for g, (_, dil) in enumerate(ATT_GROUPS)]
    c["nbias"] = jnp.asarray(np.stack([_sample_new_bias(slopes[g], dil, steps, SUBLANES)
                                       for g, (_, dil) in enumerate(ATT_GROUPS)]), dtype=F32)
    return c


def _layer(xp, xs, w, big, c, layer, caches, state, new_states, conv_prev, ffn_prev, tm):
    bsz, seq, _ = xp.shape
    m = bsz * seq
    dec, steps, _ = xs.shape
    ms = dec * steps
    assert steps >= SSM_CONV - 1
    xp2 = xp.reshape(m, D_MODEL)
    xs2 = xs.reshape(ms, D_MODEL)
    qkv, z, xbc, dt = _inproj(xp2, w["norm_mix"], big["w_in"], w["dtb"], layer, tm)
    qkv_s, z_s, xbc_s, dt_s = _inproj(xs2, w["norm_mix"], big["w_in"], w["dtb"], layer, ms)

    qkv_t = qkv.reshape(QKV_SLABS, bsz, seq, PAIR_W)
    os_, ls_, kv_p = [], [], []
    for g, (win, dil) in enumerate(ATT_GROUPS):
        nbk = max(1, PROMPT_ATT_ROWS // (BAND * dil))
        o, lse, kn = _attn_prompt(qkv_t, g, dil, nbk, c["pbias"][g], w["gq_p"], w["gk_p"], c["bd_p"])
        os_.append(o.reshape(2, m, PAIR_W))
        ls_.append(lse.reshape(2, m, PAIR_W))
        keep = min(win, seq)
        v_g = qkv_t[2 * PAIRS + 2 * g:2 * PAIRS + 2 * g + 2, :, seq - keep:]
        kv = jnp.stack([kn[:, :, seq - keep:], v_g])
        kv = jnp.transpose(kv, (2, 3, 0, 1, 4))
        kv_p.append(kv.reshape(bsz, keep, 2, HEADS_PER_GROUP, HEAD_DIM))

    xbc3 = xbc.reshape(bsz, seq, XBC_WIDTH)
    dt3 = dt.reshape(bsz, seq, SSM_HEADS)
    ssd_consts = (w["cw"], w["cb"], w["alr"], w["alc"], w["dsk"], w["gn"], c["tri"], c["expand"])
    att_consts = (*c["cbias"], c["nbias"], w["gq_g"], w["gk_g"], c["bd_g"], c["hm"])
    y, h_last, o_s, lse_s, kn_s = _ssd_prompt_attn_sample(
        xbc3, z.reshape(bsz, seq, SSM_WIDTH), dt3, jnp.swapaxes(dt3, 1, 2), ssd_consts,
        qkv_s, caches, layer, steps, att_consts)

    x1 = _outproj(os_, ls_, y.reshape(m, SSM_WIDTH), xp2, big["w_out"], layer, tm)
    x2, tail = _ffn_prompt(x1.reshape(bsz, seq, D_MODEL), w["norm_ffn"], big["w_up"], w["fw"], w["fb"],
                           big["w_down"], layer, tm)
    h_last = h_last.reshape(bsz, SSM_HEADS, SSM_HEAD_DIM, SSM_STATE)
    conv_p = xbc3[:, seq - (SSM_CONV - 1):]
    ffn_p = tail[:, SUBLANES - (FFN_CONV - 1):]

    kv_s = []
    for g in range(len(ATT_GROUPS)):
        v_g = qkv_s[2 * PAIRS + 2 * g:2 * PAIRS + 2 * g + 2]
        kv = jnp.stack([kn_s[2 * g:2 * g + 2], v_g])
        kv = jnp.transpose(kv, (2, 0, 1, 3))
        kv_s.append(kv.reshape(dec, steps, 2, HEADS_PER_GROUP, HEAD_DIM))
    bm, cm, xde, pre, ea, zg, cd = _ssd_sample_pre(xbc_s, z_s, dt_s, _prev_rows(conv_prev, steps), w["cw"],
                                                   w["cb"], w["alr"], w["dsk"], c["expand"], c["gexp"], steps)
    cd_last = cd.reshape(dec, steps, SSM_HEADS)[:, steps - 1]
    y_s, new_states = _ssd_sample_state(bm, cm, xde, pre, ea, zg, cd_last, w["gn"], state, layer, steps, 8,
                                        new_states)
    x1_s = _outproj([o_s], [lse_s], y_s, xs2, big["w_out"], layer, ms)
    x2_s, u_raw = _ffn_sample(x1_s, steps, w["norm_ffn"], big["w_up"], w["fw"], w["fb"], big["w_down"], layer,
                              _prev_rows(ffn_prev, steps))
    conv_s = jnp.concatenate([conv_prev, xbc_s.reshape(dec, steps, XBC_WIDTH)], axis=1)[:, steps:]
    ffn_s = jnp.concatenate([ffn_prev, u_raw.reshape(dec, steps, 2 * D_FF)], axis=1)[:, steps:]
    return (x2, x2_s.reshape(dec, steps, D_MODEL), (*kv_p, h_last, conv_p, ffn_p), (*kv_s, conv_s, ffn_s),
            new_states)


def kernel(x_prompt, x_sample, cache_kv0, cache_kv1, cache_kv2, state_ssm, state_conv, state_ffn_conv, norm_mix, w_in, q_norm, k_norm, conv_w, conv_b, dt_bias, a_log, d_skip, ssm_norm, w_out, norm_ffn, w_up, ffn_conv_w, ffn_conv_b, w_down):
    stacked = (norm_mix, q_norm, k_norm, conv_w, conv_b, dt_bias, a_log, d_skip, ssm_norm,
               norm_ffn, ffn_conv_w, ffn_conv_b)
    big = {"w_in": w_in.astype(BF16), "w_out": w_out.astype(BF16), "w_up": w_up.astype(BF16),
           "w_down": w_down.astype(BF16)}
    depth = w_in.shape[0]
    dec_batch, steps = x_sample.shape[:2]
    caches = tuple(_cache_view(cache) for cache in (cache_kv0, cache_kv1, cache_kv2))
    for cache, (_, dil) in zip(caches, ATT_GROUPS):
        assert cache.shape[-1] == BAND * dil
    state = state_ssm.reshape(depth, dec_batch, SSM_WIDTH, SSM_STATE)
    c = _constants(steps, tuple(cache.shape[-1] for cache in caches))
    tm = 512
    assert x_prompt.shape[1] % tm == 0
    y_prompt, y_sample = x_prompt, x_sample
    outs_p = [[] for _ in range(6)]
    outs_s = [[] for _ in range(5)]
    new_states = None
    for layer in range(depth):
        w = _layer_weights(tuple(a[layer] for a in stacked))
        y_prompt, y_sample, res_p, res_s, new_states = _layer(
            y_prompt, y_sample, w, big, c, layer, caches, state, new_states, state_conv[layer],
            state_ffn_conv[layer], tm)
        for lst, val in zip(outs_p, res_p):
            lst.append(val)
        for lst, val in zip(outs_s, res_s):
            lst.append(val)
    stack_s = [jnp.stack(l) for l in outs_s]
    return (y_prompt, y_sample, *[jnp.stack(l) for l in outs_p],
            *stack_s[:3], new_states.reshape(state_ssm.shape), *stack_s[3:])
```

```python
import functools
import math

import jax
import jax.numpy as jnp
import numpy as np
from jax import lax
from jax.experimental import pallas as pl
from jax.experimental.pallas import tpu as pltpu

F32 = jnp.float32
BF16 = jnp.bfloat16

D_MODEL = 1024
HEAD_DIM = 64
ATT_GROUPS = ((128, 1), (512, 4), (2048, 16))
BAND = 128
HEADS_PER_GROUP = 4
GROUP_W = HEADS_PER_GROUP * HEAD_DIM
N_ATT_HEADS = HEADS_PER_GROUP * len(ATT_GROUPS)
ATT_WIDTH = N_ATT_HEADS * HEAD_DIM
QKV_W = 3 * ATT_WIDTH
PAIR_W = 2 * HEAD_DIM
PAIRS = ATT_WIDTH // PAIR_W
QKV_SLABS = QKV_W // PAIR_W
SSM_HEAD_DIM = 64
SSM_WIDTH = 1024
SSM_HEADS = SSM_WIDTH // SSM_HEAD_DIM
SSM_STATE = 128
SSM_GROUPS = 2
SSM_HPG = SSM_HEADS // SSM_GROUPS
SSM_GROUP_W = SSM_HPG * SSM_HEAD_DIM
SSM_CONV = 4
SSM_CHUNK = 128
BC_W = SSM_GROUPS * SSM_STATE
XBC_WIDTH = SSM_WIDTH + 2 * BC_W
D_FF = 2816
FFN_CONV = 3
FF_CHUNK = 256
EPS = 1e-6
IN_WIDTH = QKV_W + SSM_WIDTH + XBC_WIDTH + SSM_HEADS
SUBLANES = 8
LANES = 128
SAMPLE_ROWS = 128
PROMPT_ATT_ROWS = 512
RESIDUE_UNROLL = 4
VMEM_LIMIT = 56 * 1024 * 1024
NEG_INF = float("-inf")


def _dot(a, b):
    return jnp.dot(a, b, preferred_element_type=F32)


def _dot_nt(a, b):
    return lax.dot_general(a, b, (((1,), (1,)), ((), ())), preferred_element_type=F32)


def _dot_tn(a, b):
    return lax.dot_general(a, b, (((0,), (0,)), ((), ())), preferred_element_type=F32)


def _split(x, pieces):
    out = []
    r = x
    for _ in range(pieces):
        p = r.astype(BF16)
        out.append(p)
        r = r - p.astype(F32)
    return out


def _dot_sel(x, sel, pieces=3):
    acc = None
    for p in _split(x, pieces):
        t = _dot(p, sel)
        acc = t if acc is None else acc + t
    return acc


def _sel_dot(sel, x, pieces=3):
    acc = None
    for p in _split(x, pieces):
        t = _dot(sel, p)
        acc = t if acc is None else acc + t
    return acc


def _sel_dot_nt(x, sel, pieces=3):
    acc = None
    for p in _split(x, pieces):
        t = _dot_nt(p, sel)
        acc = t if acc is None else acc + t
    return acc


def _silu(x):
    return x * jax.nn.sigmoid(x)


def _softplus(x):
    return jnp.maximum(x, 0.0) + jnp.log1p(jnp.exp(-jnp.abs(x)))


def _rms(x, g):
    ms = jnp.mean(x * x, axis=-1, keepdims=True)
    return x * lax.rsqrt(ms + EPS) * g


def _layer_spec(arr, layer):
    nd = arr.ndim
    return pl.BlockSpec((1,) + arr.shape[1:], lambda *_: (layer,) + (0,) * (nd - 1), pipeline_mode=pl.Buffered(1))


def _const_spec(shape):
    nd = len(shape)
    return pl.BlockSpec(shape, lambda *_: (0,) * nd, pipeline_mode=pl.Buffered(1))


def _params(sem):
    return pltpu.CompilerParams(dimension_semantics=sem, vmem_limit_bytes=VMEM_LIMIT)


def _head_block_diag(width):
    h = np.arange(width) // HEAD_DIM
    return jnp.asarray(np.where(h[:, None] == h[None, :], 1.0 / HEAD_DIM, 0.0), dtype=BF16)


def _head_rms(t, g, bd):
    ms = _dot_sel(t * t, bd, pieces=2)
    return t * lax.rsqrt(ms + EPS) * g


def _rows_back(slab_ref, s, d, n):
    return slab_ref[s, pl.ds(SUBLANES - d, n, stride=1), :]


def _inproj_body(x_ref, g_ref, w_ref, dtb_ref, qkv_ref, z_ref, xbc_ref, dt_ref):
    h = _rms(x_ref[...], g_ref[...]).astype(BF16)
    qkv = _dot(h, w_ref[0, :, 0:QKV_W])
    for s in range(QKV_SLABS):
        qkv_ref[s] = qkv[:, s * PAIR_W:(s + 1) * PAIR_W]
    z_ref[...] = _dot(h, w_ref[0, :, QKV_W:QKV_W + SSM_WIDTH])
    xd = _dot(h, w_ref[0, :, QKV_W + SSM_WIDTH:IN_WIDTH])
    xbc_ref[...] = xd[:, :XBC_WIDTH]
    dt_ref[...] = _softplus(xd[:, XBC_WIDTH:] + dtb_ref[...])


def _inproj(x, g, w_in, dtb, layer, tm):
    m = x.shape[0]
    row = lambda w: pl.BlockSpec((tm, w), lambda i: (i, 0))
    widths = (SSM_WIDTH, XBC_WIDTH, SSM_HEADS)
    return pl.pallas_call(
        _inproj_body,
        grid=(m // tm,),
        in_specs=[row(D_MODEL), _const_spec(g.shape), _layer_spec(w_in, layer), _const_spec(dtb.shape)],
        out_specs=[pl.BlockSpec((QKV_SLABS, tm, PAIR_W), lambda i: (0, i, 0))] + [row(w) for w in widths],
        out_shape=[jax.ShapeDtypeStruct((QKV_SLABS, m, PAIR_W), F32)]
                  + [jax.ShapeDtypeStruct((m, w), F32) for w in widths],
        compiler_params=_params(("parallel",)),
        name="inproj",
    )(x, g, w_in, dtb)


def _attn_prompt_body(dil, nbk, q0_ref, q1_ref, k0_ref, k1_ref, v0_ref, v1_ref, bias_ref, gq_ref, gk_ref, bd_ref,
                      o_ref, lse_ref, kn_ref, kv_scr):
    n = pl.program_id(1)
    rows_per = nbk * BAND
    in_refs = ((q0_ref, q1_ref), (k0_ref, k1_ref), (v0_ref, v1_ref))

    @pl.when(n == 0)
    def _():
        kv_scr[...] = jnp.zeros(kv_scr.shape, BF16)

    variant = jnp.minimum(n, 1)
    bd = bd_ref[...]
    gq = gq_ref[...]
    gk = gk_ref[...]
    low = lax.broadcasted_iota(jnp.int32, (BAND, PAIR_W), 1) < HEAD_DIM

    def rows_of(r, start, count):
        return pl.ds(r + start * dil, count, stride=dil) if dil > 1 else pl.ds(start, count)

    def group(residues):
        items = []
        for r in residues:
            for p in range(2):
                q, k, v = (in_refs[kind][p][0, 0, rows_of(r, 0, rows_per), :] for kind in range(3))
                qn = _head_rms(q, gq, bd) * (HEAD_DIM ** -0.5)
                kn = _head_rms(k, gk, bd)
                kall = jnp.concatenate([kv_scr[0, p, r], kn.astype(BF16)], axis=0)
                vall = jnp.concatenate([kv_scr[1, p, r], v.astype(BF16)], axis=0)
                items.append((r, p, qn, kn, kall, vall))
        scores = []
        for r, p, qn, kn, kall, vall in items:
            for j in range(nbk):
                qj = qn[j * BAND:(j + 1) * BAND]
                for hh in range(2):
                    keep = low if hh == 0 else jnp.logical_not(low)
                    qm = jnp.where(keep, qj, 0.0).astype(BF16)
                    bias = bias_ref[variant if j == 0 else 1, 2 * p + hh]
                    scores.append(_dot_nt(qm, kall[j * BAND:(j + 2) * BAND]) + bias)
        probs = []
        for s in scores:
            mx = jnp.max(s, axis=-1, keepdims=True)
            e = jnp.exp(s - mx)
            den = jnp.sum(e, axis=-1, keepdims=True)
            probs.append((e.astype(BF16), den, mx + jnp.log(den)))
        results = []
        it = iter(probs)
        for r, p, qn, kn, kall, vall in items:
            for j in range(nbk):
                (e0, d0, l0), (e1, d1, l1) = next(it), next(it)
                vwin = vall[j * BAND:(j + 2) * BAND]
                o = jnp.where(low, _dot(e0, vwin) / d0, _dot(e1, vwin) / d1)
                results.append((r, p, j, o, jnp.where(low, l0, l1)))
        for r, p, j, o, lse in results:
            o_ref[p, 0, rows_of(r, j * BAND, BAND), :] = o
            lse_ref[p, 0, rows_of(r, j * BAND, BAND), :] = lse
        for r, p, qn, kn, kall, vall in items:
            kn_ref[p, 0, rows_of(r, 0, rows_per), :] = kn
            kv_scr[0, p, r] = kall[rows_per:]
            kv_scr[1, p, r] = vall[rows_per:]

    if dil <= RESIDUE_UNROLL:
        group(range(dil))
    else:
        def step(i, carry):
            group([i * RESIDUE_UNROLL + k for k in range(RESIDUE_UNROLL)])
            return carry
        lax.fori_loop(0, dil // RESIDUE_UNROLL, step, 0)


def _prompt_bias(slopes, dil):
    i = np.arange(BAND)[:, None]
    j = np.arange(2 * BAND)[None, :]
    step = i + BAND - j
    valid = (step >= 0) & (step <= BAND)
    bias = -slopes[:, None, None] * (step * dil).astype(np.float32)
    full = np.where(valid[None], bias, NEG_INF)
    first = np.where((valid & (j >= BAND))[None], bias, NEG_INF)
    return jnp.asarray(np.stack([first, full]), dtype=F32)


def _attn_prompt(qkv_t, g, dil, nbk, bias, gq, gk, bd):
    _, bsz, seq, _ = qkv_t.shape
    blk = nbk * BAND * dil
    assert seq % blk == 0
    slab = lambda s: pl.BlockSpec((1, 1, blk, PAIR_W), lambda b, n: (s, b, n, 0))
    ospec = pl.BlockSpec((2, 1, blk, PAIR_W), lambda b, n: (0, b, n, 0))
    first = [kind * PAIRS + 2 * g for kind in range(3)]
    return pl.pallas_call(
        functools.partial(_attn_prompt_body, dil, nbk),
        grid=(bsz, seq // blk),
        in_specs=[slab(first[0]), slab(first[0] + 1), slab(first[1]), slab(first[1] + 1),
                  slab(first[2]), slab(first[2] + 1),
                  _const_spec(bias.shape), _const_spec(gq.shape), _const_spec(gk.shape), _const_spec(bd.shape)],
        out_specs=[ospec, ospec, ospec],
        out_shape=[jax.ShapeDtypeStruct((2, bsz, seq, PAIR_W), F32)] * 3,
        scratch_shapes=[pltpu.VMEM((2, 2, dil, BAND, PAIR_W), BF16)],
        compiler_params=_params(("parallel", "arbitrary")),
        name=f"attn_prompt_g{g}",
    )(*([qkv_t] * 6), bias, gq, gk, bd)


def _ssd_stage_conv(xbc_ref, conv_scr):
    raw = xbc_ref[0]
    for s in range(XBC_WIDTH // LANES):
        conv_scr[s, SUBLANES:, :] = raw[:, s * LANES:(s + 1) * LANES]
    return raw


def _ssd_chunk(raw, z_ref, dt_ref, dtt_ref, cw_ref, cb_ref, alr_ref, alc_ref, dsk_ref, gn_ref,
               tri_ref, exp_ref, conv_scr, state_scr):
    q = SSM_CHUNK

    cw = cw_ref[...]
    z = z_ref[0]
    state = state_scr[...]
    n_slabs = XBC_WIDTH // LANES
    parts = []
    for s in range(n_slabs):
        lanes = slice(s * LANES, (s + 1) * LANES)
        acc = _rows_back(conv_scr, s, 3, q) * cw[0:1, lanes]
        acc = acc + _rows_back(conv_scr, s, 2, q) * cw[1:2, lanes]
        acc = acc + _rows_back(conv_scr, s, 1, q) * cw[2:3, lanes]
        parts.append(_silu(acc + raw[:, lanes] * cw[3:4, lanes] + cb_ref[:, lanes]))
        if s % 3 == 2:
            yield
    xc = jnp.concatenate(parts, axis=1)
    xs = xc[:, :SSM_WIDTH]
    bm = xc[:, SSM_WIDTH:SSM_WIDTH + BC_W]
    cm = xc[:, SSM_WIDTH + BC_W:]

    tri = tri_ref[...]
    dt = dt_ref[0]
    acum = _sel_dot(tri, dt * (-jnp.exp(alr_ref[...])))
    acum_t = _sel_dot_nt(dtt_ref[0] * (-jnp.exp(alc_ref[...])), tri)
    yield
    last = acum[q - 1:q, :]
    expand = exp_ref[...]
    dt_x = _dot_sel(dt, expand)
    ea_x = _dot_sel(jnp.exp(acum), expand)
    de_x = _dot_sel(jnp.exp(last - acum), expand)
    xdt = xs * dt_x
    xde = (xdt * de_x).astype(BF16)
    xdt_b = xdt.astype(BF16)
    causal = lax.broadcasted_iota(jnp.int32, (q, q), 0) >= lax.broadcasted_iota(jnp.int32, (q, q), 1)
    low = lax.broadcasted_iota(jnp.int32, (q, 2 * SSM_HEAD_DIM), 1) < SSM_HEAD_DIM

    y_parts = []
    new_state = []
    for g in range(SSM_GROUPS):
        bm_g = bm[:, g * SSM_STATE:(g + 1) * SSM_STATE].astype(BF16)
        cm_g = cm[:, g * SSM_STATE:(g + 1) * SSM_STATE].astype(BF16)
        cb = _dot_nt(cm_g, bm_g)
        rows = slice(g * SSM_GROUP_W, (g + 1) * SSM_GROUP_W)
        st_g = state[rows, :]
        inter = _dot_nt(cm_g, st_g.astype(BF16))
        for hp in range(SSM_HPG // 2):
            pair = None
            for k in range(2):
                h = g * SSM_HPG + 2 * hp + k
                seg = acum[:, h:h + 1] - acum_t[h:h + 1, :]
                decay = jnp.exp(jnp.where(causal, seg, NEG_INF))
                gmat = (cb * decay).astype(BF16)
                lanes = slice((2 * hp) * SSM_HEAD_DIM + g * SSM_GROUP_W,
                              (2 * hp + 2) * SSM_HEAD_DIM + g * SSM_GROUP_W)
                x_pair = xdt_b[:, lanes]
                keep = low if k == 0 else jnp.logical_not(low)
                t = _dot(gmat, jnp.where(keep, x_pair, jnp.zeros_like(x_pair)))
                pair = t if pair is None else pair + t
            y_parts.append(pair + ea_x[:, lanes] * inter[:, lanes.start - g * SSM_GROUP_W:lanes.stop - g * SSM_GROUP_W])
            yield
        new = _dot_tn(xde[:, rows], bm_g)
        for hh in range(SSM_HPG):
            h = g * SSM_HPG + hh
            cd = jnp.exp(acum_t[h:h + 1, q - 1:q])
            r = slice(h * SSM_HEAD_DIM, (h + 1) * SSM_HEAD_DIM)
            rl = slice(hh * SSM_HEAD_DIM, (hh + 1) * SSM_HEAD_DIM)
            new_state.append((r, st_g[rl, :] * cd + new[rl, :]))

    yield
    y = jnp.concatenate(y_parts, axis=1) + dsk_ref[...] * xs
    y = y * _silu(z)
    return _rms(y, gn_ref[...]), new_state


def _ssd_store(raw, y, new_state, y_ref, conv_scr, state_scr):
    y_ref[0] = y
    for r, val in new_state:
        state_scr[r, :] = val
    for s in range(XBC_WIDTH // LANES):
        conv_scr[s, 0:SUBLANES, :] = raw[SSM_CHUNK - SUBLANES:, s * LANES:(s + 1) * LANES]


def _interleave(*gens):
    results = [None] * len(gens)
    live = list(range(len(gens)))
    while live:
        for i in list(live):
            try:
                next(gens[i])
            except StopIteration as done:
                results[i] = done.value
                live.remove(i)
    return results


def _ssd_attn_body(steps, dils, n_ssd, *refs):
    ssd_in, rest = refs[:n_ssd], refs[n_ssd:]
    (qkv_ref, kv0_ref, kv1_ref, kv2_ref, cb0_ref, cb1_ref, cb2_ref, nb_ref, gq_ref, gk_ref, bd_ref, hm_ref,
     y_ref, hl_ref, o_ref, lse_ref, kn_ref, conv_scr, state_scr) = rest
    c = pl.program_id(1)
    n_b = SUBLANES // steps
    b = (pl.program_id(0) * pl.num_programs(1) + c) % n_b

    @pl.when(c == 0)
    def _():
        conv_scr[:, 0:SUBLANES, :] = jnp.zeros((conv_scr.shape[0], SUBLANES, LANES), F32)
        state_scr[...] = jnp.zeros_like(state_scr)

    raw = _ssd_stage_conv(ssd_in[0], conv_scr)
    sample, (y, new_state) = _interleave(
        _sample_attn_batch(steps, dils, b, qkv_ref, (kv0_ref, kv1_ref, kv2_ref), (cb0_ref, cb1_ref, cb2_ref),
                           nb_ref, gq_ref, gk_ref, bd_ref, hm_ref),
        _ssd_chunk(raw, *ssd_in[1:], conv_scr, state_scr))
    _ssd_store(raw, y, new_state, y_ref, conv_scr, state_scr)

    mine = lax.broadcasted_iota(jnp.int32, (SUBLANES, PAIR_W), 0) // steps == b

    @pl.when(b == 0)
    def _():
        for g, (o_g, l_g, kn) in enumerate(sample):
            for p in range(2):
                lanes = slice(p * PAIR_W, (p + 1) * PAIR_W)
                o_ref[2 * g + p] = jnp.where(mine, o_g[:, lanes], 0.0)
                lse_ref[2 * g + p] = jnp.where(mine, l_g[:, lanes], 0.0)
                kn_ref[2 * g + p] = kn[:, lanes]

    @pl.when(b > 0)
    def _():
        for g, (o_g, l_g, kn) in enumerate(sample):
            for p in range(2):
                lanes = slice(p * PAIR_W, (p + 1) * PAIR_W)
                o_ref[2 * g + p] = jnp.where(mine, o_g[:, lanes], o_ref[2 * g + p])
                lse_ref[2 * g + p] = jnp.where(mine, l_g[:, lanes], lse_ref[2 * g + p])

    @pl.when(c == pl.num_programs(1) - 1)
    def _():
        hl_ref[0] = state_scr[...]


def _ssd_prompt_attn_sample(xbc, z, dt, dtt, ssd_consts, qkv_t, caches, layer, steps, att_consts):
    bsz, seq, _ = xbc.shape
    assert seq % SSM_CHUNK == 0
    nc = seq // SSM_CHUNK
    m = qkv_t.shape[1]
    n_b = SUBLANES // steps
    assert SUBLANES % steps == 0 and m == bsz * nc * steps
    dils = tuple(d for _, d in ATT_GROUPS)
    blk = lambda w: pl.BlockSpec((1, SSM_CHUNK, w), lambda b, c: (b, c, 0))
    tile = lambda b, c: (b * nc + c) // n_b
    kv_specs = [pl.BlockSpec((1, 1) + cache.shape[2:], lambda b, c: (layer, b * nc + c, 0, 0, 0))
                for cache in caches]
    ospec = pl.BlockSpec((PAIRS, SUBLANES, PAIR_W), lambda b, c: (0, tile(b, c), 0))
    ssd_in = [xbc, z, dt, dtt, *ssd_consts]
    return pl.pallas_call(
        functools.partial(_ssd_attn_body, steps, dils, len(ssd_in)),
        grid=(bsz, nc),
        in_specs=[blk(XBC_WIDTH), blk(SSM_WIDTH), blk(SSM_HEADS),
                  pl.BlockSpec((1, SSM_HEADS, SSM_CHUNK), lambda b, c: (b, 0, c))]
                 + [_const_spec(a.shape) for a in ssd_consts]
                 + [pl.BlockSpec((QKV_SLABS, SUBLANES, PAIR_W), lambda b, c: (0, tile(b, c), 0))] + kv_specs
                 + [_const_spec(a.shape) for a in att_consts],
        out_specs=[blk(SSM_WIDTH), pl.BlockSpec((1, SSM_WIDTH, SSM_STATE), lambda b, c: (b, 0, 0)),
                   ospec, ospec, ospec],
        out_shape=[jax.ShapeDtypeStruct((bsz, seq, SSM_WIDTH), F32),
                   jax.ShapeDtypeStruct((bsz, SSM_WIDTH, SSM_STATE), F32)]
                  + [jax.ShapeDtypeStruct((PAIRS, m, PAIR_W), F32)] * 3,
        scratch_shapes=[pltpu.VMEM((XBC_WIDTH // LANES, SUBLANES + SSM_CHUNK, LANES), F32),
                        pltpu.VMEM((SSM_WIDTH, SSM_STATE), F32)],
        compiler_params=_params(("arbitrary", "arbitrary")),
        name="ssd_prompt_attn_sample",
    )(*ssd_in, qkv_t, *caches, *att_consts)


def _outproj_body(slabs, *refs):
    n_arr = len(slabs)
    o_refs, l_refs = refs[:n_arr], refs[n_arr:2 * n_arr]
    y_ref, x_ref, w_ref, out_ref = refs[2 * n_arr:]
    o_sl = [r[s] for r, cnt in zip(o_refs, slabs) for s in range(cnt)]
    l_sl = [r[s] for r, cnt in zip(l_refs, slabs) for s in range(cnt)]
    ngrp = len(ATT_GROUPS)
    att = [None] * PAIRS
    for p in range(PAIRS // ngrp):
        ls = [l_sl[2 * g + p] for g in range(ngrp)]
        mx = functools.reduce(jnp.maximum, ls)
        es = [jnp.exp(l - mx) for l in ls]
        inv = 1.0 / functools.reduce(jnp.add, es)
        for g in range(ngrp):
            att[2 * g + p] = (o_sl[2 * g + p] * (es[g] * inv)).astype(BF16)
    acc = x_ref[...] + _dot(y_ref[...].astype(BF16), w_ref[0, ATT_WIDTH:, :])
    out_ref[...] = acc + _dot(jnp.concatenate(att, axis=1), w_ref[0, 0:ATT_WIDTH, :])


def _outproj(os_, ls_, y, x, w_out, layer, tm):
    m = x.shape[0]
    slabs = tuple(a.shape[0] for a in os_)
    assert sum(slabs) == PAIRS
    row = lambda w: pl.BlockSpec((tm, w), lambda i: (i, 0))
    slab_specs = [pl.BlockSpec((n, tm, PAIR_W), lambda i: (0, i, 0)) for n in slabs]
    return pl.pallas_call(
        functools.partial(_outproj_body, slabs),
        grid=(m // tm,),
        in_specs=slab_specs * 2 + [row(SSM_WIDTH), row(D_MODEL), _layer_spec(w_out, layer)],
        out_specs=row(D_MODEL),
        out_shape=jax.ShapeDtypeStruct((m, D_MODEL), F32),
        compiler_params=_params(("parallel",)),
        name="outproj",
    )(*os_, *ls_, y, x, w_out)


def _ffn_chunk(h, c, wu_ref, fw_ref, fb_ref, shifted):
    halves = []
    for base in (0, D_FF):
        cols = slice(base + c * FF_CHUNK, base + (c + 1) * FF_CHUNK)
        u = _dot(h, wu_ref[0, :, cols])
        s2, s1 = shifted(u, cols)
        w = fw_ref[:, cols]
        halves.append(s2 * w[0:1] + s1 * w[1:2] + u * w[2:3] + fb_ref[:, cols])
    return (_silu(halves[0]) * halves[1]).astype(BF16)


def _ffn_prompt_body(x_ref, g_ref, wu_ref, fw_ref, fb_ref, wd_ref, out_ref, tail_ref, u_scr, act_scr):
    i = pl.program_id(1)
    n_slabs = u_scr.shape[0]

    @pl.when(i == 0)
    def _():
        u_scr[:, 0:SUBLANES, :] = jnp.zeros((n_slabs, SUBLANES, LANES), F32)

    x = x_ref[0]
    tm = x.shape[0]
    h = _rms(x, g_ref[...]).astype(BF16)

    def shifted(u, cols):
        s2, s1 = [], []
        for k in range(FF_CHUNK // LANES):
            s = cols.start // LANES + k
            u_scr[s, SUBLANES:, :] = u[:, k * LANES:(k + 1) * LANES]
            s2.append(_rows_back(u_scr, s, 2, tm))
            s1.append(_rows_back(u_scr, s, 1, tm))
        return jnp.concatenate(s2, axis=1), jnp.concatenate(s1, axis=1)

    for c in range(D_FF // FF_CHUNK):
        act_scr[:, c * FF_CHUNK:(c + 1) * FF_CHUNK] = _ffn_chunk(h, c, wu_ref, fw_ref, fb_ref, shifted)
    out_ref[0] = x + _dot(act_scr[...], wd_ref[0])
    u_scr[:, 0:SUBLANES, :] = u_scr[:, tm:tm + SUBLANES, :]

    @pl.when(i == pl.num_programs(1) - 1)
    def _():
        for s in range(n_slabs):
            tail_ref[0, :, s * LANES:(s + 1) * LANES] = u_scr[s, 0:SUBLANES, :]


def _ffn_prompt(x, g, wu, fw, fb, wd, layer, tm):
    bsz, seq, _ = x.shape
    return pl.pallas_call(
        _ffn_prompt_body,
        grid=(bsz, seq // tm),
        in_specs=[pl.BlockSpec((1, tm, D_MODEL), lambda b, i: (b, i, 0)), _const_spec(g.shape),
                  _layer_spec(wu, layer), _const_spec(fw.shape), _const_spec(fb.shape), _layer_spec(wd, layer)],
        out_specs=[pl.BlockSpec((1, tm, D_MODEL), lambda b, i: (b, i, 0)),
                   pl.BlockSpec((1, SUBLANES, 2 * D_FF), lambda b, i: (b, 0, 0))],
        out_shape=[jax.ShapeDtypeStruct((bsz, seq, D_MODEL), F32),
                   jax.ShapeDtypeStruct((bsz, SUBLANES, 2 * D_FF), F32)],
        scratch_shapes=[pltpu.VMEM((2 * D_FF // LANES, SUBLANES + tm, LANES), F32), pltpu.VMEM((tm, D_FF), BF16)],
        compiler_params=_params(("parallel", "arbitrary")),
        name="ffn_prompt",
    )(x, g, wu, fw, fb, wd)


def _ffn_sample_body(steps, x_ref, g_ref, wu_ref, fw_ref, fb_ref, wd_ref, prev_ref, out_ref, u_ref):
    x = x_ref[...]
    rows = x.shape[0]
    h = _rms(x, g_ref[...]).astype(BF16)
    t = lax.broadcasted_iota(jnp.int32, (rows, FF_CHUNK), 0) % steps

    def shifted(u, cols):
        u_ref[:, cols] = u
        prev = prev_ref[:, cols]
        s1 = jnp.where(t >= 1, pltpu.roll(u, 1, axis=0), pltpu.roll(prev, rows - 1, axis=0))
        s2 = jnp.where(t >= 2, pltpu.roll(u, 2, axis=0), prev)
        return s2, s1

    act = [_ffn_chunk(h, c, wu_ref, fw_ref, fb_ref, shifted) for c in range(D_FF // FF_CHUNK)]
    out_ref[...] = x + _dot(jnp.concatenate(act, axis=1), wd_ref[0])


def _ffn_sample(x, steps, g, wu, fw, fb, wd, layer, prev):
    m = x.shape[0]
    tr = min(SAMPLE_ROWS, m)
    assert m % tr == 0 and tr % steps == 0
    row = lambda w: pl.BlockSpec((tr, w), lambda i: (i, 0))
    return pl.pallas_call(
        functools.partial(_ffn_sample_body, steps),
        grid=(m // tr,),
        in_specs=[row(D_MODEL), _const_spec(g.shape), _layer_spec(wu, layer), _const_spec(fw.shape),
                  _const_spec(fb.shape), _layer_spec(wd, layer), row(2 * D_FF)],
        out_specs=[row(D_MODEL), row(2 * D_FF)],
        out_shape=[jax.ShapeDtypeStruct((m, D_MODEL), F32), jax.ShapeDtypeStruct((m, 2 * D_FF), F32)],
        compiler_params=_params(("parallel",)),
        name="ffn_sample",
    )(x, g, wu, fw, fb, wd, prev)


def _sample_attn_batch(steps, dils, b, qkv_ref, kv_refs, cb_refs, nb_ref, gq_ref, gk_ref, bd_ref, hm_ref):
    rows = qkv_ref.shape[1]
    assert rows == 2 * steps
    bd = bd_ref[...]
    hm = hm_ref[...]
    groups = range(len(dils))
    st = [dict() for _ in groups]
    for g in groups:
        slab = lambda kind: jnp.concatenate([qkv_ref[kind * PAIRS + 2 * g], qkv_ref[kind * PAIRS + 2 * g + 1]], axis=1)
        qn = _head_rms(slab(0), gq_ref[...], bd) * (HEAD_DIM ** -0.5)
        kn = _head_rms(slab(1), gk_ref[...], bd)
        vn = slab(2)
        st[g]["kn"] = kn
        st[g]["kn_b"] = jnp.where(b == 0, kn[0:steps], kn[steps:])
        st[g]["vn_b"] = jnp.where(b == 0, vn[0:steps], vn[steps:])
        st[g]["qbd"] = jnp.concatenate([qn * hm[h:h + 1] for h in range(HEADS_PER_GROUP)], axis=0)
        yield
    for g in groups:
        kc = kv_refs[g][0, 0, 0].astype(BF16)
        st[g]["s_c"] = _dot(st[g]["qbd"].astype(BF16), kc) + cb_refs[g][...]
        yield
    for g in groups:
        d = st[g]
        mx = jnp.max(d["s_c"], axis=-1, keepdims=True)
        d["s_n"] = []
        for t2 in range(steps):
            sn = jnp.sum(d["qbd"] * d["kn_b"][t2:t2 + 1], axis=-1, keepdims=True) + nb_ref[g, t2]
            d["s_n"].append(sn)
            mx = jnp.maximum(mx, sn)
        d["mx"] = mx
        d["e_c"] = jnp.exp(d["s_c"] - mx)
        d["den"] = jnp.sum(d["e_c"], axis=-1, keepdims=True)
        yield
    for g in groups:
        vc = kv_refs[g][0, 0, 1].astype(BF16)
        st[g]["acc"] = _dot_nt(st[g]["e_c"].astype(BF16), vc)
        yield
    results = []
    for g in groups:
        d = st[g]
        acc, den = d["acc"], d["den"]
        for t2, sn in enumerate(d["s_n"]):
            e_n = jnp.exp(sn - d["mx"])
            den = den + e_n
            acc = acc + e_n * d["vn_b"][t2:t2 + 1]
        o_all = acc / den
        l_all = d["mx"] + jnp.log(den)
        o_g = None
        l_g = None
        for h in range(HEADS_PER_GROUP):
            o_h = o_all[h * rows:(h + 1) * rows] * hm[h:h + 1]
            l_h = l_all[h * rows:(h + 1) * rows] * hm[h:h + 1]
            o_g = o_h if o_g is None else o_g + o_h
            l_g = l_h if l_g is None else l_g + l_h
        results.append((o_g, l_g, d["kn"]))
        yield
    return results


def _sample_cache_bias(slopes, dil, steps, cache_len, rows):
    t = (np.arange(rows) % steps)[None, :, None]
    c = np.arange(cache_len)[None, None, :]
    dist = cache_len + t - c
    j = dist // dil
    valid = (dist % dil == 0) & (j >= 1) & (j <= BAND)
    bias = -slopes[:, None, None] * dist.astype(np.float32)
    return jnp.asarray(np.where(valid, bias, NEG_INF).reshape(slopes.shape[0] * rows, cache_len), dtype=F32)


def _sample_new_bias(slopes, dil, steps, rows):
    t2 = np.arange(steps)[:, None, None]
    t = (np.arange(rows) % steps)[None, None, :]
    dist = t - t2
    valid = (dist >= 0) & (dist % dil == 0) & (dist // dil <= BAND)
    bias = -slopes[None, :, None] * dist.astype(np.float32)
    return np.where(valid, bias, NEG_INF).reshape(steps, slopes.shape[0] * rows, 1)


def _ssd_sample_pre_body(steps, xbc_ref, z_ref, dt_ref, prev_ref, cw_ref, cb_ref, al_ref, dsk_ref,
                         exp_ref, gexp_ref, bm_ref, cm_ref, xde_ref, pre_ref, ea_ref, zg_ref, cd_ref):
    raw = xbc_ref[...]
    rows = raw.shape[0]
    tw = lax.broadcasted_iota(jnp.int32, (rows, XBC_WIDTH), 0) % steps
    cw = cw_ref[...]
    prev = prev_ref[...]
    acc = jnp.where(tw >= 3, pltpu.roll(raw, 3, axis=0), prev) * cw[0:1]
    acc = acc + jnp.where(tw >= 2, pltpu.roll(raw, 2, axis=0), pltpu.roll(prev, rows - 1, axis=0)) * cw[1:2]
    acc = acc + jnp.where(tw >= 1, pltpu.roll(raw, 1, axis=0), pltpu.roll(prev, rows - 2, axis=0)) * cw[2:3]
    acc = acc + raw * cw[3:4]
    xc = _silu(acc + cb_ref[...])
    xs = xc[:, :SSM_WIDTH]
    bm = xc[:, SSM_WIDTH:SSM_WIDTH + BC_W]
    cm = xc[:, SSM_WIDTH + BC_W:]
    bm_ref[...] = bm
    cm_ref[...] = cm

    dt = dt_ref[...]
    da = dt * (-jnp.exp(al_ref[...]))
    th = lax.broadcasted_iota(jnp.int32, (rows, SSM_HEADS), 0) % steps
    acum = da
    for d in range(1, steps):
        acum = acum + jnp.where(th >= d, pltpu.roll(da, d, axis=0), 0.0)
    tail = jnp.zeros_like(da)
    for d in range(1, steps):
        tail = tail + jnp.where(th + d < steps, pltpu.roll(da, rows - d, axis=0), 0.0)
    expand = exp_ref[...]
    dt_x = _dot_sel(dt, expand)
    acum_x = _dot_sel(acum, expand)
    ea_ref[...] = jnp.exp(acum_x)
    xdt = xs * dt_x
    xde_ref[...] = xdt * _dot_sel(jnp.exp(tail), expand)
    cd_ref[...] = jnp.exp(acum)

    ts = lax.broadcasted_iota(jnp.int32, (rows, SSM_WIDTH), 0) % steps
    gexp = gexp_ref[...]
    y = dsk_ref[...] * xs
    for d in range(steps):
        if d == 0:
            bm_d, xdt_d, ac_d = bm, xdt, acum_x
        else:
            bm_d = pltpu.roll(bm, d, axis=0)
            xdt_d = pltpu.roll(xdt, d, axis=0)
            ac_d = pltpu.roll(acum_x, d, axis=0)
        cb_x = _dot_sel(cm * bm_d, gexp, pieces=2)
        term = cb_x * jnp.exp(acum_x - ac_d) * xdt_d
        y = y + jnp.where(ts >= d, term, 0.0)
    pre_ref[...] = y
    zg_ref[...] = _silu(z_ref[...])


def _ssd_sample_pre(xbc, z, dt, prev, cw, cb, al, dsk, expand, gexp, steps):
    m = xbc.shape[0]
    tr = min(SAMPLE_ROWS, m)
    assert m % tr == 0 and tr % steps == 0
    row = lambda w: pl.BlockSpec((tr, w), lambda i: (i, 0))
    consts = (cw, cb, al, dsk, expand, gexp)
    widths = (BC_W, BC_W, SSM_WIDTH, SSM_WIDTH, SSM_WIDTH, SSM_WIDTH, SSM_HEADS)
    return pl.pallas_call(
        functools.partial(_ssd_sample_pre_body, steps),
        grid=(m // tr,),
        in_specs=[row(XBC_WIDTH), row(SSM_WIDTH), row(SSM_HEADS), row(XBC_WIDTH)]
                 + [_const_spec(a.shape) for a in consts],
        out_specs=[row(w) for w in widths],
        out_shape=[jax.ShapeDtypeStruct((m, w), F32) for w in widths],
        compiler_params=_params(("parallel",)),
        name="ssd_sample_pre",
    )(xbc, z, dt, prev, *consts)


def _ssd_sample_state_body(steps, bm_ref, cm_ref, xde_ref, pre_ref, ea_ref, zg_ref, cd_ref, gn_ref, st_ref,
                           *rest):
    y_ref, ns_ref = rest[-2:]
    bb = st_ref.shape[1]
    blk = pl.program_id(1)
    last_pass = pl.num_programs(0) - 1

    @pl.when(pl.program_id(0) < last_pass)
    def _():
        ns_ref[...] = jnp.zeros(ns_ref.shape, F32)

    @pl.when(pl.program_id(0) == last_pass)
    def _():
        _ssd_sample_state_step(steps, bb, blk, bm_ref, cm_ref, xde_ref, pre_ref, ea_ref, zg_ref, cd_ref, gn_ref,
                               st_ref, y_ref, ns_ref)


def _ssd_sample_state_step(steps, bb, blk, bm_ref, cm_ref, xde_ref, pre_ref, ea_ref, zg_ref, cd_ref, gn_ref,
                           st_ref, y_ref, ns_ref):

    def dots(b):
        rows = slice(b * steps, (b + 1) * steps)
        bm = bm_ref[rows, :].astype(BF16)
        cm = cm_ref[rows, :].astype(BF16)
        xde = xde_ref[rows, :].astype(BF16)
        inter, new = [], []
        for g in range(SSM_GROUPS):
            srows = slice(g * SSM_GROUP_W, (g + 1) * SSM_GROUP_W)
            lanes = slice(g * SSM_STATE, (g + 1) * SSM_STATE)
            inter.append(_dot_nt(cm[:, lanes], st_ref[0, b, srows, :].astype(BF16)))
            new.append(_dot_tn(xde[:, srows], bm[:, lanes]))
        return inter, new

    def finish(b, inter, new):
        rows = slice(b * steps, (b + 1) * steps)
        y = pre_ref[rows, :] + ea_ref[rows, :] * jnp.concatenate(inter, axis=1)
        y_ref[rows, :] = _rms(y * zg_ref[rows, :], gn_ref[...])
        for h in range(SSM_HEADS):
            g, hh = divmod(h, SSM_HPG)
            cd = cd_ref[blk * bb + b, h]
            r = slice(h * SSM_HEAD_DIM, (h + 1) * SSM_HEAD_DIM)
            ns_ref[0, b, r, :] = st_ref[0, b, r, :] * cd + new[g][hh * SSM_HEAD_DIM:(hh + 1) * SSM_HEAD_DIM, :]

    pending = dots(0)
    for b in range(bb):
        nxt = dots(b + 1) if b + 1 < bb else None
        finish(b, *pending)
        pending = nxt


def _ssd_sample_state(bm, cm, xde, pre, ea, zg, cd, gn, state, layer, steps, bb, new_states):
    bsz = bm.shape[0] // steps
    depth = state.shape[0]
    assert bsz % bb == 0 and (bb * steps) % SUBLANES == 0
    passes = depth if new_states is None else 1
    row_blk = lambda p, i: jnp.where(p == passes - 1, i, 0)
    blk = lambda w: pl.BlockSpec((bb * steps, w), lambda p, i: (row_blk(p, i), 0))
    st_in = pl.BlockSpec((1, bb, SSM_WIDTH, SSM_STATE), lambda p, i: (layer, row_blk(p, i), 0, 0))
    st_out = pl.BlockSpec((1, bb, SSM_WIDTH, SSM_STATE), lambda p, i: ((layer + 1 + p) % passes if passes > 1
                                                                          else layer, i, 0, 0))
    ins = [bm, cm, xde, pre, ea, zg, cd, gn, state]
    in_specs = [blk(BC_W), blk(BC_W), blk(SSM_WIDTH), blk(SSM_WIDTH), blk(SSM_WIDTH), blk(SSM_WIDTH),
                pl.BlockSpec(memory_space=pltpu.SMEM), _const_spec(gn.shape), st_in]
    aliases = {}
    if new_states is not None:
        aliases = {len(ins): 1}
        ins.append(new_states)
        in_specs.append(pl.BlockSpec(memory_space=pl.ANY))
    return pl.pallas_call(
        functools.partial(_ssd_sample_state_body, steps),
        grid=(passes, bsz // bb),
        in_specs=in_specs,
        out_specs=[blk(SSM_WIDTH), st_out],
        out_shape=[jax.ShapeDtypeStruct((bsz * steps, SSM_WIDTH), F32),
                   jax.ShapeDtypeStruct(state.shape, F32)],
        input_output_aliases=aliases,
        compiler_params=_params(("arbitrary", "arbitrary")),
        name="ssd_sample_state",
    )(*ins)


def _alibi_slopes():
    h = np.arange(1, N_ATT_HEADS + 1, dtype=np.float64)
    return np.exp2(-8.0 * h / N_ATT_HEADS).astype(np.float32).reshape(len(ATT_GROUPS), HEADS_PER_GROUP)


def _cache_view(cache):
    depth, bsz, cache_len = cache.shape[:3]
    return jnp.transpose(cache, (0, 1, 3, 4, 5, 2)).reshape(depth, bsz, 2, GROUP_W, cache_len)


def _prev_rows(prev, steps):
    bsz, km1, c = prev.shape
    assert km1 <= steps
    return jnp.pad(prev, ((0, 0), (0, steps - km1), (0, 0))).reshape(bsz * steps, c)


def _layer_weights(lw):
    (norm_mix, q_norm, k_norm, conv_w, conv_b, dt_bias, a_log, d_skip, ssm_norm,
     norm_ffn, ffn_conv_w, ffn_conv_b) = lw
    w = {}
    w["norm_mix"] = norm_mix.reshape(1, D_MODEL)
    w["dtb"] = dt_bias.reshape(1, SSM_HEADS)
    w["gq_g"] = jnp.tile(q_norm, HEADS_PER_GROUP).reshape(1, GROUP_W)
    w["gk_g"] = jnp.tile(k_norm, HEADS_PER_GROUP).reshape(1, GROUP_W)
    w["gq_p"] = jnp.tile(q_norm, 2).reshape(1, PAIR_W)
    w["gk_p"] = jnp.tile(k_norm, 2).reshape(1, PAIR_W)
    w["cw"] = conv_w
    w["cb"] = conv_b.reshape(1, XBC_WIDTH)
    w["alr"] = a_log.reshape(1, SSM_HEADS)
    w["alc"] = a_log.reshape(SSM_HEADS, 1)
    w["dsk"] = jnp.repeat(d_skip, SSM_HEAD_DIM).reshape(1, SSM_WIDTH)
    w["gn"] = ssm_norm.reshape(1, SSM_WIDTH)
    w["norm_ffn"] = norm_ffn.reshape(1, D_MODEL)
    w["fw"] = ffn_conv_w
    w["fb"] = ffn_conv_b.reshape(1, 2 * D_FF)
    return w


def _constants(steps, cache_lens):
    c = {}
    slopes = _alibi_slopes()
    c["bd_g"] = _head_block_diag(GROUP_W)
    c["bd_p"] = _head_block_diag(PAIR_W)
    i = np.arange(SSM_CHUNK)
    c["tri"] = jnp.asarray(i[None, :] <= i[:, None], dtype=BF16)
    lane_head = np.arange(SSM_WIDTH) // SSM_HEAD_DIM
    c["expand"] = jnp.asarray(np.arange(SSM_HEADS)[:, None] == lane_head[None, :], dtype=BF16)
    bc_group = np.arange(BC_W) // SSM_STATE
    c["gexp"] = jnp.asarray(bc_group[:, None] == (lane_head // SSM_HPG)[None, :], dtype=BF16)
    c["hm"] = jnp.asarray(np.arange(HEADS_PER_GROUP)[:, None] == (np.arange(GROUP_W) // HEAD_DIM)[None, :], dtype=F32)
    c["pbias"] = [_prompt_bias(slopes[g], dil) for g, (_, dil) in enumerate(ATT_GROUPS)]
    c["cbias"] = [_sample_cache_bias(slopes[g], dil, steps, cache_lens[g], SUBLANES)
                  for g, (_, dil) in enumerate(ATT_GROUPS)]
    c["nbias"] = jnp.asarray(np.stack([_sample_new_bias(slopes[g], dil, steps, SUBLANES)
                                       for g, (_, dil) in enumerate(ATT_GROUPS)]), dtype=F32)
    return c


def _layer(xp, xs, w, big, c, layer, caches, state, new_states, conv_prev, ffn_prev, tm):
    bsz, seq, _ = xp.shape
    m = bsz * seq
    dec, steps, _ = xs.shape
    ms = dec * steps
    assert steps >= SSM_CONV - 1
    xp2 = xp.reshape(m, D_MODEL)
    xs2 = xs.reshape(ms, D_MODEL)
    qkv, z, xbc, dt = _inproj(xp2, w["norm_mix"], big["w_in"], w["dtb"], layer, tm)
    qkv_s, z_s, xbc_s, dt_s = _inproj(xs2, w["norm_mix"], big["w_in"], w["dtb"], layer, ms)

    qkv_t = qkv.reshape(QKV_SLABS, bsz, seq, PAIR_W)
    os_, ls_, kv_p = [], [], []
    for g, (win, dil) in enumerate(ATT_GROUPS):
        nbk = max(1, PROMPT_ATT_ROWS // (BAND * dil))
        o, lse, kn = _attn_prompt(qkv_t, g, dil, nbk, c["pbias"][g], w["gq_p"], w["gk_p"], c["bd_p"])
        os_.append(o.reshape(2, m, PAIR_W))
        ls_.append(lse.reshape(2, m, PAIR_W))
        keep = min(win, seq)
        v_g = qkv_t[2 * PAIRS + 2 * g:2 * PAIRS + 2 * g + 2, :, seq - keep:]
        kv = jnp.stack([kn[:, :, seq - keep:], v_g])
        kv = jnp.transpose(kv, (2, 3, 0, 1, 4))
        kv_p.append(kv.reshape(bsz, keep, 2, HEADS_PER_GROUP, HEAD_DIM))

    xbc3 = xbc.reshape(bsz, seq, XBC_WIDTH)
    dt3 = dt.reshape(bsz, seq, SSM_HEADS)
    ssd_consts = (w["cw"], w["cb"], w["alr"], w["alc"], w["dsk"], w["gn"], c["tri"], c["expand"])
    att_consts = (*c["cbias"], c["nbias"], w["gq_g"], w["gk_g"], c["bd_g"], c["hm"])
    y, h_last, o_s, lse_s, kn_s = _ssd_prompt_attn_sample(
        xbc3, z.reshape(bsz, seq, SSM_WIDTH), dt3, jnp.swapaxes(dt3, 1, 2), ssd_consts,
        qkv_s, caches, layer, steps, att_consts)

    x1 = _outproj(os_, ls_, y.reshape(m, SSM_WIDTH), xp2, big["w_out"], layer, tm)
    x2, tail = _ffn_prompt(x1.reshape(bsz, seq, D_MODEL), w["norm_ffn"], big["w_up"], w["fw"], w["fb"],
                           big["w_down"], layer, tm)
    h_last = h_last.reshape(bsz, SSM_HEADS, SSM_HEAD_DIM, SSM_STATE)
    conv_p = xbc3[:, seq - (SSM_CONV - 1):]
    ffn_p = tail[:, SUBLANES - (FFN_CONV - 1):]

    kv_s = []
    for g in range(len(ATT_GROUPS)):
        v_g = qkv_s[2 * PAIRS + 2 * g:2 * PAIRS + 2 * g + 2]
        kv = jnp.stack([kn_s[2 * g:2 * g + 2], v_g])
        kv = jnp.transpose(kv, (2, 0, 1, 3))
        kv_s.append(kv.reshape(dec, steps, 2, HEADS_PER_GROUP, HEAD_DIM))
    bm, cm, xde, pre, ea, zg, cd = _ssd_sample_pre(xbc_s, z_s, dt_s, _prev_rows(conv_prev, steps), w["cw"],
                                                   w["cb"], w["alr"], w["dsk"], c["expand"], c["gexp"], steps)
    cd_last = cd.reshape(dec, steps, SSM_HEADS)[:, steps - 1]
    y_s, new_states = _ssd_sample_state(bm, cm, xde, pre, ea, zg, cd_last, w["gn"], state, layer, steps, 8,
                                        new_states)
    x1_s = _outproj([o_s], [lse_s], y_s, xs2, big["w_out"], layer, ms)
    x2_s, u_raw = _ffn_sample(x1_s, steps, w["norm_ffn"], big["w_up"], w["fw"], w["fb"], big["w_down"], layer,
                              _prev_rows(ffn_prev, steps))
    conv_s = jnp.concatenate([conv_prev, xbc_s.reshape(dec, steps, XBC_WIDTH)], axis=1)[:, steps:]
    ffn_s = jnp.concatenate([ffn_prev, u_raw.reshape(dec, steps, 2 * D_FF)], axis=1)[:, steps:]
    return (x2, x2_s.reshape(dec, steps, D_MODEL), (*kv_p, h_last, conv_p, ffn_p), (*kv_s, conv_s, ffn_s),
            new_states)


def kernel(x_prompt, x_sample, cache_kv0, cache_kv1, cache_kv2, state_ssm, state_conv, state_ffn_conv, norm_mix, w_in, q_norm, k_norm, conv_w, conv_b, dt_bias, a_log, d_skip, ssm_norm, w_out, norm_ffn, w_up, ffn_conv_w, ffn_conv_b, w_down):
    stacked = (norm_mix, q_norm, k_norm, conv_w, conv_b, dt_bias, a_log, d_skip, ssm_norm,
               norm_ffn, ffn_conv_w, ffn_conv_b)
    big = {"w_in": w_in.astype(BF16), "w_out": w_out.astype(BF16), "w_up": w_up.astype(BF16),
           "w_down": w_down.astype(BF16)}
    depth = w_in.shape[0]
    dec_batch, steps = x_sample.shape[:2]
    caches = tuple(_cache_view(cache) for cache in (cache_kv0, cache_kv1, cache_kv2))
    for cache, (_, dil) in zip(caches, ATT_GROUPS):
        assert cache.shape[-1] == BAND * dil
    state = state_ssm.reshape(depth, dec_batch, SSM_WIDTH, SSM_STATE)
    c = _constants(steps, tuple(cache.shape[-1] for cache in caches))
    tm = 512
    assert x_prompt.shape[1] % tm == 0
    y_prompt, y_sample = x_prompt, x_sample
    outs_p = [[] for _ in range(6)]
    outs_s = [[] for _ in range(5)]
    new_states = None
    for layer in range(depth):
        w = _layer_weights(tuple(a[layer] for a in stacked))
        y_prompt, y_sample, res_p, res_s, new_states = _layer(
            y_prompt, y_sample, w, big, c, layer, caches, state, new_states, state_conv[layer],
            state_ffn_conv[layer], tm)
        for lst, val in zip(outs_p, res_p):
            lst.append(val)
        for lst, val in zip(outs_s, res_s):
            lst.append(val)
    stack_s = [jnp.stack(l) for l in outs_s]
    return (y_prompt, y_sample, *[jnp.stack(l) for l in outs_p],
            *stack_s[:3], new_states.reshape(state_ssm.shape), *stack_s[3:])
```

```python
import functools
import math

import jax
import jax.numpy as jnp
import numpy as np
from jax import lax
from jax.experimental import pallas as pl
from jax.experimental.pallas import tpu as pltpu

F32 = jnp.float32
BF16 = jnp.bfloat16

D_MODEL = 1024
HEAD_DIM = 64
ATT_GROUPS = ((128, 1), (512, 4), (2048, 16))
BAND = 128
HEADS_PER_GROUP = 4
GROUP_W = HEADS_PER_GROUP * HEAD_DIM
N_ATT_HEADS = HEADS_PER_GROUP * len(ATT_GROUPS)
ATT_WIDTH = N_ATT_HEADS * HEAD_DIM
QKV_W = 3 * ATT_WIDTH
PAIR_W = 2 * HEAD_DIM
PAIRS = ATT_WIDTH // PAIR_W
QKV_SLABS = QKV_W // PAIR_W
SSM_HEAD_DIM = 64
SSM_WIDTH = 1024
SSM_HEADS = SSM_WIDTH // SSM_HEAD_DIM
SSM_STATE = 128
SSM_GROUPS = 2
SSM_HPG = SSM_HEADS // SSM_GROUPS
SSM_GROUP_W = SSM_HPG * SSM_HEAD_DIM
SSM_CONV = 4
SSM_CHUNK = 128
BC_W = SSM_GROUPS * SSM_STATE
XBC_WIDTH = SSM_WIDTH + 2 * BC_W
D_FF = 2816
FFN_CONV = 3
FF_CHUNK = 256
EPS = 1e-6
IN_WIDTH = QKV_W + SSM_WIDTH + XBC_WIDTH + SSM_HEADS
SUBLANES = 8
LANES = 128
SAMPLE_ROWS = 128
PROMPT_ATT_ROWS = 512
RESIDUE_UNROLL = 4
VMEM_LIMIT = 56 * 1024 * 1024
NEG_INF = float("-inf")


def _dot(a, b):
    return jnp.dot(a, b, preferred_element_type=F32)


def _dot_nt(a, b):
    return lax.dot_general(a, b, (((1,), (1,)), ((), ())), preferred_element_type=F32)


def _dot_tn(a, b):
    return lax.dot_general(a, b, (((0,), (0,)), ((), ())), preferred_element_type=F32)


def _split(x, pieces):
    out = []
    r = x
    for _ in range(pieces):
        p = r.astype(BF16)
        out.append(p)
        r = r - p.astype(F32)
    return out


def _dot_sel(x, sel, pieces=3):
    acc = None
    for p in _split(x, pieces):
        t = _dot(p, sel)
        acc = t if acc is None else acc + t
    return acc


def _sel_dot(sel, x, pieces=3):
    acc = None
    for p in _split(x, pieces):
        t = _dot(sel, p)
        acc = t if acc is None else acc + t
    return acc


def _sel_dot_nt(x, sel, pieces=3):
    acc = None
    for p in _split(x, pieces):
        t = _dot_nt(p, sel)
        acc = t if acc is None else acc + t
    return acc


def _silu(x):
    return x * jax.nn.sigmoid(x)


def _softplus(x):
    return jnp.maximum(x, 0.0) + jnp.log1p(jnp.exp(-jnp.abs(x)))


def _rms(x, g):
    ms = jnp.mean(x * x, axis=-1, keepdims=True)
    return x * lax.rsqrt(ms + EPS) * g


def _layer_spec(arr, layer):
    nd = arr.ndim
    return pl.BlockSpec((1,) + arr.shape[1:], lambda *_: (layer,) + (0,) * (nd - 1), pipeline_mode=pl.Buffered(1))


def _const_spec(shape):
    nd = len(shape)
    return pl.BlockSpec(shape, lambda *_: (0,) * nd, pipeline_mode=pl.Buffered(1))


def _params(sem):
    return pltpu.CompilerParams(dimension_semantics=sem, vmem_limit_bytes=VMEM_LIMIT)


def _head_block_diag(width):
    h = np.arange(width) // HEAD_DIM
    return jnp.asarray(np.where(h[:, None] == h[None, :], 1.0 / HEAD_DIM, 0.0), dtype=BF16)


def _head_rms(t, g, bd):
    ms = _dot_sel(t * t, bd, pieces=2)
    return t * lax.rsqrt(ms + EPS) * g


def _rows_back(slab_ref, s, d, n):
    return slab_ref[s, pl.ds(SUBLANES - d, n, stride=1), :]


def _inproj_body(x_ref, g_ref, w_ref, dtb_ref, qkv_ref, z_ref, xbc_ref, dt_ref):
    h = _rms(x_ref[...], g_ref[...]).astype(BF16)
    qkv = _dot(h, w_ref[0, :, 0:QKV_W])
    for s in range(QKV_SLABS):
        qkv_ref[s] = qkv[:, s * PAIR_W:(s + 1) * PAIR_W]
    z_ref[...] = _dot(h, w_ref[0, :, QKV_W:QKV_W + SSM_WIDTH])
    xd = _dot(h, w_ref[0, :, QKV_W + SSM_WIDTH:IN_WIDTH])
    xbc_ref[...] = xd[:, :XBC_WIDTH]
    dt_ref[...] = _softplus(xd[:, XBC_WIDTH:] + dtb_ref[...])


def _inproj(x, g, w_in, dtb, layer, tm):
    m = x.shape[0]
    row = lambda w: pl.BlockSpec((tm, w), lambda i: (i, 0))
    widths = (SSM_WIDTH, XBC_WIDTH, SSM_HEADS)
    return pl.pallas_call(
        _inproj_body,
        grid=(m // tm,),
        in_specs=[row(D_MODEL), _const_spec(g.shape), _layer_spec(w_in, layer), _const_spec(dtb.shape)],
        out_specs=[pl.BlockSpec((QKV_SLABS, tm, PAIR_W), lambda i: (0, i, 0))] + [row(w) for w in widths],
        out_shape=[jax.ShapeDtypeStruct((QKV_SLABS, m, PAIR_W), F32)]
                  + [jax.ShapeDtypeStruct((m, w), F32) for w in widths],
        compiler_params=_params(("parallel",)),
        name="inproj",
    )(x, g, w_in, dtb)


def _attn_prompt_body(dil, nbk, q0_ref, q1_ref, k0_ref, k1_ref, v0_ref, v1_ref, bias_ref, gq_ref, gk_ref, bd_ref,
                      o_ref, lse_ref, kn_ref, kv_scr):
    n = pl.program_id(1)
    rows_per = nbk * BAND
    in_refs = ((q0_ref, q1_ref), (k0_ref, k1_ref), (v0_ref, v1_ref))

    @pl.when(n == 0)
    def _():
        kv_scr[...] = jnp.zeros(kv_scr.shape, BF16)

    variant = jnp.minimum(n, 1)
    bd = bd_ref[...]
    gq = gq_ref[...]
    gk = gk_ref[...]
    low = lax.broadcasted_iota(jnp.int32, (BAND, PAIR_W), 1) < HEAD_DIM

    def rows_of(r, start, count):
        return pl.ds(r + start * dil, count, stride=dil) if dil > 1 else pl.ds(start, count)

    def group(residues):
        items = []
        for r in residues:
            for p in range(2):
                q, k, v = (in_refs[kind][p][0, 0, rows_of(r, 0, rows_per), :] for kind in range(3))
                qn = _head_rms(q, gq, bd) * (HEAD_DIM ** -0.5)
                kn = _head_rms(k, gk, bd)
                kall = jnp.concatenate([kv_scr[0, p, r], kn.astype(BF16)], axis=0)
                vall = jnp.concatenate([kv_scr[1, p, r], v.astype(BF16)], axis=0)
                items.append((r, p, qn, kn, kall, vall))
        scores = []
        for r, p, qn, kn, kall, vall in items:
            for j in range(nbk):
                qj = qn[j * BAND:(j + 1) * BAND]
                for hh in range(2):
                    keep = low if hh == 0 else jnp.logical_not(low)
                    qm = jnp.where(keep, qj, 0.0).astype(BF16)
                    bias = bias_ref[variant if j == 0 else 1, 2 * p + hh]
                    scores.append(_dot_nt(qm, kall[j * BAND:(j + 2) * BAND]) + bias)
        probs = []
        for s in scores:
            mx = jnp.max(s, axis=-1, keepdims=True)
            e = jnp.exp(s - mx)
            den = jnp.sum(e, axis=-1, keepdims=True)
            probs.append((e.astype(BF16), den, mx + jnp.log(den)))
        results = []
        it = iter(probs)
        for r, p, qn, kn, kall, vall in items:
            for j in range(nbk):
                (e0, d0, l0), (e1, d1, l1) = next(it), next(it)
                vwin = vall[j * BAND:(j + 2) * BAND]
                o = jnp.where(low, _dot(e0, vwin) / d0, _dot(e1, vwin) / d1)
                results.append((r, p, j, o, jnp.where(low, l0, l1)))
        for r, p, j, o, lse in results:
            o_ref[p, 0, rows_of(r, j * BAND, BAND), :] = o
            lse_ref[p, 0, rows_of(r, j * BAND, BAND), :] = lse
        for r, p, qn, kn, kall, vall in items:
            kn_ref[p, 0, rows_of(r, 0, rows_per), :] = kn
            kv_scr[0, p, r] = kall[rows_per:]
            kv_scr[1, p, r] = vall[rows_per:]

    if dil <= RESIDUE_UNROLL:
        group(range(dil))
    else:
        def step(i, carry):
            group([i * RESIDUE_UNROLL + k for k in range(RESIDUE_UNROLL)])
            return carry
        lax.fori_loop(0, dil // RESIDUE_UNROLL, step, 0)


def _prompt_bias(slopes, dil):
    i = np.arange(BAND)[:, None]
    j = np.arange(2 * BAND)[None, :]
    step = i + BAND - j
    valid = (step >= 0) & (step <= BAND)
    bias = -slopes[:, None, None] * (step * dil).astype(np.float32)
    full = np.where(valid[None], bias, NEG_INF)
    first = np.where((valid & (j >= BAND))[None], bias, NEG_INF)
    return jnp.asarray(np.stack([first, full]), dtype=F32)


def _attn_prompt(qkv_t, g, dil, nbk, bias, gq, gk, bd):
    _, bsz, seq, _ = qkv_t.shape
    blk = nbk * BAND * dil
    assert seq % blk == 0
    slab = lambda s: pl.BlockSpec((1, 1, blk, PAIR_W), lambda b, n: (s, b, n, 0))
    ospec = pl.BlockSpec((2, 1, blk, PAIR_W), lambda b, n: (0, b, n, 0))
    first = [kind * PAIRS + 2 * g for kind in range(3)]
    return pl.pallas_call(
        functools.partial(_attn_prompt_body, dil, nbk),
        grid=(bsz, seq // blk),
        in_specs=[slab(first[0]), slab(first[0] + 1), slab(first[1]), slab(first[1] + 1),
                  slab(first[2]), slab(first[2] + 1),
                  _const_spec(bias.shape), _const_spec(gq.shape), _const_spec(gk.shape), _const_spec(bd.shape)],
        out_specs=[ospec, ospec, ospec],
        out_shape=[jax.ShapeDtypeStruct((2, bsz, seq, PAIR_W), F32)] * 3,
        scratch_shapes=[pltpu.VMEM((2, 2, dil, BAND, PAIR_W), BF16)],
        compiler_params=_params(("parallel", "arbitrary")),
        name=f"attn_prompt_g{g}",
    )(*([qkv_t] * 6), bias, gq, gk, bd)


def _ssd_stage_conv(xbc_ref, conv_scr):
    raw = xbc_ref[0]
    for s in range(XBC_WIDTH // LANES):
        conv_scr[s, SUBLANES:, :] = raw[:, s * LANES:(s + 1) * LANES]
    return raw


def _ssd_chunk(raw, z_ref, dt_ref, dtt_ref, cw_ref, cb_ref, alr_ref, alc_ref, dsk_ref, gn_ref,
               tri_ref, exp_ref, conv_scr, state_scr):
    q = SSM_CHUNK

    cw = cw_ref[...]
    z = z_ref[0]
    state = state_scr[...]
    n_slabs = XBC_WIDTH // LANES
    parts = []
    for s in range(n_slabs):
        lanes = slice(s * LANES, (s + 1) * LANES)
        acc = _rows_back(conv_scr, s, 3, q) * cw[0:1, lanes]
        acc = acc + _rows_back(conv_scr, s, 2, q) * cw[1:2, lanes]
        acc = acc + _rows_back(conv_scr, s, 1, q) * cw[2:3, lanes]
        parts.append(_silu(acc + raw[:, lanes] * cw[3:4, lanes] + cb_ref[:, lanes]))
        if s % 3 == 2:
            yield
    xc = jnp.concatenate(parts, axis=1)
    xs = xc[:, :SSM_WIDTH]
    bm = xc[:, SSM_WIDTH:SSM_WIDTH + BC_W]
    cm = xc[:, SSM_WIDTH + BC_W:]

    tri = tri_ref[...]
    dt = dt_ref[0]
    acum = _sel_dot(tri, dt * (-jnp.exp(alr_ref[...])))
    acum_t = _sel_dot_nt(dtt_ref[0] * (-jnp.exp(alc_ref[...])), tri)
    yield
    last = acum[q - 1:q, :]
    expand = exp_ref[...]
    dt_x = _dot_sel(dt, expand)
    ea_x = _dot_sel(jnp.exp(acum), expand)
    de_x = _dot_sel(jnp.exp(last - acum), expand)
    xdt = xs * dt_x
    xde = (xdt * de_x).astype(BF16)
    xdt_b = xdt.astype(BF16)
    causal = lax.broadcasted_iota(jnp.int32, (q, q), 0) >= lax.broadcasted_iota(jnp.int32, (q, q), 1)
    low = lax.broadcasted_iota(jnp.int32, (q, 2 * SSM_HEAD_DIM), 1) < SSM_HEAD_DIM

    y_parts = []
    new_state = []
    for g in range(SSM_GROUPS):
        bm_g = bm[:, g * SSM_STATE:(g + 1) * SSM_STATE].astype(BF16)
        cm_g = cm[:, g * SSM_STATE:(g + 1) * SSM_STATE].astype(BF16)
        cb = _dot_nt(cm_g, bm_g)
        rows = slice(g * SSM_GROUP_W, (g + 1) * SSM_GROUP_W)
        st_g = state[rows, :]
        inter = _dot_nt(cm_g, st_g.astype(BF16))
        for hp in range(SSM_HPG // 2):
            pair = None
            for k in range(2):
                h = g * SSM_HPG + 2 * hp + k
                seg = acum[:, h:h + 1] - acum_t[h:h + 1, :]
                decay = jnp.exp(jnp.where(causal, seg, NEG_INF))
                gmat = (cb * decay).astype(BF16)
                lanes = slice((2 * hp) * SSM_HEAD_DIM + g * SSM_GROUP_W,
                              (2 * hp + 2) * SSM_HEAD_DIM + g * SSM_GROUP_W)
                x_pair = xdt_b[:, lanes]
                keep = low if k == 0 else jnp.logical_not(low)
                t = _dot(gmat, jnp.where(keep, x_pair, jnp.zeros_like(x_pair)))
                pair = t if pair is None else pair + t
            y_parts.append(pair + ea_x[:, lanes] * inter[:, lanes.start - g * SSM_GROUP_W:lanes.stop - g * SSM_GROUP_W])
            yield
        new = _dot_tn(xde[:, rows], bm_g)
        for hh in range(SSM_HPG):
            h = g * SSM_HPG + hh
            cd = jnp.exp(acum_t[h:h + 1, q - 1:q])
            r = slice(h * SSM_HEAD_DIM, (h + 1) * SSM_HEAD_DIM)
            rl = slice(hh * SSM_HEAD_DIM, (hh + 1) * SSM_HEAD_DIM)
            new_state.append((r, st_g[rl, :] * cd + new[rl, :]))

    yield
    y = jnp.concatenate(y_parts, axis=1) + dsk_ref[...] * xs
    y = y * _silu(z)
    return _rms(y, gn_ref[...]), new_state


def _ssd_store(raw, y, new_state, y_ref, conv_scr, state_scr):
    y_ref[0] = y
    for r, val in new_state:
        state_scr[r, :] = val
    for s in range(XBC_WIDTH // LANES):
        conv_scr[s, 0:SUBLANES, :] = raw[SSM_CHUNK - SUBLANES:, s * LANES:(s + 1) * LANES]


def _interleave(*gens):
    results = [None] * len(gens)
    live = list(range(len(gens)))
    while live:
        for i in list(live):
            try:
                next(gens[i])
            except StopIteration as done:
                results[i] = done.value
                live.remove(i)
    return results


def _ssd_attn_body(steps, dils, n_ssd, *refs):
    ssd_in, rest = refs[:n_ssd], refs[n_ssd:]
    (qkv_ref, kv0_ref, kv1_ref, kv2_ref, cb0_ref, cb1_ref, cb2_ref, nb_ref, gq_ref, gk_ref, bd_ref, hm_ref,
     y_ref, hl_ref, o_ref, lse_ref, kn_ref, conv_scr, state_scr) = rest
    c = pl.program_id(1)
    n_b = SUBLANES // steps
    b = (pl.program_id(0) * pl.num_programs(1) + c) % n_b

    @pl.when(c == 0)
    def _():
        conv_scr[:, 0:SUBLANES, :] = jnp.zeros((conv_scr.shape[0], SUBLANES, LANES), F32)
        state_scr[...] = jnp.zeros_like(state_scr)

    raw = _ssd_stage_conv(ssd_in[0], conv_scr)
    sample, (y, new_state) = _interleave(
        _sample_attn_batch(steps, dils, b, qkv_ref, (kv0_ref, kv1_ref, kv2_ref), (cb0_ref, cb1_ref, cb2_ref),
                           nb_ref, gq_ref, gk_ref, bd_ref, hm_ref),
        _ssd_chunk(raw, *ssd_in[1:], conv_scr, state_scr))
    _ssd_store(raw, y, new_state, y_ref, conv_scr, state_scr)

    mine = lax.broadcasted_iota(jnp.int32, (SUBLANES, PAIR_W), 0) // steps == b

    @pl.when(b == 0)
    def _():
        for g, (o_g, l_g, kn) in enumerate(sample):
            for p in range(2):
                lanes = slice(p * PAIR_W, (p + 1) * PAIR_W)
                o_ref[2 * g + p] = jnp.where(mine, o_g[:, lanes], 0.0)
                lse_ref[2 * g + p] = jnp.where(mine, l_g[:, lanes], 0.0)
                kn_ref[2 * g + p] = kn[:, lanes]

    @pl.when(b > 0)
    def _():
        for g, (o_g, l_g, kn) in enumerate(sample):
            for p in range(2):
                lanes = slice(p * PAIR_W, (p + 1) * PAIR_W)
                o_ref[2 * g + p] = jnp.where(mine, o_g[:, lanes], o_ref[2 * g + p])
                lse_ref[2 * g + p] = jnp.where(mine, l_g[:, lanes], lse_ref[2 * g + p])

    @pl.when(c == pl.num_programs(1) - 1)
    def _():
        hl_ref[0] = state_scr[...]


def _ssd_prompt_attn_sample(xbc, z, dt, dtt, ssd_consts, qkv_t, caches, layer, steps, att_consts):
    bsz, seq, _ = xbc.shape
    assert seq % SSM_CHUNK == 0
    nc = seq // SSM_CHUNK
    m = qkv_t.shape[1]
    n_b = SUBLANES // steps
    assert SUBLANES % steps == 0 and m == bsz * nc * steps
    dils = tuple(d for _, d in ATT_GROUPS)
    blk = lambda w: pl.BlockSpec((1, SSM_CHUNK, w), lambda b, c: (b, c, 0))
    tile = lambda b, c: (b * nc + c) // n_b
    kv_specs = [pl.BlockSpec((1, 1) + cache.shape[2:], lambda b, c: (layer, b * nc + c, 0, 0, 0))
                for cache in caches]
    ospec = pl.BlockSpec((PAIRS, SUBLANES, PAIR_W), lambda b, c: (0, tile(b, c), 0))
    ssd_in = [xbc, z, dt, dtt, *ssd_consts]
    return pl.pallas_call(
        functools.partial(_ssd_attn_body, steps, dils, len(ssd_in)),
        grid=(bsz, nc),
        in_specs=[blk(XBC_WIDTH), blk(SSM_WIDTH), blk(SSM_HEADS),
                  pl.BlockSpec((1, SSM_HEADS, SSM_CHUNK), lambda b, c: (b, 0, c))]
                 + [_const_spec(a.shape) for a in ssd_consts]
                 + [pl.BlockSpec((QKV_SLABS, SUBLANES, PAIR_W), lambda b, c: (0, tile(b, c), 0))] + kv_specs
                 + [_const_spec(a.shape) for a in att_consts],
        out_specs=[blk(SSM_WIDTH), pl.BlockSpec((1, SSM_WIDTH, SSM_STATE), lambda b, c: (b, 0, 0)),
                   ospec, ospec, ospec],
        out_shape=[jax.ShapeDtypeStruct((bsz, seq, SSM_WIDTH), F32),
                   jax.ShapeDtypeStruct((bsz, SSM_WIDTH, SSM_STATE), F32)]
                  + [jax.ShapeDtypeStruct((PAIRS, m, PAIR_W), F32)] * 3,
        scratch_shapes=[pltpu.VMEM((XBC_WIDTH // LANES, SUBLANES + SSM_CHUNK, LANES), F32),
                        pltpu.VMEM((SSM_WIDTH, SSM_STATE), F32)],
        compiler_params=_params(("arbitrary", "arbitrary")),
        name="ssd_prompt_attn_sample",
    )(*ssd_in, qkv_t, *caches, *att_consts)


def _mix_project(o_sl, l_sl, y, x, w_ref):
    ngrp = len(ATT_GROUPS)
    att = [None] * PAIRS
    for p in range(PAIRS // ngrp):
        ls = [l_sl[2 * g + p] for g in range(ngrp)]
        mx = functools.reduce(jnp.maximum, ls)
        es = [jnp.exp(l - mx) for l in ls]
        inv = 1.0 / functools.reduce(jnp.add, es)
        for g in range(ngrp):
            att[2 * g + p] = (o_sl[2 * g + p] * (es[g] * inv)).astype(BF16)
    acc = x + _dot(y.astype(BF16), w_ref[0, ATT_WIDTH:, :])
    return acc + _dot(jnp.concatenate(att, axis=1), w_ref[0, 0:ATT_WIDTH, :])


def _outproj_body(slabs, *refs):
    n_arr = len(slabs)
    o_refs, l_refs = refs[:n_arr], refs[n_arr:2 * n_arr]
    y_ref, x_ref, w_ref, out_ref = refs[2 * n_arr:]
    o_sl = [r[s] for r, cnt in zip(o_refs, slabs) for s in range(cnt)]
    l_sl = [r[s] for r, cnt in zip(l_refs, slabs) for s in range(cnt)]
    out_ref[...] = _mix_project(o_sl, l_sl, y_ref[...], x_ref[...], w_ref)


def _outproj(os_, ls_, y, x, w_out, layer, tm):
    m = x.shape[0]
    slabs = tuple(a.shape[0] for a in os_)
    assert sum(slabs) == PAIRS
    row = lambda w: pl.BlockSpec((tm, w), lambda i: (i, 0))
    slab_specs = [pl.BlockSpec((n, tm, PAIR_W), lambda i: (0, i, 0)) for n in slabs]
    return pl.pallas_call(
        functools.partial(_outproj_body, slabs),
        grid=(m // tm,),
        in_specs=slab_specs * 2 + [row(SSM_WIDTH), row(D_MODEL), _layer_spec(w_out, layer)],
        out_specs=row(D_MODEL),
        out_shape=jax.ShapeDtypeStruct((m, D_MODEL), F32),
        compiler_params=_params(("parallel",)),
        name="outproj",
    )(*os_, *ls_, y, x, w_out)


def _ffn_chunk(h, c, wu_ref, fw_ref, fb_ref, shifted):
    halves = []
    for base in (0, D_FF):
        cols = slice(base + c * FF_CHUNK, base + (c + 1) * FF_CHUNK)
        u = _dot(h, wu_ref[0, :, cols])
        s2, s1 = shifted(u, cols)
        w = fw_ref[:, cols]
        halves.append(s2 * w[0:1] + s1 * w[1:2] + u * w[2:3] + fb_ref[:, cols])
    return (_silu(halves[0]) * halves[1]).astype(BF16)


def _ffn_prompt_body(slabs, *refs):
    n_arr = len(slabs)
    o_refs, l_refs = refs[:n_arr], refs[n_arr:2 * n_arr]
    (y_ref, x_ref, wo_ref, g_ref, wu_ref, fw_ref, fb_ref, wd_ref,
     out_ref, tail_ref, u_scr, carry_scr, act_scr) = refs[2 * n_arr:]
    i = pl.program_id(1)

    @pl.when(i == 0)
    def _():
        carry_scr[...] = jnp.zeros_like(carry_scr)

    o_sl = [r[s, 0] for r, cnt in zip(o_refs, slabs) for s in range(cnt)]
    l_sl = [r[s, 0] for r, cnt in zip(l_refs, slabs) for s in range(cnt)]
    x = _mix_project(o_sl, l_sl, y_ref[0], x_ref[0], wo_ref)
    tm = x.shape[0]
    h = _rms(x, g_ref[...]).astype(BF16)
    new_carry = []

    def shifted(u, cols):
        half, start = divmod(cols.start, D_FF)
        par = (start // FF_CHUNK) % 2
        s2, s1 = [], []
        for k in range(FF_CHUNK // LANES):
            sl = half * (FF_CHUNK // LANES) + k
            lanes = slice(cols.start + k * LANES, cols.start + (k + 1) * LANES)
            u_scr[par, sl, 0:SUBLANES, :] = carry_scr[:, lanes]
            u_scr[par, sl, SUBLANES:, :] = u[:, k * LANES:(k + 1) * LANES]
            s2.append(u_scr[par, sl, pl.ds(SUBLANES - 2, tm, stride=1), :])
            s1.append(u_scr[par, sl, pl.ds(SUBLANES - 1, tm, stride=1), :])
            new_carry.append((lanes, u[tm - SUBLANES:, k * LANES:(k + 1) * LANES]))
        return jnp.concatenate(s2, axis=1), jnp.concatenate(s1, axis=1)

    for c in range(D_FF // FF_CHUNK):
        act_scr[:, c * FF_CHUNK:(c + 1) * FF_CHUNK] = _ffn_chunk(h, c, wu_ref, fw_ref, fb_ref, shifted)
    out_ref[0] = x + _dot(act_scr[...], wd_ref[0])
    for lanes, rows in new_carry:
        carry_scr[:, lanes] = rows

    @pl.when(i == pl.num_programs(1) - 1)
    def _():
        tail_ref[0] = carry_scr[...]


def _ffn_prompt(os_, ls_, y, x, w_out, g, wu, fw, fb, wd, layer, tm):
    bsz, seq, _ = x.shape
    slabs = tuple(a.shape[0] for a in os_)
    assert sum(slabs) == PAIRS
    row = lambda w: pl.BlockSpec((1, tm, w), lambda b, i: (b, i, 0))
    slab_specs = [pl.BlockSpec((n, 1, tm, PAIR_W), lambda b, i: (0, b, i, 0)) for n in slabs]
    chunk_slabs = 2 * FF_CHUNK // LANES
    return pl.pallas_call(
        functools.partial(_ffn_prompt_body, slabs),
        grid=(bsz, seq // tm),
        in_specs=slab_specs * 2 + [row(SSM_WIDTH), row(D_MODEL), _layer_spec(w_out, layer), _const_spec(g.shape),
                                   _layer_spec(wu, layer), _const_spec(fw.shape), _const_spec(fb.shape),
                                   _layer_spec(wd, layer)],
        out_specs=[row(D_MODEL), pl.BlockSpec((1, SUBLANES, 2 * D_FF), lambda b, i: (b, 0, 0))],
        out_shape=[jax.ShapeDtypeStruct((bsz, seq, D_MODEL), F32),
                   jax.ShapeDtypeStruct((bsz, SUBLANES, 2 * D_FF), F32)],
        scratch_shapes=[pltpu.VMEM((2, chunk_slabs, SUBLANES + tm, LANES), F32),
                        pltpu.VMEM((SUBLANES, 2 * D_FF), F32), pltpu.VMEM((tm, D_FF), BF16)],
        compiler_params=_params(("parallel", "arbitrary")),
        name="outproj_ffn_prompt",
    )(*os_, *ls_, y, x, w_out, g, wu, fw, fb, wd)


def _ffn_sample_body(steps, x_ref, g_ref, wu_ref, fw_ref, fb_ref, wd_ref, prev_ref, out_ref, u_ref):
    x = x_ref[...]
    rows = x.shape[0]
    h = _rms(x, g_ref[...]).astype(BF16)
    t = lax.broadcasted_iota(jnp.int32, (rows, FF_CHUNK), 0) % steps

    def shifted(u, cols):
        u_ref[:, cols] = u
        prev = prev_ref[:, cols]
        s1 = jnp.where(t >= 1, pltpu.roll(u, 1, axis=0), pltpu.roll(prev, rows - 1, axis=0))
        s2 = jnp.where(t >= 2, pltpu.roll(u, 2, axis=0), prev)
        return s2, s1

    act = [_ffn_chunk(h, c, wu_ref, fw_ref, fb_ref, shifted) for c in range(D_FF // FF_CHUNK)]
    out_ref[...] = x + _dot(jnp.concatenate(act, axis=1), wd_ref[0])


def _ffn_sample(x, steps, g, wu, fw, fb, wd, layer, prev):
    m = x.shape[0]
    tr = min(SAMPLE_ROWS, m)
    assert m % tr == 0 and tr % steps == 0
    row = lambda w: pl.BlockSpec((tr, w), lambda i: (i, 0))
    return pl.pallas_call(
        functools.partial(_ffn_sample_body, steps),
        grid=(m // tr,),
        in_specs=[row(D_MODEL), _const_spec(g.shape), _layer_spec(wu, layer), _const_spec(fw.shape),
                  _const_spec(fb.shape), _layer_spec(wd, layer), row(2 * D_FF)],
        out_specs=[row(D_MODEL), row(2 * D_FF)],
        out_shape=[jax.ShapeDtypeStruct((m, D_MODEL), F32), jax.ShapeDtypeStruct((m, 2 * D_FF), F32)],
        compiler_params=_params(("parallel",)),
        name="ffn_sample",
    )(x, g, wu, fw, fb, wd, prev)


def _sample_attn_batch(steps, dils, b, qkv_ref, kv_refs, cb_refs, nb_ref, gq_ref, gk_ref, bd_ref, hm_ref):
    rows = qkv_ref.shape[1]
    assert rows == 2 * steps
    bd = bd_ref[...]
    hm = hm_ref[...]
    groups = range(len(dils))
    st = [dict() for _ in groups]
    for g in groups:
        slab = lambda kind: jnp.concatenate([qkv_ref[kind * PAIRS + 2 * g], qkv_ref[kind * PAIRS + 2 * g + 1]], axis=1)
        qn = _head_rms(slab(0), gq_ref[...], bd) * (HEAD_DIM ** -0.5)
        kn = _head_rms(slab(1), gk_ref[...], bd)
        vn = slab(2)
        st[g]["kn"] = kn
        st[g]["kn_b"] = jnp.where(b == 0, kn[0:steps], kn[steps:])
        st[g]["vn_b"] = jnp.where(b == 0, vn[0:steps], vn[steps:])
        st[g]["qbd"] = jnp.concatenate([qn * hm[h:h + 1] for h in range(HEADS_PER_GROUP)], axis=0)
        yield
    for g in groups:
        kc = kv_refs[g][0, 0, 0].astype(BF16)
        st[g]["s_c"] = _dot(st[g]["qbd"].astype(BF16), kc) + cb_refs[g][...]
        yield
    for g in groups:
        d = st[g]
        mx = jnp.max(d["s_c"], axis=-1, keepdims=True)
        d["s_n"] = []
        for t2 in range(steps):
            sn = jnp.sum(d["qbd"] * d["kn_b"][t2:t2 + 1], axis=-1, keepdims=True) + nb_ref[g, t2]
            d["s_n"].append(sn)
            mx = jnp.maximum(mx, sn)
        d["mx"] = mx
        d["e_c"] = jnp.exp(d["s_c"] - mx)
        d["den"] = jnp.sum(d["e_c"], axis=-1, keepdims=True)
        yield
    for g in groups:
        vc = kv_refs[g][0, 0, 1].astype(BF16)
        st[g]["acc"] = _dot_nt(st[g]["e_c"].astype(BF16), vc)
        yield
    results = []
    for g in groups:
        d = st[g]
        acc, den = d["acc"], d["den"]
        for t2, sn in enumerate(d["s_n"]):
            e_n = jnp.exp(sn - d["mx"])
            den = den + e_n
            acc = acc + e_n * d["vn_b"][t2:t2 + 1]
        o_all = acc / den
        l_all = d["mx"] + jnp.log(den)
        o_g = None
        l_g = None
        for h in range(HEADS_PER_GROUP):
            o_h = o_all[h * rows:(h + 1) * rows] * hm[h:h + 1]
            l_h = l_all[h * rows:(h + 1) * rows] * hm[h:h + 1]
            o_g = o_h if o_g is None else o_g + o_h
            l_g = l_h if l_g is None else l_g + l_h
        results.append((o_g, l_g, d["kn"]))
        yield
    return results


def _sample_cache_bias(slopes, dil, steps, cache_len, rows):
    t = (np.arange(rows) % steps)[None, :, None]
    c = np.arange(cache_len)[None, None, :]
    dist = cache_len + t - c
    j = dist // dil
    valid = (dist % dil == 0) & (j >= 1) & (j <= BAND)
    bias = -slopes[:, None, None] * dist.astype(np.float32)
    return jnp.asarray(np.where(valid, bias, NEG_INF).reshape(slopes.shape[0] * rows, cache_len), dtype=F32)


def _sample_new_bias(slopes, dil, steps, rows):
    t2 = np.arange(steps)[:, None, None]
    t = (np.arange(rows) % steps)[None, None, :]
    dist = t - t2
    valid = (dist >= 0) & (dist % dil == 0) & (dist // dil <= BAND)
    bias = -slopes[None, :, None] * dist.astype(np.float32)
    return np.where(valid, bias, NEG_INF).reshape(steps, slopes.shape[0] * rows, 1)


def _ssd_sample_pre_body(steps, xbc_ref, z_ref, dt_ref, prev_ref, cw_ref, cb_ref, al_ref, dsk_ref,
                         exp_ref, gexp_ref, bm_ref, cm_ref, xde_ref, pre_ref, ea_ref, zg_ref, cd_ref):
    raw = xbc_ref[...]
    rows = raw.shape[0]
    tw = lax.broadcasted_iota(jnp.int32, (rows, XBC_WIDTH), 0) % steps
    cw = cw_ref[...]
    prev = prev_ref[...]
    acc = jnp.where(tw >= 3, pltpu.roll(raw, 3, axis=0), prev) * cw[0:1]
    acc = acc + jnp.where(tw >= 2, pltpu.roll(raw, 2, axis=0), pltpu.roll(prev, rows - 1, axis=0)) * cw[1:2]
    acc = acc + jnp.where(tw >= 1, pltpu.roll(raw, 1, axis=0), pltpu.roll(prev, rows - 2, axis=0)) * cw[2:3]
    acc = acc + raw * cw[3:4]
    xc = _silu(acc + cb_ref[...])
    xs = xc[:, :SSM_WIDTH]
    bm = xc[:, SSM_WIDTH:SSM_WIDTH + BC_W]
    cm = xc[:, SSM_WIDTH + BC_W:]
    bm_ref[...] = bm
    cm_ref[...] = cm

    dt = dt_ref[...]
    da = dt * (-jnp.exp(al_ref[...]))
    th = lax.broadcasted_iota(jnp.int32, (rows, SSM_HEADS), 0) % steps
    acum = da
    for d in range(1, steps):
        acum = acum + jnp.where(th >= d, pltpu.roll(da, d, axis=0), 0.0)
    tail = jnp.zeros_like(da)
    for d in range(1, steps):
        tail = tail + jnp.where(th + d < steps, pltpu.roll(da, rows - d, axis=0), 0.0)
    expand = exp_ref[...]
    dt_x = _dot_sel(dt, expand)
    acum_x = _dot_sel(acum, expand)
    ea_ref[...] = jnp.exp(acum_x)
    xdt = xs * dt_x
    xde_ref[...] = xdt * _dot_sel(jnp.exp(tail), expand)
    cd_ref[...] = jnp.exp(acum)

    ts = lax.broadcasted_iota(jnp.int32, (rows, SSM_WIDTH), 0) % steps
    gexp = gexp_ref[...]
    y = dsk_ref[...] * xs
    for d in range(steps):
        if d == 0:
            bm_d, xdt_d, ac_d = bm, xdt, acum_x
        else:
            bm_d = pltpu.roll(bm, d, axis=0)
            xdt_d = pltpu.roll(xdt, d, axis=0)
            ac_d = pltpu.roll(acum_x, d, axis=0)
        cb_x = _dot_sel(cm * bm_d, gexp, pieces=2)
        term = cb_x * jnp.exp(acum_x - ac_d) * xdt_d
        y = y + jnp.where(ts >= d, term, 0.0)
    pre_ref[...] = y
    zg_ref[...] = _silu(z_ref[...])


def _ssd_sample_pre(xbc, z, dt, prev, cw, cb, al, dsk, expand, gexp, steps):
    m = xbc.shape[0]
    tr = min(SAMPLE_ROWS, m)
    assert m % tr == 0 and tr % steps == 0
    row = lambda w: pl.BlockSpec((tr, w), lambda i: (i, 0))
    consts = (cw, cb, al, dsk, expand, gexp)
    widths = (BC_W, BC_W, SSM_WIDTH, SSM_WIDTH, SSM_WIDTH, SSM_WIDTH, SSM_HEADS)
    return pl.pallas_call(
        functools.partial(_ssd_sample_pre_body, steps),
        grid=(m // tr,),
        in_specs=[row(XBC_WIDTH), row(SSM_WIDTH), row(SSM_HEADS), row(XBC_WIDTH)]
                 + [_const_spec(a.shape) for a in consts],
        out_specs=[row(w) for w in widths],
        out_shape=[jax.ShapeDtypeStruct((m, w), F32) for w in widths],
        compiler_params=_params(("parallel",)),
        name="ssd_sample_pre",
    )(xbc, z, dt, prev, *consts)


def _ssd_sample_state_body(steps, bm_ref, cm_ref, xde_ref, pre_ref, ea_ref, zg_ref, cd_ref, gn_ref, st_ref,
                           *rest):
    y_ref, ns_ref = rest[-2:]
    bb = st_ref.shape[1]
    blk = pl.program_id(1)
    last_pass = pl.num_programs(0) - 1

    @pl.when(pl.program_id(0) < last_pass)
    def _():
        ns_ref[...] = jnp.zeros(ns_ref.shape, F32)

    @pl.when(pl.program_id(0) == last_pass)
    def _():
        _ssd_sample_state_step(steps, bb, blk, bm_ref, cm_ref, xde_ref, pre_ref, ea_ref, zg_ref, cd_ref, gn_ref,
                               st_ref, y_ref, ns_ref)


def _ssd_sample_state_step(steps, bb, blk, bm_ref, cm_ref, xde_ref, pre_ref, ea_ref, zg_ref, cd_ref, gn_ref,
                           st_ref, y_ref, ns_ref):

    def dots(b):
        rows = slice(b * steps, (b + 1) * steps)
        bm = bm_ref[rows, :].astype(BF16)
        cm = cm_ref[rows, :].astype(BF16)
        xde = xde_ref[rows, :].astype(BF16)
        inter, new = [], []
        for g in range(SSM_GROUPS):
            srows = slice(g * SSM_GROUP_W, (g + 1) * SSM_GROUP_W)
            lanes = slice(g * SSM_STATE, (g + 1) * SSM_STATE)
            inter.append(_dot_nt(cm[:, lanes], st_ref[0, b, srows, :].astype(BF16)))
            new.append(_dot_tn(xde[:, srows], bm[:, lanes]))
        return inter, new

    def finish(b, inter, new):
        rows = slice(b * steps, (b + 1) * steps)
        y = pre_ref[rows, :] + ea_ref[rows, :] * jnp.concatenate(inter, axis=1)
        y_ref[rows, :] = _rms(y * zg_ref[rows, :], gn_ref[...])
        for h in range(SSM_HEADS):
            g, hh = divmod(h, SSM_HPG)
            cd = cd_ref[blk * bb + b, h]
            r = slice(h * SSM_HEAD_DIM, (h + 1) * SSM_HEAD_DIM)
            ns_ref[0, b, r, :] = st_ref[0, b, r, :] * cd + new[g][hh * SSM_HEAD_DIM:(hh + 1) * SSM_HEAD_DIM, :]

    pending = dots(0)
    for b in range(bb):
        nxt = dots(b + 1) if b + 1 < bb else None
        finish(b, *pending)
        pending = nxt


def _ssd_sample_state(bm, cm, xde, pre, ea, zg, cd, gn, state, layer, steps, bb, new_states):
    bsz = bm.shape[0] // steps
    depth = state.shape[0]
    assert bsz % bb == 0 and (bb * steps) % SUBLANES == 0
    passes = depth if new_states is None else 1
    row_blk = lambda p, i: jnp.where(p == passes - 1, i, 0)
    blk = lambda w: pl.BlockSpec((bb * steps, w), lambda p, i: (row_blk(p, i), 0))
    st_in = pl.BlockSpec((1, bb, SSM_WIDTH, SSM_STATE), lambda p, i: (layer, row_blk(p, i), 0, 0))
    st_out = pl.BlockSpec((1, bb, SSM_WIDTH, SSM_STATE), lambda p, i: ((layer + 1 + p) % passes if passes > 1
                                                                          else layer, i, 0, 0))
    ins = [bm, cm, xde, pre, ea, zg, cd, gn, state]
    in_specs = [blk(BC_W), blk(BC_W), blk(SSM_WIDTH), blk(SSM_WIDTH), blk(SSM_WIDTH), blk(SSM_WIDTH),
                pl.BlockSpec(memory_space=pltpu.SMEM), _const_spec(gn.shape), st_in]
    aliases = {}
    if new_states is not None:
        aliases = {len(ins): 1}
        ins.append(new_states)
        in_specs.append(pl.BlockSpec(memory_space=pl.ANY))
    return pl.pallas_call(
        functools.partial(_ssd_sample_state_body, steps),
        grid=(passes, bsz // bb),
        in_specs=in_specs,
        out_specs=[blk(SSM_WIDTH), st_out],
        out_shape=[jax.ShapeDtypeStruct((bsz * steps, SSM_WIDTH), F32),
                   jax.ShapeDtypeStruct(state.shape, F32)],
        input_output_aliases=aliases,
        compiler_params=_params(("arbitrary", "arbitrary")),
        name="ssd_sample_state",
    )(*ins)


def _alibi_slopes():
    h = np.arange(1, N_ATT_HEADS + 1, dtype=np.float64)
    return np.exp2(-8.0 * h / N_ATT_HEADS).astype(np.float32).reshape(len(ATT_GROUPS), HEADS_PER_GROUP)


def _cache_view(cache):
    depth, bsz, cache_len = cache.shape[:3]
    return jnp.transpose(cache, (0, 1, 3, 4, 5, 2)).reshape(depth, bsz, 2, GROUP_W, cache_len)


def _prev_rows(prev, steps):
    bsz, km1, c = prev.shape
    assert km1 <= steps
    return jnp.pad(prev, ((0, 0), (0, steps - km1), (0, 0))).reshape(bsz * steps, c)


def _layer_weights(lw):
    (norm_mix, q_norm, k_norm, conv_w, conv_b, dt_bias, a_log, d_skip, ssm_norm,
     norm_ffn, ffn_conv_w, ffn_conv_b) = lw
    w = {}
    w["norm_mix"] = norm_mix.reshape(1, D_MODEL)
    w["dtb"] = dt_bias.reshape(1, SSM_HEADS)
    w["gq_g"] = jnp.tile(q_norm, HEADS_PER_GROUP).reshape(1, GROUP_W)
    w["gk_g"] = jnp.tile(k_norm, HEADS_PER_GROUP).reshape(1, GROUP_W)
    w["gq_p"] = jnp.tile(q_norm, 2).reshape(1, PAIR_W)
    w["gk_p"] = jnp.tile(k_norm, 2).reshape(1, PAIR_W)
    w["cw"] = conv_w
    w["cb"] = conv_b.reshape(1, XBC_WIDTH)
    w["alr"] = a_log.reshape(1, SSM_HEADS)
    w["alc"] = a_log.reshape(SSM_HEADS, 1)
    w["dsk"] = jnp.repeat(d_skip, SSM_HEAD_DIM).reshape(1, SSM_WIDTH)
    w["gn"] = ssm_norm.reshape(1, SSM_WIDTH)
    w["norm_ffn"] = norm_ffn.reshape(1, D_MODEL)
    w["fw"] = ffn_conv_w
    w["fb"] = ffn_conv_b.reshape(1, 2 * D_FF)
    return w


def _constants(steps, cache_lens):
    c = {}
    slopes = _alibi_slopes()
    c["bd_g"] = _head_block_diag(GROUP_W)
    c["bd_p"] = _head_block_diag(PAIR_W)
    i = np.arange(SSM_CHUNK)
    c["tri"] = jnp.asarray(i[None, :] <= i[:, None], dtype=BF16)
    lane_head = np.arange(SSM_WIDTH) // SSM_HEAD_DIM
    c["expand"] = jnp.asarray(np.arange(SSM_HEADS)[:, None] == lane_head[None, :], dtype=BF16)
    bc_group = np.arange(BC_W) // SSM_STATE
    c["gexp"] = jnp.asarray(bc_group[:, None] == (lane_head // SSM_HPG)[None, :], dtype=BF16)
    c["hm"] = jnp.asarray(np.arange(HEADS_PER_GROUP)[:, None] == (np.arange(GROUP_W) // HEAD_DIM)[None, :], dtype=F32)
    c["pbias"] = [_prompt_bias(slopes[g], dil) for g, (_, dil) in enumerate(ATT_GROUPS)]
    c["cbias"] = [_sample_cache_bias(slopes[g], dil, steps, cache_lens[g], SUBLANES)
                  for g, (_, dil) in enumerate(ATT_GROUPS)]
    c["nbias"] = jnp.asarray(np.stack([_sample_new_bias(slopes[g], dil, steps, SUBLANES)
                                       for g, (_, dil) in enumerate(ATT_GROUPS)]), dtype=F32)
    return c


def _layer(xp, xs, w, big, c, layer, caches, state, new_states, conv_prev, ffn_prev, tm):
    bsz, seq, _ = xp.shape
    m = bsz * seq
    dec, steps, _ = xs.shape
    ms = dec * steps
    assert steps >= SSM_CONV - 1
    xp2 = xp.reshape(m, D_MODEL)
    xs2 = xs.reshape(ms, D_MODEL)
    qkv, z, xbc, dt = _inproj(xp2, w["norm_mix"], big["w_in"], w["dtb"], layer, tm)
    qkv_s, z_s, xbc_s, dt_s = _inproj(xs2, w["norm_mix"], big["w_in"], w["dtb"], layer, ms)

    qkv_t = qkv.reshape(QKV_SLABS, bsz, seq, PAIR_W)
    os_, ls_, kv_p = [], [], []
    for g, (win, dil) in enumerate(ATT_GROUPS):
        nbk = max(1, PROMPT_ATT_ROWS // (BAND * dil))
        o, lse, kn = _attn_prompt(qkv_t, g, dil, nbk, c["pbias"][g], w["gq_p"], w["gk_p"], c["bd_p"])
        os_.append(o)
        ls_.append(lse)
        keep = min(win, seq)
        v_g = qkv_t[2 * PAIRS + 2 * g:2 * PAIRS + 2 * g + 2, :, seq - keep:]
        kv = jnp.stack([kn[:, :, seq - keep:], v_g])
        kv = jnp.transpose(kv, (2, 3, 0, 1, 4))
        kv_p.append(kv.reshape(bsz, keep, 2, HEADS_PER_GROUP, HEAD_DIM))

    xbc3 = xbc.reshape(bsz, seq, XBC_WIDTH)
    dt3 = dt.reshape(bsz, seq, SSM_HEADS)
    ssd_consts = (w["cw"], w["cb"], w["alr"], w["alc"], w["dsk"], w["gn"], c["tri"], c["expand"])
    att_consts = (*c["cbias"], c["nbias"], w["gq_g"], w["gk_g"], c["bd_g"], c["hm"])
    y, h_last, o_s, lse_s, kn_s = _ssd_prompt_attn_sample(
        xbc3, z.reshape(bsz, seq, SSM_WIDTH), dt3, jnp.swapaxes(dt3, 1, 2), ssd_consts,
        qkv_s, caches, layer, steps, att_consts)

    x2, tail = _ffn_prompt(os_, ls_, y, xp, big["w_out"], w["norm_ffn"], big["w_up"], w["fw"], w["fb"],
                           big["w_down"], layer, tm)
    h_last = h_last.reshape(bsz, SSM_HEADS, SSM_HEAD_DIM, SSM_STATE)
    conv_p = xbc3[:, seq - (SSM_CONV - 1):]
    ffn_p = tail[:, SUBLANES - (FFN_CONV - 1):]

    kv_s = []
    for g in range(len(ATT_GROUPS)):
        v_g = qkv_s[2 * PAIRS + 2 * g:2 * PAIRS + 2 * g + 2]
        kv = jnp.stack([kn_s[2 * g:2 * g + 2], v_g])
        kv = jnp.transpose(kv, (2, 0, 1, 3))
        kv_s.append(kv.reshape(dec, steps, 2, HEADS_PER_GROUP, HEAD_DIM))
    bm, cm, xde, pre, ea, zg, cd = _ssd_sample_pre(xbc_s, z_s, dt_s, _prev_rows(conv_prev, steps), w["cw"],
                                                   w["cb"], w["alr"], w["dsk"], c["expand"], c["gexp"], steps)
    cd_last = cd.reshape(dec, steps, SSM_HEADS)[:, steps - 1]
    y_s, new_states = _ssd_sample_state(bm, cm, xde, pre, ea, zg, cd_last, w["gn"], state, layer, steps, 8,
                                        new_states)
    x1_s = _outproj([o_s], [lse_s], y_s, xs2, big["w_out"], layer, ms)
    x2_s, u_raw = _ffn_sample(x1_s, steps, w["norm_ffn"], big["w_up"], w["fw"], w["fb"], big["w_down"], layer,
                              _prev_rows(ffn_prev, steps))
    conv_s = jnp.concatenate([conv_prev, xbc_s.reshape(dec, steps, XBC_WIDTH)], axis=1)[:, steps:]
    ffn_s = jnp.concatenate([ffn_prev, u_raw.reshape(dec, steps, 2 * D_FF)], axis=1)[:, steps:]
    return (x2, x2_s.reshape(dec, steps, D_MODEL), (*kv_p, h_last, conv_p, ffn_p), (*kv_s, conv_s, ffn_s),
            new_states)


def kernel(x_prompt, x_sample, cache_kv0, cache_kv1, cache_kv2, state_ssm, state_conv, state_ffn_conv, norm_mix, w_in, q_norm, k_norm, conv_w, conv_b, dt_bias, a_log, d_skip, ssm_norm, w_out, norm_ffn, w_up, ffn_conv_w, ffn_conv_b, w_down):
    stacked = (norm_mix, q_norm, k_norm, conv_w, conv_b, dt_bias, a_log, d_skip, ssm_norm,
               norm_ffn, ffn_conv_w, ffn_conv_b)
    big = {"w_in": w_in.astype(BF16), "w_out": w_out.astype(BF16), "w_up": w_up.astype(BF16),
           "w_down": w_down.astype(BF16)}
    depth = w_in.shape[0]
    dec_batch, steps = x_sample.shape[:2]
    caches = tuple(_cache_view(cache) for cache in (cache_kv0, cache_kv1, cache_kv2))
    for cache, (_, dil) in zip(caches, ATT_GROUPS):
        assert cache.shape[-1] == BAND * dil
    state = state_ssm.reshape(depth, dec_batch, SSM_WIDTH, SSM_STATE)
    c = _constants(steps, tuple(cache.shape[-1] for cache in caches))
    tm = 512
    assert x_prompt.shape[1] % tm == 0
    y_prompt, y_sample = x_prompt, x_sample
    outs_p = [[] for _ in range(6)]
    outs_s = [[] for _ in range(5)]
    new_states = None
    for layer in range(depth):
        w = _layer_weights(tuple(a[layer] for a in stacked))
        y_prompt, y_sample, res_p, res_s, new_states = _layer(
            y_prompt, y_sample, w, big, c, layer, caches, state, new_states, state_conv[layer],
            state_ffn_conv[layer], tm)
        for lst, val in zip(outs_p, res_p):
            lst.append(val)
        for lst, val in zip(outs_s, res_s):
            lst.append(val)
    stack_s = [jnp.stack(l) for l in outs_s]
    return (y_prompt, y_sample, *[jnp.stack(l) for l in outs_p],
            *stack_s[:3], new_states.reshape(state_ssm.shape), *stack_s[3:])
```

```python
import functools
import math

import jax
import jax.numpy as jnp
import numpy as np
from jax import lax
from jax.experimental import pallas as pl
from jax.experimental.pallas import tpu as pltpu

F32 = jnp.float32
BF16 = jnp.bfloat16

D_MODEL = 1024
HEAD_DIM = 64
ATT_GROUPS = ((128, 1), (512, 4), (2048, 16))
BAND = 128
HEADS_PER_GROUP = 4
GROUP_W = HEADS_PER_GROUP * HEAD_DIM
N_ATT_HEADS = HEADS_PER_GROUP * len(ATT_GROUPS)
ATT_WIDTH = N_ATT_HEADS * HEAD_DIM
QKV_W = 3 * ATT_WIDTH
PAIR_W = 2 * HEAD_DIM
PAIRS = ATT_WIDTH // PAIR_W
QKV_SLABS = QKV_W // PAIR_W
SSM_HEAD_DIM = 64
SSM_WIDTH = 1024
SSM_HEADS = SSM_WIDTH // SSM_HEAD_DIM
SSM_STATE = 128
SSM_GROUPS = 2
SSM_HPG = SSM_HEADS // SSM_GROUPS
SSM_GROUP_W = SSM_HPG * SSM_HEAD_DIM
SSM_CONV = 4
SSM_CHUNK = 128
BC_W = SSM_GROUPS * SSM_STATE
XBC_WIDTH = SSM_WIDTH + 2 * BC_W
D_FF = 2816
FFN_CONV = 3
FF_CHUNK = 256
EPS = 1e-6
IN_WIDTH = QKV_W + SSM_WIDTH + XBC_WIDTH + SSM_HEADS
SUBLANES = 8
LANES = 128
SAMPLE_ROWS = 128
PROMPT_ATT_ROWS = 512
RESIDUE_UNROLL = 4
VMEM_LIMIT = 56 * 1024 * 1024
NEG_INF = float("-inf")


def _dot(a, b):
    return jnp.dot(a, b, preferred_element_type=F32)


def _dot_nt(a, b):
    return lax.dot_general(a, b, (((1,), (1,)), ((), ())), preferred_element_type=F32)


def _dot_tn(a, b):
    return lax.dot_general(a, b, (((0,), (0,)), ((), ())), preferred_element_type=F32)


def _split(x, pieces):
    out = []
    r = x
    for _ in range(pieces):
        p = r.astype(BF16)
        out.append(p)
        r = r - p.astype(F32)
    return out


def _dot_sel(x, sel, pieces=3):
    acc = None
    for p in _split(x, pieces):
        t = _dot(p, sel)
        acc = t if acc is None else acc + t
    return acc


def _sel_dot(sel, x, pieces=3):
    acc = None
    for p in _split(x, pieces):
        t = _dot(sel, p)
        acc = t if acc is None else acc + t
    return acc


def _sel_dot_nt(x, sel, pieces=3):
    acc = None
    for p in _split(x, pieces):
        t = _dot_nt(p, sel)
        acc = t if acc is None else acc + t
    return acc


def _silu(x):
    h = 0.5 * x
    return h * (1.0 + jnp.tanh(h))


def _softplus(x):
    return jnp.maximum(x, 0.0) + jnp.log1p(jnp.exp(-jnp.abs(x)))


def _rms(x, g):
    ms = jnp.mean(x * x, axis=-1, keepdims=True)
    return x * lax.rsqrt(ms + EPS) * g


def _layer_spec(arr, layer):
    nd = arr.ndim
    return pl.BlockSpec((1,) + arr.shape[1:], lambda *_: (layer,) + (0,) * (nd - 1), pipeline_mode=pl.Buffered(1))


def _const_spec(shape):
    nd = len(shape)
    return pl.BlockSpec(shape, lambda *_: (0,) * nd, pipeline_mode=pl.Buffered(1))


def _params(sem):
    return pltpu.CompilerParams(dimension_semantics=sem, vmem_limit_bytes=VMEM_LIMIT)


def _head_block_diag(width):
    h = np.arange(width) // HEAD_DIM
    return jnp.asarray(np.where(h[:, None] == h[None, :], 1.0 / HEAD_DIM, 0.0), dtype=BF16)


def _head_rms(t, g, bd):
    ms = _dot_sel(t * t, bd, pieces=2)
    return t * lax.rsqrt(ms + EPS) * g


def _rows_back(slab_ref, s, d, n):
    return slab_ref[s, pl.ds(SUBLANES - d, n, stride=1), :]


def _inproj_body(x_ref, g_ref, w_ref, dtb_ref, qkv_ref, z_ref, xbc_ref, dt_ref):
    h = _rms(x_ref[...], g_ref[...]).astype(BF16)
    qkv = _dot(h, w_ref[0, :, 0:QKV_W])
    for s in range(QKV_SLABS):
        qkv_ref[s] = qkv[:, s * PAIR_W:(s + 1) * PAIR_W]
    z_ref[...] = _dot(h, w_ref[0, :, QKV_W:QKV_W + SSM_WIDTH])
    xd = _dot(h, w_ref[0, :, QKV_W + SSM_WIDTH:IN_WIDTH])
    xbc_ref[...] = xd[:, :XBC_WIDTH]
    dt_ref[...] = _softplus(xd[:, XBC_WIDTH:] + dtb_ref[...])


def _inproj(x, g, w_in, dtb, layer, tm):
    m = x.shape[0]
    row = lambda w: pl.BlockSpec((tm, w), lambda i: (i, 0))
    widths = (SSM_WIDTH, XBC_WIDTH, SSM_HEADS)
    return pl.pallas_call(
        _inproj_body,
        grid=(m // tm,),
        in_specs=[row(D_MODEL), _const_spec(g.shape), _layer_spec(w_in, layer), _const_spec(dtb.shape)],
        out_specs=[pl.BlockSpec((QKV_SLABS, tm, PAIR_W), lambda i: (0, i, 0))] + [row(w) for w in widths],
        out_shape=[jax.ShapeDtypeStruct((QKV_SLABS, m, PAIR_W), F32)]
                  + [jax.ShapeDtypeStruct((m, w), F32) for w in widths],
        compiler_params=_params(("parallel",)),
        name="inproj",
    )(x, g, w_in, dtb)


def _attn_prompt_body(dil, nbk, q0_ref, q1_ref, k0_ref, k1_ref, v0_ref, v1_ref, bias_ref, gq_ref, gk_ref, bd_ref,
                      o_ref, lse_ref, kn_ref, kv_scr):
    n = pl.program_id(1)
    rows_per = nbk * BAND
    in_refs = ((q0_ref, q1_ref), (k0_ref, k1_ref), (v0_ref, v1_ref))

    @pl.when(n == 0)
    def _():
        kv_scr[...] = jnp.zeros(kv_scr.shape, BF16)

    variant = jnp.minimum(n, 1)
    bd = bd_ref[...]
    gq = gq_ref[...]
    gk = gk_ref[...]
    low = lax.broadcasted_iota(jnp.int32, (BAND, PAIR_W), 1) < HEAD_DIM

    def rows_of(r, start, count):
        return pl.ds(r + start * dil, count, stride=dil) if dil > 1 else pl.ds(start, count)

    def group(residues):
        items = []
        for r in residues:
            for p in range(2):
                q, k, v = (in_refs[kind][p][0, 0, rows_of(r, 0, rows_per), :] for kind in range(3))
                qn = _head_rms(q, gq, bd) * (HEAD_DIM ** -0.5)
                kn = _head_rms(k, gk, bd)
                kall = jnp.concatenate([kv_scr[0, p, r], kn.astype(BF16)], axis=0)
                vall = jnp.concatenate([kv_scr[1, p, r], v.astype(BF16)], axis=0)
                items.append((r, p, qn, kn, kall, vall))
        scores = []
        for r, p, qn, kn, kall, vall in items:
            for j in range(nbk):
                qj = qn[j * BAND:(j + 1) * BAND]
                for hh in range(2):
                    keep = low if hh == 0 else jnp.logical_not(low)
                    qm = jnp.where(keep, qj, 0.0).astype(BF16)
                    bias = bias_ref[variant if j == 0 else 1, 2 * p + hh]
                    scores.append(_dot_nt(qm, kall[j * BAND:(j + 2) * BAND]) + bias)
        probs = []
        for s in scores:
            mx = jnp.max(s, axis=-1, keepdims=True)
            e = jnp.exp(s - mx)
            den = jnp.sum(e, axis=-1, keepdims=True)
            probs.append((e.astype(BF16), den, mx + jnp.log(den)))
        results = []
        it = iter(probs)
        for r, p, qn, kn, kall, vall in items:
            for j in range(nbk):
                (e0, d0, l0), (e1, d1, l1) = next(it), next(it)
                vwin = vall[j * BAND:(j + 2) * BAND]
                o = jnp.where(low, _dot(e0, vwin) / d0, _dot(e1, vwin) / d1)
                results.append((r, p, j, o, jnp.where(low, l0, l1)))
        for r, p, j, o, lse in results:
            o_ref[p, 0, rows_of(r, j * BAND, BAND), :] = o
            lse_ref[p, 0, rows_of(r, j * BAND, BAND), :] = lse
        for r, p, qn, kn, kall, vall in items:
            kn_ref[p, 0, rows_of(r, 0, rows_per), :] = kn
            kv_scr[0, p, r] = kall[rows_per:]
            kv_scr[1, p, r] = vall[rows_per:]

    if dil <= RESIDUE_UNROLL:
        group(range(dil))
    else:
        def step(i, carry):
            group([i * RESIDUE_UNROLL + k for k in range(RESIDUE_UNROLL)])
            return carry
        lax.fori_loop(0, dil // RESIDUE_UNROLL, step, 0)


def _prompt_bias(slopes, dil):
    i = np.arange(BAND)[:, None]
    j = np.arange(2 * BAND)[None, :]
    step = i + BAND - j
    valid = (step >= 0) & (step <= BAND)
    bias = -slopes[:, None, None] * (step * dil).astype(np.float32)
    full = np.where(valid[None], bias, NEG_INF)
    first = np.where((valid & (j >= BAND))[None], bias, NEG_INF)
    return jnp.asarray(np.stack([first, full]), dtype=F32)


def _attn_prompt(qkv_t, g, dil, nbk, bias, gq, gk, bd):
    _, bsz, seq, _ = qkv_t.shape
    blk = nbk * BAND * dil
    assert seq % blk == 0
    slab = lambda s: pl.BlockSpec((1, 1, blk, PAIR_W), lambda b, n: (s, b, n, 0))
    ospec = pl.BlockSpec((2, 1, blk, PAIR_W), lambda b, n: (0, b, n, 0))
    first = [kind * PAIRS + 2 * g for kind in range(3)]
    return pl.pallas_call(
        functools.partial(_attn_prompt_body, dil, nbk),
        grid=(bsz, seq // blk),
        in_specs=[slab(first[0]), slab(first[0] + 1), slab(first[1]), slab(first[1] + 1),
                  slab(first[2]), slab(first[2] + 1),
                  _const_spec(bias.shape), _const_spec(gq.shape), _const_spec(gk.shape), _const_spec(bd.shape)],
        out_specs=[ospec, ospec, ospec],
        out_shape=[jax.ShapeDtypeStruct((2, bsz, seq, PAIR_W), F32)] * 3,
        scratch_shapes=[pltpu.VMEM((2, 2, dil, BAND, PAIR_W), BF16)],
        compiler_params=_params(("parallel", "arbitrary")),
        name=f"attn_prompt_g{g}",
    )(*([qkv_t] * 6), bias, gq, gk, bd)


def _ssd_stage_conv(xbc_ref, conv_scr):
    raw = xbc_ref[0]
    for s in range(XBC_WIDTH // LANES):
        conv_scr[s, SUBLANES:, :] = raw[:, s * LANES:(s + 1) * LANES]
    return raw


def _ssd_chunk(raw, z_ref, dt_ref, dtt_ref, cw_ref, cb_ref, alr_ref, alc_ref, dsk_ref, gn_ref,
               tri_ref, exp_ref, conv_scr, state_scr):
    q = SSM_CHUNK

    cw = cw_ref[...]
    z = z_ref[0]
    state = state_scr[...]
    n_slabs = XBC_WIDTH // LANES
    parts = []
    for s in range(n_slabs):
        lanes = slice(s * LANES, (s + 1) * LANES)
        acc = _rows_back(conv_scr, s, 3, q) * cw[0:1, lanes]
        acc = acc + _rows_back(conv_scr, s, 2, q) * cw[1:2, lanes]
        acc = acc + _rows_back(conv_scr, s, 1, q) * cw[2:3, lanes]
        parts.append(_silu(acc + raw[:, lanes] * cw[3:4, lanes] + cb_ref[:, lanes]))
        if s % 3 == 2:
            yield
    xc = jnp.concatenate(parts, axis=1)
    xs = xc[:, :SSM_WIDTH]
    bm = xc[:, SSM_WIDTH:SSM_WIDTH + BC_W]
    cm = xc[:, SSM_WIDTH + BC_W:]

    tri = tri_ref[...]
    dt = dt_ref[0]
    acum = _sel_dot(tri, dt * (-jnp.exp(alr_ref[...])))
    acum_t = _sel_dot_nt(dtt_ref[0] * (-jnp.exp(alc_ref[...])), tri)
    yield
    last = acum[q - 1:q, :]
    expand = exp_ref[...]
    dt_x = _dot_sel(dt, expand)
    ea_x = _dot_sel(jnp.exp(acum), expand)
    de_x = _dot_sel(jnp.exp(last - acum), expand)
    xdt = xs * dt_x
    xde = (xdt * de_x).astype(BF16)
    xdt_b = xdt.astype(BF16)
    causal = lax.broadcasted_iota(jnp.int32, (q, q), 0) >= lax.broadcasted_iota(jnp.int32, (q, q), 1)
    low = lax.broadcasted_iota(jnp.int32, (q, 2 * SSM_HEAD_DIM), 1) < SSM_HEAD_DIM

    y_parts = []
    new_state = []
    for g in range(SSM_GROUPS):
        bm_g = bm[:, g * SSM_STATE:(g + 1) * SSM_STATE].astype(BF16)
        cm_g = cm[:, g * SSM_STATE:(g + 1) * SSM_STATE].astype(BF16)
        cb = _dot_nt(cm_g, bm_g)
        rows = slice(g * SSM_GROUP_W, (g + 1) * SSM_GROUP_W)
        st_g = state[rows, :]
        inter = _dot_nt(cm_g, st_g.astype(BF16))
        for hp in range(SSM_HPG // 2):
            pair = None
            for k in range(2):
                h = g * SSM_HPG + 2 * hp + k
                seg = acum[:, h:h + 1] - acum_t[h:h + 1, :]
                decay = jnp.exp(jnp.where(causal, seg, NEG_INF))
                gmat = (cb * decay).astype(BF16)
                lanes = slice((2 * hp) * SSM_HEAD_DIM + g * SSM_GROUP_W,
                              (2 * hp + 2) * SSM_HEAD_DIM + g * SSM_GROUP_W)
                x_pair = xdt_b[:, lanes]
                keep = low if k == 0 else jnp.logical_not(low)
                t = _dot(gmat, jnp.where(keep, x_pair, jnp.zeros_like(x_pair)))
                pair = t if pair is None else pair + t
            y_parts.append(pair + ea_x[:, lanes] * inter[:, lanes.start - g * SSM_GROUP_W:lanes.stop - g * SSM_GROUP_W])
            yield
        new = _dot_tn(xde[:, rows], bm_g)
        for hh in range(SSM_HPG):
            h = g * SSM_HPG + hh
            cd = jnp.exp(acum_t[h:h + 1, q - 1:q])
            r = slice(h * SSM_HEAD_DIM, (h + 1) * SSM_HEAD_DIM)
            rl = slice(hh * SSM_HEAD_DIM, (hh + 1) * SSM_HEAD_DIM)
            new_state.append((r, st_g[rl, :] * cd + new[rl, :]))

    yield
    y = jnp.concatenate(y_parts, axis=1) + dsk_ref[...] * xs
    y = y * _silu(z)
    return _rms(y, gn_ref[...]), new_state


def _ssd_store(raw, y, new_state, y_ref, conv_scr, state_scr):
    y_ref[0] = y
    for r, val in new_state:
        state_scr[r, :] = val
    for s in range(XBC_WIDTH // LANES):
        conv_scr[s, 0:SUBLANES, :] = raw[SSM_CHUNK - SUBLANES:, s * LANES:(s + 1) * LANES]


def _interleave(*gens):
    results = [None] * len(gens)
    live = list(range(len(gens)))
    while live:
        for i in list(live):
            try:
                next(gens[i])
            except StopIteration as done:
                results[i] = done.value
                live.remove(i)
    return results


def _sample_state_batch(steps, mine, batch, bm_ref, cm_ref, xde_ref, pre_ref, ea_ref, zg_ref, cd_ref, gn_ref,
                        st_ref):
    bm = bm_ref[...].astype(BF16)
    cm = cm_ref[...].astype(BF16)
    xde = jnp.where(mine, xde_ref[...], 0.0).astype(BF16)
    inter, new = [], []
    for g in range(SSM_GROUPS):
        srows = slice(g * SSM_GROUP_W, (g + 1) * SSM_GROUP_W)
        lanes = slice(g * SSM_STATE, (g + 1) * SSM_STATE)
        inter.append(_dot_nt(cm[:, lanes], st_ref[0, 0, srows, :].astype(BF16)))
        new.append(_dot_tn(xde[:, srows], bm[:, lanes]))
        yield
    y = pre_ref[...] + ea_ref[...] * jnp.concatenate(inter, axis=1)
    y = _rms(y * zg_ref[...], gn_ref[...])
    new_state = []
    for h in range(SSM_HEADS):
        g, hh = divmod(h, SSM_HPG)
        r = slice(h * SSM_HEAD_DIM, (h + 1) * SSM_HEAD_DIM)
        new_state.append((r, st_ref[0, 0, r, :] * cd_ref[batch, h]
                          + new[g][hh * SSM_HEAD_DIM:(hh + 1) * SSM_HEAD_DIM, :]))
    return y, new_state


def _ssd_attn_body(steps, dils, layer, fills, n_ssd, *refs):
    ssd_in, rest = refs[:n_ssd], refs[n_ssd:]
    (qkv_ref, kv0_ref, kv1_ref, kv2_ref, cb0_ref, cb1_ref, cb2_ref, nb_ref, gq_ref, gk_ref, bd_ref, hm_ref,
     bm_ref, cm_ref, xde_ref, pre_ref, ea_ref, zg_ref, cd_ref, st_ref) = rest[:20]
    y_ref, hl_ref, o_ref, lse_ref, kn_ref, ys_ref, ns_ref, conv_scr, state_scr = rest[-9:]
    c = pl.program_id(1)
    n_b = SUBLANES // steps
    batch = pl.program_id(0) * pl.num_programs(1) + c
    b = batch % n_b

    @pl.when(c == 0)
    def _():
        conv_scr[:, 0:SUBLANES, :] = jnp.zeros((conv_scr.shape[0], SUBLANES, LANES), F32)
        state_scr[...] = jnp.zeros_like(state_scr)

    raw = _ssd_stage_conv(ssd_in[0], conv_scr)
    mine_w = lax.broadcasted_iota(jnp.int32, (SUBLANES, SSM_WIDTH), 0) // steps == b
    sample, (y, new_state), (ys, new_sample_state) = _interleave(
        _sample_attn_batch(steps, dils, b, qkv_ref, (kv0_ref, kv1_ref, kv2_ref), (cb0_ref, cb1_ref, cb2_ref),
                           nb_ref, gq_ref, gk_ref, bd_ref, hm_ref),
        _ssd_chunk(raw, *ssd_in[1:], conv_scr, state_scr),
        _sample_state_batch(steps, mine_w, batch, bm_ref, cm_ref, xde_ref, pre_ref, ea_ref, zg_ref, cd_ref,
                            ssd_in[9], st_ref))
    _ssd_store(raw, y, new_state, y_ref, conv_scr, state_scr)
    own = layer if fills else 0
    for l in range(ns_ref.shape[0]):
        if l != own:
            ns_ref[l] = jnp.zeros(ns_ref.shape[1:], F32)
    for r, val in new_sample_state:
        ns_ref[own, 0, r, :] = val

    mine = lax.broadcasted_iota(jnp.int32, (SUBLANES, PAIR_W), 0) // steps == b

    @pl.when(b == 0)
    def _():
        ys_ref[...] = jnp.where(mine_w, ys, 0.0)
        for g, (o_g, l_g, kn) in enumerate(sample):
            for p in range(2):
                lanes = slice(p * PAIR_W, (p + 1) * PAIR_W)
                o_ref[2 * g + p] = jnp.where(mine, o_g[:, lanes], 0.0)
                lse_ref[2 * g + p] = jnp.where(mine, l_g[:, lanes], 0.0)
                kn_ref[2 * g + p] = kn[:, lanes]

    @pl.when(b > 0)
    def _():
        ys_ref[...] = jnp.where(mine_w, ys, ys_ref[...])
        for g, (o_g, l_g, kn) in enumerate(sample):
            for p in range(2):
                lanes = slice(p * PAIR_W, (p + 1) * PAIR_W)
                o_ref[2 * g + p] = jnp.where(mine, o_g[:, lanes], o_ref[2 * g + p])
                lse_ref[2 * g + p] = jnp.where(mine, l_g[:, lanes], lse_ref[2 * g + p])

    @pl.when(c == pl.num_programs(1) - 1)
    def _():
        hl_ref[0] = state_scr[...]


def _ssd_prompt_sample(xbc, z, dt, dtt, ssd_consts, qkv_t, caches, layer, steps, att_consts,
                       state_rows, cd, state, new_states):
    bsz, seq, _ = xbc.shape
    assert seq % SSM_CHUNK == 0
    nc = seq // SSM_CHUNK
    m = qkv_t.shape[1]
    n_b = SUBLANES // steps
    assert SUBLANES % steps == 0 and m == bsz * nc * steps
    dils = tuple(d for _, d in ATT_GROUPS)
    blk = lambda w: pl.BlockSpec((1, SSM_CHUNK, w), lambda b, c: (b, c, 0))
    tile = lambda b, c: (b * nc + c) // n_b
    rows = lambda w: pl.BlockSpec((SUBLANES, w), lambda b, c: (tile(b, c), 0))
    per_batch = lambda arr, lead, first: pl.BlockSpec((lead, 1) + arr.shape[2:],
                                                      lambda b, c: (first, b * nc + c) + (0,) * (arr.ndim - 2))
    ospec = pl.BlockSpec((PAIRS, SUBLANES, PAIR_W), lambda b, c: (0, tile(b, c), 0))
    ssd_in = [xbc, z, dt, dtt, *ssd_consts]
    fills = new_states is None
    ins = [*ssd_in, qkv_t, *caches, *att_consts, *state_rows, cd, state]
    in_specs = ([blk(XBC_WIDTH), blk(SSM_WIDTH), blk(SSM_HEADS),
                 pl.BlockSpec((1, SSM_HEADS, SSM_CHUNK), lambda b, c: (b, 0, c))]
                + [_const_spec(a.shape) for a in ssd_consts]
                + [pl.BlockSpec((QKV_SLABS, SUBLANES, PAIR_W), lambda b, c: (0, tile(b, c), 0))]
                + [per_batch(cache, 1, layer) for cache in caches]
                + [_const_spec(a.shape) for a in att_consts]
                + [rows(a.shape[1]) for a in state_rows]
                + [pl.BlockSpec(memory_space=pltpu.SMEM), per_batch(state, 1, layer)])
    aliases = {}
    if not fills:
        aliases = {len(ins): 6}
        ins.append(new_states)
        in_specs.append(pl.BlockSpec(memory_space=pl.ANY))
    ns_spec = per_batch(state, state.shape[0], 0) if fills else per_batch(state, 1, layer)
    return pl.pallas_call(
        functools.partial(_ssd_attn_body, steps, dils, layer, fills, len(ssd_in)),
        grid=(bsz, nc),
        in_specs=in_specs,
        out_specs=[blk(SSM_WIDTH), pl.BlockSpec((1, SSM_WIDTH, SSM_STATE), lambda b, c: (b, 0, 0)),
                   ospec, ospec, ospec, rows(SSM_WIDTH), ns_spec],
        out_shape=[jax.ShapeDtypeStruct((bsz, seq, SSM_WIDTH), F32),
                   jax.ShapeDtypeStruct((bsz, SSM_WIDTH, SSM_STATE), F32)]
                  + [jax.ShapeDtypeStruct((PAIRS, m, PAIR_W), F32)] * 3
                  + [jax.ShapeDtypeStruct((m, SSM_WIDTH), F32), jax.ShapeDtypeStruct(state.shape, F32)],
        scratch_shapes=[pltpu.VMEM((XBC_WIDTH // LANES, SUBLANES + SSM_CHUNK, LANES), F32),
                        pltpu.VMEM((SSM_WIDTH, SSM_STATE), F32)],
        input_output_aliases=aliases,
        compiler_params=_params(("arbitrary", "arbitrary")),
        name="ssd_prompt_sample",
    )(*ins)


def _mix_project(o_sl, l_sl, y, x, w_ref):
    ngrp = len(ATT_GROUPS)
    att = [None] * PAIRS
    for p in range(PAIRS // ngrp):
        ls = [l_sl[2 * g + p] for g in range(ngrp)]
        mx = functools.reduce(jnp.maximum, ls)
        es = [jnp.exp(l - mx) for l in ls]
        inv = 1.0 / functools.reduce(jnp.add, es)
        for g in range(ngrp):
            att[2 * g + p] = (o_sl[2 * g + p] * (es[g] * inv)).astype(BF16)
    acc = x + _dot(y.astype(BF16), w_ref[0, ATT_WIDTH:, :])
    return acc + _dot(jnp.concatenate(att, axis=1), w_ref[0, 0:ATT_WIDTH, :])


def _outproj_body(slabs, *refs):
    n_arr = len(slabs)
    o_refs, l_refs = refs[:n_arr], refs[n_arr:2 * n_arr]
    y_ref, x_ref, w_ref, out_ref = refs[2 * n_arr:]
    o_sl = [r[s] for r, cnt in zip(o_refs, slabs) for s in range(cnt)]
    l_sl = [r[s] for r, cnt in zip(l_refs, slabs) for s in range(cnt)]
    out_ref[...] = _mix_project(o_sl, l_sl, y_ref[...], x_ref[...], w_ref)


def _outproj(os_, ls_, y, x, w_out, layer, tm):
    m = x.shape[0]
    slabs = tuple(a.shape[0] for a in os_)
    assert sum(slabs) == PAIRS
    row = lambda w: pl.BlockSpec((tm, w), lambda i: (i, 0))
    slab_specs = [pl.BlockSpec((n, tm, PAIR_W), lambda i: (0, i, 0)) for n in slabs]
    return pl.pallas_call(
        functools.partial(_outproj_body, slabs),
        grid=(m // tm,),
        in_specs=slab_specs * 2 + [row(SSM_WIDTH), row(D_MODEL), _layer_spec(w_out, layer)],
        out_specs=row(D_MODEL),
        out_shape=jax.ShapeDtypeStruct((m, D_MODEL), F32),
        compiler_params=_params(("parallel",)),
        name="outproj",
    )(*os_, *ls_, y, x, w_out)


def _ffn_chunk(h, c, wu_ref, fw_ref, fb_ref, shifted):
    halves = []
    for base in (0, D_FF):
        cols = slice(base + c * FF_CHUNK, base + (c + 1) * FF_CHUNK)
        u = _dot(h, wu_ref[0, :, cols])
        s2, s1 = shifted(u, cols)
        w = fw_ref[:, cols]
        halves.append(s2 * w[0:1] + s1 * w[1:2] + u * w[2:3] + fb_ref[:, cols])
    return (_silu(halves[0]) * halves[1]).astype(BF16)


def _ffn_prompt_body(slabs, *refs):
    n_arr = len(slabs)
    o_refs, l_refs = refs[:n_arr], refs[n_arr:2 * n_arr]
    (y_ref, x_ref, wo_ref, g_ref, wu_ref, fw_ref, fb_ref, wd_ref,
     out_ref, tail_ref, u_scr, carry_scr, act_scr) = refs[2 * n_arr:]
    i = pl.program_id(1)

    @pl.when(i == 0)
    def _():
        carry_scr[...] = jnp.zeros_like(carry_scr)

    o_sl = [r[s, 0] for r, cnt in zip(o_refs, slabs) for s in range(cnt)]
    l_sl = [r[s, 0] for r, cnt in zip(l_refs, slabs) for s in range(cnt)]
    x = _mix_project(o_sl, l_sl, y_ref[0], x_ref[0], wo_ref)
    tm = x.shape[0]
    h = _rms(x, g_ref[...]).astype(BF16)
    new_carry = []

    def shifted(u, cols):
        half, start = divmod(cols.start, D_FF)
        par = (start // FF_CHUNK) % 2
        s2, s1 = [], []
        for k in range(FF_CHUNK // LANES):
            sl = half * (FF_CHUNK // LANES) + k
            lanes = slice(cols.start + k * LANES, cols.start + (k + 1) * LANES)
            u_scr[par, sl, 0:SUBLANES, :] = carry_scr[:, lanes]
            u_scr[par, sl, SUBLANES:, :] = u[:, k * LANES:(k + 1) * LANES]
            s2.append(u_scr[par, sl, pl.ds(SUBLANES - 2, tm, stride=1), :])
            s1.append(u_scr[par, sl, pl.ds(SUBLANES - 1, tm, stride=1), :])
            new_carry.append((lanes, u[tm - SUBLANES:, k * LANES:(k + 1) * LANES]))
        return jnp.concatenate(s2, axis=1), jnp.concatenate(s1, axis=1)

    for c in range(D_FF // FF_CHUNK):
        act_scr[:, c * FF_CHUNK:(c + 1) * FF_CHUNK] = _ffn_chunk(h, c, wu_ref, fw_ref, fb_ref, shifted)
    out_ref[0] = x + _dot(act_scr[...], wd_ref[0])
    for lanes, rows in new_carry:
        carry_scr[:, lanes] = rows

    @pl.when(i == pl.num_programs(1) - 1)
    def _():
        tail_ref[0] = carry_scr[...]


def _ffn_prompt(os_, ls_, y, x, w_out, g, wu, fw, fb, wd, layer, tm):
    bsz, seq, _ = x.shape
    slabs = tuple(a.shape[0] for a in os_)
    assert sum(slabs) == PAIRS
    row = lambda w: pl.BlockSpec((1, tm, w), lambda b, i: (b, i, 0))
    slab_specs = [pl.BlockSpec((n, 1, tm, PAIR_W), lambda b, i: (0, b, i, 0)) for n in slabs]
    chunk_slabs = 2 * FF_CHUNK // LANES
    return pl.pallas_call(
        functools.partial(_ffn_prompt_body, slabs),
        grid=(bsz, seq // tm),
        in_specs=slab_specs * 2 + [row(SSM_WIDTH), row(D_MODEL), _layer_spec(w_out, layer), _const_spec(g.shape),
                                   _layer_spec(wu, layer), _const_spec(fw.shape), _const_spec(fb.shape),
                                   _layer_spec(wd, layer)],
        out_specs=[row(D_MODEL), pl.BlockSpec((1, SUBLANES, 2 * D_FF), lambda b, i: (b, 0, 0))],
        out_shape=[jax.ShapeDtypeStruct((bsz, seq, D_MODEL), F32),
                   jax.ShapeDtypeStruct((bsz, SUBLANES, 2 * D_FF), F32)],
        scratch_shapes=[pltpu.VMEM((2, chunk_slabs, SUBLANES + tm, LANES), F32),
                        pltpu.VMEM((SUBLANES, 2 * D_FF), F32), pltpu.VMEM((tm, D_FF), BF16)],
        compiler_params=_params(("parallel", "arbitrary")),
        name="outproj_ffn_prompt",
    )(*os_, *ls_, y, x, w_out, g, wu, fw, fb, wd)


def _ffn_sample_body(steps, x_ref, g_ref, wu_ref, fw_ref, fb_ref, wd_ref, prev_ref, out_ref, u_ref):
    x = x_ref[...]
    rows = x.shape[0]
    h = _rms(x, g_ref[...]).astype(BF16)
    t = lax.broadcasted_iota(jnp.int32, (rows, FF_CHUNK), 0) % steps

    def shifted(u, cols):
        u_ref[:, cols] = u
        prev = prev_ref[:, cols]
        s1 = jnp.where(t >= 1, pltpu.roll(u, 1, axis=0), pltpu.roll(prev, rows - 1, axis=0))
        s2 = jnp.where(t >= 2, pltpu.roll(u, 2, axis=0), prev)
        return s2, s1

    act = [_ffn_chunk(h, c, wu_ref, fw_ref, fb_ref, shifted) for c in range(D_FF // FF_CHUNK)]
    out_ref[...] = x + _dot(jnp.concatenate(act, axis=1), wd_ref[0])


def _ffn_sample(x, steps, g, wu, fw, fb, wd, layer, prev):
    m = x.shape[0]
    tr = min(SAMPLE_ROWS, m)
    assert m % tr == 0 and tr % steps == 0
    row = lambda w: pl.BlockSpec((tr, w), lambda i: (i, 0))
    return pl.pallas_call(
        functools.partial(_ffn_sample_body, steps),
        grid=(m // tr,),
        in_specs=[row(D_MODEL), _const_spec(g.shape), _layer_spec(wu, layer), _const_spec(fw.shape),
                  _const_spec(fb.shape), _layer_spec(wd, layer), row(2 * D_FF)],
        out_specs=[row(D_MODEL), row(2 * D_FF)],
        out_shape=[jax.ShapeDtypeStruct((m, D_MODEL), F32), jax.ShapeDtypeStruct((m, 2 * D_FF), F32)],
        compiler_params=_params(("parallel",)),
        name="ffn_sample",
    )(x, g, wu, fw, fb, wd, prev)


def _sample_attn_batch(steps, dils, b, qkv_ref, kv_refs, cb_refs, nb_ref, gq_ref, gk_ref, bd_ref, hm_ref):
    rows = qkv_ref.shape[1]
    assert rows == 2 * steps
    bd = bd_ref[...]
    hm = hm_ref[...]
    groups = range(len(dils))
    st = [dict() for _ in groups]
    for g in groups:
        slab = lambda kind: jnp.concatenate([qkv_ref[kind * PAIRS + 2 * g], qkv_ref[kind * PAIRS + 2 * g + 1]], axis=1)
        qn = _head_rms(slab(0), gq_ref[...], bd) * (HEAD_DIM ** -0.5)
        kn = _head_rms(slab(1), gk_ref[...], bd)
        vn = slab(2)
        st[g]["kn"] = kn
        st[g]["kn_b"] = jnp.where(b == 0, kn[0:steps], kn[steps:])
        st[g]["vn_b"] = jnp.where(b == 0, vn[0:steps], vn[steps:])
        st[g]["qbd"] = jnp.concatenate([qn * hm[h:h + 1] for h in range(HEADS_PER_GROUP)], axis=0)
        yield
    for g in groups:
        kc = kv_refs[g][0, 0, 0].astype(BF16)
        st[g]["s_c"] = _dot(st[g]["qbd"].astype(BF16), kc) + cb_refs[g][...]
        yield
    for g in groups:
        d = st[g]
        mx = jnp.max(d["s_c"], axis=-1, keepdims=True)
        d["s_n"] = []
        for t2 in range(steps):
            sn = jnp.sum(d["qbd"] * d["kn_b"][t2:t2 + 1], axis=-1, keepdims=True) + nb_ref[g, t2]
            d["s_n"].append(sn)
            mx = jnp.maximum(mx, sn)
        d["mx"] = mx
        d["e_c"] = jnp.exp(d["s_c"] - mx)
        d["den"] = jnp.sum(d["e_c"], axis=-1, keepdims=True)
        yield
    for g in groups:
        vc = kv_refs[g][0, 0, 1].astype(BF16)
        st[g]["acc"] = _dot_nt(st[g]["e_c"].astype(BF16), vc)
        yield
    results = []
    for g in groups:
        d = st[g]
        acc, den = d["acc"], d["den"]
        for t2, sn in enumerate(d["s_n"]):
            e_n = jnp.exp(sn - d["mx"])
            den = den + e_n
            acc = acc + e_n * d["vn_b"][t2:t2 + 1]
        o_all = acc / den
        l_all = d["mx"] + jnp.log(den)
        o_g = None
        l_g = None
        for h in range(HEADS_PER_GROUP):
            o_h = o_all[h * rows:(h + 1) * rows] * hm[h:h + 1]
            l_h = l_all[h * rows:(h + 1) * rows] * hm[h:h + 1]
            o_g = o_h if o_g is None else o_g + o_h
            l_g = l_h if l_g is None else l_g + l_h
        results.append((o_g, l_g, d["kn"]))
        yield
    return results


def _sample_cache_bias(slopes, dil, steps, cache_len, rows):
    t = (np.arange(rows) % steps)[None, :, None]
    c = np.arange(cache_len)[None, None, :]
    dist = cache_len + t - c
    j = dist // dil
    valid = (dist % dil == 0) & (j >= 1) & (j <= BAND)
    bias = -slopes[:, None, None] * dist.astype(np.float32)
    return jnp.asarray(np.where(valid, bias, NEG_INF).reshape(slopes.shape[0] * rows, cache_len), dtype=F32)


def _sample_new_bias(slopes, dil, steps, rows):
    t2 = np.arange(steps)[:, None, None]
    t = (np.arange(rows) % steps)[None, None, :]
    dist = t - t2
    valid = (dist >= 0) & (dist % dil == 0) & (dist // dil <= BAND)
    bias = -slopes[None, :, None] * dist.astype(np.float32)
    return np.where(valid, bias, NEG_INF).reshape(steps, slopes.shape[0] * rows, 1)


def _ssd_sample_pre_body(steps, xbc_ref, z_ref, dt_ref, prev_ref, cw_ref, cb_ref, al_ref, dsk_ref,
                         exp_ref, gexp_ref, bm_ref, cm_ref, xde_ref, pre_ref, ea_ref, zg_ref, cd_ref):
    raw = xbc_ref[...]
    rows = raw.shape[0]
    tw = lax.broadcasted_iota(jnp.int32, (rows, XBC_WIDTH), 0) % steps
    cw = cw_ref[...]
    prev = prev_ref[...]
    acc = jnp.where(tw >= 3, pltpu.roll(raw, 3, axis=0), prev) * cw[0:1]
    acc = acc + jnp.where(tw >= 2, pltpu.roll(raw, 2, axis=0), pltpu.roll(prev, rows - 1, axis=0)) * cw[1:2]
    acc = acc + jnp.where(tw >= 1, pltpu.roll(raw, 1, axis=0), pltpu.roll(prev, rows - 2, axis=0)) * cw[2:3]
    acc = acc + raw * cw[3:4]
    xc = _silu(acc + cb_ref[...])
    xs = xc[:, :SSM_WIDTH]
    bm = xc[:, SSM_WIDTH:SSM_WIDTH + BC_W]
    cm = xc[:, SSM_WIDTH + BC_W:]
    bm_ref[...] = bm
    cm_ref[...] = cm

    dt = dt_ref[...]
    da = dt * (-jnp.exp(al_ref[...]))
    th = lax.broadcasted_iota(jnp.int32, (rows, SSM_HEADS), 0) % steps
    acum = da
    for d in range(1, steps):
        acum = acum + jnp.where(th >= d, pltpu.roll(da, d, axis=0), 0.0)
    tail = jnp.zeros_like(da)
    for d in range(1, steps):
        tail = tail + jnp.where(th + d < steps, pltpu.roll(da, rows - d, axis=0), 0.0)
    expand = exp_ref[...]
    dt_x = _dot_sel(dt, expand)
    acum_x = _dot_sel(acum, expand)
    ea_ref[...] = jnp.exp(acum_x)
    xdt = xs * dt_x
    xde_ref[...] = xdt * _dot_sel(jnp.exp(tail), expand)
    cd_ref[...] = jnp.exp(acum)

    ts = lax.broadcasted_iota(jnp.int32, (rows, SSM_WIDTH), 0) % steps
    gexp = gexp_ref[...]
    y = dsk_ref[...] * xs
    for d in range(steps):
        if d == 0:
            bm_d, xdt_d, ac_d = bm, xdt, acum_x
        else:
            bm_d = pltpu.roll(bm, d, axis=0)
            xdt_d = pltpu.roll(xdt, d, axis=0)
            ac_d = pltpu.roll(acum_x, d, axis=0)
        cb_x = _dot_sel(cm * bm_d, gexp, pieces=2)
        term = cb_x * jnp.exp(acum_x - ac_d) * xdt_d
        y = y + jnp.where(ts >= d, term, 0.0)
    pre_ref[...] = y
    zg_ref[...] = _silu(z_ref[...])


def _ssd_sample_pre(xbc, z, dt, prev, cw, cb, al, dsk, expand, gexp, steps):
    m = xbc.shape[0]
    tr = min(SAMPLE_ROWS, m)
    assert m % tr == 0 and tr % steps == 0
    row = lambda w: pl.BlockSpec((tr, w), lambda i: (i, 0))
    consts = (cw, cb, al, dsk, expand, gexp)
    widths = (BC_W, BC_W, SSM_WIDTH, SSM_WIDTH, SSM_WIDTH, SSM_WIDTH, SSM_HEADS)
    return pl.pallas_call(
        functools.partial(_ssd_sample_pre_body, steps),
        grid=(m // tr,),
        in_specs=[row(XBC_WIDTH), row(SSM_WIDTH), row(SSM_HEADS), row(XBC_WIDTH)]
                 + [_const_spec(a.shape) for a in consts],
        out_specs=[row(w) for w in widths],
        out_shape=[jax.ShapeDtypeStruct((m, w), F32) for w in widths],
        compiler_params=_params(("parallel",)),
        name="ssd_sample_pre",
    )(xbc, z, dt, prev, *consts)


def _ssd_sample_state_body(steps, bm_ref, cm_ref, xde_ref, pre_ref, ea_ref, zg_ref, cd_ref, gn_ref, st_ref,
                           *rest):
    y_ref, ns_ref = rest[-2:]
    bb = st_ref.shape[1]
    blk = pl.program_id(1)
    last_pass = pl.num_programs(0) - 1

    @pl.when(pl.program_id(0) < last_pass)
    def _():
        ns_ref[...] = jnp.zeros(ns_ref.shape, F32)

    @pl.when(pl.program_id(0) == last_pass)
    def _():
        _ssd_sample_state_step(steps, bb, blk, bm_ref, cm_ref, xde_ref, pre_ref, ea_ref, zg_ref, cd_ref, gn_ref,
                               st_ref, y_ref, ns_ref)


def _ssd_sample_state_step(steps, bb, blk, bm_ref, cm_ref, xde_ref, pre_ref, ea_ref, zg_ref, cd_ref, gn_ref,
                           st_ref, y_ref, ns_ref):

    def dots(b):
        rows = slice(b * steps, (b + 1) * steps)
        bm = bm_ref[rows, :].astype(BF16)
        cm = cm_ref[rows, :].astype(BF16)
        xde = xde_ref[rows, :].astype(BF16)
        inter, new = [], []
        for g in range(SSM_GROUPS):
            srows = slice(g * SSM_GROUP_W, (g + 1) * SSM_GROUP_W)
            lanes = slice(g * SSM_STATE, (g + 1) * SSM_STATE)
            inter.append(_dot_nt(cm[:, lanes], st_ref[0, b, srows, :].astype(BF16)))
            new.append(_dot_tn(xde[:, srows], bm[:, lanes]))
        return inter, new

    def finish(b, inter, new):
        rows = slice(b * steps, (b + 1) * steps)
        y = pre_ref[rows, :] + ea_ref[rows, :] * jnp.concatenate(inter, axis=1)
        y_ref[rows, :] = _rms(y * zg_ref[rows, :], gn_ref[...])
        for h in range(SSM_HEADS):
            g, hh = divmod(h, SSM_HPG)
            cd = cd_ref[blk * bb + b, h]
            r = slice(h * SSM_HEAD_DIM, (h + 1) * SSM_HEAD_DIM)
            ns_ref[0, b, r, :] = st_ref[0, b, r, :] * cd + new[g][hh * SSM_HEAD_DIM:(hh + 1) * SSM_HEAD_DIM, :]

    pending = dots(0)
    for b in range(bb):
        nxt = dots(b + 1) if b + 1 < bb else None
        finish(b, *pending)
        pending = nxt


def _ssd_sample_state(bm, cm, xde, pre, ea, zg, cd, gn, state, layer, steps, bb, new_states):
    bsz = bm.shape[0] // steps
    depth = state.shape[0]
    assert bsz % bb == 0 and (bb * steps) % SUBLANES == 0
    passes = depth if new_states is None else 1
    row_blk = lambda p, i: jnp.where(p == passes - 1, i, 0)
    blk = lambda w: pl.BlockSpec((bb * steps, w), lambda p, i: (row_blk(p, i), 0))
    st_in = pl.BlockSpec((1, bb, SSM_WIDTH, SSM_STATE), lambda p, i: (layer, row_blk(p, i), 0, 0))
    st_out = pl.BlockSpec((1, bb, SSM_WIDTH, SSM_STATE), lambda p, i: ((layer + 1 + p) % passes if passes > 1
                                                                          else layer, i, 0, 0))
    ins = [bm, cm, xde, pre, ea, zg, cd, gn, state]
    in_specs = [blk(BC_W), blk(BC_W), blk(SSM_WIDTH), blk(SSM_WIDTH), blk(SSM_WIDTH), blk(SSM_WIDTH),
                pl.BlockSpec(memory_space=pltpu.SMEM), _const_spec(gn.shape), st_in]
    aliases = {}
    if new_states is not None:
        aliases = {len(ins): 1}
        ins.append(new_states)
        in_specs.append(pl.BlockSpec(memory_space=pl.ANY))
    return pl.pallas_call(
        functools.partial(_ssd_sample_state_body, steps),
        grid=(passes, bsz // bb),
        in_specs=in_specs,
        out_specs=[blk(SSM_WIDTH), st_out],
        out_shape=[jax.ShapeDtypeStruct((bsz * steps, SSM_WIDTH), F32),
                   jax.ShapeDtypeStruct(state.shape, F32)],
        input_output_aliases=aliases,
        compiler_params=_params(("arbitrary", "arbitrary")),
        name="ssd_sample_state",
    )(*ins)


def _alibi_slopes():
    h = np.arange(1, N_ATT_HEADS + 1, dtype=np.float64)
    return np.exp2(-8.0 * h / N_ATT_HEADS).astype(np.float32).reshape(len(ATT_GROUPS), HEADS_PER_GROUP)


def _cache_view(cache):
    depth, bsz, cache_len = cache.shape[:3]
    return jnp.transpose(cache, (0, 1, 3, 4, 5, 2)).reshape(depth, bsz, 2, GROUP_W, cache_len)


def _prev_rows(prev, steps):
    bsz, km1, c = prev.shape
    assert km1 <= steps
    return jnp.pad(prev, ((0, 0), (0, steps - km1), (0, 0))).reshape(bsz * steps, c)


def _layer_weights(lw):
    (norm_mix, q_norm, k_norm, conv_w, conv_b, dt_bias, a_log, d_skip, ssm_norm,
     norm_ffn, ffn_conv_w, ffn_conv_b) = lw
    w = {}
    w["norm_mix"] = norm_mix.reshape(1, D_MODEL)
    w["dtb"] = dt_bias.reshape(1, SSM_HEADS)
    w["gq_g"] = jnp.tile(q_norm, HEADS_PER_GROUP).reshape(1, GROUP_W)
    w["gk_g"] = jnp.tile(k_norm, HEADS_PER_GROUP).reshape(1, GROUP_W)
    w["gq_p"] = jnp.tile(q_norm, 2).reshape(1, PAIR_W)
    w["gk_p"] = jnp.tile(k_norm, 2).reshape(1, PAIR_W)
    w["cw"] = conv_w
    w["cb"] = conv_b.reshape(1, XBC_WIDTH)
    w["alr"] = a_log.reshape(1, SSM_HEADS)
    w["alc"] = a_log.reshape(SSM_HEADS, 1)
    w["dsk"] = jnp.repeat(d_skip, SSM_HEAD_DIM).reshape(1, SSM_WIDTH)
    w["gn"] = ssm_norm.reshape(1, SSM_WIDTH)
    w["norm_ffn"] = norm_ffn.reshape(1, D_MODEL)
    w["fw"] = ffn_conv_w
    w["fb"] = ffn_conv_b.reshape(1, 2 * D_FF)
    return w


def _constants(steps, cache_lens):
    c = {}
    slopes = _alibi_slopes()
    c["bd_g"] = _head_block_diag(GROUP_W)
    c["bd_p"] = _head_block_diag(PAIR_W)
    i = np.arange(SSM_CHUNK)
    c["tri"] = jnp.asarray(i[None, :] <= i[:, None], dtype=BF16)
    lane_head = np.arange(SSM_WIDTH) // SSM_HEAD_DIM
    c["expand"] = jnp.asarray(np.arange(SSM_HEADS)[:, None] == lane_head[None, :], dtype=BF16)
    bc_group = np.arange(BC_W) // SSM_STATE
    c["gexp"] = jnp.asarray(bc_group[:, None] == (lane_head // SSM_HPG)[None, :], dtype=BF16)
    c["hm"] = jnp.asarray(np.arange(HEADS_PER_GROUP)[:, None] == (np.arange(GROUP_W) // HEAD_DIM)[None, :], dtype=F32)
    c["pbias"] = [_prompt_bias(slopes[g], dil) for g, (_, dil) in enumerate(ATT_GROUPS)]
    c["cbias"] = [_sample_cache_bias(slopes[g], dil, steps, cache_lens[g], SUBLANES)
                  for g, (_, dil) in enumerate(ATT_GROUPS)]
    c["nbias"] = jnp.asarray(np.stack([_sample_new_bias(slopes[g], dil, steps, SUBLANES)
                                       for g, (_, dil) in enumerate(ATT_GROUPS)]), dtype=F32)
    return c


def _layer(xp, xs, w, big, c, layer, caches, state, new_states, conv_prev, ffn_prev, tm):
    bsz, seq, _ = xp.shape
    m = bsz * seq
    dec, steps, _ = xs.shape
    ms = dec * steps
    assert steps >= SSM_CONV - 1
    xp2 = xp.reshape(m, D_MODEL)
    xs2 = xs.reshape(ms, D_MODEL)
    qkv, z, xbc, dt = _inproj(xp2, w["norm_mix"], big["w_in"], w["dtb"], layer, tm)
    qkv_s, z_s, xbc_s, dt_s = _inproj(xs2, w["norm_mix"], big["w_in"], w["dtb"], layer, ms)

    qkv_t = qkv.reshape(QKV_SLABS, bsz, seq, PAIR_W)
    os_, ls_, kv_p = [], [], []
    for g, (win, dil) in enumerate(ATT_GROUPS):
        nbk = max(1, PROMPT_ATT_ROWS // (BAND * dil))
        o, lse, kn = _attn_prompt(qkv_t, g, dil, nbk, c["pbias"][g], w["gq_p"], w["gk_p"], c["bd_p"])
        os_.append(o)
        ls_.append(lse)
        keep = min(win, seq)
        v_g = qkv_t[2 * PAIRS + 2 * g:2 * PAIRS + 2 * g + 2, :, seq - keep:]
        kv = jnp.stack([kn[:, :, seq - keep:], v_g])
        kv = jnp.transpose(kv, (2, 3, 0, 1, 4))
        kv_p.append(kv.reshape(bsz, keep, 2, HEADS_PER_GROUP, HEAD_DIM))

    xbc3 = xbc.reshape(bsz, seq, XBC_WIDTH)
    dt3 = dt.reshape(bsz, seq, SSM_HEADS)
    ssd_consts = (w["cw"], w["cb"], w["alr"], w["alc"], w["dsk"], w["gn"], c["tri"], c["expand"])
    att_consts = (*c["cbias"], c["nbias"], w["gq_g"], w["gk_g"], c["bd_g"], c["hm"])
    *state_rows, cd = _ssd_sample_pre(xbc_s, z_s, dt_s, _prev_rows(conv_prev, steps), w["cw"], w["cb"],
                                      w["alr"], w["dsk"], c["expand"], c["gexp"], steps)
    cd_last = cd.reshape(dec, steps, SSM_HEADS)[:, steps - 1]
    y, h_last, o_s, lse_s, kn_s, y_s, new_states = _ssd_prompt_sample(
        xbc3, z.reshape(bsz, seq, SSM_WIDTH), dt3, jnp.swapaxes(dt3, 1, 2), ssd_consts,
        qkv_s, caches, layer, steps, att_consts, state_rows, cd_last, state, new_states)

    x2, tail = _ffn_prompt(os_, ls_, y, xp, big["w_out"], w["norm_ffn"], big["w_up"], w["fw"], w["fb"],
                           big["w_down"], layer, tm)
    h_last = h_last.reshape(bsz, SSM_HEADS, SSM_HEAD_DIM, SSM_STATE)
    conv_p = xbc3[:, seq - (SSM_CONV - 1):]
    ffn_p = tail[:, SUBLANES - (FFN_CONV - 1):]

    kv_s = []
    for g in range(len(ATT_GROUPS)):
        v_g = qkv_s[2 * PAIRS + 2 * g:2 * PAIRS + 2 * g + 2]
        kv = jnp.stack([kn_s[2 * g:2 * g + 2], v_g])
        kv = jnp.transpose(kv, (2, 0, 1, 3))
        kv_s.append(kv.reshape(dec, steps, 2, HEADS_PER_GROUP, HEAD_DIM))
    x1_s = _outproj([o_s], [lse_s], y_s, xs2, big["w_out"], layer, ms)
    x2_s, u_raw = _ffn_sample(x1_s, steps, w["norm_ffn"], big["w_up"], w["fw"], w["fb"], big["w_down"], layer,
                              _prev_rows(ffn_prev, steps))
    conv_s = jnp.concatenate([conv_prev, xbc_s.reshape(dec, steps, XBC_WIDTH)], axis=1)[:, steps:]
    ffn_s = jnp.concatenate([ffn_prev, u_raw.reshape(dec, steps, 2 * D_FF)], axis=1)[:, steps:]
    return (x2, x2_s.reshape(dec, steps, D_MODEL), (*kv_p, h_last, conv_p, ffn_p), (*kv_s, conv_s, ffn_s),
            new_states)


def kernel(x_prompt, x_sample, cache_kv0, cache_kv1, cache_kv2, state_ssm, state_conv, state_ffn_conv, norm_mix, w_in, q_norm, k_norm, conv_w, conv_b, dt_bias, a_log, d_skip, ssm_norm, w_out, norm_ffn, w_up, ffn_conv_w, ffn_conv_b, w_down):
    stacked = (norm_mix, q_norm, k_norm, conv_w, conv_b, dt_bias, a_log, d_skip, ssm_norm,
               norm_ffn, ffn_conv_w, ffn_conv_b)
    big = {"w_in": w_in.astype(BF16), "w_out": w_out.astype(BF16), "w_up": w_up.astype(BF16),
           "w_down": w_down.astype(BF16)}
    depth = w_in.shape[0]
    dec_batch, steps = x_sample.shape[:2]
    caches = tuple(_cache_view(cache) for cache in (cache_kv0, cache_kv1, cache_kv2))
    for cache, (_, dil) in zip(caches, ATT_GROUPS):
        assert cache.shape[-1] == BAND * dil
    state = state_ssm.reshape(depth, dec_batch, SSM_WIDTH, SSM_STATE)
    c = _constants(steps, tuple(cache.shape[-1] for cache in caches))
    tm = 512
    assert x_prompt.shape[1] % tm == 0
    y_prompt, y_sample = x_prompt, x_sample
    outs_p = [[] for _ in range(6)]
    outs_s = [[] for _ in range(5)]
    new_states = None
    for layer in range(depth):
        w = _layer_weights(tuple(a[layer] for a in stacked))
        y_prompt, y_sample, res_p, res_s, new_states = _layer(
            y_prompt, y_sample, w, big, c, layer, caches, state, new_states, state_conv[layer],
            state_ffn_conv[layer], tm)
        for lst, val in zip(outs_p, res_p):
            lst.append(val)
        for lst, val in zip(outs_s, res_s):
            lst.append(val)
    stack_s = [jnp.stack(l) for l in outs_s]
    return (y_prompt, y_sample, *[jnp.stack(l) for l in outs_p],
            *stack_s[:3], new_states.reshape(state_ssm.shape), *stack_s[3:])
```

```python
import functools
import math

import jax
import jax.numpy as jnp
import numpy as np
from jax import lax
from jax.experimental import pallas as pl
from jax.experimental.pallas import tpu as pltpu

F32 = jnp.float32
BF16 = jnp.bfloat16

D_MODEL = 1024
HEAD_DIM = 64
ATT_GROUPS = ((128, 1), (512, 4), (2048, 16))
BAND = 128
HEADS_PER_GROUP = 4
GROUP_W = HEADS_PER_GROUP * HEAD_DIM
N_ATT_HEADS = HEADS_PER_GROUP * len(ATT_GROUPS)
ATT_WIDTH = N_ATT_HEADS * HEAD_DIM
QKV_W = 3 * ATT_WIDTH
PAIR_W = 2 * HEAD_DIM
PAIRS = ATT_WIDTH // PAIR_W
QKV_SLABS = QKV_W // PAIR_W
SSM_HEAD_DIM = 64
SSM_WIDTH = 1024
SSM_HEADS = SSM_WIDTH // SSM_HEAD_DIM
SSM_STATE = 128
SSM_GROUPS = 2
SSM_HPG = SSM_HEADS // SSM_GROUPS
SSM_GROUP_W = SSM_HPG * SSM_HEAD_DIM
SSM_CONV = 4
SSM_CHUNK = 128
BC_W = SSM_GROUPS * SSM_STATE
XBC_WIDTH = SSM_WIDTH + 2 * BC_W
D_FF = 2816
FFN_CONV = 3
FF_CHUNK = 256
EPS = 1e-6
IN_WIDTH = QKV_W + SSM_WIDTH + XBC_WIDTH + SSM_HEADS
SUBLANES = 8
LANES = 128
SAMPLE_ROWS = 128
PROMPT_ATT_ROWS = 512
RESIDUE_UNROLL = 4
VMEM_LIMIT = 56 * 1024 * 1024
NEG_INF = float("-inf")


def _dot(a, b):
    return jnp.dot(a, b, preferred_element_type=F32)


def _dot_nt(a, b):
    return lax.dot_general(a, b, (((1,), (1,)), ((), ())), preferred_element_type=F32)


def _dot_tn(a, b):
    return lax.dot_general(a, b, (((0,), (0,)), ((), ())), preferred_element_type=F32)


def _split(x, pieces):
    out = []
    r = x
    for _ in range(pieces):
        p = r.astype(BF16)
        out.append(p)
        r = r - p.astype(F32)
    return out


def _dot_sel(x, sel, pieces=3):
    acc = None
    for p in _split(x, pieces):
        t = _dot(p, sel)
        acc = t if acc is None else acc + t
    return acc


def _sel_dot(sel, x, pieces=3):
    acc = None
    for p in _split(x, pieces):
        t = _dot(sel, p)
        acc = t if acc is None else acc + t
    return acc


def _sel_dot_nt(x, sel, pieces=3):
    acc = None
    for p in _split(x, pieces):
        t = _dot_nt(p, sel)
        acc = t if acc is None else acc + t
    return acc


def _silu(x):
    h = 0.5 * x
    return h * (1.0 + jnp.tanh(h))


def _softplus(x):
    return jnp.maximum(x, 0.0) + jnp.log1p(jnp.exp(-jnp.abs(x)))


def _rms(x, g):
    ms = jnp.mean(x * x, axis=-1, keepdims=True)
    return x * lax.rsqrt(ms + EPS) * g


def _layer_spec(arr, layer):
    nd = arr.ndim
    return pl.BlockSpec((1,) + arr.shape[1:], lambda *_: (layer,) + (0,) * (nd - 1), pipeline_mode=pl.Buffered(1))


def _const_spec(shape):
    nd = len(shape)
    return pl.BlockSpec(shape, lambda *_: (0,) * nd, pipeline_mode=pl.Buffered(1))


def _params(sem):
    return pltpu.CompilerParams(dimension_semantics=sem, vmem_limit_bytes=VMEM_LIMIT)


def _head_block_diag(width):
    h = np.arange(width) // HEAD_DIM
    return jnp.asarray(np.where(h[:, None] == h[None, :], 1.0 / HEAD_DIM, 0.0), dtype=BF16)


def _head_rms(t, g, bd):
    ms = _dot_sel(t * t, bd, pieces=2)
    return t * lax.rsqrt(ms + EPS) * g


def _rows_back(slab_ref, s, d, n):
    return slab_ref[s, pl.ds(SUBLANES - d, n, stride=1), :]


def _inproj_body(x_ref, g_ref, w_ref, dtb_ref, qkv_ref, z_ref, xbc_ref, dt_ref):
    h = _rms(x_ref[...], g_ref[...]).astype(BF16)
    qkv = _dot(h, w_ref[0, :, 0:QKV_W])
    for s in range(QKV_SLABS):
        qkv_ref[s] = qkv[:, s * PAIR_W:(s + 1) * PAIR_W]
    z_ref[...] = _dot(h, w_ref[0, :, QKV_W:QKV_W + SSM_WIDTH])
    xd = _dot(h, w_ref[0, :, QKV_W + SSM_WIDTH:IN_WIDTH])
    xbc_ref[...] = xd[:, :XBC_WIDTH]
    dt_ref[...] = _softplus(xd[:, XBC_WIDTH:] + dtb_ref[...])


def _inproj(x, g, w_in, dtb, layer, tm):
    m = x.shape[0]
    row = lambda w: pl.BlockSpec((tm, w), lambda i: (i, 0))
    widths = (SSM_WIDTH, XBC_WIDTH, SSM_HEADS)
    return pl.pallas_call(
        _inproj_body,
        grid=(m // tm,),
        in_specs=[row(D_MODEL), _const_spec(g.shape), _layer_spec(w_in, layer), _const_spec(dtb.shape)],
        out_specs=[pl.BlockSpec((QKV_SLABS, tm, PAIR_W), lambda i: (0, i, 0))] + [row(w) for w in widths],
        out_shape=[jax.ShapeDtypeStruct((QKV_SLABS, m, PAIR_W), F32)]
                  + [jax.ShapeDtypeStruct((m, w), F32) for w in widths],
        compiler_params=_params(("parallel",)),
        name="inproj",
    )(x, g, w_in, dtb)


def _attn_prompt_body(dil, nbk, q0_ref, q1_ref, k0_ref, k1_ref, v0_ref, v1_ref, bias_ref, gq_ref, gk_ref, bd_ref,
                      o_ref, lse_ref, kn_ref, kv_scr):
    n = pl.program_id(1)
    rows_per = nbk * BAND
    in_refs = ((q0_ref, q1_ref), (k0_ref, k1_ref), (v0_ref, v1_ref))

    @pl.when(n == 0)
    def _():
        kv_scr[...] = jnp.zeros(kv_scr.shape, BF16)

    variant = jnp.minimum(n, 1)
    bd = bd_ref[...]
    gq = gq_ref[...]
    gk = gk_ref[...]
    low = lax.broadcasted_iota(jnp.int32, (BAND, PAIR_W), 1) < HEAD_DIM

    def rows_of(r, start, count):
        return pl.ds(r + start * dil, count, stride=dil) if dil > 1 else pl.ds(start, count)

    def group(residues):
        items = []
        for r in residues:
            for p in range(2):
                q, k, v = (in_refs[kind][p][0, 0, rows_of(r, 0, rows_per), :] for kind in range(3))
                qn = _head_rms(q, gq, bd) * (HEAD_DIM ** -0.5)
                kn = _head_rms(k, gk, bd)
                kall = jnp.concatenate([kv_scr[0, p, r], kn.astype(BF16)], axis=0)
                vall = jnp.concatenate([kv_scr[1, p, r], v.astype(BF16)], axis=0)
                items.append((r, p, qn, kn, kall, vall))
        scores = []
        for r, p, qn, kn, kall, vall in items:
            for j in range(nbk):
                qj = qn[j * BAND:(j + 1) * BAND]
                for hh in range(2):
                    keep = low if hh == 0 else jnp.logical_not(low)
                    qm = jnp.where(keep, qj, 0.0).astype(BF16)
                    bias = bias_ref[variant if j == 0 else 1, 2 * p + hh]
                    scores.append(_dot_nt(qm, kall[j * BAND:(j + 2) * BAND]) + bias)
        probs = []
        for s in scores:
            mx = jnp.max(s, axis=-1, keepdims=True)
            e = jnp.exp(s - mx)
            den = jnp.sum(e, axis=-1, keepdims=True)
            probs.append((e.astype(BF16), den, mx + jnp.log(den)))
        results = []
        it = iter(probs)
        for r, p, qn, kn, kall, vall in items:
            for j in range(nbk):
                (e0, d0, l0), (e1, d1, l1) = next(it), next(it)
                vwin = vall[j * BAND:(j + 2) * BAND]
                o = jnp.where(low, _dot(e0, vwin) / d0, _dot(e1, vwin) / d1)
                results.append((r, p, j, o, jnp.where(low, l0, l1)))
        for r, p, j, o, lse in results:
            o_ref[p, 0, rows_of(r, j * BAND, BAND), :] = o
            lse_ref[p, 0, rows_of(r, j * BAND, BAND), :] = lse
        for r, p, qn, kn, kall, vall in items:
            kn_ref[p, 0, rows_of(r, 0, rows_per), :] = kn
            kv_scr[0, p, r] = kall[rows_per:]
            kv_scr[1, p, r] = vall[rows_per:]

    if dil <= RESIDUE_UNROLL:
        group(range(dil))
    else:
        def step(i, carry):
            group([i * RESIDUE_UNROLL + k for k in range(RESIDUE_UNROLL)])
            return carry
        lax.fori_loop(0, dil // RESIDUE_UNROLL, step, 0)


def _prompt_bias(slopes, dil):
    i = np.arange(BAND)[:, None]
    j = np.arange(2 * BAND)[None, :]
    step = i + BAND - j
    valid = (step >= 0) & (step <= BAND)
    bias = -slopes[:, None, None] * (step * dil).astype(np.float32)
    full = np.where(valid[None], bias, NEG_INF)
    first = np.where((valid & (j >= BAND))[None], bias, NEG_INF)
    return jnp.asarray(np.stack([first, full]), dtype=F32)


def _attn_prompt(qkv_t, g, dil, nbk, bias, gq, gk, bd):
    _, bsz, seq, _ = qkv_t.shape
    blk = nbk * BAND * dil
    assert seq % blk == 0
    slab = lambda s: pl.BlockSpec((1, 1, blk, PAIR_W), lambda b, n: (s, b, n, 0))
    ospec = pl.BlockSpec((2, 1, blk, PAIR_W), lambda b, n: (0, b, n, 0))
    first = [kind * PAIRS + 2 * g for kind in range(3)]
    return pl.pallas_call(
        functools.partial(_attn_prompt_body, dil, nbk),
        grid=(bsz, seq // blk),
        in_specs=[slab(first[0]), slab(first[0] + 1), slab(first[1]), slab(first[1] + 1),
                  slab(first[2]), slab(first[2] + 1),
                  _const_spec(bias.shape), _const_spec(gq.shape), _const_spec(gk.shape), _const_spec(bd.shape)],
        out_specs=[ospec, ospec, ospec],
        out_shape=[jax.ShapeDtypeStruct((2, bsz, seq, PAIR_W), F32)] * 3,
        scratch_shapes=[pltpu.VMEM((2, 2, dil, BAND, PAIR_W), BF16)],
        compiler_params=_params(("parallel", "arbitrary")),
        name=f"attn_prompt_g{g}",
    )(*([qkv_t] * 6), bias, gq, gk, bd)


def _ssd_stage_conv(xbc_ref, conv_scr):
    raw = xbc_ref[0]
    for s in range(XBC_WIDTH // LANES):
        conv_scr[s, SUBLANES:, :] = raw[:, s * LANES:(s + 1) * LANES]
    return raw


def _ssd_chunk(raw, z_ref, dt_ref, dtt_ref, cw_ref, cb_ref, alr_ref, alc_ref, dsk_ref, gn_ref,
               tri_ref, exp_ref, conv_scr, state_scr):
    q = SSM_CHUNK

    cw = cw_ref[...]
    z = z_ref[0]
    state = state_scr[...]
    n_slabs = XBC_WIDTH // LANES
    parts = []
    for s in range(n_slabs):
        lanes = slice(s * LANES, (s + 1) * LANES)
        acc = _rows_back(conv_scr, s, 3, q) * cw[0:1, lanes]
        acc = acc + _rows_back(conv_scr, s, 2, q) * cw[1:2, lanes]
        acc = acc + _rows_back(conv_scr, s, 1, q) * cw[2:3, lanes]
        parts.append(_silu(acc + raw[:, lanes] * cw[3:4, lanes] + cb_ref[:, lanes]))
        if s % 3 == 2:
            yield
    xc = jnp.concatenate(parts, axis=1)
    xs = xc[:, :SSM_WIDTH]
    bm = xc[:, SSM_WIDTH:SSM_WIDTH + BC_W]
    cm = xc[:, SSM_WIDTH + BC_W:]

    tri = tri_ref[...]
    dt = dt_ref[0]
    acum = _sel_dot(tri, dt * (-jnp.exp(alr_ref[...])))
    acum_t = _sel_dot_nt(dtt_ref[0] * (-jnp.exp(alc_ref[...])), tri)
    yield
    last = acum[q - 1:q, :]
    expand = exp_ref[...]
    dt_x = _dot_sel(dt, expand)
    ea_x = _dot_sel(jnp.exp(acum), expand)
    de_x = _dot_sel(jnp.exp(last - acum), expand)
    xdt = xs * dt_x
    xde = (xdt * de_x).astype(BF16)
    xdt_b = xdt.astype(BF16)
    causal = lax.broadcasted_iota(jnp.int32, (q, q), 0) >= lax.broadcasted_iota(jnp.int32, (q, q), 1)
    low = lax.broadcasted_iota(jnp.int32, (q, 2 * SSM_HEAD_DIM), 1) < SSM_HEAD_DIM

    y_parts = []
    new_state = []
    for g in range(SSM_GROUPS):
        bm_g = bm[:, g * SSM_STATE:(g + 1) * SSM_STATE].astype(BF16)
        cm_g = cm[:, g * SSM_STATE:(g + 1) * SSM_STATE].astype(BF16)
        cb = _dot_nt(cm_g, bm_g)
        rows = slice(g * SSM_GROUP_W, (g + 1) * SSM_GROUP_W)
        st_g = state[rows, :]
        inter = _dot_nt(cm_g, st_g.astype(BF16))
        for hp in range(SSM_HPG // 2):
            pair = None
            for k in range(2):
                h = g * SSM_HPG + 2 * hp + k
                seg = acum[:, h:h + 1] - acum_t[h:h + 1, :]
                decay = jnp.exp(jnp.where(causal, seg, NEG_INF))
                gmat = (cb * decay).astype(BF16)
                lanes = slice((2 * hp) * SSM_HEAD_DIM + g * SSM_GROUP_W,
                              (2 * hp + 2) * SSM_HEAD_DIM + g * SSM_GROUP_W)
                x_pair = xdt_b[:, lanes]
                keep = low if k == 0 else jnp.logical_not(low)
                t = _dot(gmat, jnp.where(keep, x_pair, jnp.zeros_like(x_pair)))
                pair = t if pair is None else pair + t
            y_parts.append(pair + ea_x[:, lanes] * inter[:, lanes.start - g * SSM_GROUP_W:lanes.stop - g * SSM_GROUP_W])
            yield
        new = _dot_tn(xde[:, rows], bm_g)
        for hh in range(SSM_HPG):
            h = g * SSM_HPG + hh
            cd = jnp.exp(acum_t[h:h + 1, q - 1:q])
            r = slice(h * SSM_HEAD_DIM, (h + 1) * SSM_HEAD_DIM)
            rl = slice(hh * SSM_HEAD_DIM, (hh + 1) * SSM_HEAD_DIM)
            new_state.append((r, st_g[rl, :] * cd + new[rl, :]))

    yield
    y = jnp.concatenate(y_parts, axis=1) + dsk_ref[...] * xs
    y = y * _silu(z)
    return _rms(y, gn_ref[...]), new_state


def _ssd_store(raw, y, new_state, y_ref, conv_scr, state_scr):
    y_ref[0] = y
    for r, val in new_state:
        state_scr[r, :] = val
    for s in range(XBC_WIDTH // LANES):
        conv_scr[s, 0:SUBLANES, :] = raw[SSM_CHUNK - SUBLANES:, s * LANES:(s + 1) * LANES]


def _interleave(*gens):
    results = [None] * len(gens)
    live = list(range(len(gens)))
    while live:
        for i in list(live):
            try:
                next(gens[i])
            except StopIteration as done:
                results[i] = done.value
                live.remove(i)
    return results


def _sample_state_batch(steps, mine, batch, bm_ref, cm_ref, xde_ref, pre_ref, ea_ref, zg_ref, cd_ref, gn_ref,
                        st_ref):
    bm = bm_ref[...].astype(BF16)
    cm = cm_ref[...].astype(BF16)
    xde = jnp.where(mine, xde_ref[...], 0.0).astype(BF16)
    inter, new = [], []
    for g in range(SSM_GROUPS):
        srows = slice(g * SSM_GROUP_W, (g + 1) * SSM_GROUP_W)
        lanes = slice(g * SSM_STATE, (g + 1) * SSM_STATE)
        inter.append(_dot_nt(cm[:, lanes], st_ref[0, 0, srows, :].astype(BF16)))
        new.append(_dot_tn(xde[:, srows], bm[:, lanes]))
        yield
    y = pre_ref[...] + ea_ref[...] * jnp.concatenate(inter, axis=1)
    y = _rms(y * zg_ref[...], gn_ref[...])
    new_state = []
    for h in range(SSM_HEADS):
        g, hh = divmod(h, SSM_HPG)
        r = slice(h * SSM_HEAD_DIM, (h + 1) * SSM_HEAD_DIM)
        new_state.append((r, st_ref[0, 0, r, :] * cd_ref[batch, h]
                          + new[g][hh * SSM_HEAD_DIM:(hh + 1) * SSM_HEAD_DIM, :]))
    return y, new_state


def _ssd_attn_body(steps, dils, layer, fills, n_ssd, *refs):
    ssd_in, rest = refs[:n_ssd], refs[n_ssd:]
    (qkv_ref, kv0_ref, kv1_ref, kv2_ref, cb0_ref, cb1_ref, cb2_ref, nb_ref, gq_ref, gk_ref, bd_ref, hm_ref,
     bm_ref, cm_ref, xde_ref, pre_ref, ea_ref, zg_ref, cd_ref, st_ref) = rest[:20]
    y_ref, hl_ref, o_ref, lse_ref, kn_ref, ys_ref, ns_ref, conv_scr, state_scr = rest[-9:]
    c = pl.program_id(1)
    n_b = SUBLANES // steps
    batch = pl.program_id(0) * pl.num_programs(1) + c
    b = batch % n_b

    @pl.when(c == 0)
    def _():
        conv_scr[:, 0:SUBLANES, :] = jnp.zeros((conv_scr.shape[0], SUBLANES, LANES), F32)
        state_scr[...] = jnp.zeros_like(state_scr)

    raw = _ssd_stage_conv(ssd_in[0], conv_scr)
    mine_w = lax.broadcasted_iota(jnp.int32, (SUBLANES, SSM_WIDTH), 0) // steps == b
    sample, (y, new_state), (ys, new_sample_state) = _interleave(
        _sample_attn_batch(steps, dils, b, qkv_ref, (kv0_ref, kv1_ref, kv2_ref), (cb0_ref, cb1_ref, cb2_ref),
                           nb_ref, gq_ref, gk_ref, bd_ref, hm_ref),
        _ssd_chunk(raw, *ssd_in[1:], conv_scr, state_scr),
        _sample_state_batch(steps, mine_w, batch, bm_ref, cm_ref, xde_ref, pre_ref, ea_ref, zg_ref, cd_ref,
                            ssd_in[9], st_ref))
    _ssd_store(raw, y, new_state, y_ref, conv_scr, state_scr)
    own = layer if fills else 0
    for l in range(ns_ref.shape[0]):
        if l != own:
            ns_ref[l] = jnp.zeros(ns_ref.shape[1:], F32)
    for r, val in new_sample_state:
        ns_ref[own, 0, r, :] = val

    mine = lax.broadcasted_iota(jnp.int32, (SUBLANES, PAIR_W), 0) // steps == b

    @pl.when(b == 0)
    def _():
        ys_ref[...] = jnp.where(mine_w, ys, 0.0)
        for g, (o_g, l_g, kn) in enumerate(sample):
            for p in range(2):
                lanes = slice(p * PAIR_W, (p + 1) * PAIR_W)
                o_ref[2 * g + p] = jnp.where(mine, o_g[:, lanes], 0.0)
                lse_ref[2 * g + p] = jnp.where(mine, l_g[:, lanes], 0.0)
                kn_ref[2 * g + p] = kn[:, lanes]

    @pl.when(b > 0)
    def _():
        ys_ref[...] = jnp.where(mine_w, ys, ys_ref[...])
        for g, (o_g, l_g, kn) in enumerate(sample):
            for p in range(2):
                lanes = slice(p * PAIR_W, (p + 1) * PAIR_W)
                o_ref[2 * g + p] = jnp.where(mine, o_g[:, lanes], o_ref[2 * g + p])
                lse_ref[2 * g + p] = jnp.where(mine, l_g[:, lanes], lse_ref[2 * g + p])

    @pl.when(c == pl.num_programs(1) - 1)
    def _():
        hl_ref[0] = state_scr[...]


def _ssd_prompt_sample(xbc, z, dt, dtt, ssd_consts, qkv_t, caches, layer, steps, att_consts,
                       state_rows, cd, state, new_states):
    bsz, seq, _ = xbc.shape
    assert seq % SSM_CHUNK == 0
    nc = seq // SSM_CHUNK
    m = qkv_t.shape[1]
    n_b = SUBLANES // steps
    assert SUBLANES % steps == 0 and m == bsz * nc * steps
    dils = tuple(d for _, d in ATT_GROUPS)
    blk = lambda w: pl.BlockSpec((1, SSM_CHUNK, w), lambda b, c: (b, c, 0))
    tile = lambda b, c: (b * nc + c) // n_b
    rows = lambda w: pl.BlockSpec((SUBLANES, w), lambda b, c: (tile(b, c), 0))
    per_batch = lambda arr, lead, first: pl.BlockSpec((lead, 1) + arr.shape[2:],
                                                      lambda b, c: (first, b * nc + c) + (0,) * (arr.ndim - 2))
    ospec = pl.BlockSpec((PAIRS, SUBLANES, PAIR_W), lambda b, c: (0, tile(b, c), 0))
    ssd_in = [xbc, z, dt, dtt, *ssd_consts]
    fills = new_states is None
    ins = [*ssd_in, qkv_t, *caches, *att_consts, *state_rows, cd, state]
    in_specs = ([blk(XBC_WIDTH), blk(SSM_WIDTH), blk(SSM_HEADS),
                 pl.BlockSpec((1, SSM_HEADS, SSM_CHUNK), lambda b, c: (b, 0, c))]
                + [_const_spec(a.shape) for a in ssd_consts]
                + [pl.BlockSpec((QKV_SLABS, SUBLANES, PAIR_W), lambda b, c: (0, tile(b, c), 0))]
                + [per_batch(cache, 1, layer) for cache in caches]
                + [_const_spec(a.shape) for a in att_consts]
                + [rows(a.shape[1]) for a in state_rows]
                + [pl.BlockSpec(memory_space=pltpu.SMEM), per_batch(state, 1, layer)])
    aliases = {}
    if not fills:
        aliases = {len(ins): 6}
        ins.append(new_states)
        in_specs.append(pl.BlockSpec(memory_space=pl.ANY))
    ns_spec = per_batch(state, state.shape[0], 0) if fills else per_batch(state, 1, layer)
    return pl.pallas_call(
        functools.partial(_ssd_attn_body, steps, dils, layer, fills, len(ssd_in)),
        grid=(bsz, nc),
        in_specs=in_specs,
        out_specs=[blk(SSM_WIDTH), pl.BlockSpec((1, SSM_WIDTH, SSM_STATE), lambda b, c: (b, 0, 0)),
                   ospec, ospec, ospec, rows(SSM_WIDTH), ns_spec],
        out_shape=[jax.ShapeDtypeStruct((bsz, seq, SSM_WIDTH), F32),
                   jax.ShapeDtypeStruct((bsz, SSM_WIDTH, SSM_STATE), F32)]
                  + [jax.ShapeDtypeStruct((PAIRS, m, PAIR_W), F32)] * 3
                  + [jax.ShapeDtypeStruct((m, SSM_WIDTH), F32), jax.ShapeDtypeStruct(state.shape, F32)],
        scratch_shapes=[pltpu.VMEM((XBC_WIDTH // LANES, SUBLANES + SSM_CHUNK, LANES), F32),
                        pltpu.VMEM((SSM_WIDTH, SSM_STATE), F32)],
        input_output_aliases=aliases,
        compiler_params=_params(("arbitrary", "arbitrary")),
        name="ssd_prompt_sample",
    )(*ins)


def _mix_project(o_sl, l_sl, y, x, w_ref):
    ngrp = len(ATT_GROUPS)
    att = [None] * PAIRS
    for p in range(PAIRS // ngrp):
        ls = [l_sl[2 * g + p] for g in range(ngrp)]
        mx = functools.reduce(jnp.maximum, ls)
        es = [jnp.exp(l - mx) for l in ls]
        inv = 1.0 / functools.reduce(jnp.add, es)
        for g in range(ngrp):
            att[2 * g + p] = (o_sl[2 * g + p] * (es[g] * inv)).astype(BF16)
    acc = x + _dot(y.astype(BF16), w_ref[0, ATT_WIDTH:, :])
    return acc + _dot(jnp.concatenate(att, axis=1), w_ref[0, 0:ATT_WIDTH, :])


def _outproj_body(slabs, *refs):
    n_arr = len(slabs)
    o_refs, l_refs = refs[:n_arr], refs[n_arr:2 * n_arr]
    y_ref, x_ref, w_ref, out_ref = refs[2 * n_arr:]
    o_sl = [r[s] for r, cnt in zip(o_refs, slabs) for s in range(cnt)]
    l_sl = [r[s] for r, cnt in zip(l_refs, slabs) for s in range(cnt)]
    out_ref[...] = _mix_project(o_sl, l_sl, y_ref[...], x_ref[...], w_ref)


def _outproj(os_, ls_, y, x, w_out, layer, tm):
    m = x.shape[0]
    slabs = tuple(a.shape[0] for a in os_)
    assert sum(slabs) == PAIRS
    row = lambda w: pl.BlockSpec((tm, w), lambda i: (i, 0))
    slab_specs = [pl.BlockSpec((n, tm, PAIR_W), lambda i: (0, i, 0)) for n in slabs]
    return pl.pallas_call(
        functools.partial(_outproj_body, slabs),
        grid=(m // tm,),
        in_specs=slab_specs * 2 + [row(SSM_WIDTH), row(D_MODEL), _layer_spec(w_out, layer)],
        out_specs=row(D_MODEL),
        out_shape=jax.ShapeDtypeStruct((m, D_MODEL), F32),
        compiler_params=_params(("parallel",)),
        name="outproj",
    )(*os_, *ls_, y, x, w_out)


def _ffn_chunk(h, c, wu_ref, fw_ref, fb_ref, shifted):
    halves = []
    for base in (0, D_FF):
        cols = slice(base + c * FF_CHUNK, base + (c + 1) * FF_CHUNK)
        u = _dot(h, wu_ref[0, :, cols])
        s2, s1 = shifted(u, cols)
        w = fw_ref[:, cols]
        halves.append(s2 * w[0:1] + s1 * w[1:2] + u * w[2:3] + fb_ref[:, cols])
    return (_silu(halves[0]) * halves[1]).astype(BF16)


def _ffn_prompt_body(slabs, *refs):
    n_arr = len(slabs)
    o_refs, l_refs = refs[:n_arr], refs[n_arr:2 * n_arr]
    (y_ref, x_ref, wo_ref, g_ref, wu_ref, fw_ref, fb_ref, wd_ref,
     out_ref, tail_ref, u_scr, carry_scr, act_scr) = refs[2 * n_arr:]
    i = pl.program_id(1)

    @pl.when(i == 0)
    def _():
        carry_scr[...] = jnp.zeros_like(carry_scr)

    o_sl = [r[s, 0] for r, cnt in zip(o_refs, slabs) for s in range(cnt)]
    l_sl = [r[s, 0] for r, cnt in zip(l_refs, slabs) for s in range(cnt)]
    x = _mix_project(o_sl, l_sl, y_ref[0], x_ref[0], wo_ref)
    tm = x.shape[0]
    h = _rms(x, g_ref[...]).astype(BF16)
    new_carry = []

    def shifted(u, cols):
        half, start = divmod(cols.start, D_FF)
        par = (start // FF_CHUNK) % 2
        s2, s1 = [], []
        for k in range(FF_CHUNK // LANES):
            sl = half * (FF_CHUNK // LANES) + k
            lanes = slice(cols.start + k * LANES, cols.start + (k + 1) * LANES)
            u_scr[par, sl, 0:SUBLANES, :] = carry_scr[:, lanes]
            u_scr[par, sl, SUBLANES:, :] = u[:, k * LANES:(k + 1) * LANES]
            s2.append(u_scr[par, sl, pl.ds(SUBLANES - 2, tm, stride=1), :])
            s1.append(u_scr[par, sl, pl.ds(SUBLANES - 1, tm, stride=1), :])
            new_carry.append((lanes, u[tm - SUBLANES:, k * LANES:(k + 1) * LANES]))
        return jnp.concatenate(s2, axis=1), jnp.concatenate(s1, axis=1)

    for c in range(D_FF // FF_CHUNK):
        act_scr[:, c * FF_CHUNK:(c + 1) * FF_CHUNK] = _ffn_chunk(h, c, wu_ref, fw_ref, fb_ref, shifted)
    out_ref[0] = x + _dot(act_scr[...], wd_ref[0])
    for lanes, rows in new_carry:
        carry_scr[:, lanes] = rows

    @pl.when(i == pl.num_programs(1) - 1)
    def _():
        tail_ref[0] = carry_scr[...]


def _ffn_prompt(os_, ls_, y, x, w_out, g, wu, fw, fb, wd, layer, tm):
    bsz, seq, _ = x.shape
    slabs = tuple(a.shape[0] for a in os_)
    assert sum(slabs) == PAIRS
    row = lambda w: pl.BlockSpec((1, tm, w), lambda b, i: (b, i, 0))
    slab_specs = [pl.BlockSpec((n, 1, tm, PAIR_W), lambda b, i: (0, b, i, 0)) for n in slabs]
    chunk_slabs = 2 * FF_CHUNK // LANES
    return pl.pallas_call(
        functools.partial(_ffn_prompt_body, slabs),
        grid=(bsz, seq // tm),
        in_specs=slab_specs * 2 + [row(SSM_WIDTH), row(D_MODEL), _layer_spec(w_out, layer), _const_spec(g.shape),
                                   _layer_spec(wu, layer), _const_spec(fw.shape), _const_spec(fb.shape),
                                   _layer_spec(wd, layer)],
        out_specs=[row(D_MODEL), pl.BlockSpec((1, SUBLANES, 2 * D_FF), lambda b, i: (b, 0, 0))],
        out_shape=[jax.ShapeDtypeStruct((bsz, seq, D_MODEL), F32),
                   jax.ShapeDtypeStruct((bsz, SUBLANES, 2 * D_FF), F32)],
        scratch_shapes=[pltpu.VMEM((2, chunk_slabs, SUBLANES + tm, LANES), F32),
                        pltpu.VMEM((SUBLANES, 2 * D_FF), F32), pltpu.VMEM((tm, D_FF), BF16)],
        compiler_params=_params(("parallel", "arbitrary")),
        name="outproj_ffn_prompt",
    )(*os_, *ls_, y, x, w_out, g, wu, fw, fb, wd)


def _ffn_sample_body(steps, x_ref, g_ref, wu_ref, fw_ref, fb_ref, wd_ref, prev_ref, out_ref, tail_ref,
                     e_scr, u_scr):
    x = x_ref[...]
    rows = x.shape[0]
    n_b = rows // steps
    km1 = prev_ref.shape[0]
    h = _rms(x, g_ref[...]).astype(BF16)
    t = lax.broadcasted_iota(jnp.int32, (rows, FF_CHUNK), 0) % steps

    def shifted(u, cols):
        prev = []
        for k in range(FF_CHUNK // LANES):
            lanes = slice(cols.start + k * LANES, cols.start + (k + 1) * LANES)
            e_scr[k] = jnp.zeros((rows, LANES), F32)
            u_scr[k] = u[:, k * LANES:(k + 1) * LANES]
            for j in range(km1):
                e_scr[k, pl.ds(j, n_b, stride=steps), :] = prev_ref[j, :, lanes]
                tail_ref[j, :, lanes] = u_scr[k, pl.ds(steps - km1 + j, n_b, stride=steps), :]
            prev.append(e_scr[k])
        prev = jnp.concatenate(prev, axis=1)
        s1 = jnp.where(t >= 1, pltpu.roll(u, 1, axis=0), pltpu.roll(prev, rows - 1, axis=0))
        s2 = jnp.where(t >= 2, pltpu.roll(u, 2, axis=0), prev)
        return s2, s1

    act = [_ffn_chunk(h, c, wu_ref, fw_ref, fb_ref, shifted) for c in range(D_FF // FF_CHUNK)]
    out_ref[...] = x + _dot(jnp.concatenate(act, axis=1), wd_ref[0])


def _ffn_sample(x, steps, g, wu, fw, fb, wd, layer, prev):
    m = x.shape[0]
    tr = min(SAMPLE_ROWS, m)
    km1 = prev.shape[0]
    assert m % tr == 0 and tr % steps == 0 and km1 <= steps
    n_b = tr // steps
    row = lambda w: pl.BlockSpec((tr, w), lambda i: (i, 0))
    state = pl.BlockSpec((km1, n_b, 2 * D_FF), lambda i: (0, i, 0))
    return pl.pallas_call(
        functools.partial(_ffn_sample_body, steps),
        grid=(m // tr,),
        in_specs=[row(D_MODEL), _const_spec(g.shape), _layer_spec(wu, layer), _const_spec(fw.shape),
                  _const_spec(fb.shape), _layer_spec(wd, layer), state],
        out_specs=[row(D_MODEL), state],
        out_shape=[jax.ShapeDtypeStruct((m, D_MODEL), F32), jax.ShapeDtypeStruct(prev.shape, F32)],
        scratch_shapes=[pltpu.VMEM((FF_CHUNK // LANES, tr, LANES), F32)] * 2,
        compiler_params=_params(("parallel",)),
        name="ffn_sample",
    )(x, g, wu, fw, fb, wd, prev)


def _sample_attn_batch(steps, dils, b, qkv_ref, kv_refs, cb_refs, nb_ref, gq_ref, gk_ref, bd_ref, hm_ref):
    rows = qkv_ref.shape[1]
    assert rows == 2 * steps
    bd = bd_ref[...]
    hm = hm_ref[...]
    groups = range(len(dils))
    st = [dict() for _ in groups]
    for g in groups:
        slab = lambda kind: jnp.concatenate([qkv_ref[kind * PAIRS + 2 * g], qkv_ref[kind * PAIRS + 2 * g + 1]], axis=1)
        qn = _head_rms(slab(0), gq_ref[...], bd) * (HEAD_DIM ** -0.5)
        kn = _head_rms(slab(1), gk_ref[...], bd)
        vn = slab(2)
        st[g]["kn"] = kn
        st[g]["kn_b"] = jnp.where(b == 0, kn[0:steps], kn[steps:])
        st[g]["vn_b"] = jnp.where(b == 0, vn[0:steps], vn[steps:])
        st[g]["qbd"] = jnp.concatenate([qn * hm[h:h + 1] for h in range(HEADS_PER_GROUP)], axis=0)
        yield
    for g in groups:
        kc = kv_refs[g][0, 0, 0].astype(BF16)
        st[g]["s_c"] = _dot(st[g]["qbd"].astype(BF16), kc) + cb_refs[g][...]
        yield
    for g in groups:
        d = st[g]
        mx = jnp.max(d["s_c"], axis=-1, keepdims=True)
        d["s_n"] = []
        for t2 in range(steps):
            sn = jnp.sum(d["qbd"] * d["kn_b"][t2:t2 + 1], axis=-1, keepdims=True) + nb_ref[g, t2]
            d["s_n"].append(sn)
            mx = jnp.maximum(mx, sn)
        d["mx"] = mx
        d["e_c"] = jnp.exp(d["s_c"] - mx)
        d["den"] = jnp.sum(d["e_c"], axis=-1, keepdims=True)
        yield
    for g in groups:
        vc = kv_refs[g][0, 0, 1].astype(BF16)
        st[g]["acc"] = _dot_nt(st[g]["e_c"].astype(BF16), vc)
        yield
    results = []
    for g in groups:
        d = st[g]
        acc, den = d["acc"], d["den"]
        for t2, sn in enumerate(d["s_n"]):
            e_n = jnp.exp(sn - d["mx"])
            den = den + e_n
            acc = acc + e_n * d["vn_b"][t2:t2 + 1]
        o_all = acc / den
        l_all = d["mx"] + jnp.log(den)
        o_g = None
        l_g = None
        for h in range(HEADS_PER_GROUP):
            o_h = o_all[h * rows:(h + 1) * rows] * hm[h:h + 1]
            l_h = l_all[h * rows:(h + 1) * rows] * hm[h:h + 1]
            o_g = o_h if o_g is None else o_g + o_h
            l_g = l_h if l_g is None else l_g + l_h
        results.append((o_g, l_g, d["kn"]))
        yield
    return results


def _sample_cache_bias(slopes, dil, steps, cache_len, rows):
    t = (np.arange(rows) % steps)[None, :, None]
    c = np.arange(cache_len)[None, None, :]
    dist = cache_len + t - c
    j = dist // dil
    valid = (dist % dil == 0) & (j >= 1) & (j <= BAND)
    bias = -slopes[:, None, None] * dist.astype(np.float32)
    return jnp.asarray(np.where(valid, bias, NEG_INF).reshape(slopes.shape[0] * rows, cache_len), dtype=F32)


def _sample_new_bias(slopes, dil, steps, rows):
    t2 = np.arange(steps)[:, None, None]
    t = (np.arange(rows) % steps)[None, None, :]
    dist = t - t2
    valid = (dist >= 0) & (dist % dil == 0) & (dist // dil <= BAND)
    bias = -slopes[None, :, None] * dist.astype(np.float32)
    return np.where(valid, bias, NEG_INF).reshape(steps, slopes.shape[0] * rows, 1)


def _ssd_sample_pre_body(steps, xbc_ref, z_ref, dt_ref, prev_ref, cw_ref, cb_ref, al_ref, dsk_ref,
                         exp_ref, gexp_ref, bm_ref, cm_ref, xde_ref, pre_ref, ea_ref, zg_ref, cd_ref, tail_ref,
                         e_scr, u_scr):
    raw = xbc_ref[...]
    rows = raw.shape[0]
    n_b = rows // steps
    km1 = prev_ref.shape[0]
    tw = lax.broadcasted_iota(jnp.int32, (rows, XBC_WIDTH), 0) % steps
    cw = cw_ref[...]
    prev = []
    for s in range(XBC_WIDTH // LANES):
        lanes = slice(s * LANES, (s + 1) * LANES)
        e_scr[s] = jnp.zeros((rows, LANES), F32)
        u_scr[s] = raw[:, lanes]
        for j in range(km1):
            e_scr[s, pl.ds(j, n_b, stride=steps), :] = prev_ref[j, :, lanes]
            tail_ref[j, :, lanes] = u_scr[s, pl.ds(steps - km1 + j, n_b, stride=steps), :]
        prev.append(e_scr[s])
    prev = jnp.concatenate(prev, axis=1)
    acc = jnp.where(tw >= 3, pltpu.roll(raw, 3, axis=0), prev) * cw[0:1]
    acc = acc + jnp.where(tw >= 2, pltpu.roll(raw, 2, axis=0), pltpu.roll(prev, rows - 1, axis=0)) * cw[1:2]
    acc = acc + jnp.where(tw >= 1, pltpu.roll(raw, 1, axis=0), pltpu.roll(prev, rows - 2, axis=0)) * cw[2:3]
    acc = acc + raw * cw[3:4]
    xc = _silu(acc + cb_ref[...])
    xs = xc[:, :SSM_WIDTH]
    bm = xc[:, SSM_WIDTH:SSM_WIDTH + BC_W]
    cm = xc[:, SSM_WIDTH + BC_W:]
    bm_ref[...] = bm
    cm_ref[...] = cm

    dt = dt_ref[...]
    da = dt * (-jnp.exp(al_ref[...]))
    th = lax.broadcasted_iota(jnp.int32, (rows, SSM_HEADS), 0) % steps
    acum = da
    for d in range(1, steps):
        acum = acum + jnp.where(th >= d, pltpu.roll(da, d, axis=0), 0.0)
    tail = jnp.zeros_like(da)
    for d in range(1, steps):
        tail = tail + jnp.where(th + d < steps, pltpu.roll(da, rows - d, axis=0), 0.0)
    expand = exp_ref[...]
    dt_x = _dot_sel(dt, expand)
    acum_x = _dot_sel(acum, expand)
    ea_ref[...] = jnp.exp(acum_x)
    xdt = xs * dt_x
    xde_ref[...] = xdt * _dot_sel(jnp.exp(tail), expand)
    cd_ref[...] = jnp.exp(acum)

    ts = lax.broadcasted_iota(jnp.int32, (rows, SSM_WIDTH), 0) % steps
    gexp = gexp_ref[...]
    y = dsk_ref[...] * xs
    for d in range(steps):
        if d == 0:
            bm_d, xdt_d, ac_d = bm, xdt, acum_x
        else:
            bm_d = pltpu.roll(bm, d, axis=0)
            xdt_d = pltpu.roll(xdt, d, axis=0)
            ac_d = pltpu.roll(acum_x, d, axis=0)
        cb_x = _dot_sel(cm * bm_d, gexp, pieces=2)
        term = cb_x * jnp.exp(acum_x - ac_d) * xdt_d
        y = y + jnp.where(ts >= d, term, 0.0)
    pre_ref[...] = y
    zg_ref[...] = _silu(z_ref[...])


def _ssd_sample_pre(xbc, z, dt, prev, cw, cb, al, dsk, expand, gexp, steps):
    m = xbc.shape[0]
    tr = min(SAMPLE_ROWS, m)
    km1 = prev.shape[0]
    assert m % tr == 0 and tr % steps == 0 and km1 <= steps
    row = lambda w: pl.BlockSpec((tr, w), lambda i: (i, 0))
    state = pl.BlockSpec((km1, tr // steps, XBC_WIDTH), lambda i: (0, i, 0))
    consts = (cw, cb, al, dsk, expand, gexp)
    widths = (BC_W, BC_W, SSM_WIDTH, SSM_WIDTH, SSM_WIDTH, SSM_WIDTH, SSM_HEADS)
    return pl.pallas_call(
        functools.partial(_ssd_sample_pre_body, steps),
        grid=(m // tr,),
        in_specs=[row(XBC_WIDTH), row(SSM_WIDTH), row(SSM_HEADS), state]
                 + [_const_spec(a.shape) for a in consts],
        out_specs=[row(w) for w in widths] + [state],
        out_shape=[jax.ShapeDtypeStruct((m, w), F32) for w in widths] + [jax.ShapeDtypeStruct(prev.shape, F32)],
        scratch_shapes=[pltpu.VMEM((XBC_WIDTH // LANES, tr, LANES), F32)] * 2,
        compiler_params=_params(("parallel",)),
        name="ssd_sample_pre",
    )(xbc, z, dt, prev, *consts)


def _alibi_slopes():
    h = np.arange(1, N_ATT_HEADS + 1, dtype=np.float64)
    return np.exp2(-8.0 * h / N_ATT_HEADS).astype(np.float32).reshape(len(ATT_GROUPS), HEADS_PER_GROUP)


def _cache_view(cache):
    depth, bsz, cache_len = cache.shape[:3]
    return jnp.transpose(cache, (0, 1, 3, 4, 5, 2)).reshape(depth, bsz, 2, GROUP_W, cache_len)


def _layer_weights(lw):
    (norm_mix, q_norm, k_norm, conv_w, conv_b, dt_bias, a_log, d_skip, ssm_norm,
     norm_ffn, ffn_conv_w, ffn_conv_b) = lw
    w = {}
    w["norm_mix"] = norm_mix.reshape(1, D_MODEL)
    w["dtb"] = dt_bias.reshape(1, SSM_HEADS)
    w["gq_g"] = jnp.tile(q_norm, HEADS_PER_GROUP).reshape(1, GROUP_W)
    w["gk_g"] = jnp.tile(k_norm, HEADS_PER_GROUP).reshape(1, GROUP_W)
    w["gq_p"] = jnp.tile(q_norm, 2).reshape(1, PAIR_W)
    w["gk_p"] = jnp.tile(k_norm, 2).reshape(1, PAIR_W)
    w["cw"] = conv_w
    w["cb"] = conv_b.reshape(1, XBC_WIDTH)
    w["alr"] = a_log.reshape(1, SSM_HEADS)
    w["alc"] = a_log.reshape(SSM_HEADS, 1)
    w["dsk"] = jnp.repeat(d_skip, SSM_HEAD_DIM).reshape(1, SSM_WIDTH)
    w["gn"] = ssm_norm.reshape(1, SSM_WIDTH)
    w["norm_ffn"] = norm_ffn.reshape(1, D_MODEL)
    w["fw"] = ffn_conv_w
    w["fb"] = ffn_conv_b.reshape(1, 2 * D_FF)
    return w


def _constants(steps, cache_lens):
    c = {}
    slopes = _alibi_slopes()
    c["bd_g"] = _head_block_diag(GROUP_W)
    c["bd_p"] = _head_block_diag(PAIR_W)
    i = np.arange(SSM_CHUNK)
    c["tri"] = jnp.asarray(i[None, :] <= i[:, None], dtype=BF16)
    lane_head = np.arange(SSM_WIDTH) // SSM_HEAD_DIM
    c["expand"] = jnp.asarray(np.arange(SSM_HEADS)[:, None] == lane_head[None, :], dtype=BF16)
    bc_group = np.arange(BC_W) // SSM_STATE
    c["gexp"] = jnp.asarray(bc_group[:, None] == (lane_head // SSM_HPG)[None, :], dtype=BF16)
    c["hm"] = jnp.asarray(np.arange(HEADS_PER_GROUP)[:, None] == (np.arange(GROUP_W) // HEAD_DIM)[None, :], dtype=F32)
    c["pbias"] = [_prompt_bias(slopes[g], dil) for g, (_, dil) in enumerate(ATT_GROUPS)]
    c["cbias"] = [_sample_cache_bias(slopes[g], dil, steps, cache_lens[g], SUBLANES)
                  for g, (_, dil) in enumerate(ATT_GROUPS)]
    c["nbias"] = jnp.asarray(np.stack([_sample_new_bias(slopes[g], dil, steps, SUBLANES)
                                       for g, (_, dil) in enumerate(ATT_GROUPS)]), dtype=F32)
    return c


def _layer(xp, xs, w, big, c, layer, caches, state, new_states, conv_prev, ffn_prev, tm):
    bsz, seq, _ = xp.shape
    m = bsz * seq
    dec, steps, _ = xs.shape
    ms = dec * steps
    assert steps >= SSM_CONV - 1
    xp2 = xp.reshape(m, D_MODEL)
    xs2 = xs.reshape(ms, D_MODEL)
    qkv, z, xbc, dt = _inproj(xp2, w["norm_mix"], big["w_in"], w["dtb"], layer, tm)
    qkv_s, z_s, xbc_s, dt_s = _inproj(xs2, w["norm_mix"], big["w_in"], w["dtb"], layer, ms)

    qkv_t = qkv.reshape(QKV_SLABS, bsz, seq, PAIR_W)
    os_, ls_, kv_p = [], [], []
    for g, (win, dil) in enumerate(ATT_GROUPS):
        nbk = max(1, PROMPT_ATT_ROWS // (BAND * dil))
        o, lse, kn = _attn_prompt(qkv_t, g, dil, nbk, c["pbias"][g], w["gq_p"], w["gk_p"], c["bd_p"])
        os_.append(o)
        ls_.append(lse)
        keep = min(win, seq)
        v_g = qkv_t[2 * PAIRS + 2 * g:2 * PAIRS + 2 * g + 2, :, seq - keep:]
        kv = jnp.stack([kn[:, :, seq - keep:], v_g])
        kv = jnp.transpose(kv, (2, 3, 0, 1, 4))
        kv_p.append(kv.reshape(bsz, keep, 2, HEADS_PER_GROUP, HEAD_DIM))

    xbc3 = xbc.reshape(bsz, seq, XBC_WIDTH)
    dt3 = dt.reshape(bsz, seq, SSM_HEADS)
    ssd_consts = (w["cw"], w["cb"], w["alr"], w["alc"], w["dsk"], w["gn"], c["tri"], c["expand"])
    att_consts = (*c["cbias"], c["nbias"], w["gq_g"], w["gk_g"], c["bd_g"], c["hm"])
    *state_rows, cd, conv_tail = _ssd_sample_pre(xbc_s, z_s, dt_s, jnp.swapaxes(conv_prev, 0, 1), w["cw"],
                                                 w["cb"], w["alr"], w["dsk"], c["expand"], c["gexp"], steps)
    cd_last = cd.reshape(dec, steps, SSM_HEADS)[:, steps - 1]
    y, h_last, o_s, lse_s, kn_s, y_s, new_states = _ssd_prompt_sample(
        xbc3, z.reshape(bsz, seq, SSM_WIDTH), dt3, jnp.swapaxes(dt3, 1, 2), ssd_consts,
        qkv_s, caches, layer, steps, att_consts, state_rows, cd_last, state, new_states)

    x2, tail = _ffn_prompt(os_, ls_, y, xp, big["w_out"], w["norm_ffn"], big["w_up"], w["fw"], w["fb"],
                           big["w_down"], layer, tm)
    h_last = h_last.reshape(bsz, SSM_HEADS, SSM_HEAD_DIM, SSM_STATE)
    conv_p = xbc3[:, seq - (SSM_CONV - 1):]
    ffn_p = tail[:, SUBLANES - (FFN_CONV - 1):]

    kv_s = []
    for g in range(len(ATT_GROUPS)):
        v_g = qkv_s[2 * PAIRS + 2 * g:2 * PAIRS + 2 * g + 2]
        kv = jnp.stack([kn_s[2 * g:2 * g + 2], v_g])
        kv = jnp.transpose(kv, (2, 0, 1, 3))
        kv_s.append(kv.reshape(dec, steps, 2, HEADS_PER_GROUP, HEAD_DIM))
    x1_s = _outproj([o_s], [lse_s], y_s, xs2, big["w_out"], layer, ms)
    x2_s, ffn_tail = _ffn_sample(x1_s, steps, w["norm_ffn"], big["w_up"], w["fw"], w["fb"], big["w_down"], layer,
                                 jnp.swapaxes(ffn_prev, 0, 1))
    conv_s = jnp.swapaxes(conv_tail, 0, 1)
    ffn_s = jnp.swapaxes(ffn_tail, 0, 1)
    return (x2, x2_s.reshape(dec, steps, D_MODEL), (*kv_p, h_last, conv_p, ffn_p), (*kv_s, conv_s, ffn_s),
            new_states)


def kernel(x_prompt, x_sample, cache_kv0, cache_kv1, cache_kv2, state_ssm, state_conv, state_ffn_conv, norm_mix, w_in, q_norm, k_norm, conv_w, conv_b, dt_bias, a_log, d_skip, ssm_norm, w_out, norm_ffn, w_up, ffn_conv_w, ffn_conv_b, w_down):
    stacked = (norm_mix, q_norm, k_norm, conv_w, conv_b, dt_bias, a_log, d_skip, ssm_norm,
               norm_ffn, ffn_conv_w, ffn_conv_b)
    big = {"w_in": w_in.astype(BF16), "w_out": w_out.astype(BF16), "w_up": w_up.astype(BF16),
           "w_down": w_down.astype(BF16)}
    depth = w_in.shape[0]
    dec_batch, steps = x_sample.shape[:2]
    caches = tuple(_cache_view(cache) for cache in (cache_kv0, cache_kv1, cache_kv2))
    for cache, (_, dil) in zip(caches, ATT_GROUPS):
        assert cache.shape[-1] == BAND * dil
    state = state_ssm.reshape(depth, dec_batch, SSM_WIDTH, SSM_STATE)
    c = _constants(steps, tuple(cache.shape[-1] for cache in caches))
    tm = 512
    assert x_prompt.shape[1] % tm == 0
    y_prompt, y_sample = x_prompt, x_sample
    outs_p = [[] for _ in range(6)]
    outs_s = [[] for _ in range(5)]
    new_states = None
    for layer in range(depth):
        w = _layer_weights(tuple(a[layer] for a in stacked))
        y_prompt, y_sample, res_p, res_s, new_states = _layer(
            y_prompt, y_sample, w, big, c, layer, caches, state, new_states, state_conv[layer],
            state_ffn_conv[layer], tm)
        for lst, val in zip(outs_p, res_p):
            lst.append(val)
        for lst, val in zip(outs_s, res_s):
            lst.append(val)
    stack_s = [jnp.stack(l) for l in outs_s]
    return (y_prompt, y_sample, *[jnp.stack(l) for l in outs_p],
            *stack_s[:3], new_states.reshape(state_ssm.shape), *stack_s[3:])
```

```python
import functools
import math

import jax
import jax.numpy as jnp
import numpy as np
from jax import lax
from jax.experimental import pallas as pl
from jax.experimental.pallas import tpu as pltpu

F32 = jnp.float32
BF16 = jnp.bfloat16

D_MODEL = 1024
HEAD_DIM = 64
ATT_GROUPS = ((128, 1), (512, 4), (2048, 16))
BAND = 128
HEADS_PER_GROUP = 4
GROUP_W = HEADS_PER_GROUP * HEAD_DIM
N_ATT_HEADS = HEADS_PER_GROUP * len(ATT_GROUPS)
ATT_WIDTH = N_ATT_HEADS * HEAD_DIM
QKV_W = 3 * ATT_WIDTH
PAIR_W = 2 * HEAD_DIM
PAIRS = ATT_WIDTH // PAIR_W
QKV_SLABS = QKV_W // PAIR_W
SSM_HEAD_DIM = 64
SSM_WIDTH = 1024
SSM_HEADS = SSM_WIDTH // SSM_HEAD_DIM
SSM_STATE = 128
SSM_GROUPS = 2
SSM_HPG = SSM_HEADS // SSM_GROUPS
SSM_GROUP_W = SSM_HPG * SSM_HEAD_DIM
SSM_CONV = 4
SSM_CHUNK = 128
BC_W = SSM_GROUPS * SSM_STATE
XBC_WIDTH = SSM_WIDTH + 2 * BC_W
D_FF = 2816
FFN_CONV = 3
FF_CHUNK = 256
EPS = 1e-6
IN_WIDTH = QKV_W + SSM_WIDTH + XBC_WIDTH + SSM_HEADS
SUBLANES = 8
LANES = 128
SAMPLE_ROWS = 128
PROMPT_ATT_ROWS = 1024
RESIDUE_UNROLL = 4
VMEM_LIMIT = 56 * 1024 * 1024
NEG_INF = float("-inf")
LOG2E = math.log2(math.e)
LN2 = math.log(2.0)


def _dot(a, b):
    return jnp.dot(a, b, preferred_element_type=F32)


def _dot_nt(a, b):
    return lax.dot_general(a, b, (((1,), (1,)), ((), ())), preferred_element_type=F32)


def _dot_tn(a, b):
    return lax.dot_general(a, b, (((0,), (0,)), ((), ())), preferred_element_type=F32)


def _split(x, pieces):
    out = []
    r = x
    for _ in range(pieces):
        p = r.astype(BF16)
        out.append(p)
        r = r - p.astype(F32)
    return out


def _dot_sel(x, sel, pieces=3):
    acc = None
    for p in _split(x, pieces):
        t = _dot(p, sel)
        acc = t if acc is None else acc + t
    return acc


def _sel_dot(sel, x, pieces=3):
    acc = None
    for p in _split(x, pieces):
        t = _dot(sel, p)
        acc = t if acc is None else acc + t
    return acc


def _sel_dot_nt(x, sel, pieces=3):
    acc = None
    for p in _split(x, pieces):
        t = _dot_nt(p, sel)
        acc = t if acc is None else acc + t
    return acc


def _silu(x):
    h = 0.5 * x
    return h * (1.0 + jnp.tanh(h))


def _softplus(x):
    return jnp.maximum(x, 0.0) + jnp.log1p(jnp.exp(-jnp.abs(x)))


def _rms(x, g):
    ms = jnp.mean(x * x, axis=-1, keepdims=True)
    return x * lax.rsqrt(ms + EPS) * g


def _layer_spec(arr, layer):
    nd = arr.ndim
    return pl.BlockSpec((1,) + arr.shape[1:], lambda *_: (layer,) + (0,) * (nd - 1), pipeline_mode=pl.Buffered(1))


def _const_spec(shape):
    nd = len(shape)
    return pl.BlockSpec(shape, lambda *_: (0,) * nd, pipeline_mode=pl.Buffered(1))


def _params(sem):
    return pltpu.CompilerParams(dimension_semantics=sem, vmem_limit_bytes=VMEM_LIMIT)


def _head_block_diag(width):
    h = np.arange(width) // HEAD_DIM
    return jnp.asarray(np.where(h[:, None] == h[None, :], 1.0 / HEAD_DIM, 0.0), dtype=BF16)


def _head_rms(t, g, bd):
    ms = _dot_sel(t * t, bd, pieces=2)
    return t * lax.rsqrt(ms + EPS) * g


def _rows_back(slab_ref, s, d, n):
    return slab_ref[s, pl.ds(SUBLANES - d, n, stride=1), :]


def _inproj_body(x_ref, g_ref, w_ref, dtb_ref, qkv_ref, z_ref, xbc_ref, dt_ref):
    h = _rms(x_ref[...], g_ref[...]).astype(BF16)
    qkv = _dot(h, w_ref[0, :, 0:QKV_W])
    for s in range(QKV_SLABS):
        qkv_ref[s] = qkv[:, s * PAIR_W:(s + 1) * PAIR_W]
    z_ref[...] = _dot(h, w_ref[0, :, QKV_W:QKV_W + SSM_WIDTH])
    xd = _dot(h, w_ref[0, :, QKV_W + SSM_WIDTH:IN_WIDTH])
    xbc_ref[...] = xd[:, :XBC_WIDTH]
    dt_ref[...] = _softplus(xd[:, XBC_WIDTH:] + dtb_ref[...])


def _inproj(x, g, w_in, dtb, layer, tm):
    m = x.shape[0]
    row = lambda w: pl.BlockSpec((tm, w), lambda i: (i, 0))
    widths = (SSM_WIDTH, XBC_WIDTH, SSM_HEADS)
    return pl.pallas_call(
        _inproj_body,
        grid=(m // tm,),
        in_specs=[row(D_MODEL), _const_spec(g.shape), _layer_spec(w_in, layer), _const_spec(dtb.shape)],
        out_specs=[pl.BlockSpec((QKV_SLABS, tm, PAIR_W), lambda i: (0, i, 0))] + [row(w) for w in widths],
        out_shape=[jax.ShapeDtypeStruct((QKV_SLABS, m, PAIR_W), F32)]
                  + [jax.ShapeDtypeStruct((m, w), F32) for w in widths],
        compiler_params=_params(("parallel",)),
        name="inproj",
    )(x, g, w_in, dtb)


def _attn_prompt_body(dil, nbk, q0_ref, q1_ref, k0_ref, k1_ref, v0_ref, v1_ref, bias_ref, gq_ref, gk_ref, bd_ref,
                      o_ref, lse_ref, kn_ref, kv_scr):
    n = pl.program_id(1)
    rows_per = nbk * BAND
    in_refs = ((q0_ref, q1_ref), (k0_ref, k1_ref), (v0_ref, v1_ref))

    @pl.when(n == 0)
    def _():
        kv_scr[...] = jnp.zeros(kv_scr.shape, BF16)

    variant = jnp.minimum(n, 1)
    bd = bd_ref[...]
    gq = gq_ref[...]
    gk = gk_ref[...]
    low = lax.broadcasted_iota(jnp.int32, (BAND, PAIR_W), 1) < HEAD_DIM

    def rows_of(r, start, count):
        return pl.ds(r + start * dil, count, stride=dil) if dil > 1 else pl.ds(start, count)

    def group(residues):
        items = []
        for r in residues:
            for p in range(2):
                q, k, v = (in_refs[kind][p][0, 0, rows_of(r, 0, rows_per), :] for kind in range(3))
                qn = _head_rms(q, gq, bd) * (HEAD_DIM ** -0.5 * LOG2E)
                kn = _head_rms(k, gk, bd)
                kall = jnp.concatenate([kv_scr[0, p, r], kn.astype(BF16)], axis=0)
                vall = jnp.concatenate([kv_scr[1, p, r], v.astype(BF16)], axis=0)
                items.append((r, p, qn, kn, kall, vall))
        scores = []
        for r, p, qn, kn, kall, vall in items:
            for j in range(nbk):
                qj = qn[j * BAND:(j + 1) * BAND]
                for hh in range(2):
                    keep = low if hh == 0 else jnp.logical_not(low)
                    qm = jnp.where(keep, qj, 0.0).astype(BF16)
                    bias = bias_ref[variant if j == 0 else 1, 2 * p + hh]
                    scores.append(_dot_nt(qm, kall[j * BAND:(j + 2) * BAND]) + bias)
        probs = []
        for s in scores:
            mx = jnp.max(s, axis=-1, keepdims=True)
            e = jnp.exp2(s - mx)
            den = jnp.sum(e, axis=-1, keepdims=True)
            probs.append((e.astype(BF16), den, (mx + jnp.log2(den)) * LN2))
        results = []
        it = iter(probs)
        for r, p, qn, kn, kall, vall in items:
            for j in range(nbk):
                (e0, d0, l0), (e1, d1, l1) = next(it), next(it)
                vwin = vall[j * BAND:(j + 2) * BAND]
                o = jnp.where(low, _dot(e0, vwin) / d0, _dot(e1, vwin) / d1)
                results.append((r, p, j, o, jnp.where(low, l0, l1)))
        for r, p, j, o, lse in results:
            o_ref[p, 0, rows_of(r, j * BAND, BAND), :] = o
            lse_ref[p, 0, rows_of(r, j * BAND, BAND), :] = lse
        for r, p, qn, kn, kall, vall in items:
            kn_ref[p, 0, rows_of(r, 0, rows_per), :] = kn
            kv_scr[0, p, r] = kall[rows_per:]
            kv_scr[1, p, r] = vall[rows_per:]

    if dil <= RESIDUE_UNROLL:
        group(range(dil))
    else:
        def step(i, carry):
            group([i * RESIDUE_UNROLL + k for k in range(RESIDUE_UNROLL)])
            return carry
        lax.fori_loop(0, dil // RESIDUE_UNROLL, step, 0)


def _prompt_bias(slopes, dil):
    i = np.arange(BAND)[:, None]
    j = np.arange(2 * BAND)[None, :]
    step = i + BAND - j
    valid = (step >= 0) & (step <= BAND)
    bias = -slopes[:, None, None] * (step * dil).astype(np.float32)
    full = np.where(valid[None], bias, NEG_INF)
    first = np.where((valid & (j >= BAND))[None], bias, NEG_INF)
    return jnp.asarray(np.stack([first, full]) * LOG2E, dtype=F32)


def _attn_prompt(qkv_t, g, dil, nbk, bias, gq, gk, bd):
    _, bsz, seq, _ = qkv_t.shape
    blk = nbk * BAND * dil
    assert seq % blk == 0
    slab = lambda s: pl.BlockSpec((1, 1, blk, PAIR_W), lambda b, n: (s, b, n, 0))
    ospec = pl.BlockSpec((2, 1, blk, PAIR_W), lambda b, n: (0, b, n, 0))
    first = [kind * PAIRS + 2 * g for kind in range(3)]
    return pl.pallas_call(
        functools.partial(_attn_prompt_body, dil, nbk),
        grid=(bsz, seq // blk),
        in_specs=[slab(first[0]), slab(first[0] + 1), slab(first[1]), slab(first[1] + 1),
                  slab(first[2]), slab(first[2] + 1),
                  _const_spec(bias.shape), _const_spec(gq.shape), _const_spec(gk.shape), _const_spec(bd.shape)],
        out_specs=[ospec, ospec, ospec],
        out_shape=[jax.ShapeDtypeStruct((2, bsz, seq, PAIR_W), F32)] * 3,
        scratch_shapes=[pltpu.VMEM((2, 2, dil, BAND, PAIR_W), BF16)],
        compiler_params=_params(("parallel", "arbitrary")),
        name=f"attn_prompt_g{g}",
    )(*([qkv_t] * 6), bias, gq, gk, bd)


def _ssd_stage_conv(xbc_ref, conv_scr):
    raw = xbc_ref[0]
    for s in range(XBC_WIDTH // LANES):
        conv_scr[s, SUBLANES:, :] = raw[:, s * LANES:(s + 1) * LANES]
    return raw


def _ssd_chunk(raw, z_ref, dt_ref, dtt_ref, cw_ref, cb_ref, alr_ref, alc_ref, dsk_ref, gn_ref,
               tri_ref, exp_ref, conv_scr, state_scr):
    q = SSM_CHUNK

    cw = cw_ref[...]
    z = z_ref[0]
    state = state_scr[...]
    n_slabs = XBC_WIDTH // LANES
    parts = []
    for s in range(n_slabs):
        lanes = slice(s * LANES, (s + 1) * LANES)
        acc = _rows_back(conv_scr, s, 3, q) * cw[0:1, lanes]
        acc = acc + _rows_back(conv_scr, s, 2, q) * cw[1:2, lanes]
        acc = acc + _rows_back(conv_scr, s, 1, q) * cw[2:3, lanes]
        parts.append(_silu(acc + raw[:, lanes] * cw[3:4, lanes] + cb_ref[:, lanes]))
        if s % 3 == 2:
            yield
    xc = jnp.concatenate(parts, axis=1)
    xs = xc[:, :SSM_WIDTH]
    bm = xc[:, SSM_WIDTH:SSM_WIDTH + BC_W]
    cm = xc[:, SSM_WIDTH + BC_W:]

    tri = tri_ref[...]
    dt = dt_ref[0]
    acum = _sel_dot(tri, dt * (-jnp.exp(alr_ref[...])))
    acum_t = _sel_dot_nt(dtt_ref[0] * (-jnp.exp(alc_ref[...])), tri)
    yield
    last = acum[q - 1:q, :]
    expand = exp_ref[...]
    dt_x = _dot_sel(dt, expand)
    ea_x = _dot_sel(jnp.exp(acum), expand)
    de_x = _dot_sel(jnp.exp(last - acum), expand)
    xdt = xs * dt_x
    xde = (xdt * de_x).astype(BF16)
    xdt_b = xdt.astype(BF16)
    causal = lax.broadcasted_iota(jnp.int32, (q, q), 0) >= lax.broadcasted_iota(jnp.int32, (q, q), 1)
    low = lax.broadcasted_iota(jnp.int32, (q, 2 * SSM_HEAD_DIM), 1) < SSM_HEAD_DIM

    y_parts = []
    new_state = []
    for g in range(SSM_GROUPS):
        bm_g = bm[:, g * SSM_STATE:(g + 1) * SSM_STATE].astype(BF16)
        cm_g = cm[:, g * SSM_STATE:(g + 1) * SSM_STATE].astype(BF16)
        cb = _dot_nt(cm_g, bm_g)
        rows = slice(g * SSM_GROUP_W, (g + 1) * SSM_GROUP_W)
        st_g = state[rows, :]
        inter = _dot_nt(cm_g, st_g.astype(BF16))
        for hp in range(SSM_HPG // 2):
            pair = None
            for k in range(2):
                h = g * SSM_HPG + 2 * hp + k
                seg = acum[:, h:h + 1] - acum_t[h:h + 1, :]
                decay = jnp.exp(jnp.where(causal, seg, NEG_INF))
                gmat = (cb * decay).astype(BF16)
                lanes = slice((2 * hp) * SSM_HEAD_DIM + g * SSM_GROUP_W,
                              (2 * hp + 2) * SSM_HEAD_DIM + g * SSM_GROUP_W)
                x_pair = xdt_b[:, lanes]
                keep = low if k == 0 else jnp.logical_not(low)
                t = _dot(gmat, jnp.where(keep, x_pair, jnp.zeros_like(x_pair)))
                pair = t if pair is None else pair + t
            y_parts.append(pair + ea_x[:, lanes] * inter[:, lanes.start - g * SSM_GROUP_W:lanes.stop - g * SSM_GROUP_W])
            yield
        new = _dot_tn(xde[:, rows], bm_g)
        for hh in range(SSM_HPG):
            h = g * SSM_HPG + hh
            cd = jnp.exp(acum_t[h:h + 1, q - 1:q])
            r = slice(h * SSM_HEAD_DIM, (h + 1) * SSM_HEAD_DIM)
            rl = slice(hh * SSM_HEAD_DIM, (hh + 1) * SSM_HEAD_DIM)
            new_state.append((r, st_g[rl, :] * cd + new[rl, :]))

    yield
    y = jnp.concatenate(y_parts, axis=1) + dsk_ref[...] * xs
    y = y * _silu(z)
    return _rms(y, gn_ref[...]), new_state


def _ssd_store(raw, y, new_state, y_ref, conv_scr, state_scr):
    y_ref[0] = y
    for r, val in new_state:
        state_scr[r, :] = val
    for s in range(XBC_WIDTH // LANES):
        conv_scr[s, 0:SUBLANES, :] = raw[SSM_CHUNK - SUBLANES:, s * LANES:(s + 1) * LANES]


def _interleave(*gens):
    results = [None] * len(gens)
    live = list(range(len(gens)))
    while live:
        for i in list(live):
            try:
                next(gens[i])
            except StopIteration as done:
                results[i] = done.value
                live.remove(i)
    return results


def _sample_state_batch(steps, mine, batch, bm_ref, cm_ref, xde_ref, pre_ref, ea_ref, zg_ref, cd_ref, gn_ref,
                        st_ref):
    bm = bm_ref[...].astype(BF16)
    cm = cm_ref[...].astype(BF16)
    xde = jnp.where(mine, xde_ref[...], 0.0).astype(BF16)
    inter, new = [], []
    for g in range(SSM_GROUPS):
        srows = slice(g * SSM_GROUP_W, (g + 1) * SSM_GROUP_W)
        lanes = slice(g * SSM_STATE, (g + 1) * SSM_STATE)
        inter.append(_dot_nt(cm[:, lanes], st_ref[0, 0, srows, :].astype(BF16)))
        new.append(_dot_tn(xde[:, srows], bm[:, lanes]))
        yield
    y = pre_ref[...] + ea_ref[...] * jnp.concatenate(inter, axis=1)
    y = _rms(y * zg_ref[...], gn_ref[...])
    new_state = []
    for h in range(SSM_HEADS):
        g, hh = divmod(h, SSM_HPG)
        r = slice(h * SSM_HEAD_DIM, (h + 1) * SSM_HEAD_DIM)
        new_state.append((r, st_ref[0, 0, r, :] * cd_ref[batch, h]
                          + new[g][hh * SSM_HEAD_DIM:(hh + 1) * SSM_HEAD_DIM, :]))
    return y, new_state


def _ssd_attn_body(steps, dils, layer, fills, n_ssd, *refs):
    ssd_in, rest = refs[:n_ssd], refs[n_ssd:]
    (qkv_ref, kv0_ref, kv1_ref, kv2_ref, cb0_ref, cb1_ref, cb2_ref, nb_ref, gq_ref, gk_ref, bd_ref, hm_ref,
     bm_ref, cm_ref, xde_ref, pre_ref, ea_ref, zg_ref, cd_ref, st_ref) = rest[:20]
    y_ref, hl_ref, o_ref, lse_ref, kn_ref, ys_ref, ns_ref, conv_scr, state_scr = rest[-9:]
    c = pl.program_id(1)
    n_b = SUBLANES // steps
    batch = pl.program_id(0) * pl.num_programs(1) + c
    b = batch % n_b

    @pl.when(c == 0)
    def _():
        conv_scr[:, 0:SUBLANES, :] = jnp.zeros((conv_scr.shape[0], SUBLANES, LANES), F32)
        state_scr[...] = jnp.zeros_like(state_scr)

    raw = _ssd_stage_conv(ssd_in[0], conv_scr)
    mine_w = lax.broadcasted_iota(jnp.int32, (SUBLANES, SSM_WIDTH), 0) // steps == b
    sample, (y, new_state), (ys, new_sample_state) = _interleave(
        _sample_attn_batch(steps, dils, b, qkv_ref, (kv0_ref, kv1_ref, kv2_ref), (cb0_ref, cb1_ref, cb2_ref),
                           nb_ref, gq_ref, gk_ref, bd_ref, hm_ref),
        _ssd_chunk(raw, *ssd_in[1:], conv_scr, state_scr),
        _sample_state_batch(steps, mine_w, batch, bm_ref, cm_ref, xde_ref, pre_ref, ea_ref, zg_ref, cd_ref,
                            ssd_in[9], st_ref))
    _ssd_store(raw, y, new_state, y_ref, conv_scr, state_scr)
    own = layer if fills else 0
    for l in range(ns_ref.shape[0]):
        if l != own:
            ns_ref[l] = jnp.zeros(ns_ref.shape[1:], F32)
    for r, val in new_sample_state:
        ns_ref[own, 0, r, :] = val

    mine = lax.broadcasted_iota(jnp.int32, (SUBLANES, PAIR_W), 0) // steps == b

    @pl.when(b == 0)
    def _():
        ys_ref[...] = jnp.where(mine_w, ys, 0.0)
        for g, (o_g, l_g, kn) in enumerate(sample):
            for p in range(2):
                lanes = slice(p * PAIR_W, (p + 1) * PAIR_W)
                o_ref[2 * g + p] = jnp.where(mine, o_g[:, lanes], 0.0)
                lse_ref[2 * g + p] = jnp.where(mine, l_g[:, lanes], 0.0)
                kn_ref[2 * g + p] = kn[:, lanes]

    @pl.when(b > 0)
    def _():
        ys_ref[...] = jnp.where(mine_w, ys, ys_ref[...])
        for g, (o_g, l_g, kn) in enumerate(sample):
            for p in range(2):
                lanes = slice(p * PAIR_W, (p + 1) * PAIR_W)
                o_ref[2 * g + p] = jnp.where(mine, o_g[:, lanes], o_ref[2 * g + p])
                lse_ref[2 * g + p] = jnp.where(mine, l_g[:, lanes], lse_ref[2 * g + p])

    @pl.when(c == pl.num_programs(1) - 1)
    def _():
        hl_ref[0] = state_scr[...]


def _ssd_prompt_sample(xbc, z, dt, dtt, ssd_consts, qkv_t, caches, layer, steps, att_consts,
                       state_rows, cd, state, new_states):
    bsz, seq, _ = xbc.shape
    assert seq % SSM_CHUNK == 0
    nc = seq // SSM_CHUNK
    m = qkv_t.shape[1]
    n_b = SUBLANES // steps
    assert SUBLANES % steps == 0 and m == bsz * nc * steps
    dils = tuple(d for _, d in ATT_GROUPS)
    blk = lambda w: pl.BlockSpec((1, SSM_CHUNK, w), lambda b, c: (b, c, 0))
    tile = lambda b, c: (b * nc + c) // n_b
    rows = lambda w: pl.BlockSpec((SUBLANES, w), lambda b, c: (tile(b, c), 0))
    per_batch = lambda arr, lead, first: pl.BlockSpec((lead, 1) + arr.shape[2:],
                                                      lambda b, c: (first, b * nc + c) + (0,) * (arr.ndim - 2))
    ospec = pl.BlockSpec((PAIRS, SUBLANES, PAIR_W), lambda b, c: (0, tile(b, c), 0))
    ssd_in = [xbc, z, dt, dtt, *ssd_consts]
    fills = new_states is None
    ins = [*ssd_in, qkv_t, *caches, *att_consts, *state_rows, cd, state]
    in_specs = ([blk(XBC_WIDTH), blk(SSM_WIDTH), blk(SSM_HEADS),
                 pl.BlockSpec((1, SSM_HEADS, SSM_CHUNK), lambda b, c: (b, 0, c))]
                + [_const_spec(a.shape) for a in ssd_consts]
                + [pl.BlockSpec((QKV_SLABS, SUBLANES, PAIR_W), lambda b, c: (0, tile(b, c), 0))]
                + [per_batch(cache, 1, layer) for cache in caches]
                + [_const_spec(a.shape) for a in att_consts]
                + [rows(a.shape[1]) for a in state_rows]
                + [pl.BlockSpec(memory_space=pltpu.SMEM), per_batch(state, 1, layer)])
    aliases = {}
    if not fills:
        aliases = {len(ins): 6}
        ins.append(new_states)
        in_specs.append(pl.BlockSpec(memory_space=pl.ANY))
    ns_spec = per_batch(state, state.shape[0], 0) if fills else per_batch(state, 1, layer)
    return pl.pallas_call(
        functools.partial(_ssd_attn_body, steps, dils, layer, fills, len(ssd_in)),
        grid=(bsz, nc),
        in_specs=in_specs,
        out_specs=[blk(SSM_WIDTH), pl.BlockSpec((1, SSM_WIDTH, SSM_STATE), lambda b, c: (b, 0, 0)),
                   ospec, ospec, ospec, rows(SSM_WIDTH), ns_spec],
        out_shape=[jax.ShapeDtypeStruct((bsz, seq, SSM_WIDTH), F32),
                   jax.ShapeDtypeStruct((bsz, SSM_WIDTH, SSM_STATE), F32)]
                  + [jax.ShapeDtypeStruct((PAIRS, m, PAIR_W), F32)] * 3
                  + [jax.ShapeDtypeStruct((m, SSM_WIDTH), F32), jax.ShapeDtypeStruct(state.shape, F32)],
        scratch_shapes=[pltpu.VMEM((XBC_WIDTH // LANES, SUBLANES + SSM_CHUNK, LANES), F32),
                        pltpu.VMEM((SSM_WIDTH, SSM_STATE), F32)],
        input_output_aliases=aliases,
        compiler_params=_params(("arbitrary", "arbitrary")),
        name="ssd_prompt_sample",
    )(*ins)


def _mix_project(o_sl, l_sl, y, x, w_ref):
    ngrp = len(ATT_GROUPS)
    att = [None] * PAIRS
    for p in range(PAIRS // ngrp):
        ls = [l_sl[2 * g + p] for g in range(ngrp)]
        mx = functools.reduce(jnp.maximum, ls)
        es = [jnp.exp(l - mx) for l in ls]
        inv = 1.0 / functools.reduce(jnp.add, es)
        for g in range(ngrp):
            att[2 * g + p] = (o_sl[2 * g + p] * (es[g] * inv)).astype(BF16)
    acc = x + _dot(y.astype(BF16), w_ref[0, ATT_WIDTH:, :])
    return acc + _dot(jnp.concatenate(att, axis=1), w_ref[0, 0:ATT_WIDTH, :])


def _outproj_body(slabs, *refs):
    n_arr = len(slabs)
    o_refs, l_refs = refs[:n_arr], refs[n_arr:2 * n_arr]
    y_ref, x_ref, w_ref, out_ref = refs[2 * n_arr:]
    o_sl = [r[s] for r, cnt in zip(o_refs, slabs) for s in range(cnt)]
    l_sl = [r[s] for r, cnt in zip(l_refs, slabs) for s in range(cnt)]
    out_ref[...] = _mix_project(o_sl, l_sl, y_ref[...], x_ref[...], w_ref)


def _outproj(os_, ls_, y, x, w_out, layer, tm):
    m = x.shape[0]
    slabs = tuple(a.shape[0] for a in os_)
    assert sum(slabs) == PAIRS
    row = lambda w: pl.BlockSpec((tm, w), lambda i: (i, 0))
    slab_specs = [pl.BlockSpec((n, tm, PAIR_W), lambda i: (0, i, 0)) for n in slabs]
    return pl.pallas_call(
        functools.partial(_outproj_body, slabs),
        grid=(m // tm,),
        in_specs=slab_specs * 2 + [row(SSM_WIDTH), row(D_MODEL), _layer_spec(w_out, layer)],
        out_specs=row(D_MODEL),
        out_shape=jax.ShapeDtypeStruct((m, D_MODEL), F32),
        compiler_params=_params(("parallel",)),
        name="outproj",
    )(*os_, *ls_, y, x, w_out)


def _ffn_chunk(h, c, wu_ref, fw_ref, fb_ref, shifted):
    halves = []
    for base in (0, D_FF):
        cols = slice(base + c * FF_CHUNK, base + (c + 1) * FF_CHUNK)
        u = _dot(h, wu_ref[0, :, cols])
        s2, s1 = shifted(u, cols)
        w = fw_ref[:, cols]
        halves.append(s2 * w[0:1] + s1 * w[1:2] + u * w[2:3] + fb_ref[:, cols])
    return (_silu(halves[0]) * halves[1]).astype(BF16)


def _ffn_prompt_body(slabs, *refs):
    n_arr = len(slabs)
    o_refs, l_refs = refs[:n_arr], refs[n_arr:2 * n_arr]
    (y_ref, x_ref, wo_ref, g_ref, wu_ref, fw_ref, fb_ref, wd_ref,
     out_ref, tail_ref, u_scr, carry_scr, act_scr) = refs[2 * n_arr:]
    i = pl.program_id(1)

    @pl.when(i == 0)
    def _():
        carry_scr[...] = jnp.zeros_like(carry_scr)

    o_sl = [r[s, 0] for r, cnt in zip(o_refs, slabs) for s in range(cnt)]
    l_sl = [r[s, 0] for r, cnt in zip(l_refs, slabs) for s in range(cnt)]
    x = _mix_project(o_sl, l_sl, y_ref[0], x_ref[0], wo_ref)
    tm = x.shape[0]
    h = _rms(x, g_ref[...]).astype(BF16)
    new_carry = []

    def shifted(u, cols):
        half, start = divmod(cols.start, D_FF)
        par = (start // FF_CHUNK) % 2
        s2, s1 = [], []
        for k in range(FF_CHUNK // LANES):
            sl = half * (FF_CHUNK // LANES) + k
            lanes = slice(cols.start + k * LANES, cols.start + (k + 1) * LANES)
            u_scr[par, sl, 0:SUBLANES, :] = carry_scr[:, lanes]
            u_scr[par, sl, SUBLANES:, :] = u[:, k * LANES:(k + 1) * LANES]
            s2.append(u_scr[par, sl, pl.ds(SUBLANES - 2, tm, stride=1), :])
            s1.append(u_scr[par, sl, pl.ds(SUBLANES - 1, tm, stride=1), :])
            new_carry.append((lanes, u[tm - SUBLANES:, k * LANES:(k + 1) * LANES]))
        return jnp.concatenate(s2, axis=1), jnp.concatenate(s1, axis=1)

    for c in range(D_FF // FF_CHUNK):
        act_scr[:, c * FF_CHUNK:(c + 1) * FF_CHUNK] = _ffn_chunk(h, c, wu_ref, fw_ref, fb_ref, shifted)
    out_ref[0] = x + _dot(act_scr[...], wd_ref[0])
    for lanes, rows in new_carry:
        carry_scr[:, lanes] = rows

    @pl.when(i == pl.num_programs(1) - 1)
    def _():
        tail_ref[0] = carry_scr[...]


def _ffn_prompt(os_, ls_, y, x, w_out, g, wu, fw, fb, wd, layer, tm):
    bsz, seq, _ = x.shape
    slabs = tuple(a.shape[0] for a in os_)
    assert sum(slabs) == PAIRS
    row = lambda w: pl.BlockSpec((1, tm, w), lambda b, i: (b, i, 0))
    slab_specs = [pl.BlockSpec((n, 1, tm, PAIR_W), lambda b, i: (0, b, i, 0)) for n in slabs]
    chunk_slabs = 2 * FF_CHUNK // LANES
    return pl.pallas_call(
        functools.partial(_ffn_prompt_body, slabs),
        grid=(bsz, seq // tm),
        in_specs=slab_specs * 2 + [row(SSM_WIDTH), row(D_MODEL), _layer_spec(w_out, layer), _const_spec(g.shape),
                                   _layer_spec(wu, layer), _const_spec(fw.shape), _const_spec(fb.shape),
                                   _layer_spec(wd, layer)],
        out_specs=[row(D_MODEL), pl.BlockSpec((1, SUBLANES, 2 * D_FF), lambda b, i: (b, 0, 0))],
        out_shape=[jax.ShapeDtypeStruct((bsz, seq, D_MODEL), F32),
                   jax.ShapeDtypeStruct((bsz, SUBLANES, 2 * D_FF), F32)],
        scratch_shapes=[pltpu.VMEM((2, chunk_slabs, SUBLANES + tm, LANES), F32),
                        pltpu.VMEM((SUBLANES, 2 * D_FF), F32), pltpu.VMEM((tm, D_FF), BF16)],
        compiler_params=_params(("parallel", "arbitrary")),
        name="outproj_ffn_prompt",
    )(*os_, *ls_, y, x, w_out, g, wu, fw, fb, wd)


def _ffn_sample_body(steps, x_ref, g_ref, wu_ref, fw_ref, fb_ref, wd_ref, prev_ref, out_ref, tail_ref,
                     e_scr, u_scr):
    x = x_ref[...]
    rows = x.shape[0]
    n_b = rows // steps
    km1 = prev_ref.shape[0]
    h = _rms(x, g_ref[...]).astype(BF16)
    t = lax.broadcasted_iota(jnp.int32, (rows, FF_CHUNK), 0) % steps

    def shifted(u, cols):
        prev = []
        for k in range(FF_CHUNK // LANES):
            lanes = slice(cols.start + k * LANES, cols.start + (k + 1) * LANES)
            e_scr[k] = jnp.zeros((rows, LANES), F32)
            u_scr[k] = u[:, k * LANES:(k + 1) * LANES]
            for j in range(km1):
                e_scr[k, pl.ds(j, n_b, stride=steps), :] = prev_ref[j, :, lanes]
                tail_ref[j, :, lanes] = u_scr[k, pl.ds(steps - km1 + j, n_b, stride=steps), :]
            prev.append(e_scr[k])
        prev = jnp.concatenate(prev, axis=1)
        s1 = jnp.where(t >= 1, pltpu.roll(u, 1, axis=0), pltpu.roll(prev, rows - 1, axis=0))
        s2 = jnp.where(t >= 2, pltpu.roll(u, 2, axis=0), prev)
        return s2, s1

    act = [_ffn_chunk(h, c, wu_ref, fw_ref, fb_ref, shifted) for c in range(D_FF // FF_CHUNK)]
    out_ref[...] = x + _dot(jnp.concatenate(act, axis=1), wd_ref[0])


def _ffn_sample(x, steps, g, wu, fw, fb, wd, layer, prev):
    m = x.shape[0]
    tr = min(SAMPLE_ROWS, m)
    km1 = prev.shape[0]
    assert m % tr == 0 and tr % steps == 0 and km1 <= steps
    n_b = tr // steps
    row = lambda w: pl.BlockSpec((tr, w), lambda i: (i, 0))
    state = pl.BlockSpec((km1, n_b, 2 * D_FF), lambda i: (0, i, 0))
    return pl.pallas_call(
        functools.partial(_ffn_sample_body, steps),
        grid=(m // tr,),
        in_specs=[row(D_MODEL), _const_spec(g.shape), _layer_spec(wu, layer), _const_spec(fw.shape),
                  _const_spec(fb.shape), _layer_spec(wd, layer), state],
        out_specs=[row(D_MODEL), state],
        out_shape=[jax.ShapeDtypeStruct((m, D_MODEL), F32), jax.ShapeDtypeStruct(prev.shape, F32)],
        scratch_shapes=[pltpu.VMEM((FF_CHUNK // LANES, tr, LANES), F32)] * 2,
        compiler_params=_params(("parallel",)),
        name="ffn_sample",
    )(x, g, wu, fw, fb, wd, prev)


def _sample_attn_batch(steps, dils, b, qkv_ref, kv_refs, cb_refs, nb_ref, gq_ref, gk_ref, bd_ref, hm_ref):
    rows = qkv_ref.shape[1]
    assert rows == 2 * steps
    bd = bd_ref[...]
    hm = hm_ref[...]
    groups = range(len(dils))
    st = [dict() for _ in groups]
    for g in groups:
        slab = lambda kind: jnp.concatenate([qkv_ref[kind * PAIRS + 2 * g], qkv_ref[kind * PAIRS + 2 * g + 1]], axis=1)
        qn = _head_rms(slab(0), gq_ref[...], bd) * (HEAD_DIM ** -0.5)
        kn = _head_rms(slab(1), gk_ref[...], bd)
        vn = slab(2)
        st[g]["kn"] = kn
        st[g]["kn_b"] = jnp.where(b == 0, kn[0:steps], kn[steps:])
        st[g]["vn_b"] = jnp.where(b == 0, vn[0:steps], vn[steps:])
        st[g]["qbd"] = jnp.concatenate([qn * hm[h:h + 1] for h in range(HEADS_PER_GROUP)], axis=0)
        yield
    for g in groups:
        kc = kv_refs[g][0, 0, 0].astype(BF16)
        st[g]["s_c"] = _dot(st[g]["qbd"].astype(BF16), kc) + cb_refs[g][...]
        yield
    for g in groups:
        d = st[g]
        mx = jnp.max(d["s_c"], axis=-1, keepdims=True)
        d["s_n"] = []
        for t2 in range(steps):
            sn = jnp.sum(d["qbd"] * d["kn_b"][t2:t2 + 1], axis=-1, keepdims=True) + nb_ref[g, t2]
            d["s_n"].append(sn)
            mx = jnp.maximum(mx, sn)
        d["mx"] = mx
        d["e_c"] = jnp.exp(d["s_c"] - mx)
        d["den"] = jnp.sum(d["e_c"], axis=-1, keepdims=True)
        yield
    for g in groups:
        vc = kv_refs[g][0, 0, 1].astype(BF16)
        st[g]["acc"] = _dot_nt(st[g]["e_c"].astype(BF16), vc)
        yield
    results = []
    for g in groups:
        d = st[g]
        acc, den = d["acc"], d["den"]
        for t2, sn in enumerate(d["s_n"]):
            e_n = jnp.exp(sn - d["mx"])
            den = den + e_n
            acc = acc + e_n * d["vn_b"][t2:t2 + 1]
        o_all = acc / den
        l_all = d["mx"] + jnp.log(den)
        o_g = None
        l_g = None
        for h in range(HEADS_PER_GROUP):
            o_h = o_all[h * rows:(h + 1) * rows] * hm[h:h + 1]
            l_h = l_all[h * rows:(h + 1) * rows] * hm[h:h + 1]
            o_g = o_h if o_g is None else o_g + o_h
            l_g = l_h if l_g is None else l_g + l_h
        results.append((o_g, l_g, d["kn"]))
        yield
    return results


def _sample_cache_bias(slopes, dil, steps, cache_len, rows):
    t = (np.arange(rows) % steps)[None, :, None]
    c = np.arange(cache_len)[None, None, :]
    dist = cache_len + t - c
    j = dist // dil
    valid = (dist % dil == 0) & (j >= 1) & (j <= BAND)
    bias = -slopes[:, None, None] * dist.astype(np.float32)
    return jnp.asarray(np.where(valid, bias, NEG_INF).reshape(slopes.shape[0] * rows, cache_len), dtype=F32)


def _sample_new_bias(slopes, dil, steps, rows):
    t2 = np.arange(steps)[:, None, None]
    t = (np.arange(rows) % steps)[None, None, :]
    dist = t - t2
    valid = (dist >= 0) & (dist % dil == 0) & (dist // dil <= BAND)
    bias = -slopes[None, :, None] * dist.astype(np.float32)
    return np.where(valid, bias, NEG_INF).reshape(steps, slopes.shape[0] * rows, 1)


def _ssd_sample_pre_body(steps, xbc_ref, z_ref, dt_ref, prev_ref, cw_ref, cb_ref, al_ref, dsk_ref,
                         exp_ref, gexp_ref, bm_ref, cm_ref, xde_ref, pre_ref, ea_ref, zg_ref, cd_ref, tail_ref,
                         e_scr, u_scr):
    raw = xbc_ref[...]
    rows = raw.shape[0]
    n_b = rows // steps
    km1 = prev_ref.shape[0]
    tw = lax.broadcasted_iota(jnp.int32, (rows, XBC_WIDTH), 0) % steps
    cw = cw_ref[...]
    prev = []
    for s in range(XBC_WIDTH // LANES):
        lanes = slice(s * LANES, (s + 1) * LANES)
        e_scr[s] = jnp.zeros((rows, LANES), F32)
        u_scr[s] = raw[:, lanes]
        for j in range(km1):
            e_scr[s, pl.ds(j, n_b, stride=steps), :] = prev_ref[j, :, lanes]
            tail_ref[j, :, lanes] = u_scr[s, pl.ds(steps - km1 + j, n_b, stride=steps), :]
        prev.append(e_scr[s])
    prev = jnp.concatenate(prev, axis=1)
    acc = jnp.where(tw >= 3, pltpu.roll(raw, 3, axis=0), prev) * cw[0:1]
    acc = acc + jnp.where(tw >= 2, pltpu.roll(raw, 2, axis=0), pltpu.roll(prev, rows - 1, axis=0)) * cw[1:2]
    acc = acc + jnp.where(tw >= 1, pltpu.roll(raw, 1, axis=0), pltpu.roll(prev, rows - 2, axis=0)) * cw[2:3]
    acc = acc + raw * cw[3:4]
    xc = _silu(acc + cb_ref[...])
    xs = xc[:, :SSM_WIDTH]
    bm = xc[:, SSM_WIDTH:SSM_WIDTH + BC_W]
    cm = xc[:, SSM_WIDTH + BC_W:]
    bm_ref[...] = bm
    cm_ref[...] = cm

    dt = dt_ref[...]
    da = dt * (-jnp.exp(al_ref[...]))
    th = lax.broadcasted_iota(jnp.int32, (rows, SSM_HEADS), 0) % steps
    acum = da
    for d in range(1, steps):
        acum = acum + jnp.where(th >= d, pltpu.roll(da, d, axis=0), 0.0)
    tail = jnp.zeros_like(da)
    for d in range(1, steps):
        tail = tail + jnp.where(th + d < steps, pltpu.roll(da, rows - d, axis=0), 0.0)
    expand = exp_ref[...]
    dt_x = _dot_sel(dt, expand)
    acum_x = _dot_sel(acum, expand)
    ea_ref[...] = jnp.exp(acum_x)
    xdt = xs * dt_x
    xde_ref[...] = xdt * _dot_sel(jnp.exp(tail), expand)
    cd_ref[...] = jnp.exp(acum)

    ts = lax.broadcasted_iota(jnp.int32, (rows, SSM_WIDTH), 0) % steps
    gexp = gexp_ref[...]
    y = dsk_ref[...] * xs
    for d in range(steps):
        if d == 0:
            bm_d, xdt_d, ac_d = bm, xdt, acum_x
        else:
            bm_d = pltpu.roll(bm, d, axis=0)
            xdt_d = pltpu.roll(xdt, d, axis=0)
            ac_d = pltpu.roll(acum_x, d, axis=0)
        cb_x = _dot_sel(cm * bm_d, gexp, pieces=2)
        term = cb_x * jnp.exp(acum_x - ac_d) * xdt_d
        y = y + jnp.where(ts >= d, term, 0.0)
    pre_ref[...] = y
    zg_ref[...] = _silu(z_ref[...])


def _ssd_sample_pre(xbc, z, dt, prev, cw, cb, al, dsk, expand, gexp, steps):
    m = xbc.shape[0]
    tr = min(SAMPLE_ROWS, m)
    km1 = prev.shape[0]
    assert m % tr == 0 and tr % steps == 0 and km1 <= steps
    row = lambda w: pl.BlockSpec((tr, w), lambda i: (i, 0))
    state = pl.BlockSpec((km1, tr // steps, XBC_WIDTH), lambda i: (0, i, 0))
    consts = (cw, cb, al, dsk, expand, gexp)
    widths = (BC_W, BC_W, SSM_WIDTH, SSM_WIDTH, SSM_WIDTH, SSM_WIDTH, SSM_HEADS)
    return pl.pallas_call(
        functools.partial(_ssd_sample_pre_body, steps),
        grid=(m // tr,),
        in_specs=[row(XBC_WIDTH), row(SSM_WIDTH), row(SSM_HEADS), state]
                 + [_const_spec(a.shape) for a in consts],
        out_specs=[row(w) for w in widths] + [state],
        out_shape=[jax.ShapeDtypeStruct((m, w), F32) for w in widths] + [jax.ShapeDtypeStruct(prev.shape, F32)],
        scratch_shapes=[pltpu.VMEM((XBC_WIDTH // LANES, tr, LANES), F32)] * 2,
        compiler_params=_params(("parallel",)),
        name="ssd_sample_pre",
    )(xbc, z, dt, prev, *consts)


def _alibi_slopes():
    h = np.arange(1, N_ATT_HEADS + 1, dtype=np.float64)
    return np.exp2(-8.0 * h / N_ATT_HEADS).astype(np.float32).reshape(len(ATT_GROUPS), HEADS_PER_GROUP)


def _cache_view(cache):
    depth, bsz, cache_len = cache.shape[:3]
    return jnp.transpose(cache, (0, 1, 3, 4, 5, 2)).reshape(depth, bsz, 2, GROUP_W, cache_len)


def _layer_weights(lw):
    (norm_mix, q_norm, k_norm, conv_w, conv_b, dt_bias, a_log, d_skip, ssm_norm,
     norm_ffn, ffn_conv_w, ffn_conv_b) = lw
    w = {}
    w["norm_mix"] = norm_mix.reshape(1, D_MODEL)
    w["dtb"] = dt_bias.reshape(1, SSM_HEADS)
    w["gq_g"] = jnp.tile(q_norm, HEADS_PER_GROUP).reshape(1, GROUP_W)
    w["gk_g"] = jnp.tile(k_norm, HEADS_PER_GROUP).reshape(1, GROUP_W)
    w["gq_p"] = jnp.tile(q_norm, 2).reshape(1, PAIR_W)
    w["gk_p"] = jnp.tile(k_norm, 2).reshape(1, PAIR_W)
    w["cw"] = conv_w
    w["cb"] = conv_b.reshape(1, XBC_WIDTH)
    w["alr"] = a_log.reshape(1, SSM_HEADS)
    w["alc"] = a_log.reshape(SSM_HEADS, 1)
    w["dsk"] = jnp.repeat(d_skip, SSM_HEAD_DIM).reshape(1, SSM_WIDTH)
    w["gn"] = ssm_norm.reshape(1, SSM_WIDTH)
    w["norm_ffn"] = norm_ffn.reshape(1, D_MODEL)
    w["fw"] = ffn_conv_w
    w["fb"] = ffn_conv_b.reshape(1, 2 * D_FF)
    return w


def _constants(steps, cache_lens):
    c = {}
    slopes = _alibi_slopes()
    c["bd_g"] = _head_block_diag(GROUP_W)
    c["bd_p"] = _head_block_diag(PAIR_W)
    i = np.arange(SSM_CHUNK)
    c["tri"] = jnp.asarray(i[None, :] <= i[:, None], dtype=BF16)
    lane_head = np.arange(SSM_WIDTH) // SSM_HEAD_DIM
    c["expand"] = jnp.asarray(np.arange(SSM_HEADS)[:, None] == lane_head[None, :], dtype=BF16)
    bc_group = np.arange(BC_W) // SSM_STATE
    c["gexp"] = jnp.asarray(bc_group[:, None] == (lane_head // SSM_HPG)[None, :], dtype=BF16)
    c["hm"] = jnp.asarray(np.arange(HEADS_PER_GROUP)[:, None] == (np.arange(GROUP_W) // HEAD_DIM)[None, :], dtype=F32)
    c["pbias"] = [_prompt_bias(slopes[g], dil) for g, (_, dil) in enumerate(ATT_GROUPS)]
    c["cbias"] = [_sample_cache_bias(slopes[g], dil, steps, cache_lens[g], SUBLANES)
                  for g, (_, dil) in enumerate(ATT_GROUPS)]
    c["nbias"] = jnp.asarray(np.stack([_sample_new_bias(slopes[g], dil, steps, SUBLANES)
                                       for g, (_, dil) in enumerate(ATT_GROUPS)]), dtype=F32)
    return c


def _layer(xp, xs, w, big, c, layer, caches, state, new_states, conv_prev, ffn_prev, tm):
    bsz, seq, _ = xp.shape
    m = bsz * seq
    dec, steps, _ = xs.shape
    ms = dec * steps
    assert steps >= SSM_CONV - 1
    xp2 = xp.reshape(m, D_MODEL)
    xs2 = xs.reshape(ms, D_MODEL)
    qkv, z, xbc, dt = _inproj(xp2, w["norm_mix"], big["w_in"], w["dtb"], layer, tm)
    qkv_s, z_s, xbc_s, dt_s = _inproj(xs2, w["norm_mix"], big["w_in"], w["dtb"], layer, ms)

    qkv_t = qkv.reshape(QKV_SLABS, bsz, seq, PAIR_W)
    os_, ls_, kv_p = [], [], []
    for g, (win, dil) in enumerate(ATT_GROUPS):
        nbk = max(1, PROMPT_ATT_ROWS // (BAND * dil))
        o, lse, kn = _attn_prompt(qkv_t, g, dil, nbk, c["pbias"][g], w["gq_p"], w["gk_p"], c["bd_p"])
        os_.append(o)
        ls_.append(lse)
        keep = min(win, seq)
        v_g = qkv_t[2 * PAIRS + 2 * g:2 * PAIRS + 2 * g + 2, :, seq - keep:]
        kv = jnp.stack([kn[:, :, seq - keep:], v_g])
        kv = jnp.transpose(kv, (2, 3, 0, 1, 4))
        kv_p.append(kv.reshape(bsz, keep, 2, HEADS_PER_GROUP, HEAD_DIM))

    xbc3 = xbc.reshape(bsz, seq, XBC_WIDTH)
    dt3 = dt.reshape(bsz, seq, SSM_HEADS)
    ssd_consts = (w["cw"], w["cb"], w["alr"], w["alc"], w["dsk"], w["gn"], c["tri"], c["expand"])
    att_consts = (*c["cbias"], c["nbias"], w["gq_g"], w["gk_g"], c["bd_g"], c["hm"])
    *state_rows, cd, conv_tail = _ssd_sample_pre(xbc_s, z_s, dt_s, jnp.swapaxes(conv_prev, 0, 1), w["cw"],
                                                 w["cb"], w["alr"], w["dsk"], c["expand"], c["gexp"], steps)
    cd_last = cd.reshape(dec, steps, SSM_HEADS)[:, steps - 1]
    y, h_last, o_s, lse_s, kn_s, y_s, new_states = _ssd_prompt_sample(
        xbc3, z.reshape(bsz, seq, SSM_WIDTH), dt3, jnp.swapaxes(dt3, 1, 2), ssd_consts,
        qkv_s, caches, layer, steps, att_consts, state_rows, cd_last, state, new_states)

    x2, tail = _ffn_prompt(os_, ls_, y, xp, big["w_out"], w["norm_ffn"], big["w_up"], w["fw"], w["fb"],
                           big["w_down"], layer, tm)
    h_last = h_last.reshape(bsz, SSM_HEADS, SSM_HEAD_DIM, SSM_STATE)
    conv_p = xbc3[:, seq - (SSM_CONV - 1):]
    ffn_p = tail[:, SUBLANES - (FFN_CONV - 1):]

    kv_s = []
    for g in range(len(ATT_GROUPS)):
        v_g = qkv_s[2 * PAIRS + 2 * g:2 * PAIRS + 2 * g + 2]
        kv = jnp.stack([kn_s[2 * g:2 * g + 2], v_g])
        kv = jnp.transpose(kv, (2, 0, 1, 3))
        kv_s.append(kv.reshape(dec, steps, 2, HEADS_PER_GROUP, HEAD_DIM))
    x1_s = _outproj([o_s], [lse_s], y_s, xs2, big["w_out"], layer, ms)
    x2_s, ffn_tail = _ffn_sample(x1_s, steps, w["norm_ffn"], big["w_up"], w["fw"], w["fb"], big["w_down"], layer,
                                 jnp.swapaxes(ffn_prev, 0, 1))
    conv_s = jnp.swapaxes(conv_tail, 0, 1)
    ffn_s = jnp.swapaxes(ffn_tail, 0, 1)
    return (x2, x2_s.reshape(dec, steps, D_MODEL), (*kv_p, h_last, conv_p, ffn_p), (*kv_s, conv_s, ffn_s),
            new_states)


def kernel(x_prompt, x_sample, cache_kv0, cache_kv1, cache_kv2, state_ssm, state_conv, state_ffn_conv, norm_mix, w_in, q_norm, k_norm, conv_w, conv_b, dt_bias, a_log, d_skip, ssm_norm, w_out, norm_ffn, w_up, ffn_conv_w, ffn_conv_b, w_down):
    stacked = (norm_mix, q_norm, k_norm, conv_w, conv_b, dt_bias, a_log, d_skip, ssm_norm,
               norm_ffn, ffn_conv_w, ffn_conv_b)
    big = {"w_in": w_in.astype(BF16), "w_out": w_out.astype(BF16), "w_up": w_up.astype(BF16),
           "w_down": w_down.astype(BF16)}
    depth = w_in.shape[0]
    dec_batch, steps = x_sample.shape[:2]
    caches = tuple(_cache_view(cache) for cache in (cache_kv0, cache_kv1, cache_kv2))
    for cache, (_, dil) in zip(caches, ATT_GROUPS):
        assert cache.shape[-1] == BAND * dil
    state = state_ssm.reshape(depth, dec_batch, SSM_WIDTH, SSM_STATE)
    c = _constants(steps, tuple(cache.shape[-1] for cache in caches))
    tm = 512
    assert x_prompt.shape[1] % tm == 0
    y_prompt, y_sample = x_prompt, x_sample
    outs_p = [[] for _ in range(6)]
    outs_s = [[] for _ in range(5)]
    new_states = None
    for layer in range(depth):
        w = _layer_weights(tuple(a[layer] for a in stacked))
        y_prompt, y_sample, res_p, res_s, new_states = _layer(
            y_prompt, y_sample, w, big, c, layer, caches, state, new_states, state_conv[layer],
            state_ffn_conv[layer], tm)
        for lst, val in zip(outs_p, res_p):
            lst.append(val)
        for lst, val in zip(outs_s, res_s):
            lst.append(val)
    stack_s = [jnp.stack(l) for l in outs_s]
    return (y_prompt, y_sample, *[jnp.stack(l) for l in outs_p],
            *stack_s[:3], new_states.reshape(state_ssm.shape), *stack_s[3:])
```

```python
import functools
import math

import jax
import jax.numpy as jnp
import numpy as np
from jax import lax
from jax.experimental import pallas as pl
from jax.experimental.pallas import tpu as pltpu

F32 = jnp.float32
BF16 = jnp.bfloat16

D_MODEL = 1024
HEAD_DIM = 64
ATT_GROUPS = ((128, 1), (512, 4), (2048, 16))
BAND = 128
HEADS_PER_GROUP = 4
GROUP_W = HEADS_PER_GROUP * HEAD_DIM
N_ATT_HEADS = HEADS_PER_GROUP * len(ATT_GROUPS)
ATT_WIDTH = N_ATT_HEADS * HEAD_DIM
QKV_W = 3 * ATT_WIDTH
PAIR_W = 2 * HEAD_DIM
PAIRS = ATT_WIDTH // PAIR_W
QKV_SLABS = QKV_W // PAIR_W
SSM_HEAD_DIM = 64
SSM_WIDTH = 1024
SSM_HEADS = SSM_WIDTH // SSM_HEAD_DIM
SSM_STATE = 128
SSM_GROUPS = 2
SSM_HPG = SSM_HEADS // SSM_GROUPS
SSM_GROUP_W = SSM_HPG * SSM_HEAD_DIM
SSM_CONV = 4
SSM_CHUNK = 128
BC_W = SSM_GROUPS * SSM_STATE
XBC_WIDTH = SSM_WIDTH + 2 * BC_W
D_FF = 2816
FFN_CONV = 3
FF_CHUNK = 256
EPS = 1e-6
IN_WIDTH = QKV_W + SSM_WIDTH + XBC_WIDTH + SSM_HEADS
SUBLANES = 8
LANES = 128
SAMPLE_ROWS = 128
PROMPT_ATT_ROWS = 1024
RESIDUE_UNROLL = 8
VMEM_LIMIT = 56 * 1024 * 1024
NEG_INF = float("-inf")
LOG2E = math.log2(math.e)
LN2 = math.log(2.0)


def _dot(a, b):
    return jnp.dot(a, b, preferred_element_type=F32)


def _dot_nt(a, b):
    return lax.dot_general(a, b, (((1,), (1,)), ((), ())), preferred_element_type=F32)


def _dot_tn(a, b):
    return lax.dot_general(a, b, (((0,), (0,)), ((), ())), preferred_element_type=F32)


def _split(x, pieces):
    out = []
    r = x
    for _ in range(pieces):
        p = r.astype(BF16)
        out.append(p)
        r = r - p.astype(F32)
    return out


def _dot_sel(x, sel, pieces=3):
    acc = None
    for p in _split(x, pieces):
        t = _dot(p, sel)
        acc = t if acc is None else acc + t
    return acc


def _sel_dot(sel, x, pieces=3):
    acc = None
    for p in _split(x, pieces):
        t = _dot(sel, p)
        acc = t if acc is None else acc + t
    return acc


def _sel_dot_nt(x, sel, pieces=3):
    acc = None
    for p in _split(x, pieces):
        t = _dot_nt(p, sel)
        acc = t if acc is None else acc + t
    return acc


def _silu(x):
    h = 0.5 * x
    return h * (1.0 + jnp.tanh(h))


def _softplus(x):
    return jnp.maximum(x, 0.0) + jnp.log1p(jnp.exp(-jnp.abs(x)))


def _rms(x, g):
    ms = jnp.mean(x * x, axis=-1, keepdims=True)
    return x * lax.rsqrt(ms + EPS) * g


def _layer_spec(arr, layer):
    nd = arr.ndim
    return pl.BlockSpec((1,) + arr.shape[1:], lambda *_: (layer,) + (0,) * (nd - 1), pipeline_mode=pl.Buffered(1))


def _const_spec(shape):
    nd = len(shape)
    return pl.BlockSpec(shape, lambda *_: (0,) * nd, pipeline_mode=pl.Buffered(1))


def _params(sem):
    return pltpu.CompilerParams(dimension_semantics=sem, vmem_limit_bytes=VMEM_LIMIT)


def _head_block_diag(width):
    h = np.arange(width) // HEAD_DIM
    return jnp.asarray(np.where(h[:, None] == h[None, :], 1.0 / HEAD_DIM, 0.0), dtype=BF16)


def _head_rms(t, g, bd):
    ms = _dot_sel(t * t, bd, pieces=2)
    return t * lax.rsqrt(ms + EPS) * g


def _rows_back(slab_ref, s, d, n):
    return slab_ref[s, pl.ds(SUBLANES - d, n, stride=1), :]


def _inproj_body(x_ref, g_ref, w_ref, dtb_ref, qkv_ref, z_ref, xbc_ref, dt_ref):
    h = _rms(x_ref[...], g_ref[...]).astype(BF16)
    qkv = _dot(h, w_ref[0, :, 0:QKV_W])
    for s in range(QKV_SLABS):
        qkv_ref[s] = qkv[:, s * PAIR_W:(s + 1) * PAIR_W]
    z_ref[...] = _dot(h, w_ref[0, :, QKV_W:QKV_W + SSM_WIDTH])
    xd = _dot(h, w_ref[0, :, QKV_W + SSM_WIDTH:IN_WIDTH])
    xbc_ref[...] = xd[:, :XBC_WIDTH]
    dt_ref[...] = _softplus(xd[:, XBC_WIDTH:] + dtb_ref[...])


def _inproj(x, g, w_in, dtb, layer, tm):
    m = x.shape[0]
    row = lambda w: pl.BlockSpec((tm, w), lambda i: (i, 0))
    widths = (SSM_WIDTH, XBC_WIDTH, SSM_HEADS)
    return pl.pallas_call(
        _inproj_body,
        grid=(m // tm,),
        in_specs=[row(D_MODEL), _const_spec(g.shape), _layer_spec(w_in, layer), _const_spec(dtb.shape)],
        out_specs=[pl.BlockSpec((QKV_SLABS, tm, PAIR_W), lambda i: (0, i, 0))] + [row(w) for w in widths],
        out_shape=[jax.ShapeDtypeStruct((QKV_SLABS, m, PAIR_W), F32)]
                  + [jax.ShapeDtypeStruct((m, w), F32) for w in widths],
        compiler_params=_params(("parallel",)),
        name="inproj",
    )(x, g, w_in, dtb)


def _attn_prompt_body(dil, nbk, q0_ref, q1_ref, k0_ref, k1_ref, v0_ref, v1_ref, bias_ref, gq_ref, gk_ref, bd_ref,
                      o_ref, lse_ref, kn_ref, kv_scr):
    n = pl.program_id(1)
    rows_per = nbk * BAND
    in_refs = ((q0_ref, q1_ref), (k0_ref, k1_ref), (v0_ref, v1_ref))

    @pl.when(n == 0)
    def _():
        kv_scr[...] = jnp.zeros(kv_scr.shape, BF16)

    variant = jnp.minimum(n, 1)
    bd = bd_ref[...]
    gq = gq_ref[...]
    gk = gk_ref[...]
    low = lax.broadcasted_iota(jnp.int32, (BAND, PAIR_W), 1) < HEAD_DIM

    def rows_of(r, start, count):
        return pl.ds(r + start * dil, count, stride=dil) if dil > 1 else pl.ds(start, count)

    def group(residues):
        items = []
        for r in residues:
            for p in range(2):
                q, k, v = (in_refs[kind][p][0, 0, rows_of(r, 0, rows_per), :] for kind in range(3))
                qn = _head_rms(q, gq, bd) * (HEAD_DIM ** -0.5 * LOG2E)
                kn = _head_rms(k, gk, bd)
                kall = jnp.concatenate([kv_scr[0, p, r], kn.astype(BF16)], axis=0)
                vall = jnp.concatenate([kv_scr[1, p, r], v.astype(BF16)], axis=0)
                items.append((r, p, qn, kn, kall, vall))
        scores = []
        for r, p, qn, kn, kall, vall in items:
            for j in range(nbk):
                qj = qn[j * BAND:(j + 1) * BAND]
                for hh in range(2):
                    keep = low if hh == 0 else jnp.logical_not(low)
                    qm = jnp.where(keep, qj, 0.0).astype(BF16)
                    bias = bias_ref[variant if j == 0 else 1, 2 * p + hh]
                    scores.append(_dot_nt(qm, kall[j * BAND:(j + 2) * BAND]) + bias)
        probs = []
        for s in scores:
            mx = jnp.max(s, axis=-1, keepdims=True)
            e = jnp.exp2(s - mx)
            den = jnp.sum(e, axis=-1, keepdims=True)
            probs.append((e.astype(BF16), den, (mx + jnp.log2(den)) * LN2))
        results = []
        it = iter(probs)
        for r, p, qn, kn, kall, vall in items:
            for j in range(nbk):
                (e0, d0, l0), (e1, d1, l1) = next(it), next(it)
                vwin = vall[j * BAND:(j + 2) * BAND]
                o = jnp.where(low, _dot(e0, vwin) / d0, _dot(e1, vwin) / d1)
                results.append((r, p, j, o, jnp.where(low, l0, l1)))
        for r, p, j, o, lse in results:
            o_ref[p, 0, rows_of(r, j * BAND, BAND), :] = o
            lse_ref[p, 0, rows_of(r, j * BAND, BAND), :] = lse
        for r, p, qn, kn, kall, vall in items:
            kn_ref[p, 0, rows_of(r, 0, rows_per), :] = kn
            kv_scr[0, p, r] = kall[rows_per:]
            kv_scr[1, p, r] = vall[rows_per:]

    if dil <= RESIDUE_UNROLL:
        group(range(dil))
    else:
        def step(i, carry):
            group([i * RESIDUE_UNROLL + k for k in range(RESIDUE_UNROLL)])
            return carry
        lax.fori_loop(0, dil // RESIDUE_UNROLL, step, 0)


def _prompt_bias(slopes, dil):
    i = np.arange(BAND)[:, None]
    j = np.arange(2 * BAND)[None, :]
    step = i + BAND - j
    valid = (step >= 0) & (step <= BAND)
    bias = -slopes[:, None, None] * (step * dil).astype(np.float32)
    full = np.where(valid[None], bias, NEG_INF)
    first = np.where((valid & (j >= BAND))[None], bias, NEG_INF)
    return jnp.asarray(np.stack([first, full]) * LOG2E, dtype=F32)


def _attn_prompt(qkv_t, g, dil, nbk, bias, gq, gk, bd):
    _, bsz, seq, _ = qkv_t.shape
    blk = nbk * BAND * dil
    assert seq % blk == 0
    slab = lambda s: pl.BlockSpec((1, 1, blk, PAIR_W), lambda b, n: (s, b, n, 0))
    ospec = pl.BlockSpec((2, 1, blk, PAIR_W), lambda b, n: (0, b, n, 0))
    first = [kind * PAIRS + 2 * g for kind in range(3)]
    return pl.pallas_call(
        functools.partial(_attn_prompt_body, dil, nbk),
        grid=(bsz, seq // blk),
        in_specs=[slab(first[0]), slab(first[0] + 1), slab(first[1]), slab(first[1] + 1),
                  slab(first[2]), slab(first[2] + 1),
                  _const_spec(bias.shape), _const_spec(gq.shape), _const_spec(gk.shape), _const_spec(bd.shape)],
        out_specs=[ospec, ospec, ospec],
        out_shape=[jax.ShapeDtypeStruct((2, bsz, seq, PAIR_W), F32)] * 3,
        scratch_shapes=[pltpu.VMEM((2, 2, dil, BAND, PAIR_W), BF16)],
        compiler_params=_params(("parallel", "arbitrary")),
        name=f"attn_prompt_g{g}",
    )(*([qkv_t] * 6), bias, gq, gk, bd)


def _ssd_stage_conv(xbc_ref, conv_scr):
    raw = xbc_ref[0]
    for s in range(XBC_WIDTH // LANES):
        conv_scr[s, SUBLANES:, :] = raw[:, s * LANES:(s + 1) * LANES]
    return raw


def _ssd_chunk(raw, z_ref, dt_ref, dtt_ref, cw_ref, cb_ref, alr_ref, alc_ref, dsk_ref, gn_ref,
               tri_ref, exp_ref, conv_scr, state_scr):
    q = SSM_CHUNK

    cw = cw_ref[...]
    z = z_ref[0]
    state = state_scr[...]
    n_slabs = XBC_WIDTH // LANES
    parts = []
    for s in range(n_slabs):
        lanes = slice(s * LANES, (s + 1) * LANES)
        acc = _rows_back(conv_scr, s, 3, q) * cw[0:1, lanes]
        acc = acc + _rows_back(conv_scr, s, 2, q) * cw[1:2, lanes]
        acc = acc + _rows_back(conv_scr, s, 1, q) * cw[2:3, lanes]
        parts.append(_silu(acc + raw[:, lanes] * cw[3:4, lanes] + cb_ref[:, lanes]))
        if s % 3 == 2:
            yield
    xc = jnp.concatenate(parts, axis=1)
    xs = xc[:, :SSM_WIDTH]
    bm = xc[:, SSM_WIDTH:SSM_WIDTH + BC_W]
    cm = xc[:, SSM_WIDTH + BC_W:]

    tri = tri_ref[...]
    dt = dt_ref[0]
    acum = _sel_dot(tri, dt * (-jnp.exp(alr_ref[...])))
    acum_t = _sel_dot_nt(dtt_ref[0] * (-jnp.exp(alc_ref[...])), tri)
    yield
    last = acum[q - 1:q, :]
    expand = exp_ref[...]
    dt_x = _dot_sel(dt, expand)
    ea_x = _dot_sel(jnp.exp(acum), expand)
    de_x = _dot_sel(jnp.exp(last - acum), expand)
    xdt = xs * dt_x
    xde = (xdt * de_x).astype(BF16)
    xdt_b = xdt.astype(BF16)
    causal = lax.broadcasted_iota(jnp.int32, (q, q), 0) >= lax.broadcasted_iota(jnp.int32, (q, q), 1)
    low = lax.broadcasted_iota(jnp.int32, (q, 2 * SSM_HEAD_DIM), 1) < SSM_HEAD_DIM

    y_parts = []
    new_state = []
    for g in range(SSM_GROUPS):
        bm_g = bm[:, g * SSM_STATE:(g + 1) * SSM_STATE].astype(BF16)
        cm_g = cm[:, g * SSM_STATE:(g + 1) * SSM_STATE].astype(BF16)
        cb = _dot_nt(cm_g, bm_g)
        rows = slice(g * SSM_GROUP_W, (g + 1) * SSM_GROUP_W)
        st_g = state[rows, :]
        inter = _dot_nt(cm_g, st_g.astype(BF16))
        for hp in range(SSM_HPG // 2):
            pair = None
            for k in range(2):
                h = g * SSM_HPG + 2 * hp + k
                seg = acum[:, h:h + 1] - acum_t[h:h + 1, :]
                decay = jnp.exp(jnp.where(causal, seg, NEG_INF))
                gmat = (cb * decay).astype(BF16)
                lanes = slice((2 * hp) * SSM_HEAD_DIM + g * SSM_GROUP_W,
                              (2 * hp + 2) * SSM_HEAD_DIM + g * SSM_GROUP_W)
                x_pair = xdt_b[:, lanes]
                keep = low if k == 0 else jnp.logical_not(low)
                t = _dot(gmat, jnp.where(keep, x_pair, jnp.zeros_like(x_pair)))
                pair = t if pair is None else pair + t
            y_parts.append(pair + ea_x[:, lanes] * inter[:, lanes.start - g * SSM_GROUP_W:lanes.stop - g * SSM_GROUP_W])
            yield
        new = _dot_tn(xde[:, rows], bm_g)
        for hh in range(SSM_HPG):
            h = g * SSM_HPG + hh
            cd = jnp.exp(acum_t[h:h + 1, q - 1:q])
            r = slice(h * SSM_HEAD_DIM, (h + 1) * SSM_HEAD_DIM)
            rl = slice(hh * SSM_HEAD_DIM, (hh + 1) * SSM_HEAD_DIM)
            new_state.append((r, st_g[rl, :] * cd + new[rl, :]))

    yield
    y = jnp.concatenate(y_parts, axis=1) + dsk_ref[...] * xs
    y = y * _silu(z)
    return _rms(y, gn_ref[...]), new_state


def _ssd_store(raw, y, new_state, y_ref, conv_scr, state_scr):
    y_ref[0] = y
    for r, val in new_state:
        state_scr[r, :] = val
    for s in range(XBC_WIDTH // LANES):
        conv_scr[s, 0:SUBLANES, :] = raw[SSM_CHUNK - SUBLANES:, s * LANES:(s + 1) * LANES]


def _interleave(*gens):
    results = [None] * len(gens)
    live = list(range(len(gens)))
    while live:
        for i in list(live):
            try:
                next(gens[i])
            except StopIteration as done:
                results[i] = done.value
                live.remove(i)
    return results


def _sample_state_batch(steps, mine, batch, bm_ref, cm_ref, xde_ref, pre_ref, ea_ref, zg_ref, cd_ref, gn_ref,
                        st_ref):
    bm = bm_ref[...].astype(BF16)
    cm = cm_ref[...].astype(BF16)
    xde = jnp.where(mine, xde_ref[...], 0.0).astype(BF16)
    inter, new = [], []
    for g in range(SSM_GROUPS):
        srows = slice(g * SSM_GROUP_W, (g + 1) * SSM_GROUP_W)
        lanes = slice(g * SSM_STATE, (g + 1) * SSM_STATE)
        inter.append(_dot_nt(cm[:, lanes], st_ref[0, 0, srows, :].astype(BF16)))
        new.append(_dot_tn(xde[:, srows], bm[:, lanes]))
        yield
    y = pre_ref[...] + ea_ref[...] * jnp.concatenate(inter, axis=1)
    y = _rms(y * zg_ref[...], gn_ref[...])
    new_state = []
    for h in range(SSM_HEADS):
        g, hh = divmod(h, SSM_HPG)
        r = slice(h * SSM_HEAD_DIM, (h + 1) * SSM_HEAD_DIM)
        new_state.append((r, st_ref[0, 0, r, :] * cd_ref[batch, h]
                          + new[g][hh * SSM_HEAD_DIM:(hh + 1) * SSM_HEAD_DIM, :]))
    return y, new_state


def _ssd_attn_body(steps, dils, layer, fills, n_ssd, *refs):
    ssd_in, rest = refs[:n_ssd], refs[n_ssd:]
    (qkv_ref, kv0_ref, kv1_ref, kv2_ref, cb0_ref, cb1_ref, cb2_ref, nb_ref, gq_ref, gk_ref, bd_ref, hm_ref,
     bm_ref, cm_ref, xde_ref, pre_ref, ea_ref, zg_ref, cd_ref, st_ref) = rest[:20]
    y_ref, hl_ref, o_ref, lse_ref, kn_ref, ys_ref, ns_ref, conv_scr, state_scr = rest[-9:]
    c = pl.program_id(1)
    n_b = SUBLANES // steps
    batch = pl.program_id(0) * pl.num_programs(1) + c
    b = batch % n_b

    @pl.when(c == 0)
    def _():
        conv_scr[:, 0:SUBLANES, :] = jnp.zeros((conv_scr.shape[0], SUBLANES, LANES), F32)
        state_scr[...] = jnp.zeros_like(state_scr)

    raw = _ssd_stage_conv(ssd_in[0], conv_scr)
    mine_w = lax.broadcasted_iota(jnp.int32, (SUBLANES, SSM_WIDTH), 0) // steps == b
    sample, (y, new_state), (ys, new_sample_state) = _interleave(
        _sample_attn_batch(steps, dils, b, qkv_ref, (kv0_ref, kv1_ref, kv2_ref), (cb0_ref, cb1_ref, cb2_ref),
                           nb_ref, gq_ref, gk_ref, bd_ref, hm_ref),
        _ssd_chunk(raw, *ssd_in[1:], conv_scr, state_scr),
        _sample_state_batch(steps, mine_w, batch, bm_ref, cm_ref, xde_ref, pre_ref, ea_ref, zg_ref, cd_ref,
                            ssd_in[9], st_ref))
    _ssd_store(raw, y, new_state, y_ref, conv_scr, state_scr)
    own = layer if fills else 0
    for l in range(ns_ref.shape[0]):
        if l != own:
            ns_ref[l] = jnp.zeros(ns_ref.shape[1:], F32)
    for r, val in new_sample_state:
        ns_ref[own, 0, r, :] = val

    mine = lax.broadcasted_iota(jnp.int32, (SUBLANES, PAIR_W), 0) // steps == b

    @pl.when(b == 0)
    def _():
        ys_ref[...] = jnp.where(mine_w, ys, 0.0)
        for g, (o_g, l_g, kn) in enumerate(sample):
            for p in range(2):
                lanes = slice(p * PAIR_W, (p + 1) * PAIR_W)
                o_ref[2 * g + p] = jnp.where(mine, o_g[:, lanes], 0.0)
                lse_ref[2 * g + p] = jnp.where(mine, l_g[:, lanes], 0.0)
                kn_ref[2 * g + p] = kn[:, lanes]

    @pl.when(b > 0)
    def _():
        ys_ref[...] = jnp.where(mine_w, ys, ys_ref[...])
        for g, (o_g, l_g, kn) in enumerate(sample):
            for p in range(2):
                lanes = slice(p * PAIR_W, (p + 1) * PAIR_W)
                o_ref[2 * g + p] = jnp.where(mine, o_g[:, lanes], o_ref[2 * g + p])
                lse_ref[2 * g + p] = jnp.where(mine, l_g[:, lanes], lse_ref[2 * g + p])

    @pl.when(c == pl.num_programs(1) - 1)
    def _():
        hl_ref[0] = state_scr[...]


def _ssd_prompt_sample(xbc, z, dt, dtt, ssd_consts, qkv_t, caches, layer, steps, att_consts,
                       state_rows, cd, state, new_states):
    bsz, seq, _ = xbc.shape
    assert seq % SSM_CHUNK == 0
    nc = seq // SSM_CHUNK
    m = qkv_t.shape[1]
    n_b = SUBLANES // steps
    assert SUBLANES % steps == 0 and m == bsz * nc * steps
    dils = tuple(d for _, d in ATT_GROUPS)
    blk = lambda w: pl.BlockSpec((1, SSM_CHUNK, w), lambda b, c: (b, c, 0))
    tile = lambda b, c: (b * nc + c) // n_b
    rows = lambda w: pl.BlockSpec((SUBLANES, w), lambda b, c: (tile(b, c), 0))
    per_batch = lambda arr, lead, first: pl.BlockSpec((lead, 1) + arr.shape[2:],
                                                      lambda b, c: (first, b * nc + c) + (0,) * (arr.ndim - 2))
    ospec = pl.BlockSpec((PAIRS, SUBLANES, PAIR_W), lambda b, c: (0, tile(b, c), 0))
    ssd_in = [xbc, z, dt, dtt, *ssd_consts]
    fills = new_states is None
    ins = [*ssd_in, qkv_t, *caches, *att_consts, *state_rows, cd, state]
    in_specs = ([blk(XBC_WIDTH), blk(SSM_WIDTH), blk(SSM_HEADS),
                 pl.BlockSpec((1, SSM_HEADS, SSM_CHUNK), lambda b, c: (b, 0, c))]
                + [_const_spec(a.shape) for a in ssd_consts]
                + [pl.BlockSpec((QKV_SLABS, SUBLANES, PAIR_W), lambda b, c: (0, tile(b, c), 0))]
                + [per_batch(cache, 1, layer) for cache in caches]
                + [_const_spec(a.shape) for a in att_consts]
                + [rows(a.shape[1]) for a in state_rows]
                + [pl.BlockSpec(memory_space=pltpu.SMEM), per_batch(state, 1, layer)])
    aliases = {}
    if not fills:
        aliases = {len(ins): 6}
        ins.append(new_states)
        in_specs.append(pl.BlockSpec(memory_space=pl.ANY))
    ns_spec = per_batch(state, state.shape[0], 0) if fills else per_batch(state, 1, layer)
    return pl.pallas_call(
        functools.partial(_ssd_attn_body, steps, dils, layer, fills, len(ssd_in)),
        grid=(bsz, nc),
        in_specs=in_specs,
        out_specs=[blk(SSM_WIDTH), pl.BlockSpec((1, SSM_WIDTH, SSM_STATE), lambda b, c: (b, 0, 0)),
                   ospec, ospec, ospec, rows(SSM_WIDTH), ns_spec],
        out_shape=[jax.ShapeDtypeStruct((bsz, seq, SSM_WIDTH), F32),
                   jax.ShapeDtypeStruct((bsz, SSM_WIDTH, SSM_STATE), F32)]
                  + [jax.ShapeDtypeStruct((PAIRS, m, PAIR_W), F32)] * 3
                  + [jax.ShapeDtypeStruct((m, SSM_WIDTH), F32), jax.ShapeDtypeStruct(state.shape, F32)],
        scratch_shapes=[pltpu.VMEM((XBC_WIDTH // LANES, SUBLANES + SSM_CHUNK, LANES), F32),
                        pltpu.VMEM((SSM_WIDTH, SSM_STATE), F32)],
        input_output_aliases=aliases,
        compiler_params=_params(("arbitrary", "arbitrary")),
        name="ssd_prompt_sample",
    )(*ins)


def _mix_project(o_sl, l_sl, y, x, w_ref):
    ngrp = len(ATT_GROUPS)
    att = [None] * PAIRS
    for p in range(PAIRS // ngrp):
        ls = [l_sl[2 * g + p] for g in range(ngrp)]
        mx = functools.reduce(jnp.maximum, ls)
        es = [jnp.exp(l - mx) for l in ls]
        inv = 1.0 / functools.reduce(jnp.add, es)
        for g in range(ngrp):
            att[2 * g + p] = (o_sl[2 * g + p] * (es[g] * inv)).astype(BF16)
    acc = x + _dot(y.astype(BF16), w_ref[0, ATT_WIDTH:, :])
    return acc + _dot(jnp.concatenate(att, axis=1), w_ref[0, 0:ATT_WIDTH, :])


def _outproj_body(slabs, *refs):
    n_arr = len(slabs)
    o_refs, l_refs = refs[:n_arr], refs[n_arr:2 * n_arr]
    y_ref, x_ref, w_ref, out_ref = refs[2 * n_arr:]
    o_sl = [r[s] for r, cnt in zip(o_refs, slabs) for s in range(cnt)]
    l_sl = [r[s] for r, cnt in zip(l_refs, slabs) for s in range(cnt)]
    out_ref[...] = _mix_project(o_sl, l_sl, y_ref[...], x_ref[...], w_ref)


def _outproj(os_, ls_, y, x, w_out, layer, tm):
    m = x.shape[0]
    slabs = tuple(a.shape[0] for a in os_)
    assert sum(slabs) == PAIRS
    row = lambda w: pl.BlockSpec((tm, w), lambda i: (i, 0))
    slab_specs = [pl.BlockSpec((n, tm, PAIR_W), lambda i: (0, i, 0)) for n in slabs]
    return pl.pallas_call(
        functools.partial(_outproj_body, slabs),
        grid=(m // tm,),
        in_specs=slab_specs * 2 + [row(SSM_WIDTH), row(D_MODEL), _layer_spec(w_out, layer)],
        out_specs=row(D_MODEL),
        out_shape=jax.ShapeDtypeStruct((m, D_MODEL), F32),
        compiler_params=_params(("parallel",)),
        name="outproj",
    )(*os_, *ls_, y, x, w_out)


def _ffn_chunk(h, c, wu_ref, fw_ref, fb_ref, shifted):
    halves = []
    for base in (0, D_FF):
        cols = slice(base + c * FF_CHUNK, base + (c + 1) * FF_CHUNK)
        u = _dot(h, wu_ref[0, :, cols])
        s2, s1 = shifted(u, cols)
        w = fw_ref[:, cols]
        halves.append(s2 * w[0:1] + s1 * w[1:2] + u * w[2:3] + fb_ref[:, cols])
    return (_silu(halves[0]) * halves[1]).astype(BF16)


def _ffn_prompt_body(slabs, *refs):
    n_arr = len(slabs)
    o_refs, l_refs = refs[:n_arr], refs[n_arr:2 * n_arr]
    (y_ref, x_ref, wo_ref, g_ref, wu_ref, fw_ref, fb_ref, wd_ref,
     out_ref, tail_ref, u_scr, carry_scr, act_scr) = refs[2 * n_arr:]
    i = pl.program_id(1)

    @pl.when(i == 0)
    def _():
        carry_scr[...] = jnp.zeros_like(carry_scr)

    o_sl = [r[s, 0] for r, cnt in zip(o_refs, slabs) for s in range(cnt)]
    l_sl = [r[s, 0] for r, cnt in zip(l_refs, slabs) for s in range(cnt)]
    x = _mix_project(o_sl, l_sl, y_ref[0], x_ref[0], wo_ref)
    tm = x.shape[0]
    h = _rms(x, g_ref[...]).astype(BF16)
    new_carry = []

    def shifted(u, cols):
        half, start = divmod(cols.start, D_FF)
        par = (start // FF_CHUNK) % 2
        s2, s1 = [], []
        for k in range(FF_CHUNK // LANES):
            sl = half * (FF_CHUNK // LANES) + k
            lanes = slice(cols.start + k * LANES, cols.start + (k + 1) * LANES)
            u_scr[par, sl, 0:SUBLANES, :] = carry_scr[:, lanes]
            u_scr[par, sl, SUBLANES:, :] = u[:, k * LANES:(k + 1) * LANES]
            s2.append(u_scr[par, sl, pl.ds(SUBLANES - 2, tm, stride=1), :])
            s1.append(u_scr[par, sl, pl.ds(SUBLANES - 1, tm, stride=1), :])
            new_carry.append((lanes, u[tm - SUBLANES:, k * LANES:(k + 1) * LANES]))
        return jnp.concatenate(s2, axis=1), jnp.concatenate(s1, axis=1)

    for c in range(D_FF // FF_CHUNK):
        act_scr[:, c * FF_CHUNK:(c + 1) * FF_CHUNK] = _ffn_chunk(h, c, wu_ref, fw_ref, fb_ref, shifted)
    out_ref[0] = x + _dot(act_scr[...], wd_ref[0])
    for lanes, rows in new_carry:
        carry_scr[:, lanes] = rows

    @pl.when(i == pl.num_programs(1) - 1)
    def _():
        tail_ref[0] = carry_scr[...]


def _ffn_prompt(os_, ls_, y, x, w_out, g, wu, fw, fb, wd, layer, tm):
    bsz, seq, _ = x.shape
    slabs = tuple(a.shape[0] for a in os_)
    assert sum(slabs) == PAIRS
    row = lambda w: pl.BlockSpec((1, tm, w), lambda b, i: (b, i, 0))
    slab_specs = [pl.BlockSpec((n, 1, tm, PAIR_W), lambda b, i: (0, b, i, 0)) for n in slabs]
    chunk_slabs = 2 * FF_CHUNK // LANES
    return pl.pallas_call(
        functools.partial(_ffn_prompt_body, slabs),
        grid=(bsz, seq // tm),
        in_specs=slab_specs * 2 + [row(SSM_WIDTH), row(D_MODEL), _layer_spec(w_out, layer), _const_spec(g.shape),
                                   _layer_spec(wu, layer), _const_spec(fw.shape), _const_spec(fb.shape),
                                   _layer_spec(wd, layer)],
        out_specs=[row(D_MODEL), pl.BlockSpec((1, SUBLANES, 2 * D_FF), lambda b, i: (b, 0, 0))],
        out_shape=[jax.ShapeDtypeStruct((bsz, seq, D_MODEL), F32),
                   jax.ShapeDtypeStruct((bsz, SUBLANES, 2 * D_FF), F32)],
        scratch_shapes=[pltpu.VMEM((2, chunk_slabs, SUBLANES + tm, LANES), F32),
                        pltpu.VMEM((SUBLANES, 2 * D_FF), F32), pltpu.VMEM((tm, D_FF), BF16)],
        compiler_params=_params(("parallel", "arbitrary")),
        name="outproj_ffn_prompt",
    )(*os_, *ls_, y, x, w_out, g, wu, fw, fb, wd)


def _ffn_sample_body(steps, x_ref, g_ref, wu_ref, fw_ref, fb_ref, wd_ref, prev_ref, out_ref, tail_ref,
                     e_scr, u_scr):
    x = x_ref[...]
    rows = x.shape[0]
    n_b = rows // steps
    km1 = prev_ref.shape[0]
    h = _rms(x, g_ref[...]).astype(BF16)
    t = lax.broadcasted_iota(jnp.int32, (rows, FF_CHUNK), 0) % steps

    def shifted(u, cols):
        prev = []
        for k in range(FF_CHUNK // LANES):
            lanes = slice(cols.start + k * LANES, cols.start + (k + 1) * LANES)
            e_scr[k] = jnp.zeros((rows, LANES), F32)
            u_scr[k] = u[:, k * LANES:(k + 1) * LANES]
            for j in range(km1):
                e_scr[k, pl.ds(j, n_b, stride=steps), :] = prev_ref[j, :, lanes]
                tail_ref[j, :, lanes] = u_scr[k, pl.ds(steps - km1 + j, n_b, stride=steps), :]
            prev.append(e_scr[k])
        prev = jnp.concatenate(prev, axis=1)
        s1 = jnp.where(t >= 1, pltpu.roll(u, 1, axis=0), pltpu.roll(prev, rows - 1, axis=0))
        s2 = jnp.where(t >= 2, pltpu.roll(u, 2, axis=0), prev)
        return s2, s1

    act = [_ffn_chunk(h, c, wu_ref, fw_ref, fb_ref, shifted) for c in range(D_FF // FF_CHUNK)]
    out_ref[...] = x + _dot(jnp.concatenate(act, axis=1), wd_ref[0])


def _ffn_sample(x, steps, g, wu, fw, fb, wd, layer, prev):
    m = x.shape[0]
    tr = min(SAMPLE_ROWS, m)
    km1 = prev.shape[0]
    assert m % tr == 0 and tr % steps == 0 and km1 <= steps
    n_b = tr // steps
    row = lambda w: pl.BlockSpec((tr, w), lambda i: (i, 0))
    state = pl.BlockSpec((km1, n_b, 2 * D_FF), lambda i: (0, i, 0))
    return pl.pallas_call(
        functools.partial(_ffn_sample_body, steps),
        grid=(m // tr,),
        in_specs=[row(D_MODEL), _const_spec(g.shape), _layer_spec(wu, layer), _const_spec(fw.shape),
                  _const_spec(fb.shape), _layer_spec(wd, layer), state],
        out_specs=[row(D_MODEL), state],
        out_shape=[jax.ShapeDtypeStruct((m, D_MODEL), F32), jax.ShapeDtypeStruct(prev.shape, F32)],
        scratch_shapes=[pltpu.VMEM((FF_CHUNK // LANES, tr, LANES), F32)] * 2,
        compiler_params=_params(("parallel",)),
        name="ffn_sample",
    )(x, g, wu, fw, fb, wd, prev)


def _sample_attn_batch(steps, dils, b, qkv_ref, kv_refs, cb_refs, nb_ref, gq_ref, gk_ref, bd_ref, hm_ref):
    rows = qkv_ref.shape[1]
    assert rows == 2 * steps
    bd = bd_ref[...]
    hm = hm_ref[...]
    groups = range(len(dils))
    st = [dict() for _ in groups]
    for g in groups:
        slab = lambda kind: jnp.concatenate([qkv_ref[kind * PAIRS + 2 * g], qkv_ref[kind * PAIRS + 2 * g + 1]], axis=1)
        qn = _head_rms(slab(0), gq_ref[...], bd) * (HEAD_DIM ** -0.5)
        kn = _head_rms(slab(1), gk_ref[...], bd)
        vn = slab(2)
        st[g]["kn"] = kn
        st[g]["kn_b"] = jnp.where(b == 0, kn[0:steps], kn[steps:])
        st[g]["vn_b"] = jnp.where(b == 0, vn[0:steps], vn[steps:])
        st[g]["qbd"] = jnp.concatenate([qn * hm[h:h + 1] for h in range(HEADS_PER_GROUP)], axis=0)
        yield
    for g in groups:
        kc = kv_refs[g][0, 0, 0].astype(BF16)
        st[g]["s_c"] = _dot(st[g]["qbd"].astype(BF16), kc) + cb_refs[g][...]
        yield
    for g in groups:
        d = st[g]
        mx = jnp.max(d["s_c"], axis=-1, keepdims=True)
        d["s_n"] = []
        for t2 in range(steps):
            sn = jnp.sum(d["qbd"] * d["kn_b"][t2:t2 + 1], axis=-1, keepdims=True) + nb_ref[g, t2]
            d["s_n"].append(sn)
            mx = jnp.maximum(mx, sn)
        d["mx"] = mx
        d["e_c"] = jnp.exp(d["s_c"] - mx)
        d["den"] = jnp.sum(d["e_c"], axis=-1, keepdims=True)
        yield
    for g in groups:
        vc = kv_refs[g][0, 0, 1].astype(BF16)
        st[g]["acc"] = _dot_nt(st[g]["e_c"].astype(BF16), vc)
        yield
    results = []
    for g in groups:
        d = st[g]
        acc, den = d["acc"], d["den"]
        for t2, sn in enumerate(d["s_n"]):
            e_n = jnp.exp(sn - d["mx"])
            den = den + e_n
            acc = acc + e_n * d["vn_b"][t2:t2 + 1]
        o_all = acc / den
        l_all = d["mx"] + jnp.log(den)
        o_g = None
        l_g = None
        for h in range(HEADS_PER_GROUP):
            o_h = o_all[h * rows:(h + 1) * rows] * hm[h:h + 1]
            l_h = l_all[h * rows:(h + 1) * rows] * hm[h:h + 1]
            o_g = o_h if o_g is None else o_g + o_h
            l_g = l_h if l_g is None else l_g + l_h
        results.append((o_g, l_g, d["kn"]))
        yield
    return results


def _sample_cache_bias(slopes, dil, steps, cache_len, rows):
    t = (np.arange(rows) % steps)[None, :, None]
    c = np.arange(cache_len)[None, None, :]
    dist = cache_len + t - c
    j = dist // dil
    valid = (dist % dil == 0) & (j >= 1) & (j <= BAND)
    bias = -slopes[:, None, None] * dist.astype(np.float32)
    return jnp.asarray(np.where(valid, bias, NEG_INF).reshape(slopes.shape[0] * rows, cache_len), dtype=F32)


def _sample_new_bias(slopes, dil, steps, rows):
    t2 = np.arange(steps)[:, None, None]
    t = (np.arange(rows) % steps)[None, None, :]
    dist = t - t2
    valid = (dist >= 0) & (dist % dil == 0) & (dist // dil <= BAND)
    bias = -slopes[None, :, None] * dist.astype(np.float32)
    return np.where(valid, bias, NEG_INF).reshape(steps, slopes.shape[0] * rows, 1)


def _ssd_sample_pre_body(steps, xbc_ref, z_ref, dt_ref, prev_ref, cw_ref, cb_ref, al_ref, dsk_ref,
                         exp_ref, gexp_ref, bm_ref, cm_ref, xde_ref, pre_ref, ea_ref, zg_ref, cd_ref, tail_ref,
                         e_scr, u_scr):
    raw = xbc_ref[...]
    rows = raw.shape[0]
    n_b = rows // steps
    km1 = prev_ref.shape[0]
    tw = lax.broadcasted_iota(jnp.int32, (rows, XBC_WIDTH), 0) % steps
    cw = cw_ref[...]
    prev = []
    for s in range(XBC_WIDTH // LANES):
        lanes = slice(s * LANES, (s + 1) * LANES)
        e_scr[s] = jnp.zeros((rows, LANES), F32)
        u_scr[s] = raw[:, lanes]
        for j in range(km1):
            e_scr[s, pl.ds(j, n_b, stride=steps), :] = prev_ref[j, :, lanes]
            tail_ref[j, :, lanes] = u_scr[s, pl.ds(steps - km1 + j, n_b, stride=steps), :]
        prev.append(e_scr[s])
    prev = jnp.concatenate(prev, axis=1)
    acc = jnp.where(tw >= 3, pltpu.roll(raw, 3, axis=0), prev) * cw[0:1]
    acc = acc + jnp.where(tw >= 2, pltpu.roll(raw, 2, axis=0), pltpu.roll(prev, rows - 1, axis=0)) * cw[1:2]
    acc = acc + jnp.where(tw >= 1, pltpu.roll(raw, 1, axis=0), pltpu.roll(prev, rows - 2, axis=0)) * cw[2:3]
    acc = acc + raw * cw[3:4]
    xc = _silu(acc + cb_ref[...])
    xs = xc[:, :SSM_WIDTH]
    bm = xc[:, SSM_WIDTH:SSM_WIDTH + BC_W]
    cm = xc[:, SSM_WIDTH + BC_W:]
    bm_ref[...] = bm
    cm_ref[...] = cm

    dt = dt_ref[...]
    da = dt * (-jnp.exp(al_ref[...]))
    th = lax.broadcasted_iota(jnp.int32, (rows, SSM_HEADS), 0) % steps
    acum = da
    for d in range(1, steps):
        acum = acum + jnp.where(th >= d, pltpu.roll(da, d, axis=0), 0.0)
    tail = jnp.zeros_like(da)
    for d in range(1, steps):
        tail = tail + jnp.where(th + d < steps, pltpu.roll(da, rows - d, axis=0), 0.0)
    expand = exp_ref[...]
    dt_x = _dot_sel(dt, expand)
    acum_x = _dot_sel(acum, expand)
    ea_ref[...] = jnp.exp(acum_x)
    xdt = xs * dt_x
    xde_ref[...] = xdt * _dot_sel(jnp.exp(tail), expand)
    cd_ref[...] = jnp.exp(acum)

    ts = lax.broadcasted_iota(jnp.int32, (rows, SSM_WIDTH), 0) % steps
    gexp = gexp_ref[...]
    y = dsk_ref[...] * xs
    for d in range(steps):
        if d == 0:
            bm_d, xdt_d, ac_d = bm, xdt, acum_x
        else:
            bm_d = pltpu.roll(bm, d, axis=0)
            xdt_d = pltpu.roll(xdt, d, axis=0)
            ac_d = pltpu.roll(acum_x, d, axis=0)
        cb_x = _dot_sel(cm * bm_d, gexp, pieces=2)
        term = cb_x * jnp.exp(acum_x - ac_d) * xdt_d
        y = y + jnp.where(ts >= d, term, 0.0)
    pre_ref[...] = y
    zg_ref[...] = _silu(z_ref[...])


def _ssd_sample_pre(xbc, z, dt, prev, cw, cb, al, dsk, expand, gexp, steps):
    m = xbc.shape[0]
    tr = min(SAMPLE_ROWS, m)
    km1 = prev.shape[0]
    assert m % tr == 0 and tr % steps == 0 and km1 <= steps
    row = lambda w: pl.BlockSpec((tr, w), lambda i: (i, 0))
    state = pl.BlockSpec((km1, tr // steps, XBC_WIDTH), lambda i: (0, i, 0))
    consts = (cw, cb, al, dsk, expand, gexp)
    widths = (BC_W, BC_W, SSM_WIDTH, SSM_WIDTH, SSM_WIDTH, SSM_WIDTH, SSM_HEADS)
    return pl.pallas_call(
        functools.partial(_ssd_sample_pre_body, steps),
        grid=(m // tr,),
        in_specs=[row(XBC_WIDTH), row(SSM_WIDTH), row(SSM_HEADS), state]
                 + [_const_spec(a.shape) for a in consts],
        out_specs=[row(w) for w in widths] + [state],
        out_shape=[jax.ShapeDtypeStruct((m, w), F32) for w in widths] + [jax.ShapeDtypeStruct(prev.shape, F32)],
        scratch_shapes=[pltpu.VMEM((XBC_WIDTH // LANES, tr, LANES), F32)] * 2,
        compiler_params=_params(("parallel",)),
        name="ssd_sample_pre",
    )(xbc, z, dt, prev, *consts)


def _alibi_slopes():
    h = np.arange(1, N_ATT_HEADS + 1, dtype=np.float64)
    return np.exp2(-8.0 * h / N_ATT_HEADS).astype(np.float32).reshape(len(ATT_GROUPS), HEADS_PER_GROUP)


def _cache_view(cache):
    depth, bsz, cache_len = cache.shape[:3]
    return jnp.transpose(cache, (0, 1, 3, 4, 5, 2)).reshape(depth, bsz, 2, GROUP_W, cache_len)


def _layer_weights(lw):
    (norm_mix, q_norm, k_norm, conv_w, conv_b, dt_bias, a_log, d_skip, ssm_norm,
     norm_ffn, ffn_conv_w, ffn_conv_b) = lw
    w = {}
    w["norm_mix"] = norm_mix.reshape(1, D_MODEL)
    w["dtb"] = dt_bias.reshape(1, SSM_HEADS)
    w["gq_g"] = jnp.tile(q_norm, HEADS_PER_GROUP).reshape(1, GROUP_W)
    w["gk_g"] = jnp.tile(k_norm, HEADS_PER_GROUP).reshape(1, GROUP_W)
    w["gq_p"] = jnp.tile(q_norm, 2).reshape(1, PAIR_W)
    w["gk_p"] = jnp.tile(k_norm, 2).reshape(1, PAIR_W)
    w["cw"] = conv_w
    w["cb"] = conv_b.reshape(1, XBC_WIDTH)
    w["alr"] = a_log.reshape(1, SSM_HEADS)
    w["alc"] = a_log.reshape(SSM_HEADS, 1)
    w["dsk"] = jnp.repeat(d_skip, SSM_HEAD_DIM).reshape(1, SSM_WIDTH)
    w["gn"] = ssm_norm.reshape(1, SSM_WIDTH)
    w["norm_ffn"] = norm_ffn.reshape(1, D_MODEL)
    w["fw"] = ffn_conv_w
    w["fb"] = ffn_conv_b.reshape(1, 2 * D_FF)
    return w


def _constants(steps, cache_lens):
    c = {}
    slopes = _alibi_slopes()
    c["bd_g"] = _head_block_diag(GROUP_W)
    c["bd_p"] = _head_block_diag(PAIR_W)
    i = np.arange(SSM_CHUNK)
    c["tri"] = jnp.asarray(i[None, :] <= i[:, None], dtype=BF16)
    lane_head = np.arange(SSM_WIDTH) // SSM_HEAD_DIM
    c["expand"] = jnp.asarray(np.arange(SSM_HEADS)[:, None] == lane_head[None, :], dtype=BF16)
    bc_group = np.arange(BC_W) // SSM_STATE
    c["gexp"] = jnp.asarray(bc_group[:, None] == (lane_head // SSM_HPG)[None, :], dtype=BF16)
    c["hm"] = jnp.asarray(np.arange(HEADS_PER_GROUP)[:, None] == (np.arange(GROUP_W) // HEAD_DIM)[None, :], dtype=F32)
    c["pbias"] = [_prompt_bias(slopes[g], dil) for g, (_, dil) in enumerate(ATT_GROUPS)]
    c["cbias"] = [_sample_cache_bias(slopes[g], dil, steps, cache_lens[g], SUBLANES)
                  for g, (_, dil) in enumerate(ATT_GROUPS)]
    c["nbias"] = jnp.asarray(np.stack([_sample_new_bias(slopes[g], dil, steps, SUBLANES)
                                       for g, (_, dil) in enumerate(ATT_GROUPS)]), dtype=F32)
    return c


def _layer(xp, xs, w, big, c, layer, caches, state, new_states, conv_prev, ffn_prev, tm):
    bsz, seq, _ = xp.shape
    m = bsz * seq
    dec, steps, _ = xs.shape
    ms = dec * steps
    assert steps >= SSM_CONV - 1
    xp2 = xp.reshape(m, D_MODEL)
    xs2 = xs.reshape(ms, D_MODEL)
    qkv, z, xbc, dt = _inproj(xp2, w["norm_mix"], big["w_in"], w["dtb"], layer, tm)
    qkv_s, z_s, xbc_s, dt_s = _inproj(xs2, w["norm_mix"], big["w_in"], w["dtb"], layer, ms)

    qkv_t = qkv.reshape(QKV_SLABS, bsz, seq, PAIR_W)
    os_, ls_, kv_p = [], [], []
    for g, (win, dil) in enumerate(ATT_GROUPS):
        nbk = max(1, PROMPT_ATT_ROWS // (BAND * dil))
        o, lse, kn = _attn_prompt(qkv_t, g, dil, nbk, c["pbias"][g], w["gq_p"], w["gk_p"], c["bd_p"])
        os_.append(o)
        ls_.append(lse)
        keep = min(win, seq)
        v_g = qkv_t[2 * PAIRS + 2 * g:2 * PAIRS + 2 * g + 2, :, seq - keep:]
        kv = jnp.stack([kn[:, :, seq - keep:], v_g])
        kv = jnp.transpose(kv, (2, 3, 0, 1, 4))
        kv_p.append(kv.reshape(bsz, keep, 2, HEADS_PER_GROUP, HEAD_DIM))

    xbc3 = xbc.reshape(bsz, seq, XBC_WIDTH)
    dt3 = dt.reshape(bsz, seq, SSM_HEADS)
    ssd_consts = (w["cw"], w["cb"], w["alr"], w["alc"], w["dsk"], w["gn"], c["tri"], c["expand"])
    att_consts = (*c["cbias"], c["nbias"], w["gq_g"], w["gk_g"], c["bd_g"], c["hm"])
    *state_rows, cd, conv_tail = _ssd_sample_pre(xbc_s, z_s, dt_s, jnp.swapaxes(conv_prev, 0, 1), w["cw"],
                                                 w["cb"], w["alr"], w["dsk"], c["expand"], c["gexp"], steps)
    cd_last = cd.reshape(dec, steps, SSM_HEADS)[:, steps - 1]
    y, h_last, o_s, lse_s, kn_s, y_s, new_states = _ssd_prompt_sample(
        xbc3, z.reshape(bsz, seq, SSM_WIDTH), dt3, jnp.swapaxes(dt3, 1, 2), ssd_consts,
        qkv_s, caches, layer, steps, att_consts, state_rows, cd_last, state, new_states)

    x2, tail = _ffn_prompt(os_, ls_, y, xp, big["w_out"], w["norm_ffn"], big["w_up"], w["fw"], w["fb"],
                           big["w_down"], layer, tm)
    h_last = h_last.reshape(bsz, SSM_HEADS, SSM_HEAD_DIM, SSM_STATE)
    conv_p = xbc3[:, seq - (SSM_CONV - 1):]
    ffn_p = tail[:, SUBLANES - (FFN_CONV - 1):]

    kv_s = []
    for g in range(len(ATT_GROUPS)):
        v_g = qkv_s[2 * PAIRS + 2 * g:2 * PAIRS + 2 * g + 2]
        kv = jnp.stack([kn_s[2 * g:2 * g + 2], v_g])
        kv = jnp.transpose(kv, (2, 0, 1, 3))
        kv_s.append(kv.reshape(dec, steps, 2, HEADS_PER_GROUP, HEAD_DIM))
    x1_s = _outproj([o_s], [lse_s], y_s, xs2, big["w_out"], layer, ms)
    x2_s, ffn_tail = _ffn_sample(x1_s, steps, w["norm_ffn"], big["w_up"], w["fw"], w["fb"], big["w_down"], layer,
                                 jnp.swapaxes(ffn_prev, 0, 1))
    conv_s = jnp.swapaxes(conv_tail, 0, 1)
    ffn_s = jnp.swapaxes(ffn_tail, 0, 1)
    return (x2, x2_s.reshape(dec, steps, D_MODEL), (*kv_p, h_last, conv_p, ffn_p), (*kv_s, conv_s, ffn_s),
            new_states)


def kernel(x_prompt, x_sample, cache_kv0, cache_kv1, cache_kv2, state_ssm, state_conv, state_ffn_conv, norm_mix, w_in, q_norm, k_norm, conv_w, conv_b, dt_bias, a_log, d_skip, ssm_norm, w_out, norm_ffn, w_up, ffn_conv_w, ffn_conv_b, w_down):
    stacked = (norm_mix, q_norm, k_norm, conv_w, conv_b, dt_bias, a_log, d_skip, ssm_norm,
               norm_ffn, ffn_conv_w, ffn_conv_b)
    big = {"w_in": w_in.astype(BF16), "w_out": w_out.astype(BF16), "w_up": w_up.astype(BF16),
           "w_down": w_down.astype(BF16)}
    depth = w_in.shape[0]
    dec_batch, steps = x_sample.shape[:2]
    caches = tuple(_cache_view(cache) for cache in (cache_kv0, cache_kv1, cache_kv2))
    for cache, (_, dil) in zip(caches, ATT_GROUPS):
        assert cache.shape[-1] == BAND * dil
    state = state_ssm.reshape(depth, dec_batch, SSM_WIDTH, SSM_STATE)
    c = _constants(steps, tuple(cache.shape[-1] for cache in caches))
    tm = 512
    assert x_prompt.shape[1] % tm == 0
    y_prompt, y_sample = x_prompt, x_sample
    outs_p = [[] for _ in range(6)]
    outs_s = [[] for _ in range(5)]
    new_states = None
    for layer in range(depth):
        w = _layer_weights(tuple(a[layer] for a in stacked))
        y_prompt, y_sample, res_p, res_s, new_states = _layer(
            y_prompt, y_sample, w, big, c, layer, caches, state, new_states, state_conv[layer],
            state_ffn_conv[layer], tm)
        for lst, val in zip(outs_p, res_p):
            lst.append(val)
        for lst, val in zip(outs_s, res_s):
            lst.append(val)
    stack_s = [jnp.stack(l) for l in outs_s]
    return (y_prompt, y_sample, *[jnp.stack(l) for l in outs_p],
            *stack_s[:3], new_states.reshape(state_ssm.shape), *stack_s[3:])
```

```python
import functools
import math

import jax
import jax.numpy as jnp
import numpy as np
from jax import lax
from jax.experimental import pallas as pl
from jax.experimental.pallas import tpu as pltpu

F32 = jnp.float32
BF16 = jnp.bfloat16

D_MODEL = 1024
HEAD_DIM = 64
ATT_GROUPS = ((128, 1), (512, 4), (2048, 16))
BAND = 128
HEADS_PER_GROUP = 4
GROUP_W = HEADS_PER_GROUP * HEAD_DIM
N_ATT_HEADS = HEADS_PER_GROUP * len(ATT_GROUPS)
ATT_WIDTH = N_ATT_HEADS * HEAD_DIM
QKV_W = 3 * ATT_WIDTH
PAIR_W = 2 * HEAD_DIM
PAIRS = ATT_WIDTH // PAIR_W
QKV_SLABS = QKV_W // PAIR_W
SSM_HEAD_DIM = 64
SSM_WIDTH = 1024
SSM_HEADS = SSM_WIDTH // SSM_HEAD_DIM
SSM_STATE = 128
SSM_GROUPS = 2
SSM_HPG = SSM_HEADS // SSM_GROUPS
SSM_GROUP_W = SSM_HPG * SSM_HEAD_DIM
SSM_CONV = 4
SSM_CHUNK = 128
BC_W = SSM_GROUPS * SSM_STATE
XBC_WIDTH = SSM_WIDTH + 2 * BC_W
D_FF = 2816
FFN_CONV = 3
FF_CHUNK = 256
EPS = 1e-6
IN_WIDTH = QKV_W + SSM_WIDTH + XBC_WIDTH + SSM_HEADS
SUBLANES = 8
LANES = 128
SAMPLE_ROWS = 128
PROMPT_ATT_ROWS = 1024
RESIDUE_UNROLL = 4
VMEM_LIMIT = 56 * 1024 * 1024
NEG_INF = float("-inf")
LOG2E = math.log2(math.e)
LN2 = math.log(2.0)


def _dot(a, b):
    return jnp.dot(a, b, preferred_element_type=F32)


def _dot_nt(a, b):
    return lax.dot_general(a, b, (((1,), (1,)), ((), ())), preferred_element_type=F32)


def _dot_tn(a, b):
    return lax.dot_general(a, b, (((0,), (0,)), ((), ())), preferred_element_type=F32)


def _split(x, pieces):
    out = []
    r = x
    for _ in range(pieces):
        p = r.astype(BF16)
        out.append(p)
        r = r - p.astype(F32)
    return out


def _dot_sel(x, sel, pieces=3):
    acc = None
    for p in _split(x, pieces):
        t = _dot(p, sel)
        acc = t if acc is None else acc + t
    return acc


def _sel_dot(sel, x, pieces=3):
    acc = None
    for p in _split(x, pieces):
        t = _dot(sel, p)
        acc = t if acc is None else acc + t
    return acc


def _sel_dot_nt(x, sel, pieces=3):
    acc = None
    for p in _split(x, pieces):
        t = _dot_nt(p, sel)
        acc = t if acc is None else acc + t
    return acc


def _silu(x):
    h = 0.5 * x
    return h * (1.0 + jnp.tanh(h))


def _softplus(x):
    return jnp.maximum(x, 0.0) + jnp.log1p(jnp.exp(-jnp.abs(x)))


def _rms(x, g):
    ms = jnp.mean(x * x, axis=-1, keepdims=True)
    return x * lax.rsqrt(ms + EPS) * g


def _layer_spec(arr, layer):
    nd = arr.ndim
    return pl.BlockSpec((1,) + arr.shape[1:], lambda *_: (layer,) + (0,) * (nd - 1), pipeline_mode=pl.Buffered(1))


def _const_spec(shape):
    nd = len(shape)
    return pl.BlockSpec(shape, lambda *_: (0,) * nd, pipeline_mode=pl.Buffered(1))


def _params(sem):
    return pltpu.CompilerParams(dimension_semantics=sem, vmem_limit_bytes=VMEM_LIMIT)


def _head_block_diag(width):
    h = np.arange(width) // HEAD_DIM
    return jnp.asarray(np.where(h[:, None] == h[None, :], 1.0 / HEAD_DIM, 0.0), dtype=BF16)


def _head_rms(t, g, bd):
    ms = _dot_sel(t * t, bd, pieces=2)
    return t * lax.rsqrt(ms + EPS) * g


def _rows_back(slab_ref, s, d, n):
    return slab_ref[s, pl.ds(SUBLANES - d, n, stride=1), :]


def _inproj_body(x_ref, g_ref, w_ref, dtb_ref, qkv_ref, z_ref, xbc_ref, dt_ref):
    h = _rms(x_ref[...], g_ref[...]).astype(BF16)
    qkv = _dot(h, w_ref[0, :, 0:QKV_W])
    for s in range(QKV_SLABS):
        qkv_ref[s] = qkv[:, s * PAIR_W:(s + 1) * PAIR_W]
    z_ref[...] = _dot(h, w_ref[0, :, QKV_W:QKV_W + SSM_WIDTH])
    xd = _dot(h, w_ref[0, :, QKV_W + SSM_WIDTH:IN_WIDTH])
    xbc_ref[...] = xd[:, :XBC_WIDTH]
    dt_ref[...] = _softplus(xd[:, XBC_WIDTH:] + dtb_ref[...])


def _inproj(x, g, w_in, dtb, layer, tm):
    m = x.shape[0]
    row = lambda w: pl.BlockSpec((tm, w), lambda i: (i, 0))
    widths = (SSM_WIDTH, XBC_WIDTH, SSM_HEADS)
    return pl.pallas_call(
        _inproj_body,
        grid=(m // tm,),
        in_specs=[row(D_MODEL), _const_spec(g.shape), _layer_spec(w_in, layer), _const_spec(dtb.shape)],
        out_specs=[pl.BlockSpec((QKV_SLABS, tm, PAIR_W), lambda i: (0, i, 0))] + [row(w) for w in widths],
        out_shape=[jax.ShapeDtypeStruct((QKV_SLABS, m, PAIR_W), F32)]
                  + [jax.ShapeDtypeStruct((m, w), F32) for w in widths],
        compiler_params=_params(("parallel",)),
        name="inproj",
    )(x, g, w_in, dtb)


def _attn_prompt_body(dil, nbk, q0_ref, q1_ref, k0_ref, k1_ref, v0_ref, v1_ref, bias_ref, gq_ref, gk_ref, bd_ref,
                      o_ref, lse_ref, kn_ref, kv_scr):
    n = pl.program_id(1)
    rows_per = nbk * BAND
    in_refs = ((q0_ref, q1_ref), (k0_ref, k1_ref), (v0_ref, v1_ref))

    @pl.when(n == 0)
    def _():
        kv_scr[...] = jnp.zeros(kv_scr.shape, BF16)

    variant = jnp.minimum(n, 1)
    bd = bd_ref[...]
    gq = gq_ref[...]
    gk = gk_ref[...]
    low = lax.broadcasted_iota(jnp.int32, (BAND, PAIR_W), 1) < HEAD_DIM

    def rows_of(r, start, count):
        return pl.ds(r + start * dil, count, stride=dil) if dil > 1 else pl.ds(start, count)

    def group(residues):
        items = []
        for r in residues:
            for p in range(2):
                q, k, v = (in_refs[kind][p][0, 0, rows_of(r, 0, rows_per), :] for kind in range(3))
                qn = _head_rms(q, gq, bd) * (HEAD_DIM ** -0.5 * LOG2E)
                kn = _head_rms(k, gk, bd)
                kall = jnp.concatenate([kv_scr[0, p, r], kn.astype(BF16)], axis=0)
                vall = jnp.concatenate([kv_scr[1, p, r], v.astype(BF16)], axis=0)
                items.append((r, p, qn, kn, kall, vall))
        scores = []
        for r, p, qn, kn, kall, vall in items:
            for j in range(nbk):
                qj = qn[j * BAND:(j + 1) * BAND]
                for hh in range(2):
                    keep = low if hh == 0 else jnp.logical_not(low)
                    qm = jnp.where(keep, qj, 0.0).astype(BF16)
                    bias = bias_ref[variant if j == 0 else 1, 2 * p + hh]
                    scores.append(_dot_nt(qm, kall[j * BAND:(j + 2) * BAND]) + bias)
        probs = []
        for s in scores:
            mx = jnp.max(s, axis=-1, keepdims=True)
            e = jnp.exp2(s - mx)
            den = jnp.sum(e, axis=-1, keepdims=True)
            probs.append((e.astype(BF16), den, (mx + jnp.log2(den)) * LN2))
        results = []
        it = iter(probs)
        for r, p, qn, kn, kall, vall in items:
            for j in range(nbk):
                (e0, d0, l0), (e1, d1, l1) = next(it), next(it)
                vwin = vall[j * BAND:(j + 2) * BAND]
                o = jnp.where(low, _dot(e0, vwin) / d0, _dot(e1, vwin) / d1)
                results.append((r, p, j, o, jnp.where(low, l0, l1)))
        for r, p, j, o, lse in results:
            o_ref[p, 0, rows_of(r, j * BAND, BAND), :] = o
            lse_ref[p, 0, rows_of(r, j * BAND, BAND), :] = lse
        for r, p, qn, kn, kall, vall in items:
            kn_ref[p, 0, rows_of(r, 0, rows_per), :] = kn
            kv_scr[0, p, r] = kall[rows_per:]
            kv_scr[1, p, r] = vall[rows_per:]

    if dil <= RESIDUE_UNROLL:
        group(range(dil))
    else:
        def step(i, carry):
            group([i * RESIDUE_UNROLL + k for k in range(RESIDUE_UNROLL)])
            return carry
        lax.fori_loop(0, dil // RESIDUE_UNROLL, step, 0)


def _prompt_bias(slopes, dil):
    i = np.arange(BAND)[:, None]
    j = np.arange(2 * BAND)[None, :]
    step = i + BAND - j
    valid = (step >= 0) & (step <= BAND)
    bias = -slopes[:, None, None] * (step * dil).astype(np.float32)
    full = np.where(valid[None], bias, NEG_INF)
    first = np.where((valid & (j >= BAND))[None], bias, NEG_INF)
    return jnp.asarray(np.stack([first, full]) * LOG2E, dtype=F32)


def _attn_prompt(qkv_t, g, dil, nbk, bias, gq, gk, bd):
    _, bsz, seq, _ = qkv_t.shape
    blk = nbk * BAND * dil
    assert seq % blk == 0
    slab = lambda s: pl.BlockSpec((1, 1, blk, PAIR_W), lambda b, n: (s, b, n, 0))
    ospec = pl.BlockSpec((2, 1, blk, PAIR_W), lambda b, n: (0, b, n, 0))
    first = [kind * PAIRS + 2 * g for kind in range(3)]
    return pl.pallas_call(
        functools.partial(_attn_prompt_body, dil, nbk),
        grid=(bsz, seq // blk),
        in_specs=[slab(first[0]), slab(first[0] + 1), slab(first[1]), slab(first[1] + 1),
                  slab(first[2]), slab(first[2] + 1),
                  _const_spec(bias.shape), _const_spec(gq.shape), _const_spec(gk.shape), _const_spec(bd.shape)],
        out_specs=[ospec, ospec, ospec],
        out_shape=[jax.ShapeDtypeStruct((2, bsz, seq, PAIR_W), F32)] * 3,
        scratch_shapes=[pltpu.VMEM((2, 2, dil, BAND, PAIR_W), BF16)],
        compiler_params=_params(("parallel", "arbitrary")),
        name=f"attn_prompt_g{g}",
    )(*([qkv_t] * 6), bias, gq, gk, bd)


def _ssd_stage_conv(xbc_ref, conv_scr):
    raw = xbc_ref[0]
    for s in range(XBC_WIDTH // LANES):
        conv_scr[s, SUBLANES:, :] = raw[:, s * LANES:(s + 1) * LANES]
    return raw


def _ssd_chunk(raw, z_ref, dt_ref, dtt_ref, cw_ref, cb_ref, alr_ref, alc_ref, dsk_ref, gn_ref,
               tri_ref, exp_ref, conv_scr, state_scr):
    q = SSM_CHUNK

    cw = cw_ref[...]
    z = z_ref[0]
    state = state_scr[...]
    n_slabs = XBC_WIDTH // LANES
    parts = []
    for s in range(n_slabs):
        lanes = slice(s * LANES, (s + 1) * LANES)
        acc = _rows_back(conv_scr, s, 3, q) * cw[0:1, lanes]
        acc = acc + _rows_back(conv_scr, s, 2, q) * cw[1:2, lanes]
        acc = acc + _rows_back(conv_scr, s, 1, q) * cw[2:3, lanes]
        parts.append(_silu(acc + raw[:, lanes] * cw[3:4, lanes] + cb_ref[:, lanes]))
        if s % 3 == 2:
            yield
    xc = jnp.concatenate(parts, axis=1)
    xs = xc[:, :SSM_WIDTH]
    bm = xc[:, SSM_WIDTH:SSM_WIDTH + BC_W]
    cm = xc[:, SSM_WIDTH + BC_W:]

    tri = tri_ref[...]
    dt = dt_ref[0]
    acum = _sel_dot(tri, dt * (-jnp.exp(alr_ref[...])))
    acum_t = _sel_dot_nt(dtt_ref[0] * (-jnp.exp(alc_ref[...])), tri)
    yield
    last = acum[q - 1:q, :]
    expand = exp_ref[...]
    dt_x = _dot_sel(dt, expand)
    ea_x = _dot_sel(jnp.exp(acum), expand)
    de_x = _dot_sel(jnp.exp(last - acum), expand)
    xdt = xs * dt_x
    xde = (xdt * de_x).astype(BF16)
    xdt_b = xdt.astype(BF16)
    causal = lax.broadcasted_iota(jnp.int32, (q, q), 0) >= lax.broadcasted_iota(jnp.int32, (q, q), 1)
    low = lax.broadcasted_iota(jnp.int32, (q, 2 * SSM_HEAD_DIM), 1) < SSM_HEAD_DIM

    y_parts = []
    new_state = []
    for g in range(SSM_GROUPS):
        bm_g = bm[:, g * SSM_STATE:(g + 1) * SSM_STATE].astype(BF16)
        cm_g = cm[:, g * SSM_STATE:(g + 1) * SSM_STATE].astype(BF16)
        cb = _dot_nt(cm_g, bm_g)
        rows = slice(g * SSM_GROUP_W, (g + 1) * SSM_GROUP_W)
        st_g = state[rows, :]
        inter = _dot_nt(cm_g, st_g.astype(BF16))
        for hp in range(SSM_HPG // 2):
            pair = None
            for k in range(2):
                h = g * SSM_HPG + 2 * hp + k
                seg = acum[:, h:h + 1] - acum_t[h:h + 1, :]
                decay = jnp.exp(jnp.where(causal, seg, NEG_INF))
                gmat = (cb * decay).astype(BF16)
                lanes = slice((2 * hp) * SSM_HEAD_DIM + g * SSM_GROUP_W,
                              (2 * hp + 2) * SSM_HEAD_DIM + g * SSM_GROUP_W)
                x_pair = xdt_b[:, lanes]
                keep = low if k == 0 else jnp.logical_not(low)
                t = _dot(gmat, jnp.where(keep, x_pair, jnp.zeros_like(x_pair)))
                pair = t if pair is None else pair + t
            y_parts.append(pair + ea_x[:, lanes] * inter[:, lanes.start - g * SSM_GROUP_W:lanes.stop - g * SSM_GROUP_W])
            yield
        new = _dot_tn(xde[:, rows], bm_g)
        for hh in range(SSM_HPG):
            h = g * SSM_HPG + hh
            cd = jnp.exp(acum_t[h:h + 1, q - 1:q])
            r = slice(h * SSM_HEAD_DIM, (h + 1) * SSM_HEAD_DIM)
            rl = slice(hh * SSM_HEAD_DIM, (hh + 1) * SSM_HEAD_DIM)
            new_state.append((r, st_g[rl, :] * cd + new[rl, :]))

    yield
    y = jnp.concatenate(y_parts, axis=1) + dsk_ref[...] * xs
    y = y * _silu(z)
    return _rms(y, gn_ref[...]), new_state


def _ssd_store(raw, y, new_state, y_ref, conv_scr, state_scr):
    y_ref[0] = y
    for r, val in new_state:
        state_scr[r, :] = val
    for s in range(XBC_WIDTH // LANES):
        conv_scr[s, 0:SUBLANES, :] = raw[SSM_CHUNK - SUBLANES:, s * LANES:(s + 1) * LANES]


def _interleave(*gens):
    results = [None] * len(gens)
    live = list(range(len(gens)))
    while live:
        for i in list(live):
            try:
                next(gens[i])
            except StopIteration as done:
                results[i] = done.value
                live.remove(i)
    return results


def _sample_state_batch(steps, mine, batch, bm_ref, cm_ref, xde_ref, pre_ref, ea_ref, zg_ref, cd_ref, gn_ref,
                        st_ref):
    bm = bm_ref[...].astype(BF16)
    cm = cm_ref[...].astype(BF16)
    xde = jnp.where(mine, xde_ref[...], 0.0).astype(BF16)
    inter, new = [], []
    for g in range(SSM_GROUPS):
        srows = slice(g * SSM_GROUP_W, (g + 1) * SSM_GROUP_W)
        lanes = slice(g * SSM_STATE, (g + 1) * SSM_STATE)
        inter.append(_dot_nt(cm[:, lanes], st_ref[0, 0, srows, :].astype(BF16)))
        new.append(_dot_tn(xde[:, srows], bm[:, lanes]))
        yield
    y = pre_ref[...] + ea_ref[...] * jnp.concatenate(inter, axis=1)
    y = _rms(y * zg_ref[...], gn_ref[...])
    new_state = []
    for h in range(SSM_HEADS):
        g, hh = divmod(h, SSM_HPG)
        r = slice(h * SSM_HEAD_DIM, (h + 1) * SSM_HEAD_DIM)
        new_state.append((r, st_ref[0, 0, r, :] * cd_ref[batch, h]
                          + new[g][hh * SSM_HEAD_DIM:(hh + 1) * SSM_HEAD_DIM, :]))
    return y, new_state


def _ssd_attn_body(steps, dils, layer, fills, n_ssd, *refs):
    ssd_in, rest = refs[:n_ssd], refs[n_ssd:]
    (qkv_ref, kv0_ref, kv1_ref, kv2_ref, cb0_ref, cb1_ref, cb2_ref, nb_ref, gq_ref, gk_ref, bd_ref, hm_ref,
     bm_ref, cm_ref, xde_ref, pre_ref, ea_ref, zg_ref, cd_ref, st_ref) = rest[:20]
    y_ref, hl_ref, o_ref, lse_ref, kn_ref, ys_ref, ns_ref, conv_scr, state_scr = rest[-9:]
    c = pl.program_id(1)
    n_b = SUBLANES // steps
    batch = pl.program_id(0) * pl.num_programs(1) + c
    b = batch % n_b

    @pl.when(c == 0)
    def _():
        conv_scr[:, 0:SUBLANES, :] = jnp.zeros((conv_scr.shape[0], SUBLANES, LANES), F32)
        state_scr[...] = jnp.zeros_like(state_scr)

    raw = _ssd_stage_conv(ssd_in[0], conv_scr)
    mine_w = lax.broadcasted_iota(jnp.int32, (SUBLANES, SSM_WIDTH), 0) // steps == b
    sample, (y, new_state), (ys, new_sample_state) = _interleave(
        _sample_attn_batch(steps, dils, b, qkv_ref, (kv0_ref, kv1_ref, kv2_ref), (cb0_ref, cb1_ref, cb2_ref),
                           nb_ref, gq_ref, gk_ref, bd_ref, hm_ref),
        _ssd_chunk(raw, *ssd_in[1:], conv_scr, state_scr),
        _sample_state_batch(steps, mine_w, batch, bm_ref, cm_ref, xde_ref, pre_ref, ea_ref, zg_ref, cd_ref,
                            ssd_in[9], st_ref))
    _ssd_store(raw, y, new_state, y_ref, conv_scr, state_scr)
    own = layer if fills else 0
    for l in range(ns_ref.shape[0]):
        if l != own:
            ns_ref[l] = jnp.zeros(ns_ref.shape[1:], F32)
    for r, val in new_sample_state:
        ns_ref[own, 0, r, :] = val

    mine = lax.broadcasted_iota(jnp.int32, (SUBLANES, PAIR_W), 0) // steps == b

    @pl.when(b == 0)
    def _():
        ys_ref[...] = jnp.where(mine_w, ys, 0.0)
        for g, (o_g, l_g, kn) in enumerate(sample):
            for p in range(2):
                lanes = slice(p * PAIR_W, (p + 1) * PAIR_W)
                o_ref[2 * g + p] = jnp.where(mine, o_g[:, lanes], 0.0)
                lse_ref[2 * g + p] = jnp.where(mine, l_g[:, lanes], 0.0)
                kn_ref[2 * g + p] = kn[:, lanes]

    @pl.when(b > 0)
    def _():
        ys_ref[...] = jnp.where(mine_w, ys, ys_ref[...])
        for g, (o_g, l_g, kn) in enumerate(sample):
            for p in range(2):
                lanes = slice(p * PAIR_W, (p + 1) * PAIR_W)
                o_ref[2 * g + p] = jnp.where(mine, o_g[:, lanes], o_ref[2 * g + p])
                lse_ref[2 * g + p] = jnp.where(mine, l_g[:, lanes], lse_ref[2 * g + p])

    @pl.when(c == pl.num_programs(1) - 1)
    def _():
        hl_ref[0] = state_scr[...]


def _ssd_prompt_sample(xbc, z, dt, dtt, ssd_consts, qkv_t, caches, layer, steps, att_consts,
                       state_rows, cd, state, new_states):
    bsz, seq, _ = xbc.shape
    assert seq % SSM_CHUNK == 0
    nc = seq // SSM_CHUNK
    m = qkv_t.shape[1]
    n_b = SUBLANES // steps
    assert SUBLANES % steps == 0 and m == bsz * nc * steps
    dils = tuple(d for _, d in ATT_GROUPS)
    blk = lambda w: pl.BlockSpec((1, SSM_CHUNK, w), lambda b, c: (b, c, 0))
    tile = lambda b, c: (b * nc + c) // n_b
    rows = lambda w: pl.BlockSpec((SUBLANES, w), lambda b, c: (tile(b, c), 0))
    per_batch = lambda arr, lead, first: pl.BlockSpec((lead, 1) + arr.shape[2:],
                                                      lambda b, c: (first, b * nc + c) + (0,) * (arr.ndim - 2))
    ospec = pl.BlockSpec((PAIRS, SUBLANES, PAIR_W), lambda b, c: (0, tile(b, c), 0))
    ssd_in = [xbc, z, dt, dtt, *ssd_consts]
    fills = new_states is None
    ins = [*ssd_in, qkv_t, *caches, *att_consts, *state_rows, cd, state]
    in_specs = ([blk(XBC_WIDTH), blk(SSM_WIDTH), blk(SSM_HEADS),
                 pl.BlockSpec((1, SSM_HEADS, SSM_CHUNK), lambda b, c: (b, 0, c))]
                + [_const_spec(a.shape) for a in ssd_consts]
                + [pl.BlockSpec((QKV_SLABS, SUBLANES, PAIR_W), lambda b, c: (0, tile(b, c), 0))]
                + [per_batch(cache, 1, layer) for cache in caches]
                + [_const_spec(a.shape) for a in att_consts]
                + [rows(a.shape[1]) for a in state_rows]
                + [pl.BlockSpec(memory_space=pltpu.SMEM), per_batch(state, 1, layer)])
    aliases = {}
    if not fills:
        aliases = {len(ins): 6}
        ins.append(new_states)
        in_specs.append(pl.BlockSpec(memory_space=pl.ANY))
    ns_spec = per_batch(state, state.shape[0], 0) if fills else per_batch(state, 1, layer)
    return pl.pallas_call(
        functools.partial(_ssd_attn_body, steps, dils, layer, fills, len(ssd_in)),
        grid=(bsz, nc),
        in_specs=in_specs,
        out_specs=[blk(SSM_WIDTH), pl.BlockSpec((1, SSM_WIDTH, SSM_STATE), lambda b, c: (b, 0, 0)),
                   ospec, ospec, ospec, rows(SSM_WIDTH), ns_spec],
        out_shape=[jax.ShapeDtypeStruct((bsz, seq, SSM_WIDTH), F32),
                   jax.ShapeDtypeStruct((bsz, SSM_WIDTH, SSM_STATE), F32)]
                  + [jax.ShapeDtypeStruct((PAIRS, m, PAIR_W), F32)] * 3
                  + [jax.ShapeDtypeStruct((m, SSM_WIDTH), F32), jax.ShapeDtypeStruct(state.shape, F32)],
        scratch_shapes=[pltpu.VMEM((XBC_WIDTH // LANES, SUBLANES + SSM_CHUNK, LANES), F32),
                        pltpu.VMEM((SSM_WIDTH, SSM_STATE), F32)],
        input_output_aliases=aliases,
        compiler_params=_params(("arbitrary", "arbitrary")),
        name="ssd_prompt_sample",
    )(*ins)


def _mix_project(o_sl, l_sl, y, x, w_ref):
    ngrp = len(ATT_GROUPS)
    att = [None] * PAIRS
    for p in range(PAIRS // ngrp):
        ls = [l_sl[2 * g + p] for g in range(ngrp)]
        mx = functools.reduce(jnp.maximum, ls)
        es = [jnp.exp(l - mx) for l in ls]
        inv = 1.0 / functools.reduce(jnp.add, es)
        for g in range(ngrp):
            att[2 * g + p] = (o_sl[2 * g + p] * (es[g] * inv)).astype(BF16)
    acc = x + _dot(y.astype(BF16), w_ref[0, ATT_WIDTH:, :])
    return acc + _dot(jnp.concatenate(att, axis=1), w_ref[0, 0:ATT_WIDTH, :])


def _ffn_chunk(h, c, wu_ref, fw_ref, fb_ref, shifted):
    halves = []
    for base in (0, D_FF):
        cols = slice(base + c * FF_CHUNK, base + (c + 1) * FF_CHUNK)
        u = _dot(h, wu_ref[0, :, cols])
        s2, s1 = shifted(u, cols)
        w = fw_ref[:, cols]
        halves.append(s2 * w[0:1] + s1 * w[1:2] + u * w[2:3] + fb_ref[:, cols])
    return (_silu(halves[0]) * halves[1]).astype(BF16)


def _ffn_prompt_body(slabs, *refs):
    n_arr = len(slabs)
    o_refs, l_refs = refs[:n_arr], refs[n_arr:2 * n_arr]
    (y_ref, x_ref, wo_ref, g_ref, wu_ref, fw_ref, fb_ref, wd_ref,
     out_ref, tail_ref, u_scr, carry_scr, act_scr) = refs[2 * n_arr:]
    i = pl.program_id(1)

    @pl.when(i == 0)
    def _():
        carry_scr[...] = jnp.zeros_like(carry_scr)

    o_sl = [r[s, 0] for r, cnt in zip(o_refs, slabs) for s in range(cnt)]
    l_sl = [r[s, 0] for r, cnt in zip(l_refs, slabs) for s in range(cnt)]
    x = _mix_project(o_sl, l_sl, y_ref[0], x_ref[0], wo_ref)
    tm = x.shape[0]
    h = _rms(x, g_ref[...]).astype(BF16)
    new_carry = []

    def shifted(u, cols):
        half, start = divmod(cols.start, D_FF)
        par = (start // FF_CHUNK) % 2
        s2, s1 = [], []
        for k in range(FF_CHUNK // LANES):
            sl = half * (FF_CHUNK // LANES) + k
            lanes = slice(cols.start + k * LANES, cols.start + (k + 1) * LANES)
            u_scr[par, sl, 0:SUBLANES, :] = carry_scr[:, lanes]
            u_scr[par, sl, SUBLANES:, :] = u[:, k * LANES:(k + 1) * LANES]
            s2.append(u_scr[par, sl, pl.ds(SUBLANES - 2, tm, stride=1), :])
            s1.append(u_scr[par, sl, pl.ds(SUBLANES - 1, tm, stride=1), :])
            new_carry.append((lanes, u[tm - SUBLANES:, k * LANES:(k + 1) * LANES]))
        return jnp.concatenate(s2, axis=1), jnp.concatenate(s1, axis=1)

    for c in range(D_FF // FF_CHUNK):
        act_scr[:, c * FF_CHUNK:(c + 1) * FF_CHUNK] = _ffn_chunk(h, c, wu_ref, fw_ref, fb_ref, shifted)
    out_ref[0] = x + _dot(act_scr[...], wd_ref[0])
    for lanes, rows in new_carry:
        carry_scr[:, lanes] = rows

    @pl.when(i == pl.num_programs(1) - 1)
    def _():
        tail_ref[0] = carry_scr[...]


def _ffn_prompt(os_, ls_, y, x, w_out, g, wu, fw, fb, wd, layer, tm):
    bsz, seq, _ = x.shape
    slabs = tuple(a.shape[0] for a in os_)
    assert sum(slabs) == PAIRS
    row = lambda w: pl.BlockSpec((1, tm, w), lambda b, i: (b, i, 0))
    slab_specs = [pl.BlockSpec((n, 1, tm, PAIR_W), lambda b, i: (0, b, i, 0)) for n in slabs]
    chunk_slabs = 2 * FF_CHUNK // LANES
    return pl.pallas_call(
        functools.partial(_ffn_prompt_body, slabs),
        grid=(bsz, seq // tm),
        in_specs=slab_specs * 2 + [row(SSM_WIDTH), row(D_MODEL), _layer_spec(w_out, layer), _const_spec(g.shape),
                                   _layer_spec(wu, layer), _const_spec(fw.shape), _const_spec(fb.shape),
                                   _layer_spec(wd, layer)],
        out_specs=[row(D_MODEL), pl.BlockSpec((1, SUBLANES, 2 * D_FF), lambda b, i: (b, 0, 0))],
        out_shape=[jax.ShapeDtypeStruct((bsz, seq, D_MODEL), F32),
                   jax.ShapeDtypeStruct((bsz, SUBLANES, 2 * D_FF), F32)],
        scratch_shapes=[pltpu.VMEM((2, chunk_slabs, SUBLANES + tm, LANES), F32),
                        pltpu.VMEM((SUBLANES, 2 * D_FF), F32), pltpu.VMEM((tm, D_FF), BF16)],
        compiler_params=_params(("parallel", "arbitrary")),
        name="outproj_ffn_prompt",
    )(*os_, *ls_, y, x, w_out, g, wu, fw, fb, wd)


def _ffn_sample_body(steps, o_ref, l_ref, y_ref, x_ref, wo_ref, g_ref, wu_ref, fw_ref, fb_ref, wd_ref, prev_ref,
                     out_ref, tail_ref, e_scr, u_scr):
    x = _mix_project([o_ref[s] for s in range(PAIRS)], [l_ref[s] for s in range(PAIRS)], y_ref[...], x_ref[...],
                     wo_ref)
    rows = x.shape[0]
    n_b = rows // steps
    km1 = prev_ref.shape[0]
    h = _rms(x, g_ref[...]).astype(BF16)
    t = lax.broadcasted_iota(jnp.int32, (rows, FF_CHUNK), 0) % steps

    def shifted(u, cols):
        prev = []
        for k in range(FF_CHUNK // LANES):
            lanes = slice(cols.start + k * LANES, cols.start + (k + 1) * LANES)
            e_scr[k] = jnp.zeros((rows, LANES), F32)
            u_scr[k] = u[:, k * LANES:(k + 1) * LANES]
            for j in range(km1):
                e_scr[k, pl.ds(j, n_b, stride=steps), :] = prev_ref[j, :, lanes]
                tail_ref[j, :, lanes] = u_scr[k, pl.ds(steps - km1 + j, n_b, stride=steps), :]
            prev.append(e_scr[k])
        prev = jnp.concatenate(prev, axis=1)
        s1 = jnp.where(t >= 1, pltpu.roll(u, 1, axis=0), pltpu.roll(prev, rows - 1, axis=0))
        s2 = jnp.where(t >= 2, pltpu.roll(u, 2, axis=0), prev)
        return s2, s1

    act = [_ffn_chunk(h, c, wu_ref, fw_ref, fb_ref, shifted) for c in range(D_FF // FF_CHUNK)]
    out_ref[...] = x + _dot(jnp.concatenate(act, axis=1), wd_ref[0])


def _ffn_sample(o, lse, y, x, w_out, steps, g, wu, fw, fb, wd, layer, prev):
    m = x.shape[0]
    tr = min(SAMPLE_ROWS, m)
    km1 = prev.shape[0]
    assert m % tr == 0 and tr % steps == 0 and km1 <= steps and o.shape[0] == PAIRS
    n_b = tr // steps
    row = lambda w: pl.BlockSpec((tr, w), lambda i: (i, 0))
    slabs = pl.BlockSpec((PAIRS, tr, PAIR_W), lambda i: (0, i, 0))
    state = pl.BlockSpec((km1, n_b, 2 * D_FF), lambda i: (0, i, 0))
    return pl.pallas_call(
        functools.partial(_ffn_sample_body, steps),
        grid=(m // tr,),
        in_specs=[slabs, slabs, row(SSM_WIDTH), row(D_MODEL), _layer_spec(w_out, layer), _const_spec(g.shape),
                  _layer_spec(wu, layer), _const_spec(fw.shape), _const_spec(fb.shape), _layer_spec(wd, layer),
                  state],
        out_specs=[row(D_MODEL), state],
        out_shape=[jax.ShapeDtypeStruct((m, D_MODEL), F32), jax.ShapeDtypeStruct(prev.shape, F32)],
        scratch_shapes=[pltpu.VMEM((FF_CHUNK // LANES, tr, LANES), F32)] * 2,
        compiler_params=_params(("parallel",)),
        name="outproj_ffn_sample",
    )(o, lse, y, x, w_out, g, wu, fw, fb, wd, prev)


def _sample_attn_batch(steps, dils, b, qkv_ref, kv_refs, cb_refs, nb_ref, gq_ref, gk_ref, bd_ref, hm_ref):
    rows = qkv_ref.shape[1]
    assert rows == 2 * steps
    bd = bd_ref[...]
    hm = hm_ref[...]
    groups = range(len(dils))
    st = [dict() for _ in groups]
    for g in groups:
        slab = lambda kind: jnp.concatenate([qkv_ref[kind * PAIRS + 2 * g], qkv_ref[kind * PAIRS + 2 * g + 1]], axis=1)
        qn = _head_rms(slab(0), gq_ref[...], bd) * (HEAD_DIM ** -0.5)
        kn = _head_rms(slab(1), gk_ref[...], bd)
        vn = slab(2)
        st[g]["kn"] = kn
        st[g]["kn_b"] = jnp.where(b == 0, kn[0:steps], kn[steps:])
        st[g]["vn_b"] = jnp.where(b == 0, vn[0:steps], vn[steps:])
        st[g]["qbd"] = jnp.concatenate([qn * hm[h:h + 1] for h in range(HEADS_PER_GROUP)], axis=0)
        yield
    for g in groups:
        kc = kv_refs[g][0, 0, 0].astype(BF16)
        st[g]["s_c"] = _dot(st[g]["qbd"].astype(BF16), kc) + cb_refs[g][...]
        yield
    for g in groups:
        d = st[g]
        mx = jnp.max(d["s_c"], axis=-1, keepdims=True)
        d["s_n"] = []
        for t2 in range(steps):
            sn = jnp.sum(d["qbd"] * d["kn_b"][t2:t2 + 1], axis=-1, keepdims=True) + nb_ref[g, t2]
            d["s_n"].append(sn)
            mx = jnp.maximum(mx, sn)
        d["mx"] = mx
        d["e_c"] = jnp.exp(d["s_c"] - mx)
        d["den"] = jnp.sum(d["e_c"], axis=-1, keepdims=True)
        yield
    for g in groups:
        vc = kv_refs[g][0, 0, 1].astype(BF16)
        st[g]["acc"] = _dot_nt(st[g]["e_c"].astype(BF16), vc)
        yield
    results = []
    for g in groups:
        d = st[g]
        acc, den = d["acc"], d["den"]
        for t2, sn in enumerate(d["s_n"]):
            e_n = jnp.exp(sn - d["mx"])
            den = den + e_n
            acc = acc + e_n * d["vn_b"][t2:t2 + 1]
        o_all = acc / den
        l_all = d["mx"] + jnp.log(den)
        o_g = None
        l_g = None
        for h in range(HEADS_PER_GROUP):
            o_h = o_all[h * rows:(h + 1) * rows] * hm[h:h + 1]
            l_h = l_all[h * rows:(h + 1) * rows] * hm[h:h + 1]
            o_g = o_h if o_g is None else o_g + o_h
            l_g = l_h if l_g is None else l_g + l_h
        results.append((o_g, l_g, d["kn"]))
        yield
    return results


def _sample_cache_bias(slopes, dil, steps, cache_len, rows):
    t = (np.arange(rows) % steps)[None, :, None]
    c = np.arange(cache_len)[None, None, :]
    dist = cache_len + t - c
    j = dist // dil
    valid = (dist % dil == 0) & (j >= 1) & (j <= BAND)
    bias = -slopes[:, None, None] * dist.astype(np.float32)
    return jnp.asarray(np.where(valid, bias, NEG_INF).reshape(slopes.shape[0] * rows, cache_len), dtype=F32)


def _sample_new_bias(slopes, dil, steps, rows):
    t2 = np.arange(steps)[:, None, None]
    t = (np.arange(rows) % steps)[None, None, :]
    dist = t - t2
    valid = (dist >= 0) & (dist % dil == 0) & (dist // dil <= BAND)
    bias = -slopes[None, :, None] * dist.astype(np.float32)
    return np.where(valid, bias, NEG_INF).reshape(steps, slopes.shape[0] * rows, 1)


def _ssd_sample_pre_body(steps, xbc_ref, z_ref, dt_ref, prev_ref, cw_ref, cb_ref, al_ref, dsk_ref,
                         exp_ref, gexp_ref, bm_ref, cm_ref, xde_ref, pre_ref, ea_ref, zg_ref, cd_ref, tail_ref,
                         e_scr, u_scr):
    raw = xbc_ref[...]
    rows = raw.shape[0]
    n_b = rows // steps
    km1 = prev_ref.shape[0]
    tw = lax.broadcasted_iota(jnp.int32, (rows, XBC_WIDTH), 0) % steps
    cw = cw_ref[...]
    prev = []
    for s in range(XBC_WIDTH // LANES):
        lanes = slice(s * LANES, (s + 1) * LANES)
        e_scr[s] = jnp.zeros((rows, LANES), F32)
        u_scr[s] = raw[:, lanes]
        for j in range(km1):
            e_scr[s, pl.ds(j, n_b, stride=steps), :] = prev_ref[j, :, lanes]
            tail_ref[j, :, lanes] = u_scr[s, pl.ds(steps - km1 + j, n_b, stride=steps), :]
        prev.append(e_scr[s])
    prev = jnp.concatenate(prev, axis=1)
    acc = jnp.where(tw >= 3, pltpu.roll(raw, 3, axis=0), prev) * cw[0:1]
    acc = acc + jnp.where(tw >= 2, pltpu.roll(raw, 2, axis=0), pltpu.roll(prev, rows - 1, axis=0)) * cw[1:2]
    acc = acc + jnp.where(tw >= 1, pltpu.roll(raw, 1, axis=0), pltpu.roll(prev, rows - 2, axis=0)) * cw[2:3]
    acc = acc + raw * cw[3:4]
    xc = _silu(acc + cb_ref[...])
    xs = xc[:, :SSM_WIDTH]
    bm = xc[:, SSM_WIDTH:SSM_WIDTH + BC_W]
    cm = xc[:, SSM_WIDTH + BC_W:]
    bm_ref[...] = bm
    cm_ref[...] = cm

    dt = dt_ref[...]
    da = dt * (-jnp.exp(al_ref[...]))
    th = lax.broadcasted_iota(jnp.int32, (rows, SSM_HEADS), 0) % steps
    acum = da
    for d in range(1, steps):
        acum = acum + jnp.where(th >= d, pltpu.roll(da, d, axis=0), 0.0)
    tail = jnp.zeros_like(da)
    for d in range(1, steps):
        tail = tail + jnp.where(th + d < steps, pltpu.roll(da, rows - d, axis=0), 0.0)
    expand = exp_ref[...]
    dt_x = _dot_sel(dt, expand)
    acum_x = _dot_sel(acum, expand)
    ea_ref[...] = jnp.exp(acum_x)
    xdt = xs * dt_x
    xde_ref[...] = xdt * _dot_sel(jnp.exp(tail), expand)
    cd_ref[...] = jnp.exp(acum)

    ts = lax.broadcasted_iota(jnp.int32, (rows, SSM_WIDTH), 0) % steps
    gexp = gexp_ref[...]
    y = dsk_ref[...] * xs
    for d in range(steps):
        if d == 0:
            bm_d, xdt_d, ac_d = bm, xdt, acum_x
        else:
            bm_d = pltpu.roll(bm, d, axis=0)
            xdt_d = pltpu.roll(xdt, d, axis=0)
            ac_d = pltpu.roll(acum_x, d, axis=0)
        cb_x = _dot_sel(cm * bm_d, gexp, pieces=2)
        term = cb_x * jnp.exp(acum_x - ac_d) * xdt_d
        y = y + jnp.where(ts >= d, term, 0.0)
    pre_ref[...] = y
    zg_ref[...] = _silu(z_ref[...])


def _ssd_sample_pre(xbc, z, dt, prev, cw, cb, al, dsk, expand, gexp, steps):
    m = xbc.shape[0]
    tr = min(SAMPLE_ROWS, m)
    km1 = prev.shape[0]
    assert m % tr == 0 and tr % steps == 0 and km1 <= steps
    row = lambda w: pl.BlockSpec((tr, w), lambda i: (i, 0))
    state = pl.BlockSpec((km1, tr // steps, XBC_WIDTH), lambda i: (0, i, 0))
    consts = (cw, cb, al, dsk, expand, gexp)
    widths = (BC_W, BC_W, SSM_WIDTH, SSM_WIDTH, SSM_WIDTH, SSM_WIDTH, SSM_HEADS)
    return pl.pallas_call(
        functools.partial(_ssd_sample_pre_body, steps),
        grid=(m // tr,),
        in_specs=[row(XBC_WIDTH), row(SSM_WIDTH), row(SSM_HEADS), state]
                 + [_const_spec(a.shape) for a in consts],
        out_specs=[row(w) for w in widths] + [state],
        out_shape=[jax.ShapeDtypeStruct((m, w), F32) for w in widths] + [jax.ShapeDtypeStruct(prev.shape, F32)],
        scratch_shapes=[pltpu.VMEM((XBC_WIDTH // LANES, tr, LANES), F32)] * 2,
        compiler_params=_params(("parallel",)),
        name="ssd_sample_pre",
    )(xbc, z, dt, prev, *consts)


def _alibi_slopes():
    h = np.arange(1, N_ATT_HEADS + 1, dtype=np.float64)
    return np.exp2(-8.0 * h / N_ATT_HEADS).astype(np.float32).reshape(len(ATT_GROUPS), HEADS_PER_GROUP)


def _cache_view(cache):
    depth, bsz, cache_len = cache.shape[:3]
    return jnp.transpose(cache, (0, 1, 3, 4, 5, 2)).reshape(depth, bsz, 2, GROUP_W, cache_len)


def _layer_weights(lw):
    (norm_mix, q_norm, k_norm, conv_w, conv_b, dt_bias, a_log, d_skip, ssm_norm,
     norm_ffn, ffn_conv_w, ffn_conv_b) = lw
    w = {}
    w["norm_mix"] = norm_mix.reshape(1, D_MODEL)
    w["dtb"] = dt_bias.reshape(1, SSM_HEADS)
    w["gq_g"] = jnp.tile(q_norm, HEADS_PER_GROUP).reshape(1, GROUP_W)
    w["gk_g"] = jnp.tile(k_norm, HEADS_PER_GROUP).reshape(1, GROUP_W)
    w["gq_p"] = jnp.tile(q_norm, 2).reshape(1, PAIR_W)
    w["gk_p"] = jnp.tile(k_norm, 2).reshape(1, PAIR_W)
    w["cw"] = conv_w
    w["cb"] = conv_b.reshape(1, XBC_WIDTH)
    w["alr"] = a_log.reshape(1, SSM_HEADS)
    w["alc"] = a_log.reshape(SSM_HEADS, 1)
    w["dsk"] = jnp.repeat(d_skip, SSM_HEAD_DIM).reshape(1, SSM_WIDTH)
    w["gn"] = ssm_norm.reshape(1, SSM_WIDTH)
    w["norm_ffn"] = norm_ffn.reshape(1, D_MODEL)
    w["fw"] = ffn_conv_w
    w["fb"] = ffn_conv_b.reshape(1, 2 * D_FF)
    return w


def _constants(steps, cache_lens):
    c = {}
    slopes = _alibi_slopes()
    c["bd_g"] = _head_block_diag(GROUP_W)
    c["bd_p"] = _head_block_diag(PAIR_W)
    i = np.arange(SSM_CHUNK)
    c["tri"] = jnp.asarray(i[None, :] <= i[:, None], dtype=BF16)
    lane_head = np.arange(SSM_WIDTH) // SSM_HEAD_DIM
    c["expand"] = jnp.asarray(np.arange(SSM_HEADS)[:, None] == lane_head[None, :], dtype=BF16)
    bc_group = np.arange(BC_W) // SSM_STATE
    c["gexp"] = jnp.asarray(bc_group[:, None] == (lane_head // SSM_HPG)[None, :], dtype=BF16)
    c["hm"] = jnp.asarray(np.arange(HEADS_PER_GROUP)[:, None] == (np.arange(GROUP_W) // HEAD_DIM)[None, :], dtype=F32)
    c["pbias"] = [_prompt_bias(slopes[g], dil) for g, (_, dil) in enumerate(ATT_GROUPS)]
    c["cbias"] = [_sample_cache_bias(slopes[g], dil, steps, cache_lens[g], SUBLANES)
                  for g, (_, dil) in enumerate(ATT_GROUPS)]
    c["nbias"] = jnp.asarray(np.stack([_sample_new_bias(slopes[g], dil, steps, SUBLANES)
                                       for g, (_, dil) in enumerate(ATT_GROUPS)]), dtype=F32)
    return c


def _layer(xp, xs, w, big, c, layer, caches, state, new_states, conv_prev, ffn_prev, tm):
    bsz, seq, _ = xp.shape
    m = bsz * seq
    dec, steps, _ = xs.shape
    ms = dec * steps
    assert steps >= SSM_CONV - 1
    xp2 = xp.reshape(m, D_MODEL)
    xs2 = xs.reshape(ms, D_MODEL)
    qkv, z, xbc, dt = _inproj(xp2, w["norm_mix"], big["w_in"], w["dtb"], layer, tm)
    qkv_s, z_s, xbc_s, dt_s = _inproj(xs2, w["norm_mix"], big["w_in"], w["dtb"], layer, ms)

    qkv_t = qkv.reshape(QKV_SLABS, bsz, seq, PAIR_W)
    os_, ls_, kv_p = [], [], []
    for g, (win, dil) in enumerate(ATT_GROUPS):
        nbk = max(1, PROMPT_ATT_ROWS // (BAND * dil))
        o, lse, kn = _attn_prompt(qkv_t, g, dil, nbk, c["pbias"][g], w["gq_p"], w["gk_p"], c["bd_p"])
        os_.append(o)
        ls_.append(lse)
        keep = min(win, seq)
        v_g = qkv_t[2 * PAIRS + 2 * g:2 * PAIRS + 2 * g + 2, :, seq - keep:]
        kv = jnp.stack([kn[:, :, seq - keep:], v_g])
        kv = jnp.transpose(kv, (2, 3, 0, 1, 4))
        kv_p.append(kv.reshape(bsz, keep, 2, HEADS_PER_GROUP, HEAD_DIM))

    xbc3 = xbc.reshape(bsz, seq, XBC_WIDTH)
    dt3 = dt.reshape(bsz, seq, SSM_HEADS)
    ssd_consts = (w["cw"], w["cb"], w["alr"], w["alc"], w["dsk"], w["gn"], c["tri"], c["expand"])
    att_consts = (*c["cbias"], c["nbias"], w["gq_g"], w["gk_g"], c["bd_g"], c["hm"])
    *state_rows, cd, conv_tail = _ssd_sample_pre(xbc_s, z_s, dt_s, jnp.swapaxes(conv_prev, 0, 1), w["cw"],
                                                 w["cb"], w["alr"], w["dsk"], c["expand"], c["gexp"], steps)
    cd_last = cd.reshape(dec, steps, SSM_HEADS)[:, steps - 1]
    y, h_last, o_s, lse_s, kn_s, y_s, new_states = _ssd_prompt_sample(
        xbc3, z.reshape(bsz, seq, SSM_WIDTH), dt3, jnp.swapaxes(dt3, 1, 2), ssd_consts,
        qkv_s, caches, layer, steps, att_consts, state_rows, cd_last, state, new_states)

    x2, tail = _ffn_prompt(os_, ls_, y, xp, big["w_out"], w["norm_ffn"], big["w_up"], w["fw"], w["fb"],
                           big["w_down"], layer, tm)
    h_last = h_last.reshape(bsz, SSM_HEADS, SSM_HEAD_DIM, SSM_STATE)
    conv_p = xbc3[:, seq - (SSM_CONV - 1):]
    ffn_p = tail[:, SUBLANES - (FFN_CONV - 1):]

    kv_s = []
    for g in range(len(ATT_GROUPS)):
        v_g = qkv_s[2 * PAIRS + 2 * g:2 * PAIRS + 2 * g + 2]
        kv = jnp.stack([kn_s[2 * g:2 * g + 2], v_g])
        kv = jnp.transpose(kv, (2, 0, 1, 3))
        kv_s.append(kv.reshape(dec, steps, 2, HEADS_PER_GROUP, HEAD_DIM))
    x2_s, ffn_tail = _ffn_sample(o_s, lse_s, y_s, xs2, big["w_out"], steps, w["norm_ffn"], big["w_up"], w["fw"],
                                 w["fb"], big["w_down"], layer, jnp.swapaxes(ffn_prev, 0, 1))
    conv_s = jnp.swapaxes(conv_tail, 0, 1)
    ffn_s = jnp.swapaxes(ffn_tail, 0, 1)
    return (x2, x2_s.reshape(dec, steps, D_MODEL), (*kv_p, h_last, conv_p, ffn_p), (*kv_s, conv_s, ffn_s),
            new_states)


def kernel(x_prompt, x_sample, cache_kv0, cache_kv1, cache_kv2, state_ssm, state_conv, state_ffn_conv, norm_mix, w_in, q_norm, k_norm, conv_w, conv_b, dt_bias, a_log, d_skip, ssm_norm, w_out, norm_ffn, w_up, ffn_conv_w, ffn_conv_b, w_down):
    stacked = (norm_mix, q_norm, k_norm, conv_w, conv_b, dt_bias, a_log, d_skip, ssm_norm,
               norm_ffn, ffn_conv_w, ffn_conv_b)
    big = {"w_in": w_in.astype(BF16), "w_out": w_out.astype(BF16), "w_up": w_up.astype(BF16),
           "w_down": w_down.astype(BF16)}
    depth = w_in.shape[0]
    dec_batch, steps = x_sample.shape[:2]
    caches = tuple(_cache_view(cache) for cache in (cache_kv0, cache_kv1, cache_kv2))
    for cache, (_, dil) in zip(caches, ATT_GROUPS):
        assert cache.shape[-1] == BAND * dil
    state = state_ssm.reshape(depth, dec_batch, SSM_WIDTH, SSM_STATE)
    c = _constants(steps, tuple(cache.shape[-1] for cache in caches))
    tm = 512
    assert x_prompt.shape[1] % tm == 0
    y_prompt, y_sample = x_prompt, x_sample
    outs_p = [[] for _ in range(6)]
    outs_s = [[] for _ in range(5)]
    new_states = None
    for layer in range(depth):
        w = _layer_weights(tuple(a[layer] for a in stacked))
        y_prompt, y_sample, res_p, res_s, new_states = _layer(
            y_prompt, y_sample, w, big, c, layer, caches, state, new_states, state_conv[layer],
            state_ffn_conv[layer], tm)
        for lst, val in zip(outs_p, res_p):
            lst.append(val)
        for lst, val in zip(outs_s, res_s):
            lst.append(val)
    stack_s = [jnp.stack(l) for l in outs_s]
    return (y_prompt, y_sample, *[jnp.stack(l) for l in outs_p],
            *stack_s[:3], new_states.reshape(state_ssm.shape), *stack_s[3:])
```
